```python
import jax, jax.numpy as jnp
from jax import lax
import numpy as np

D_MODEL = 1024
BATCH = 4
SEQ = 4096
DEPTH = 2

RET_HEADS = 4
RET_DK = 64
RET_DV = 64
RET_CHUNK = 128
ROPE_BASE = 10000.0
RWKV_HEADS = 6
RWKV_HEAD = 64
RWKV_W = RWKV_HEADS * RWKV_HEAD
DECAY_LORA = 64
AAA_LORA = 64
GATE_LORA = 128
RWKV_GN_EPS = 64e-5
GLA_HEADS = 4
GLA_DK = 48
GLA_DV = 96
GLA_GATE_LORA = 16
GLA_GATE_NORMALIZER = 16.0
GLA_CHUNK = 64
D_MIX = RET_HEADS * RET_DV + RWKV_W + GLA_HEADS * GLA_DV
PEER_HEADS = 8
PEER_NKEYS = 128
PEER_NEXPERTS = PEER_NKEYS * PEER_NKEYS
PEER_QDIM = 256
PEER_QHALF = PEER_QDIM // 2
PEER_TOPK = 16
PEER_BLOCK = 128
NORM_EPS = 1e-6
HEAD_NORM_EPS = 1e-5

RET_SIZES = (RET_HEADS * RET_DK, RET_HEADS * RET_DK, RET_HEADS * RET_DV, RET_HEADS * RET_DV)
RWKV_SIZES = (RWKV_W, RWKV_W, RWKV_W, DECAY_LORA, AAA_LORA, GATE_LORA)
GLA_SIZES = (GLA_HEADS * GLA_DK, GLA_HEADS * GLA_DK, GLA_HEADS * GLA_DV, GLA_GATE_LORA, GLA_HEADS * GLA_DV)
GROUP_SIZES = (sum(RET_SIZES), sum(RWKV_SIZES), sum(GLA_SIZES))
D_IN = sum(GROUP_SIZES)

kernel_name = "hymba_ret_rwkv7_gla_peer"


def _split(z, sizes):
    offs = np.cumsum(np.array(sizes))[:-1].tolist()
    return jnp.split(z, offs, axis=-1)


def rms_norm(x, g):
    xf = x.astype(jnp.float32)
    y = xf * lax.rsqrt(jnp.mean(xf * xf, axis=-1, keepdims=True) + NORM_EPS)
    return (y * g.astype(jnp.float32)).astype(x.dtype)


def _head_rms(o, eps):
    return o * lax.rsqrt(jnp.mean(o * o, axis=-1, keepdims=True) + eps)


def _to_bhsd(t, h):
    b, s, _ = t.shape
    return t.reshape(b, s, h, -1).transpose(0, 2, 1, 3)


def _to_chunks(t, n, c):
    b, h, _, d = t.shape
    return t.reshape(b, h, n, c, d).transpose(2, 0, 1, 3, 4)


def _from_chunks(o):
    n, b, h, c, d = o.shape
    return o.transpose(1, 2, 0, 3, 4).reshape(b, h, n * c, d)


def _rotary(t, pos):
    half = t.shape[-1] // 2
    inv = ROPE_BASE ** (-jnp.arange(half, dtype=jnp.float32) / half)
    ang = pos[:, None] * inv[None, :]
    cos, sin = jnp.cos(ang), jnp.sin(ang)
    t1, t2 = t[..., :half], t[..., half:]
    return jnp.concatenate([t1 * cos - t2 * sin, t1 * sin + t2 * cos], axis=-1)


def retention_mixer(q, k, v, g):
    f32 = jnp.float32
    b, s, _ = q.shape
    h, c = RET_HEADS, RET_CHUNK
    n = s // c
    pos = jnp.arange(s, dtype=f32)
    q = _rotary(_to_bhsd(q, h).astype(f32), pos)
    k = _rotary(_to_bhsd(k, h).astype(f32), pos) * (RET_DK ** -0.5)
    v = _to_bhsd(v, h).astype(f32)
    log_gamma = jnp.log1p(-jnp.exp(jnp.linspace(np.log(1.0 / 32.0), np.log(1.0 / 512.0), h).astype(f32)))
    idx = jnp.arange(c, dtype=f32)
    diff = idx[:, None] - idx[None, :]
    causal = diff >= 0
    dmat = jnp.where(causal, jnp.exp(log_gamma[:, None, None] * jnp.where(causal, diff, 0.0)), 0.0)
    q_decay = jnp.exp(log_gamma[:, None] * (idx + 1.0))[..., None]
    k_decay = jnp.exp(log_gamma[:, None] * (c - 1.0 - idx))[..., None]
    chunk_decay = jnp.exp(log_gamma * c)[:, None, None]

    def step(state, inp):
        qc, kc, vc = inp
        scores = jnp.einsum('bhid,bhjd->bhij', qc, kc) * dmat
        o = jnp.einsum('bhij,bhjv->bhiv', scores, vc) + jnp.einsum('bhid,bhdv->bhiv', qc * q_decay, state)
        state = state * chunk_decay + jnp.einsum('bhjd,bhjv->bhdv', kc * k_decay, vc)
        return state, o

    s0 = jnp.zeros((b, h, RET_DK, RET_DV), f32)
    _, o = lax.scan(step, s0, (_to_chunks(q, n, c), _to_chunks(k, n, c), _to_chunks(v, n, c)))
    o = _head_rms(_from_chunks(o), HEAD_NORM_EPS)
    o = o.transpose(0, 2, 1, 3).reshape(b, s, h * RET_DV)
    return o * jax.nn.silu(g.astype(f32))


def rwkv7_mixer(f, mu, w_up, w_bias, a_up, a_bias, g_up, k_k, k_a, r_k, ln_w, ln_b):
    f32 = jnp.float32
    b, s, _ = f.shape
    h, nh = RWKV_HEADS, RWKV_HEAD
    f = f.astype(f32)
    f_prev = jnp.pad(f, ((0, 0), (1, 0), (0, 0)))[:, :-1]
    f = f + (f_prev - f) * mu.astype(f32)
    r, k, v, wd, ad, gd = _split(f, RWKV_SIZES)
    w = -jax.nn.softplus(-(w_bias + jnp.tanh(wd) @ w_up)) - 0.5
    decay = jnp.exp(-jnp.exp(w.astype(f32)))
    a = jax.nn.sigmoid(a_bias + ad @ a_up).astype(f32)
    g = (jax.nn.sigmoid(gd) @ g_up).astype(f32)
    hs = lambda t: t.astype(f32).reshape(b, s, h, nh)
    kk = hs(k * k_k)
    kk = kk * lax.rsqrt(jnp.maximum(jnp.sum(kk * kk, axis=-1, keepdims=True), 1e-24))
    k = k * (1.0 + (a - 1.0) * k_a)
    r_h, k_h, v_h, a_h, w_h = hs(r), hs(k), hs(v), hs(a), hs(decay)
    tm = lambda t: t.transpose(1, 0, 2, 3)

    def step(state, inp):
        r_t, w_t, k_t, v_t, kk_t, a_t = inp
        sa = jnp.einsum('bhvk,bhk->bhv', state, -kk_t)
        state = (state * w_t[:, :, None, :] + sa[..., None] * (kk_t * a_t)[:, :, None, :]
                 + v_t[..., None] * k_t[:, :, None, :])
        return state, jnp.einsum('bhvk,bhk->bhv', state, r_t)

    s0 = jnp.zeros((b, h, nh, nh), f32)
    _, o = lax.scan(step, s0, (tm(r_h), tm(w_h), tm(k_h), tm(v_h), tm(kk), tm(a_h)))
    o = tm(o)
    mean = jnp.mean(o, axis=-1, keepdims=True)
    var = jnp.mean(jnp.square(o - mean), axis=-1, keepdims=True)
    o = (o - mean) * lax.rsqrt(var + RWKV_GN_EPS) * ln_w.astype(f32).reshape(h, nh) + ln_b.astype(f32).reshape(h, nh)
    o = o + jnp.sum(r_h * k_h * r_k.astype(f32), axis=-1, keepdims=True) * v_h
    return o.reshape(b, s, h * nh) * g


def gla_mixer(q, k, v, gk_d, g, gk_up, gk_bias, norm_w):
    f32 = jnp.float32
    b, s, _ = q.shape
    h, c = GLA_HEADS, GLA_CHUNK
    n = s // c
    gk = jax.nn.log_sigmoid((gk_d @ gk_up + gk_bias).astype(f32)) / GLA_GATE_NORMALIZER
    q = _to_bhsd(q.astype(f32), h) * (GLA_DK ** -0.5)
    k = _to_bhsd(k.astype(f32), h)
    v = _to_bhsd(v.astype(f32), h)
    gk = _to_bhsd(gk, h)
    mask = (jnp.arange(c)[:, None] >= jnp.arange(c)[None, :])[:, :, None]

    def step(state, inp):
        qc, kc, vc, gc = inp
        bcum = jnp.cumsum(gc, axis=-2)
        b_last = bcum[..., -1, :]
        o_inter = jnp.einsum('bhid,bhdv->bhiv', qc * jnp.exp(bcum), state)
        diff = bcum[..., :, None, :] - bcum[..., None, :, :]
        dec = jnp.where(mask, jnp.exp(jnp.where(mask, diff, 0.0)), 0.0)
        attn = jnp.sum(qc[..., :, None, :] * kc[..., None, :, :] * dec, axis=-1)
        o = o_inter + jnp.einsum('bhij,bhjv->bhiv', attn, vc)
        state = (state * jnp.exp(b_last)[..., None]
                 + jnp.einsum('bhjd,bhjv->bhdv', kc * jnp.exp(b_last[..., None, :] - bcum), vc))
        return state, o

    s0 = jnp.zeros((b, h, GLA_DK, GLA_DV), f32)
    _, o = lax.scan(step, s0, (_to_chunks(q, n, c), _to_chunks(k, n, c), _to_chunks(v, n, c), _to_chunks(gk, n, c)))
    o = _from_chunks(o).transpose(0, 2, 1, 3)
    o = _head_rms(o, HEAD_NORM_EPS) * norm_w.astype(f32).reshape(h, GLA_DV)
    return o.reshape(b, s, h * GLA_DV) * jax.nn.silu(g.astype(f32))


def peer_ffn(hn, w_q, sub_keys, u, v):
    b, s, d = hn.shape
    hb = hn.reshape((b * s) // PEER_BLOCK, PEER_BLOCK, d)

    def block(xb):
        p = xb.shape[0]
        q = (xb @ w_q).reshape(p, PEER_HEADS, 2, PEER_QHALF)
        sc = jnp.einsum('thpc,hpnc->thpn', q, sub_keys).astype(jnp.float32)
        s1, i1 = lax.top_k(sc[:, :, 0], PEER_TOPK)
        s2, i2 = lax.top_k(sc[:, :, 1], PEER_TOPK)
        cand = (s1[..., :, None] + s2[..., None, :]).reshape(p, PEER_HEADS, PEER_TOPK * PEER_TOPK)
        cand_idx = (i1[..., :, None] * PEER_NKEYS + i2[..., None, :]).reshape(p, PEER_HEADS, PEER_TOPK * PEER_TOPK)
        top_s, sel = lax.top_k(cand, PEER_TOPK)
        eidx = jnp.take_along_axis(cand_idx, sel, axis=-1).reshape(p, PEER_HEADS * PEER_TOPK)
        gate = jax.nn.softmax(top_s, axis=-1).reshape(p, PEER_HEADS * PEER_TOPK)
        act = jax.nn.gelu(jnp.einsum('td,ted->te', xb, u[eidx]).astype(jnp.float32), approximate=False) * gate
        return jnp.einsum('te,ted->td', act, v[eidx]).astype(xb.dtype)

    return lax.map(block, hb).reshape(b, s, d)


def setup_inputs(seed: int = 0) -> dict:
    key = jax.random.key(seed)
    ks = iter(jax.random.split(key, 32))
    nrm = lambda shape, scale: jax.random.normal(next(ks), shape, jnp.float32) * scale
    L = DEPTH
    return {
        "x": nrm((BATCH, SEQ, D_MODEL), 1.0),
        "norm_mix": 1.0 + nrm((L, D_MODEL), 0.02),
        "norm_ffn": 1.0 + nrm((L, D_MODEL), 0.02),
        "norm_final": 1.0 + nrm((D_MODEL,), 0.02),
        "w_in": nrm((L, D_MODEL, D_IN), D_MODEL ** -0.5),
        "w_out": nrm((L, D_MIX, D_MODEL), D_MIX ** -0.5),
        "rwkv_mu": jax.random.uniform(next(ks), (L, sum(RWKV_SIZES)), jnp.float32),
        "rwkv_w_up": nrm((L, DECAY_LORA, RWKV_W), 0.5 * DECAY_LORA ** -0.5),
        "rwkv_w_bias": jax.random.uniform(next(ks), (L, RWKV_W), jnp.float32, -6.0, 1.0),
        "rwkv_a_up": nrm((L, AAA_LORA, RWKV_W), AAA_LORA ** -0.5),
        "rwkv_a_bias": nrm((L, RWKV_W), 0.5),
        "rwkv_g_up": nrm((L, GATE_LORA, RWKV_W), GATE_LORA ** -0.5),
        "rwkv_k_k": 0.85 + nrm((L, RWKV_W), 0.1),
        "rwkv_k_a": 1.0 + nrm((L, RWKV_W), 0.1),
        "rwkv_r_k": nrm((L, RWKV_HEADS, RWKV_HEAD), 0.1),
        "rwkv_ln_w": 1.0 + nrm((L, RWKV_W), 0.02),
        "rwkv_ln_b": nrm((L, RWKV_W), 0.02),
        "gla_gk_up": nrm((L, GLA_GATE_LORA, GLA_HEADS * GLA_DK), GLA_GATE_LORA ** -0.5),
        "gla_gk_bias": nrm((L, GLA_HEADS * GLA_DK), 0.5),
        "gla_norm_w": 1.0 + nrm((L, GLA_HEADS * GLA_DV), 0.02),
        "peer_w_q": nrm((L, D_MODEL, PEER_HEADS * PEER_QDIM), D_MODEL ** -0.5),
        "peer_sub_keys": nrm((L, PEER_HEADS, 2, PEER_NKEYS, PEER_QHALF), PEER_QHALF ** -0.5),
        "peer_u": nrm((L, PEER_NEXPERTS, D_MODEL), D_MODEL ** -0.5),
        "peer_v": nrm((L, PEER_NEXPERTS, D_MODEL), PEER_TOPK ** -0.5),
    }


def reference(x, norm_mix, norm_ffn, norm_final, w_in, w_out, rwkv_mu, rwkv_w_up, rwkv_w_bias,
              rwkv_a_up, rwkv_a_bias, rwkv_g_up, rwkv_k_k, rwkv_k_a, rwkv_r_k, rwkv_ln_w, rwkv_ln_b,
              gla_gk_up, gla_gk_bias, gla_norm_w, peer_w_q, peer_sub_keys, peer_u, peer_v):
    for l in range(DEPTH):
        h = rms_norm(x, norm_mix[l])
        z = h @ w_in[l]
        z_ret, z_rwkv, z_gla = _split(z, GROUP_SIZES)
        o_ret = retention_mixer(*_split(z_ret, RET_SIZES))
        o_rwkv = rwkv7_mixer(z_rwkv, rwkv_mu[l], rwkv_w_up[l], rwkv_w_bias[l], rwkv_a_up[l], rwkv_a_bias[l],
                             rwkv_g_up[l], rwkv_k_k[l], rwkv_k_a[l], rwkv_r_k[l], rwkv_ln_w[l], rwkv_ln_b[l])
        o_gla = gla_mixer(*_split(z_gla, GLA_SIZES), gla_gk_up[l], gla_gk_bias[l], gla_norm_w[l])
        o = jnp.concatenate([o_ret, o_rwkv, o_gla], axis=-1).astype(x.dtype)
        x = x + o @ w_out[l]
        h = rms_norm(x, norm_ffn[l])
        x = x + peer_ffn(h, peer_w_q[l], peer_sub_keys[l], peer_u[l], peer_v[l])
    return rms_norm(x, norm_final)
```

```python
import functools

import numpy as np
import jax
import jax.numpy as jnp
from jax import lax
from jax.experimental import pallas as pl
from jax.experimental.pallas import tpu as pltpu

F32 = jnp.float32
BF16 = jnp.bfloat16
HI = lax.Precision.HIGHEST

D_MODEL = 1024
DEPTH = 2
NORM_EPS = 1e-6
HEAD_NORM_EPS = 1e-5
RET_HEADS, RET_DK, RET_DV, ROPE_BASE = 4, 64, 64, 10000.0
RET_W = RET_HEADS * RET_DK
RET_CHUNK = 128
RWKV_HEADS, RWKV_HEAD = 6, 64
RWKV_W = RWKV_HEADS * RWKV_HEAD
DECAY_LORA, AAA_LORA, GATE_LORA = 64, 64, 128
RWKV_GN_EPS = 64e-5
RWKV_CHUNK = 64
RWKV_SUB = 16
RWKV_IN = 3 * RWKV_W + DECAY_LORA + AAA_LORA + GATE_LORA
GLA_HEADS, GLA_DK, GLA_DV, GLA_GATE_LORA = 4, 48, 96, 16
GLA_GATE_NORMALIZER = 16.0
GLA_DKP, GLA_DVP = 64, 128
GLA_QW = GLA_HEADS * GLA_DKP
GLA_VW = GLA_HEADS * GLA_DVP
GLA_IN = 2 * GLA_QW + GLA_VW + 128 + GLA_VW
GLA_SUB = 16
GLA_BLOCK = 128
RET_IN = 4 * RET_W
D_INP = RET_IN + RWKV_IN + GLA_IN
D_MIXP = RET_W + RWKV_W + GLA_VW
PEER_HEADS, PEER_NKEYS, PEER_QHALF, PEER_TOPK = 8, 128, 128, 16
PEER_NEXPERTS = PEER_NKEYS * PEER_NKEYS
LANES = 128
SUBLANES = 8
VMEM_LIMIT = 56 * 1024 * 1024


def _params(sem):
    return pltpu.CompilerParams(dimension_semantics=sem, vmem_limit_bytes=VMEM_LIMIT)


def _dot(a, b, prec=None):
    return jnp.dot(a, b, precision=prec, preferred_element_type=F32)


def _dot_nt(a, b, prec=None):
    return lax.dot_general(a, b, (((1,), (1,)), ((), ())), precision=prec, preferred_element_type=F32)


def _dot_tn(a, b, prec=None):
    return lax.dot_general(a, b, (((0,), (0,)), ((), ())), precision=prec, preferred_element_type=F32)


def _bf(x):
    return x.astype(BF16)


def _iota(shape, dim):
    return lax.broadcasted_iota(jnp.int32, shape, dim)


def _same_block(shape, rblk, cblk):
    r = _iota(shape, 0) >> (rblk.bit_length() - 1)
    c = _iota(shape, 1) >> (cblk.bit_length() - 1)
    return r == c


def _full(shape):
    n = len(shape)
    return pl.BlockSpec(shape, lambda *_: (0,) * n)


def _block_ones(n, blk):
    i = np.arange(n) // blk
    return jnp.asarray((i[:, None] == i[None, :]).astype(np.float32))


def _tril_ones(n):
    return jnp.asarray(np.tril(np.ones((n, n), np.float32)))


def _norm_proj_kernel(x_ref, g_ref, w_ref, zr_ref, zk_ref, zg_ref):
    x = x_ref[...]
    y = x * lax.rsqrt(jnp.mean(x * x, axis=-1, keepdims=True) + NORM_EPS) * g_ref[...]
    yb = _bf(y)
    zr_ref[...] = _dot(yb, w_ref[:, 0:RET_IN])
    zk_ref[...] = _dot(yb, w_ref[:, RET_IN:RET_IN + RWKV_IN])
    zg_ref[...] = _dot(yb, w_ref[:, RET_IN + RWKV_IN:D_INP])


def _norm_proj(x2, gain, w_in_p, tm=512):
    t = x2.shape[0]
    return pl.pallas_call(
        _norm_proj_kernel,
        grid=(t // tm,),
        in_specs=[pl.BlockSpec((tm, D_MODEL), lambda i: (i, 0)), _full((1, D_MODEL)), _full((D_MODEL, D_INP))],
        out_specs=[pl.BlockSpec((tm, RET_IN), lambda i: (i, 0)), pl.BlockSpec((tm, RWKV_IN), lambda i: (i, 0)),
                   pl.BlockSpec((tm, GLA_IN), lambda i: (i, 0))],
        out_shape=[jax.ShapeDtypeStruct((t, RET_IN), F32), jax.ShapeDtypeStruct((t, RWKV_IN), F32),
                   jax.ShapeDtypeStruct((t, GLA_IN), F32)],
        compiler_params=_params(("arbitrary",)),
        name="norm_proj",
    )(x2, gain, w_in_p)


def _ret_kernel(z_ref, cos_ref, sin_ref, dmat_ref, qdec_ref, kdec_ref, cdec_ref, ones_ref, o_ref, st_ref):
    c = RET_CHUNK

    @pl.when(pl.program_id(1) == 0)
    def _():
        st_ref[...] = jnp.zeros_like(st_ref)

    z = z_ref[0]
    q, k, v, g = (z[:, i * RET_W:(i + 1) * RET_W] for i in range(4))
    cos, sin = cos_ref[...], sin_ref[...]
    first_half = (_iota((c, RET_W), 1) & (RET_DK - 1)) < RET_DK // 2

    def rot(t):
        return jnp.where(first_half, -pltpu.roll(t, RET_W - RET_DK // 2, 1), pltpu.roll(t, RET_DK // 2, 1))

    q = q * cos + rot(q) * sin
    k = (k * cos + rot(k) * sin) * (RET_DK ** -0.5)
    qd = q * qdec_ref[...]
    kd = k * kdec_ref[...]
    head0 = _iota((c, LANES), 1) < RET_DK
    bd = _same_block((LANES, LANES), RET_DV, RET_DK)
    outs = []
    for p in range(RET_HEADS // 2):
        sl = slice(p * LANES, (p + 1) * LANES)
        qp, kp, vp = q[:, sl], _bf(k[:, sl]), v[:, sl]
        o = None
        for hh in range(2):
            mh = head0 if hh == 0 else jnp.logical_not(head0)
            s = _dot_nt(_bf(jnp.where(mh, qp, 0.0)), kp) * dmat_ref[2 * p + hh]
            oh = _dot(_bf(s), _bf(jnp.where(mh, vp, 0.0)))
            o = oh if o is None else o + oh
        st = st_ref[p]
        o = o + _dot_nt(_bf(qd[:, sl]), _bf(st))
        upd = _dot_tn(_bf(vp), _bf(kd[:, sl]))
        st_ref[p] = st * cdec_ref[:, sl] + jnp.where(bd, upd, 0.0)
        outs.append(o)
    o = jnp.concatenate(outs, axis=1)
    ms = _dot(o * o, ones_ref[...], HI) * (1.0 / RET_DV)
    o = o * lax.rsqrt(ms + HEAD_NORM_EPS)
    o_ref[0] = o * (g * jax.nn.sigmoid(g))


def _retention_consts(s):
    c = RET_CHUNK
    log_gamma = np.log1p(-np.exp(np.linspace(np.log(1.0 / 32.0), np.log(1.0 / 512.0), RET_HEADS)))
    idx = np.arange(c, dtype=np.float64)
    diff = idx[:, None] - idx[None, :]
    dmat = np.where(diff >= 0, np.exp(log_gamma[:, None, None] * np.where(diff >= 0, diff, 0.0)), 0.0)
    lg_lane = np.repeat(log_gamma, RET_DK)[None, :]
    qdec = np.exp(lg_lane * (idx[:, None] + 1.0))
    kdec = np.exp(lg_lane * (c - 1.0 - idx[:, None]))
    cdec = np.exp(lg_lane * c)
    half = RET_DK // 2
    inv = ROPE_BASE ** (-np.arange(half, dtype=np.float64) / half)
    ang = np.arange(s, dtype=np.float64)[:, None] * inv[None, :]
    cos = np.tile(np.cos(ang), (1, 2 * RET_HEADS))
    sin = np.tile(np.sin(ang), (1, 2 * RET_HEADS))
    f = lambda a: jnp.asarray(a.astype(np.float32))
    return f(cos), f(sin), f(dmat), f(qdec), f(kdec), f(cdec)


def _retention(z_ret):
    b, s, _ = z_ret.shape
    c = RET_CHUNK
    cos, sin, dmat, qdec, kdec, cdec = _retention_consts(s)
    return pl.pallas_call(
        _ret_kernel,
        grid=(b, s // c),
        in_specs=[pl.BlockSpec((1, c, RET_IN), lambda i, j: (i, j, 0)),
                  pl.BlockSpec((c, RET_W), lambda i, j: (j, 0)), pl.BlockSpec((c, RET_W), lambda i, j: (j, 0)),
                  _full((RET_HEADS, c, c)), _full((c, RET_W)), _full((c, RET_W)), _full((1, RET_W)),
                  _full((RET_W, RET_W))],
        out_specs=pl.BlockSpec((1, c, RET_W), lambda i, j: (i, j, 0)),
        out_shape=jax.ShapeDtypeStruct((b, s, RET_W), F32),
        scratch_shapes=[pltpu.VMEM((RET_HEADS // 2, LANES, LANES), F32)],
        compiler_params=_params(("arbitrary", "arbitrary")),
        name="retention",
    )(z_ret, cos, sin, dmat, qdec, kdec, cdec, _block_ones(RET_W, RET_DV))


def _rwkv_kernel(z_ref, mu_ref, wwa_ref, wb_ref, ab_ref, gup_ref, kk_ref, ka_ref, rk_ref, lnw_ref, lnb_ref,
                 ones_ref, ltri_ref, o_ref, st_ref, carry_ref):
    c = RWKV_CHUNK
    w_ = RWKV_W

    @pl.when(pl.program_id(1) == 0)
    def _():
        st_ref[...] = jnp.zeros_like(st_ref)
        carry_ref[...] = jnp.zeros_like(carry_ref)

    f = z_ref[0]
    f_prev = jnp.where(_iota((c, RWKV_IN), 0) == 0, carry_ref[...], pltpu.roll(f, 1, 0))
    carry_ref[...] = f[c - 1:c, :]
    f = f + (f_prev - f) * mu_ref[...]
    r, k, v = f[:, 0:w_], f[:, w_:2 * w_], f[:, 2 * w_:3 * w_]
    wa, gd = f[:, 3 * w_:3 * w_ + LANES], f[:, 3 * w_ + LANES:RWKV_IN]
    lane = _iota((c, LANES), 1)
    head0 = lane < RWKV_HEAD
    proj = _dot(jnp.where(lane < DECAY_LORA, jnp.tanh(wa), wa), wwa_ref[...], HI)
    wlog = -jax.nn.softplus(-(wb_ref[...] + proj[:, 0:w_])) - 0.5
    ld = -jnp.exp(wlog)
    a = jax.nn.sigmoid(ab_ref[...] + proj[:, w_:2 * w_])
    g = _dot(_bf(jax.nn.sigmoid(gd)), gup_ref[...])
    ones = ones_ref[...]
    kk = k * kk_ref[...]
    kk = kk * lax.rsqrt(jnp.maximum(_dot(kk * kk, ones, HI), 1e-24))
    k = k * (1.0 + (a - 1.0) * ka_ref[...])
    cum = _dot(ltri_ref[...], ld, HI)
    cl = cum[c - 1:c, :]
    e_neg = jnp.exp(-cum)
    e_rem = jnp.exp(cl - cum)
    b = kk * a
    a_t = -kk * jnp.exp(cum - ld)
    b_t, k_t, r_t = b * e_neg, k * e_neg, r * jnp.exp(cum)
    b_h, k_h = b * e_rem, k * e_rem
    p_c = jnp.exp(cl)

    n2 = 2 * c
    ri, ci = _iota((n2, n2), 0), _iota((n2, n2), 1)
    same_head = _same_block((n2, n2), c, c)
    strict = jnp.logical_and(same_head, ri > ci)
    incl = jnp.logical_and(same_head, ri >= ci)
    sub = _same_block((n2, n2), RWKV_SUB, RWKV_SUB)
    eye = (ri == ci).astype(F32)
    bd = _same_block((LANES, LANES), RWKV_HEAD, RWKV_HEAD)
    eye_l = _iota((LANES, LANES), 0) == _iota((LANES, LANES), 1)

    def stack(x):
        return jnp.concatenate([jnp.where(head0, x, 0.0), jnp.where(head0, 0.0, x)], axis=0)

    def fold(x):
        return x[:c] + x[c:]

    outs = []
    for p in range(RWKV_HEADS // 2):
        sl = slice(p * LANES, (p + 1) * LANES)
        a_s, r_s, b_s, k_s, v_s = stack(a_t[:, sl]), stack(r_t[:, sl]), stack(b_t[:, sl]), stack(k_t[:, sl]), stack(v[:, sl])
        big = _dot_nt(_bf(jnp.concatenate([a_s, r_s], axis=0)), _bf(jnp.concatenate([b_s, k_s], axis=0)))
        a_ab = jnp.where(strict, big[:n2, :n2], 0.0)
        a_ak = jnp.where(strict, big[:n2, n2:], 0.0)
        a_rb = jnp.where(incl, big[n2:, :n2], 0.0)
        a_rk = jnp.where(incl, big[n2:, n2:], 0.0)
        d1 = jnp.where(sub, a_ab, 0.0)
        lo = a_ab - d1
        d2 = _dot(d1, d1, HI)
        d4 = _dot(d2, d2, HI)
        d8 = _dot(d4, d4, HI)
        t_d = _dot(_dot(_dot(eye + d1, eye + d2, HI), eye + d4, HI), eye + d8, HI)
        e1 = _dot(t_d, lo, HI)
        e2 = _dot(e1, e1, HI)
        t_inv = _dot(_dot(eye + e1, eye + e2, HI), t_d, HI)
        akv = _dot(_bf(a_ak), _bf(v_s))
        wy = _dot(t_inv, jnp.concatenate([a_s, akv], axis=1), HI)
        qo = _dot(_bf(a_rb), _bf(wy))
        w_p, y_p = fold(wy[:, :LANES]), fold(wy[:, LANES:])
        q_p = fold(r_s + qo[:, :LANES])
        ol_p = fold(qo[:, LANES:] + _dot(_bf(a_rk), _bf(v_s)))
        gmat = jnp.where(bd, _dot_tn(_bf(w_p), _bf(b_h[:, sl])), 0.0)
        nmat = jnp.where(bd, _dot_tn(_bf(jnp.concatenate([y_p, v[:, sl]], axis=0)),
                                     _bf(jnp.concatenate([b_h[:, sl], k_h[:, sl]], axis=0))), 0.0)
        st = st_ref[p]
        stb = _bf(st)
        outs.append(_dot_nt(_bf(q_p), stb) + ol_p)
        st_ref[p] = st * p_c[:, sl] + _dot(stb, _bf(gmat)) + nmat
    o = jnp.concatenate(outs, axis=1)
    inv_n = 1.0 / RWKV_HEAD
    d = o - _dot(o, ones, HI) * inv_n
    var = _dot(d * d, ones, HI) * inv_n
    o = d * lax.rsqrt(var + RWKV_GN_EPS) * lnw_ref[...] + lnb_ref[...]
    o = o + _dot(r * k * rk_ref[...], ones, HI) * v
    o_ref[0] = o * g


def _rwkv(z_rwkv, pr):
    b, s, _ = z_rwkv.shape
    c = RWKV_CHUNK
    row = lambda n: _full((1, n))
    return pl.pallas_call(
        _rwkv_kernel,
        grid=(b, s // c),
        in_specs=[pl.BlockSpec((1, c, RWKV_IN), lambda i, j: (i, j, 0)), row(RWKV_IN), _full((LANES, 2 * RWKV_W)),
                  row(RWKV_W), row(RWKV_W), _full((GATE_LORA, RWKV_W)), row(RWKV_W), row(RWKV_W), row(RWKV_W),
                  row(RWKV_W), row(RWKV_W), _full((RWKV_W, RWKV_W)), _full((c, c))],
        out_specs=pl.BlockSpec((1, c, RWKV_W), lambda i, j: (i, j, 0)),
        out_shape=jax.ShapeDtypeStruct((b, s, RWKV_W), F32),
        scratch_shapes=[pltpu.VMEM((RWKV_HEADS // 2, LANES, LANES), F32), pltpu.VMEM((1, RWKV_IN), F32)],
        compiler_params=_params(("arbitrary", "arbitrary")),
        name="rwkv7",
    )(z_rwkv, pr["mu"], pr["wwa"], pr["w_bias"], pr["a_bias"], pr["g_up"], pr["k_k"], pr["k_a"], pr["r_k"],
      pr["ln_w"], pr["ln_b"], _block_ones(RWKV_W, RWKV_HEAD), _tril_ones(c))


def _gla_kernel(z_ref, gkup_ref, gkb_ref, nw_ref, ind_ref, ones_ref, ltri_ref, o_ref, st_ref, x_ref, gk_ref, acc_ref):
    n = GLA_SUB
    tb = GLA_BLOCK

    @pl.when(pl.program_id(1) == 0)
    def _():
        st_ref[...] = jnp.zeros_like(st_ref)

    gkd = z_ref[0, :, 2 * GLA_QW + GLA_VW:2 * GLA_QW + GLA_VW + LANES]
    gk_ref[...] = jax.nn.log_sigmoid(_dot(gkd, gkup_ref[...], HI) + gkb_ref[...]) * (1.0 / GLA_GATE_NORMALIZER)
    bd = _same_block((GLA_VW, GLA_QW), GLA_DVP, GLA_DKP)
    rowi = _iota((n, GLA_VW), 0)
    ltri = ltri_ref[...]
    ind = ind_ref[...]

    def sub_chunk(sc, carry):
        r0 = pl.multiple_of(sc * n, n)
        rows = pl.ds(r0, n)
        q = z_ref[0, rows, 0:GLA_QW] * (GLA_DK ** -0.5)
        k = z_ref[0, rows, GLA_QW:2 * GLA_QW]
        v = z_ref[0, rows, 2 * GLA_QW:2 * GLA_QW + GLA_VW]
        bc = _dot(ltri, gk_ref[rows, :], HI)
        bl = bc[n - 1:n, :]
        st = st_ref[...]
        o = _dot_nt(_bf(q * jnp.exp(bc)), _bf(st))
        for j in range(n):
            x_ref[j * n:(j + 1) * n, :] = q * jnp.exp(jnp.minimum(bc - bc[j:j + 1, :], 0.0)) * k[j:j + 1, :]
        x = x_ref[...]
        xh = _bf(x)
        xl = _bf(x - xh.astype(F32))
        e = _dot(xh, ind) + _dot(xl, ind)
        for j in range(n):
            o = o + jnp.where(rowi >= j, e[j * n:(j + 1) * n, :], 0.0) * v[j:j + 1, :]
        upd = _dot_tn(_bf(v), _bf(k * jnp.exp(bl - bc)))
        st_ref[...] = st * jnp.exp(bl) + jnp.where(bd, upd, 0.0)
        acc_ref[rows, :] = o
        return carry

    lax.fori_loop(0, tb // n, sub_chunk, 0)
    o = acc_ref[...]
    ms = _dot(o * o, ones_ref[...], HI) * (1.0 / GLA_DV)
    g = z_ref[0, :, 2 * GLA_QW + GLA_VW + LANES:GLA_IN]
    o_ref[0] = o * lax.rsqrt(ms + HEAD_NORM_EPS) * nw_ref[...] * (g * jax.nn.sigmoid(g))


def _gla(z_gla, pr):
    b, s, _ = z_gla.shape
    tb = GLA_BLOCK
    hq = np.arange(GLA_QW) // GLA_DKP
    hv = np.arange(GLA_VW) // GLA_DVP
    ind = jnp.asarray((hq[:, None] == hv[None, :]).astype(np.float32), dtype=BF16)
    return pl.pallas_call(
        _gla_kernel,
        grid=(b, s // tb),
        in_specs=[pl.BlockSpec((1, tb, GLA_IN), lambda i, j: (i, j, 0)), _full((LANES, GLA_QW)), _full((1, GLA_QW)),
                  _full((1, GLA_VW)), _full((GLA_QW, GLA_VW)), _full((GLA_VW, GLA_VW)), _full((GLA_SUB, GLA_SUB))],
        out_specs=pl.BlockSpec((1, tb, GLA_VW), lambda i, j: (i, j, 0)),
        out_shape=jax.ShapeDtypeStruct((b, s, GLA_VW), F32),
        scratch_shapes=[pltpu.VMEM((GLA_VW, GLA_QW), F32), pltpu.VMEM((GLA_SUB * GLA_SUB, GLA_QW), F32),
                        pltpu.VMEM((tb, GLA_QW), F32), pltpu.VMEM((tb, GLA_VW), F32)],
        compiler_params=_params(("arbitrary", "arbitrary")),
        name="gla",
    )(z_gla, pr["gk_up"], pr["gk_bias"], pr["norm_w"], ind, _block_ones(GLA_VW, GLA_DVP), _tril_ones(GLA_SUB))


def _out_proj_kernel(x_ref, a_ref, b_ref, c_ref, w_ref, g_ref, x1_ref, hnt_ref):
    acc = _dot(_bf(a_ref[...]), w_ref[0:RET_W, :])
    acc += _dot(_bf(b_ref[...]), w_ref[RET_W:RET_W + RWKV_W, :])
    acc += _dot(_bf(c_ref[...]), w_ref[RET_W + RWKV_W:D_MIXP, :])
    x1 = x_ref[...] + acc
    x1_ref[...] = x1
    hn = x1 * lax.rsqrt(jnp.mean(x1 * x1, axis=-1, keepdims=True) + NORM_EPS) * g_ref[...]
    hnt_ref[...] = _bf(hn.T)


def _out_proj(x2, o_ret, o_rwkv, o_gla, w_out_p, gain, tm=512):
    t = x2.shape[0]
    blk = lambda n: pl.BlockSpec((tm, n), lambda i: (i, 0))
    return pl.pallas_call(
        _out_proj_kernel,
        grid=(t // tm,),
        in_specs=[blk(D_MODEL), blk(RET_W), blk(RWKV_W), blk(GLA_VW), _full((D_MIXP, D_MODEL)), _full((1, D_MODEL))],
        out_specs=[blk(D_MODEL), pl.BlockSpec((D_MODEL, tm), lambda i: (0, i))],
        out_shape=[jax.ShapeDtypeStruct((t, D_MODEL), F32), jax.ShapeDtypeStruct((D_MODEL, t), BF16)],
        compiler_params=_params(("arbitrary",)),
        name="out_proj",
    )(x2, o_ret, o_rwkv, o_gla, w_out_p, gain)


def _staircase_pairs():
    k = PEER_TOPK
    return [(a, b) for a in range(k) for b in range(k) if (a + 1) * (b + 1) <= k]


def _route_kernel(hnt_ref, wq_ref, keys_ref, thr_ref, e1_ref, s2_ref, e2_ref, work_ref, s1_ref, vals_ref):
    nh, nk, k_top = PEER_HEADS, PEER_NKEYS, PEER_TOPK
    tb = hnt_ref.shape[1]
    qt = _bf(_dot(wq_ref[...], hnt_ref[...]))
    for p in range(2):
        for h in range(nh):
            r0 = (p * nh + h) * PEER_QHALF
            s = _dot(keys_ref[p, h], qt[r0:r0 + PEER_QHALF, :])
            work_ref[p, h] = s
            if p == 0:
                s1_ref[h] = s
            else:
                s2_ref[h] = s

    groups = nk // SUBLANES

    def extract(r, carry):
        for p in range(2):
            for h in range(nh):
                tiles = [work_ref[p, h, i * SUBLANES:(i + 1) * SUBLANES, :] for i in range(groups)]
                lvl = tiles
                while len(lvl) > 1:
                    lvl = [jnp.maximum(lvl[2 * i], lvl[2 * i + 1]) for i in range(len(lvl) // 2)]
                m = lvl[0]
                for sh in (4, 2, 1):
                    m = jnp.maximum(m, pltpu.roll(m, sh, 0))
                vals_ref[p, r, pl.ds(h, 1), :] = m[0:1, :]
                for i in range(groups):
                    work_ref[p, h, i * SUBLANES:(i + 1) * SUBLANES, :] = jnp.where(tiles[i] == m, -jnp.inf, tiles[i])
        return carry

    lax.fori_loop(0, k_top, extract, 0)

    v1 = [vals_ref[0, r] for r in range(k_top)]
    v2 = [vals_ref[1, r] for r in range(k_top)]
    cands = [v1[a] + v2[b] for a, b in _staircase_pairs()]
    tau = None
    for ci in cands:
        cnt = None
        for cj in cands:
            ge = (cj >= ci).astype(F32)
            cnt = ge if cnt is None else cnt + ge
        t_i = jnp.where(cnt >= float(k_top), ci, -jnp.inf)
        tau = t_i if tau is None else jnp.maximum(tau, t_i)
    top = cands[0]
    z = None
    for ci in cands:
        zi = jnp.where(ci >= tau, jnp.exp(ci - top), 0.0)
        z = zi if z is None else z + zi
    inv_z = 1.0 / z
    m1, m2 = v1[0], v2[0]
    for h in range(nh):
        s1 = s1_ref[h]
        thr_ref[h] = tau[h:h + 1, :] - s1
        e1_ref[h] = jnp.exp(s1 - m1[h:h + 1, :])
        e2_ref[h] = jnp.exp(s2_ref[h] - m2[h:h + 1, :]) * inv_z[h:h + 1, :]


def _route(hnt, wq_t, keys, tb=256):
    t = hnt.shape[1]
    nh, nk = PEER_HEADS, PEER_NKEYS
    out = pl.BlockSpec((nh, nk, tb), lambda i: (0, 0, i))
    return pl.pallas_call(
        _route_kernel,
        grid=(t // tb,),
        in_specs=[pl.BlockSpec((D_MODEL, tb), lambda i: (0, i)), _full((2 * nh * PEER_QHALF, D_MODEL)),
                  _full((2, nh, nk, PEER_QHALF))],
        out_specs=[out, out, out, out],
        out_shape=[jax.ShapeDtypeStruct((nh, nk, t), F32)] * 4,
        scratch_shapes=[pltpu.VMEM((2, nh, nk, tb), F32), pltpu.VMEM((nh, nk, tb), F32),
                        pltpu.VMEM((2, PEER_TOPK, nh, tb), F32)],
        compiler_params=_params(("arbitrary",)),
        name="peer_route",
    )(hnt, wq_t, keys)


def _expert_kernel(hnt_ref, u_ref, vt_ref, thr_ref, e1_ref, s2_ref, e2_ref, x1_ref, gf_ref, o_ref, acc_ref, *,
                   slabs, final_norm):
    j = pl.program_id(1)
    nk = PEER_NKEYS

    @pl.when(j == 0)
    def _():
        acc_ref[...] = jnp.zeros_like(acc_ref)

    ht = _dot(u_ref[...], hnt_ref[...])
    acts = []
    for kk in range(slabs):
        i1 = j * slabs + kk
        gate = None
        for h in range(PEER_HEADS):
            hit = s2_ref[h] >= thr_ref[h, pl.ds(i1, 1), :]
            gh = jnp.where(hit, e2_ref[h] * e1_ref[h, pl.ds(i1, 1), :], 0.0)
            gate = gh if gate is None else gate + gh
        hk = ht[kk * nk:(kk + 1) * nk, :]
        gelu = 0.5 * hk * (1.0 + lax.erf(hk * (2.0 ** -0.5)))
        acts.append(_bf(gelu * gate))
    acc_ref[...] += _dot(vt_ref[...], jnp.concatenate(acts, axis=0))

    @pl.when(j == pl.num_programs(1) - 1)
    def _():
        y = acc_ref[...].T + x1_ref[...]
        if final_norm:
            y = y * lax.rsqrt(jnp.mean(y * y, axis=-1, keepdims=True) + NORM_EPS) * gf_ref[...]
        o_ref[...] = y


def _experts(hnt, u_b, vt_b, thr, e1, s2, e2, x1, gain_f, final_norm, tb=512, slabs=4):
    t = hnt.shape[1]
    nh, nk = PEER_HEADS, PEER_NKEYS
    eb = slabs * nk
    res = pl.BlockSpec((nh, nk, tb), lambda i, j: (0, 0, i))
    return pl.pallas_call(
        functools.partial(_expert_kernel, slabs=slabs, final_norm=final_norm),
        grid=(t // tb, PEER_NEXPERTS // eb),
        in_specs=[pl.BlockSpec((D_MODEL, tb), lambda i, j: (0, i)), pl.BlockSpec((eb, D_MODEL), lambda i, j: (j, 0)),
                  pl.BlockSpec((D_MODEL, eb), lambda i, j: (0, j)), res, res, res, res,
                  pl.BlockSpec((tb, D_MODEL), lambda i, j: (i, 0)), _full((1, D_MODEL))],
        out_specs=pl.BlockSpec((tb, D_MODEL), lambda i, j: (i, 0)),
        out_shape=jax.ShapeDtypeStruct((t, D_MODEL), F32),
        scratch_shapes=[pltpu.VMEM((D_MODEL, tb), F32)],
        compiler_params=_params(("arbitrary", "arbitrary")),
        name="peer_experts",
    )(hnt, u_b, vt_b, thr, e1, s2, e2, x1, gain_f)


def _pad_heads(w, heads, d, dp):
    lead = w.shape[:-1]
    w = w.reshape(*lead, heads, d)
    w = jnp.pad(w, [(0, 0)] * len(lead) + [(0, 0), (0, dp - d)])
    return w.reshape(*lead, heads * dp)


def _layer_params(l, w_in, w_out, rwkv_mu, rwkv_w_up, rwkv_w_bias, rwkv_a_up, rwkv_a_bias, rwkv_g_up, rwkv_k_k,
                  rwkv_k_a, rwkv_r_k, rwkv_ln_w, rwkv_ln_b, gla_gk_up, gla_gk_bias, gla_norm_w, peer_w_q,
                  peer_sub_keys, peer_u, peer_v):
    wi = w_in[l]
    g0 = RET_IN + RWKV_IN
    qk, vw = GLA_HEADS * GLA_DK, GLA_HEADS * GLA_DV
    gq = _pad_heads(wi[:, g0:g0 + qk], GLA_HEADS, GLA_DK, GLA_DKP)
    gkk = _pad_heads(wi[:, g0 + qk:g0 + 2 * qk], GLA_HEADS, GLA_DK, GLA_DKP)
    gv = _pad_heads(wi[:, g0 + 2 * qk:g0 + 2 * qk + vw], GLA_HEADS, GLA_DV, GLA_DVP)
    ggk = jnp.pad(wi[:, g0 + 2 * qk + vw:g0 + 2 * qk + vw + GLA_GATE_LORA], ((0, 0), (0, LANES - GLA_GATE_LORA)))
    gg = _pad_heads(wi[:, g0 + 2 * qk + vw + GLA_GATE_LORA:], GLA_HEADS, GLA_DV, GLA_DVP)
    w_in_p = _bf(jnp.concatenate([wi[:, :g0], gq, gkk, gv, ggk, gg], axis=1))
    wo = w_out[l]
    m0 = RET_W + RWKV_W
    wo_gla = jnp.pad(wo[m0:].reshape(GLA_HEADS, GLA_DV, D_MODEL), ((0, 0), (0, GLA_DVP - GLA_DV), (0, 0)))
    w_out_p = _bf(jnp.concatenate([wo[:m0], wo_gla.reshape(GLA_VW, D_MODEL)], axis=0))
    zeros = jnp.zeros((DECAY_LORA, RWKV_W), F32)
    rw = dict(
        mu=rwkv_mu[l][None, :],
        wwa=jnp.concatenate([jnp.concatenate([rwkv_w_up[l], zeros], axis=1),
                             jnp.concatenate([zeros, rwkv_a_up[l]], axis=1)], axis=0),
        w_bias=rwkv_w_bias[l][None, :], a_bias=rwkv_a_bias[l][None, :], g_up=_bf(rwkv_g_up[l]),
        k_k=rwkv_k_k[l][None, :], k_a=rwkv_k_a[l][None, :], r_k=rwkv_r_k[l].reshape(1, RWKV_W),
        ln_w=rwkv_ln_w[l][None, :], ln_b=rwkv_ln_b[l][None, :])
    gl = dict(
        gk_up=jnp.pad(_pad_heads(gla_gk_up[l], GLA_HEADS, GLA_DK, GLA_DKP), ((0, LANES - GLA_GATE_LORA), (0, 0))),
        gk_bias=_pad_heads(gla_gk_bias[l][None, :], GLA_HEADS, GLA_DK, GLA_DKP),
        norm_w=_pad_heads(gla_norm_w[l][None, :], GLA_HEADS, GLA_DV, GLA_DVP))
    wq = peer_w_q[l].reshape(D_MODEL, PEER_HEADS, 2, PEER_QHALF)
    wq_t = _bf(jnp.transpose(wq, (2, 1, 3, 0)).reshape(2 * PEER_HEADS * PEER_QHALF, D_MODEL))
    keys = _bf(jnp.transpose(peer_sub_keys[l], (1, 0, 2, 3)))
    return dict(w_in=w_in_p, w_out=w_out_p, rwkv=rw, gla=gl, wq_t=wq_t, keys=keys, u=_bf(peer_u[l]),
                vt=_bf(peer_v[l].T))


def _layer(x2, b, s, pr, gain_mix, gain_ffn, gain_final, final_norm):
    z_ret, z_rwkv, z_gla = _norm_proj(x2, gain_mix, pr["w_in"])
    o_ret = _retention(z_ret.reshape(b, s, RET_IN)).reshape(b * s, RET_W)
    o_rwkv = _rwkv(z_rwkv.reshape(b, s, RWKV_IN), pr["rwkv"]).reshape(b * s, RWKV_W)
    o_gla = _gla(z_gla.reshape(b, s, GLA_IN), pr["gla"]).reshape(b * s, GLA_VW)
    x1, hnt = _out_proj(x2, o_ret, o_rwkv, o_gla, pr["w_out"], gain_ffn)
    thr, e1, s2, e2 = _route(hnt, pr["wq_t"], pr["keys"])
    return _experts(hnt, pr["u"], pr["vt"], thr, e1, s2, e2, x1, gain_final, final_norm)


def kernel(x, norm_mix, norm_ffn, norm_final, w_in, w_out, rwkv_mu, rwkv_w_up, rwkv_w_bias, rwkv_a_up, rwkv_a_bias, rwkv_g_up, rwkv_k_k, rwkv_k_a, rwkv_r_k, rwkv_ln_w, rwkv_ln_b, gla_gk_up, gla_gk_bias, gla_norm_w, peer_w_q, peer_sub_keys, peer_u, peer_v):
    b, s, d = x.shape
    x2 = x.reshape(b * s, d)
    gain_final = norm_final[None, :]
    for l in range(DEPTH):
        pr = _layer_params(l, w_in, w_out, rwkv_mu, rwkv_w_up, rwkv_w_bias, rwkv_a_up, rwkv_a_bias, rwkv_g_up,
                           rwkv_k_k, rwkv_k_a, rwkv_r_k, rwkv_ln_w, rwkv_ln_b, gla_gk_up, gla_gk_bias, gla_norm_w,
                           peer_w_q, peer_sub_keys, peer_u, peer_v)
        x2 = _layer(x2, b, s, pr, norm_mix[l][None, :], norm_ffn[l][None, :], gain_final, l == DEPTH - 1)
    return x2.reshape(b, s, d)
```

```python
import functools

import numpy as np
import jax
import jax.numpy as jnp
from jax import lax
from jax.experimental import pallas as pl
from jax.experimental.pallas import tpu as pltpu

F32 = jnp.float32
BF16 = jnp.bfloat16
HI = lax.Precision.HIGHEST

D_MODEL = 1024
DEPTH = 2
NORM_EPS = 1e-6
HEAD_NORM_EPS = 1e-5
RET_HEADS, RET_DK, RET_DV, ROPE_BASE = 4, 64, 64, 10000.0
RET_W = RET_HEADS * RET_DK
RET_CHUNK = 128
RWKV_HEADS, RWKV_HEAD = 6, 64
RWKV_W = RWKV_HEADS * RWKV_HEAD
DECAY_LORA, AAA_LORA, GATE_LORA = 64, 64, 128
RWKV_GN_EPS = 64e-5
RWKV_CHUNK = 64
RWKV_SUB = 16
RWKV_IN = 3 * RWKV_W + DECAY_LORA + AAA_LORA + GATE_LORA
GLA_HEADS, GLA_DK, GLA_DV, GLA_GATE_LORA = 4, 48, 96, 16
GLA_GATE_NORMALIZER = 16.0
GLA_DKP, GLA_DVP = 64, 128
GLA_QW = GLA_HEADS * GLA_DKP
GLA_VW = GLA_HEADS * GLA_DVP
GLA_IN = 2 * GLA_QW + GLA_VW + 128 + GLA_VW
GLA_SUB = 16
GLA_BLOCK = 128
RET_IN = 4 * RET_W
D_INP = RET_IN + RWKV_IN + GLA_IN
D_MIXP = RET_W + RWKV_W + GLA_VW
PEER_HEADS, PEER_NKEYS, PEER_QHALF, PEER_TOPK = 8, 128, 128, 16
PEER_NEXPERTS = PEER_NKEYS * PEER_NKEYS
LANES = 128
SUBLANES = 8
VMEM_LIMIT = 56 * 1024 * 1024


def _params(sem):
    return pltpu.CompilerParams(dimension_semantics=sem, vmem_limit_bytes=VMEM_LIMIT)


def _dot(a, b, prec=None):
    return jnp.dot(a, b, precision=prec, preferred_element_type=F32)


def _dot_nt(a, b, prec=None):
    return lax.dot_general(a, b, (((1,), (1,)), ((), ())), precision=prec, preferred_element_type=F32)


def _dot_tn(a, b, prec=None):
    return lax.dot_general(a, b, (((0,), (0,)), ((), ())), precision=prec, preferred_element_type=F32)


def _bf(x):
    return x.astype(BF16)


def _iota(shape, dim):
    return lax.broadcasted_iota(jnp.int32, shape, dim)


def _same_block(shape, rblk, cblk):
    r = _iota(shape, 0) >> (rblk.bit_length() - 1)
    c = _iota(shape, 1) >> (cblk.bit_length() - 1)
    return r == c


def _full(shape):
    n = len(shape)
    return pl.BlockSpec(shape, lambda *_: (0,) * n)


def _block_ones(n, blk):
    i = np.arange(n) // blk
    return jnp.asarray((i[:, None] == i[None, :]).astype(np.float32))


def _tril_ones(n):
    return jnp.asarray(np.tril(np.ones((n, n), np.float32)))


def _norm_proj_kernel(x_ref, g_ref, w_ref, zr_ref, zk_ref, zg_ref):
    x = x_ref[...]
    y = x * lax.rsqrt(jnp.mean(x * x, axis=-1, keepdims=True) + NORM_EPS) * g_ref[...]
    yb = _bf(y)
    zr_ref[...] = _dot(yb, w_ref[:, 0:RET_IN])
    zk_ref[...] = _dot(yb, w_ref[:, RET_IN:RET_IN + RWKV_IN])
    zg_ref[...] = _dot(yb, w_ref[:, RET_IN + RWKV_IN:D_INP])


def _norm_proj(x2, gain, w_in_p, tm=512):
    t = x2.shape[0]
    return pl.pallas_call(
        _norm_proj_kernel,
        grid=(t // tm,),
        in_specs=[pl.BlockSpec((tm, D_MODEL), lambda i: (i, 0)), _full((1, D_MODEL)), _full((D_MODEL, D_INP))],
        out_specs=[pl.BlockSpec((tm, RET_IN), lambda i: (i, 0)), pl.BlockSpec((tm, RWKV_IN), lambda i: (i, 0)),
                   pl.BlockSpec((tm, GLA_IN), lambda i: (i, 0))],
        out_shape=[jax.ShapeDtypeStruct((t, RET_IN), F32), jax.ShapeDtypeStruct((t, RWKV_IN), F32),
                   jax.ShapeDtypeStruct((t, GLA_IN), F32)],
        compiler_params=_params(("arbitrary",)),
        name="norm_proj",
    )(x2, gain, w_in_p)


def _ret_kernel(z_ref, cos_ref, sin_ref, dmat_ref, qdec_ref, kdec_ref, cdec_ref, ones_ref, o_ref, st_ref):
    c = RET_CHUNK

    @pl.when(pl.program_id(1) == 0)
    def _():
        st_ref[...] = jnp.zeros_like(st_ref)

    z = z_ref[0]
    q, k, v, g = (z[:, i * RET_W:(i + 1) * RET_W] for i in range(4))
    cos, sin = cos_ref[...], sin_ref[...]
    first_half = (_iota((c, RET_W), 1) & (RET_DK - 1)) < RET_DK // 2

    def rot(t):
        return jnp.where(first_half, -pltpu.roll(t, RET_W - RET_DK // 2, 1), pltpu.roll(t, RET_DK // 2, 1))

    q = q * cos + rot(q) * sin
    k = (k * cos + rot(k) * sin) * (RET_DK ** -0.5)
    qd = q * qdec_ref[...]
    kd = k * kdec_ref[...]
    head0 = _iota((c, LANES), 1) < RET_DK
    bd = _same_block((LANES, LANES), RET_DV, RET_DK)
    outs = []
    for p in range(RET_HEADS // 2):
        sl = slice(p * LANES, (p + 1) * LANES)
        qp, kp, vp = q[:, sl], _bf(k[:, sl]), v[:, sl]
        o = None
        for hh in range(2):
            mh = head0 if hh == 0 else jnp.logical_not(head0)
            s = _dot_nt(_bf(jnp.where(mh, qp, 0.0)), kp) * dmat_ref[2 * p + hh]
            oh = _dot(_bf(s), _bf(jnp.where(mh, vp, 0.0)))
            o = oh if o is None else o + oh
        st = st_ref[p]
        o = o + _dot_nt(_bf(qd[:, sl]), _bf(st))
        upd = _dot_tn(_bf(vp), _bf(kd[:, sl]))
        st_ref[p] = st * cdec_ref[:, sl] + jnp.where(bd, upd, 0.0)
        outs.append(o)
    o = jnp.concatenate(outs, axis=1)
    ms = _dot(o * o, ones_ref[...], HI) * (1.0 / RET_DV)
    o = o * lax.rsqrt(ms + HEAD_NORM_EPS)
    o_ref[0] = o * (g * jax.nn.sigmoid(g))


def _retention_consts(s):
    c = RET_CHUNK
    log_gamma = np.log1p(-np.exp(np.linspace(np.log(1.0 / 32.0), np.log(1.0 / 512.0), RET_HEADS)))
    idx = np.arange(c, dtype=np.float64)
    diff = idx[:, None] - idx[None, :]
    dmat = np.where(diff >= 0, np.exp(log_gamma[:, None, None] * np.where(diff >= 0, diff, 0.0)), 0.0)
    lg_lane = np.repeat(log_gamma, RET_DK)[None, :]
    qdec = np.exp(lg_lane * (idx[:, None] + 1.0))
    kdec = np.exp(lg_lane * (c - 1.0 - idx[:, None]))
    cdec = np.exp(lg_lane * c)
    half = RET_DK // 2
    inv = ROPE_BASE ** (-np.arange(half, dtype=np.float64) / half)
    ang = np.arange(s, dtype=np.float64)[:, None] * inv[None, :]
    cos = np.tile(np.cos(ang), (1, 2 * RET_HEADS))
    sin = np.tile(np.sin(ang), (1, 2 * RET_HEADS))
    f = lambda a: jnp.asarray(a.astype(np.float32))
    return f(cos), f(sin), f(dmat), f(qdec), f(kdec), f(cdec)


def _retention(z_ret):
    b, s, _ = z_ret.shape
    c = RET_CHUNK
    cos, sin, dmat, qdec, kdec, cdec = _retention_consts(s)
    return pl.pallas_call(
        _ret_kernel,
        grid=(b, s // c),
        in_specs=[pl.BlockSpec((1, c, RET_IN), lambda i, j: (i, j, 0)),
                  pl.BlockSpec((c, RET_W), lambda i, j: (j, 0)), pl.BlockSpec((c, RET_W), lambda i, j: (j, 0)),
                  _full((RET_HEADS, c, c)), _full((c, RET_W)), _full((c, RET_W)), _full((1, RET_W)),
                  _full((RET_W, RET_W))],
        out_specs=pl.BlockSpec((1, c, RET_W), lambda i, j: (i, j, 0)),
        out_shape=jax.ShapeDtypeStruct((b, s, RET_W), F32),
        scratch_shapes=[pltpu.VMEM((RET_HEADS // 2, LANES, LANES), F32)],
        compiler_params=_params(("arbitrary", "arbitrary")),
        name="retention",
    )(z_ret, cos, sin, dmat, qdec, kdec, cdec, _block_ones(RET_W, RET_DV))


def _rwkv_kernel(z_ref, mu_ref, wwa_ref, wb_ref, ab_ref, gup_ref, kk_ref, ka_ref, rk_ref, lnw_ref, lnb_ref,
                 ones_ref, ltri_ref, o_ref, st_ref, carry_ref):
    c = RWKV_CHUNK
    w_ = RWKV_W

    @pl.when(pl.program_id(1) == 0)
    def _():
        st_ref[...] = jnp.zeros_like(st_ref)
        carry_ref[...] = jnp.zeros_like(carry_ref)

    f = z_ref[0]
    f_prev = jnp.where(_iota((c, RWKV_IN), 0) == 0, carry_ref[...], pltpu.roll(f, 1, 0))
    carry_ref[...] = f[c - 1:c, :]
    f = f + (f_prev - f) * mu_ref[...]
    r, k, v = f[:, 0:w_], f[:, w_:2 * w_], f[:, 2 * w_:3 * w_]
    wa, gd = f[:, 3 * w_:3 * w_ + LANES], f[:, 3 * w_ + LANES:RWKV_IN]
    lane = _iota((c, LANES), 1)
    head0 = lane < RWKV_HEAD
    proj = _dot(jnp.where(lane < DECAY_LORA, jnp.tanh(wa), wa), wwa_ref[...], HI)
    wlog = -jax.nn.softplus(-(wb_ref[...] + proj[:, 0:w_])) - 0.5
    ld = -jnp.exp(wlog)
    a = jax.nn.sigmoid(ab_ref[...] + proj[:, w_:2 * w_])
    g = _dot(_bf(jax.nn.sigmoid(gd)), gup_ref[...])
    ones = ones_ref[...]
    kk = k * kk_ref[...]
    kk = kk * lax.rsqrt(jnp.maximum(_dot(kk * kk, ones, HI), 1e-24))
    k = k * (1.0 + (a - 1.0) * ka_ref[...])
    cum = _dot(ltri_ref[...], ld, HI)
    cl = cum[c - 1:c, :]
    e_neg = jnp.exp(-cum)
    e_rem = jnp.exp(cl - cum)
    b = kk * a
    a_t = -kk * jnp.exp(cum - ld)
    b_t, k_t, r_t = b * e_neg, k * e_neg, r * jnp.exp(cum)
    b_h, k_h = b * e_rem, k * e_rem
    p_c = jnp.exp(cl)

    n2 = 2 * c
    ri, ci = _iota((n2, n2), 0), _iota((n2, n2), 1)
    same_head = _same_block((n2, n2), c, c)
    strict = jnp.logical_and(same_head, ri > ci)
    incl = jnp.logical_and(same_head, ri >= ci)
    sub = _same_block((n2, n2), RWKV_SUB, RWKV_SUB)
    eye = (ri == ci).astype(F32)
    bd = _same_block((LANES, LANES), RWKV_HEAD, RWKV_HEAD)
    eye_l = _iota((LANES, LANES), 0) == _iota((LANES, LANES), 1)

    def stack(x):
        return jnp.concatenate([jnp.where(head0, x, 0.0), jnp.where(head0, 0.0, x)], axis=0)

    def fold(x):
        return x[:c] + x[c:]

    outs = []
    for p in range(RWKV_HEADS // 2):
        sl = slice(p * LANES, (p + 1) * LANES)
        a_s, r_s, b_s, k_s, v_s = stack(a_t[:, sl]), stack(r_t[:, sl]), stack(b_t[:, sl]), stack(k_t[:, sl]), stack(v[:, sl])
        big = _dot_nt(_bf(jnp.concatenate([a_s, r_s], axis=0)), _bf(jnp.concatenate([b_s, k_s], axis=0)))
        a_ab = jnp.where(strict, big[:n2, :n2], 0.0)
        a_ak = jnp.where(strict, big[:n2, n2:], 0.0)
        a_rb = jnp.where(incl, big[n2:, :n2], 0.0)
        a_rk = jnp.where(incl, big[n2:, n2:], 0.0)
        d1 = jnp.where(sub, a_ab, 0.0)
        lo = a_ab - d1
        d2 = _dot(d1, d1, HI)
        d4 = _dot(d2, d2, HI)
        d8 = _dot(d4, d4, HI)
        t_d = _dot(_dot(_dot(eye + d1, eye + d2, HI), eye + d4, HI), eye + d8, HI)
        e1 = _dot(t_d, lo, HI)
        e2 = _dot(e1, e1, HI)
        t_inv = _dot(_dot(eye + e1, eye + e2, HI), t_d, HI)
        akv = _dot(_bf(a_ak), _bf(v_s))
        wy = _dot(t_inv, jnp.concatenate([a_s, akv], axis=1), HI)
        qo = _dot(_bf(a_rb), _bf(wy))
        w_p, y_p = fold(wy[:, :LANES]), fold(wy[:, LANES:])
        q_p = fold(r_s + qo[:, :LANES])
        ol_p = fold(qo[:, LANES:] + _dot(_bf(a_rk), _bf(v_s)))
        gmat = jnp.where(bd, _dot_tn(_bf(w_p), _bf(b_h[:, sl])), 0.0)
        nmat = jnp.where(bd, _dot_tn(_bf(jnp.concatenate([y_p, v[:, sl]], axis=0)),
                                     _bf(jnp.concatenate([b_h[:, sl], k_h[:, sl]], axis=0))), 0.0)
        st = st_ref[p]
        stb = _bf(st)
        outs.append(_dot_nt(_bf(q_p), stb) + ol_p)
        st_ref[p] = st * p_c[:, sl] + _dot(stb, _bf(gmat)) + nmat
    o = jnp.concatenate(outs, axis=1)
    inv_n = 1.0 / RWKV_HEAD
    d = o - _dot(o, ones, HI) * inv_n
    var = _dot(d * d, ones, HI) * inv_n
    o = d * lax.rsqrt(var + RWKV_GN_EPS) * lnw_ref[...] + lnb_ref[...]
    o = o + _dot(r * k * rk_ref[...], ones, HI) * v
    o_ref[0] = o * g


def _rwkv(z_rwkv, pr):
    b, s, _ = z_rwkv.shape
    c = RWKV_CHUNK
    row = lambda n: _full((1, n))
    return pl.pallas_call(
        _rwkv_kernel,
        grid=(b, s // c),
        in_specs=[pl.BlockSpec((1, c, RWKV_IN), lambda i, j: (i, j, 0)), row(RWKV_IN), _full((LANES, 2 * RWKV_W)),
                  row(RWKV_W), row(RWKV_W), _full((GATE_LORA, RWKV_W)), row(RWKV_W), row(RWKV_W), row(RWKV_W),
                  row(RWKV_W), row(RWKV_W), _full((RWKV_W, RWKV_W)), _full((c, c))],
        out_specs=pl.BlockSpec((1, c, RWKV_W), lambda i, j: (i, j, 0)),
        out_shape=jax.ShapeDtypeStruct((b, s, RWKV_W), F32),
        scratch_shapes=[pltpu.VMEM((RWKV_HEADS // 2, LANES, LANES), F32), pltpu.VMEM((1, RWKV_IN), F32)],
        compiler_params=_params(("arbitrary", "arbitrary")),
        name="rwkv7",
    )(z_rwkv, pr["mu"], pr["wwa"], pr["w_bias"], pr["a_bias"], pr["g_up"], pr["k_k"], pr["k_a"], pr["r_k"],
      pr["ln_w"], pr["ln_b"], _block_ones(RWKV_W, RWKV_HEAD), _tril_ones(c))


def _gla_kernel(z_ref, gkup_ref, gkb_ref, nw_ref, ind_ref, ones_ref, ltri_ref, o_ref, st_ref, x_ref, gk_ref, acc_ref):
    n = GLA_SUB
    tb = GLA_BLOCK

    @pl.when(pl.program_id(1) == 0)
    def _():
        st_ref[...] = jnp.zeros_like(st_ref)

    gkd = z_ref[0, :, 2 * GLA_QW + GLA_VW:2 * GLA_QW + GLA_VW + LANES]
    gk_ref[...] = jax.nn.log_sigmoid(_dot(gkd, gkup_ref[...], HI) + gkb_ref[...]) * (1.0 / GLA_GATE_NORMALIZER)
    bd = _same_block((GLA_VW, GLA_QW), GLA_DVP, GLA_DKP)
    rowi = _iota((n, GLA_VW), 0)
    ltri = ltri_ref[...]
    ind = ind_ref[...]

    def sub_chunk(sc, carry):
        r0 = pl.multiple_of(sc * n, n)
        rows = pl.ds(r0, n)
        q = z_ref[0, rows, 0:GLA_QW] * (GLA_DK ** -0.5)
        k = z_ref[0, rows, GLA_QW:2 * GLA_QW]
        v = z_ref[0, rows, 2 * GLA_QW:2 * GLA_QW + GLA_VW]
        bc = _dot(ltri, gk_ref[rows, :], HI)
        bl = bc[n - 1:n, :]
        st = st_ref[...]
        o = _dot_nt(_bf(q * jnp.exp(bc)), _bf(st))
        for j in range(n):
            x_ref[j * n:(j + 1) * n, :] = q * jnp.exp(jnp.minimum(bc - bc[j:j + 1, :], 0.0)) * k[j:j + 1, :]
        x = x_ref[...]
        xh = _bf(x)
        xl = _bf(x - xh.astype(F32))
        e = _dot(xh, ind) + _dot(xl, ind)
        for j in range(n):
            o = o + jnp.where(rowi >= j, e[j * n:(j + 1) * n, :], 0.0) * v[j:j + 1, :]
        upd = _dot_tn(_bf(v), _bf(k * jnp.exp(bl - bc)))
        st_ref[...] = st * jnp.exp(bl) + jnp.where(bd, upd, 0.0)
        acc_ref[rows, :] = o
        return carry

    lax.fori_loop(0, tb // n, sub_chunk, 0)
    o = acc_ref[...]
    ms = _dot(o * o, ones_ref[...], HI) * (1.0 / GLA_DV)
    g = z_ref[0, :, 2 * GLA_QW + GLA_VW + LANES:GLA_IN]
    o_ref[0] = o * lax.rsqrt(ms + HEAD_NORM_EPS) * nw_ref[...] * (g * jax.nn.sigmoid(g))


def _gla(z_gla, pr):
    b, s, _ = z_gla.shape
    tb = GLA_BLOCK
    hq = np.arange(GLA_QW) // GLA_DKP
    hv = np.arange(GLA_VW) // GLA_DVP
    ind = jnp.asarray((hq[:, None] == hv[None, :]).astype(np.float32), dtype=BF16)
    return pl.pallas_call(
        _gla_kernel,
        grid=(b, s // tb),
        in_specs=[pl.BlockSpec((1, tb, GLA_IN), lambda i, j: (i, j, 0)), _full((LANES, GLA_QW)), _full((1, GLA_QW)),
                  _full((1, GLA_VW)), _full((GLA_QW, GLA_VW)), _full((GLA_VW, GLA_VW)), _full((GLA_SUB, GLA_SUB))],
        out_specs=pl.BlockSpec((1, tb, GLA_VW), lambda i, j: (i, j, 0)),
        out_shape=jax.ShapeDtypeStruct((b, s, GLA_VW), F32),
        scratch_shapes=[pltpu.VMEM((GLA_VW, GLA_QW), F32), pltpu.VMEM((GLA_SUB * GLA_SUB, GLA_QW), F32),
                        pltpu.VMEM((tb, GLA_QW), F32), pltpu.VMEM((tb, GLA_VW), F32)],
        compiler_params=_params(("arbitrary", "arbitrary")),
        name="gla",
    )(z_gla, pr["gk_up"], pr["gk_bias"], pr["norm_w"], ind, _block_ones(GLA_VW, GLA_DVP), _tril_ones(GLA_SUB))


def _out_proj_kernel(x_ref, a_ref, b_ref, c_ref, w_ref, g_ref, x1_ref, hnt_ref):
    acc = _dot(_bf(a_ref[...]), w_ref[0:RET_W, :])
    acc += _dot(_bf(b_ref[...]), w_ref[RET_W:RET_W + RWKV_W, :])
    acc += _dot(_bf(c_ref[...]), w_ref[RET_W + RWKV_W:D_MIXP, :])
    x1 = x_ref[...] + acc
    x1_ref[...] = x1
    hn = x1 * lax.rsqrt(jnp.mean(x1 * x1, axis=-1, keepdims=True) + NORM_EPS) * g_ref[...]
    hnt_ref[...] = _bf(hn.T)


def _out_proj(x2, o_ret, o_rwkv, o_gla, w_out_p, gain, tm=512):
    t = x2.shape[0]
    blk = lambda n: pl.BlockSpec((tm, n), lambda i: (i, 0))
    return pl.pallas_call(
        _out_proj_kernel,
        grid=(t // tm,),
        in_specs=[blk(D_MODEL), blk(RET_W), blk(RWKV_W), blk(GLA_VW), _full((D_MIXP, D_MODEL)), _full((1, D_MODEL))],
        out_specs=[blk(D_MODEL), pl.BlockSpec((D_MODEL, tm), lambda i: (0, i))],
        out_shape=[jax.ShapeDtypeStruct((t, D_MODEL), F32), jax.ShapeDtypeStruct((D_MODEL, t), BF16)],
        compiler_params=_params(("arbitrary",)),
        name="out_proj",
    )(x2, o_ret, o_rwkv, o_gla, w_out_p, gain)


def _staircase_pairs():
    k = PEER_TOPK
    return [(a, b) for a in range(k) for b in range(k) if (a + 1) * (b + 1) <= k]


def _tree_max(xs):
    xs = list(xs)
    while len(xs) > 1:
        nxt = [jnp.maximum(xs[2 * i], xs[2 * i + 1]) for i in range(len(xs) // 2)]
        if len(xs) % 2:
            nxt.append(xs[-1])
        xs = nxt
    return xs[0]


def _route_kernel(hnt_ref, wq_ref, keys_ref, n1_ref, e1_ref, r2_ref, e2_ref, work_ref, s_ref, vals_ref, rank_ref):
    nh, nk, k_top = PEER_HEADS, PEER_NKEYS, PEER_TOPK
    qt = _bf(_dot(wq_ref[...], hnt_ref[...]))
    for p in range(2):
        for h in range(nh):
            r0 = (p * nh + h) * PEER_QHALF
            s = _dot(keys_ref[p, h], qt[r0:r0 + PEER_QHALF, :])
            work_ref[p, h] = s
            s_ref[p, h] = s
    rank_ref[...] = jnp.full(rank_ref.shape, float(k_top), F32)
    groups = nk // SUBLANES

    def extract(r, carry):
        rf = r.astype(F32)
        for p in range(2):
            for h in range(nh):
                tiles = [work_ref[p, h, i * SUBLANES:(i + 1) * SUBLANES, :] for i in range(groups)]
                m = _tree_max(tiles)
                for sh in (4, 2, 1):
                    m = jnp.maximum(m, pltpu.roll(m, sh, 0))
                vals_ref[p, r, pl.ds(h, 1), :] = m[0:1, :]
                for i in range(groups):
                    rows = slice(i * SUBLANES, (i + 1) * SUBLANES)
                    is_max = tiles[i] == m
                    work_ref[p, h, rows, :] = jnp.where(is_max, -jnp.inf, tiles[i])
                    rank_ref[p, h, rows, :] = jnp.where(is_max, rf, rank_ref[p, h, rows, :])
        return carry

    lax.fori_loop(0, k_top, extract, 0)

    v1 = [vals_ref[0, r] for r in range(k_top)]
    v2 = [vals_ref[1, r] for r in range(k_top)]
    cand = {(a, b): v1[a] + v2[b] for a, b in _staircase_pairs()}
    work = list(cand.values())
    tau = None
    for it in range(k_top):
        tau = _tree_max(work)
        if it + 1 < k_top:
            work = [jnp.where(w == tau, -jnp.inf, w) for w in work]
    top = cand[(0, 0)]
    z = None
    for c in cand.values():
        zi = jnp.where(c >= tau, jnp.exp(c - top), 0.0)
        z = zi if z is None else z + zi
    scale2 = 0.5 / z
    n_of_rank = []
    for a in range(k_top):
        cnt = None
        for b in range(k_top // (a + 1)):
            ge = (cand[(a, b)] >= tau).astype(F32)
            cnt = ge if cnt is None else cnt + ge
        n_of_rank.append(cnt)
    m1, m2 = v1[0], v2[0]
    for h in range(nh):
        r1 = rank_ref[0, h]
        n1 = jnp.zeros_like(r1)
        for a in range(k_top):
            n1 = jnp.where(r1 == float(a), n_of_rank[a][h:h + 1, :], n1)
        n1_ref[h] = n1
        e1_ref[h] = jnp.exp(s_ref[0, h] - m1[h:h + 1, :])
        r2_ref[h] = pltpu.bitcast(_bf(rank_ref[1, h]), jnp.uint32)
        e2_ref[h] = pltpu.bitcast(_bf(jnp.exp(s_ref[1, h] - m2[h:h + 1, :]) * scale2[h:h + 1, :]), jnp.uint32)


def _route(hnt, wq_t, keys, tb=256):
    t = hnt.shape[1]
    nh, nk = PEER_HEADS, PEER_NKEYS
    out = pl.BlockSpec((nh, nk, tb), lambda i: (0, 0, i))
    packed = pl.BlockSpec((nh, nk // 2, tb), lambda i: (0, 0, i))
    return pl.pallas_call(
        _route_kernel,
        grid=(t // tb,),
        in_specs=[pl.BlockSpec((D_MODEL, tb), lambda i: (0, i)), _full((2 * nh * PEER_QHALF, D_MODEL)),
                  _full((2, nh, nk, PEER_QHALF))],
        out_specs=[out, out, packed, packed],
        out_shape=[jax.ShapeDtypeStruct((nh, nk, t), F32)] * 2 + [jax.ShapeDtypeStruct((nh, nk // 2, t), jnp.uint32)] * 2,
        scratch_shapes=[pltpu.VMEM((2, nh, nk, tb), F32), pltpu.VMEM((2, nh, nk, tb), F32),
                        pltpu.VMEM((2, PEER_TOPK, nh, tb), F32), pltpu.VMEM((2, nh, nk, tb), F32)],
        compiler_params=_params(("arbitrary",)),
        name="peer_route",
    )(hnt, wq_t, keys)


EXPERT_SLABS = 8
EXPERT_PAIR = 2
PIPE_LAG = 2


def _expert_kernel(hnt_ref, u_ref, vt_ref, n1_ref, e1_ref, r2_ref, e2_ref, x1_ref, gf_ref, o_ref, acc_ref, ht0, ht1,
                   act0, act1, rn_ref, re_ref, *, n_blocks, final_norm):
    s = pl.program_id(0)
    n_pairs = pl.num_programs(0) - PIPE_LAG
    nk = PEER_NKEYS
    tb = hnt_ref.shape[1]
    j2 = lax.rem(jnp.clip(s - 1, 0, n_pairs - 1), n_blocks)
    j3 = lax.rem(jnp.clip(s - PIPE_LAG, 0, n_pairs - 1), n_blocks)
    live = jnp.logical_and(s >= 1, s <= n_pairs).astype(F32)

    @pl.when(s == 0)
    def _():
        for r in (ht0, ht1, act0, act1):
            r[...] = jnp.zeros_like(r)

    @pl.when(j3 == 0)
    def _():
        acc_ref[...] = jnp.zeros_like(acc_ref)

    for h in range(PEER_HEADS):
        for k in range(EXPERT_SLABS):
            r = h * EXPERT_SLABS + k
            rn_ref[r:r + 1, :] = n1_ref[h, pl.ds(j2 * EXPERT_SLABS + k, 1), :]
            re_ref[r:r + 1, :] = e1_ref[h, pl.ds(j2 * EXPERT_SLABS + k, 1), :] * live

    def stages(ht_w, ht_r, act_w, act_r):
        ht_w[...] = _dot(u_ref[...], hnt_ref[...])
        for st in range(tb // LANES):
            cols = slice(st * LANES, (st + 1) * LANES)
            for g0 in range(0, EXPERT_SLABS, EXPERT_PAIR):
                gates = [None] * EXPERT_PAIR
                for h in range(PEER_HEADS):
                    r2 = pltpu.bitcast(r2_ref[h, :, cols], BF16)
                    e2 = pltpu.bitcast(e2_ref[h, :, cols], BF16)
                    for kk in range(EXPERT_PAIR):
                        r = h * EXPERT_SLABS + g0 + kk
                        n1 = _bf(rn_ref[r:r + 1, cols])
                        e1 = _bf(re_ref[r:r + 1, cols])
                        gh = jnp.where(r2 < n1, e2 * e1, jnp.zeros_like(e2))
                        gates[kk] = gh if gates[kk] is None else gates[kk] + gh
                for kk in range(EXPERT_PAIR):
                    rows = slice((g0 + kk) * nk, (g0 + kk + 1) * nk)
                    hk = ht_r[rows, cols]
                    act_w[rows, cols] = _bf(hk * (1.0 + lax.erf(hk * (2.0 ** -0.5)))) * gates[kk]
        acc_ref[...] += _dot(vt_ref[...], act_r[...])

    parity = lax.rem(s, 2)

    @pl.when(parity == 0)
    def _():
        stages(ht0, ht1, act1, act0)

    @pl.when(parity == 1)
    def _():
        stages(ht1, ht0, act0, act1)

    @pl.when(jnp.logical_and(j3 == n_blocks - 1, s >= PIPE_LAG))
    def _():
        y = acc_ref[...].T + x1_ref[...]
        if final_norm:
            y = y * lax.rsqrt(jnp.mean(y * y, axis=-1, keepdims=True) + NORM_EPS) * gf_ref[...]
        o_ref[...] = y


def _experts(hnt, u_b, vt_b, n1, e1, r2, e2, x1, gain_f, final_norm, tb=512):
    t = hnt.shape[1]
    nh, nk = PEER_HEADS, PEER_NKEYS
    eb = EXPERT_SLABS * nk
    n_blocks = PEER_NEXPERTS // eb
    n_pairs = (t // tb) * n_blocks

    def pair(lag):
        def f(s):
            p = jnp.clip(s - lag, 0, n_pairs - 1)
            return p // n_blocks, lax.rem(p, n_blocks)
        return f

    tok = lambda lag: (lambda s: pair(lag)(s)[0])
    blk = lambda lag: (lambda s: pair(lag)(s)[1])
    routed = lambda rows: pl.BlockSpec((nh, rows, tb), lambda s: (0, 0, tok(1)(s)))
    nrow = PEER_HEADS * EXPERT_SLABS
    return pl.pallas_call(
        functools.partial(_expert_kernel, n_blocks=n_blocks, final_norm=final_norm),
        grid=(n_pairs + PIPE_LAG,),
        in_specs=[pl.BlockSpec((D_MODEL, tb), lambda s: (0, tok(0)(s))),
                  pl.BlockSpec((eb, D_MODEL), lambda s: (blk(0)(s), 0)),
                  pl.BlockSpec((D_MODEL, eb), lambda s: (0, blk(PIPE_LAG)(s))),
                  routed(nk), routed(nk), routed(nk // 2), routed(nk // 2),
                  pl.BlockSpec((tb, D_MODEL), lambda s: (tok(PIPE_LAG)(s), 0)), _full((1, D_MODEL))],
        out_specs=pl.BlockSpec((tb, D_MODEL), lambda s: (tok(PIPE_LAG)(s), 0)),
        out_shape=jax.ShapeDtypeStruct((t, D_MODEL), F32),
        scratch_shapes=[pltpu.VMEM((D_MODEL, tb), F32), pltpu.VMEM((eb, tb), F32), pltpu.VMEM((eb, tb), F32),
                        pltpu.VMEM((eb, tb), BF16), pltpu.VMEM((eb, tb), BF16),
                        pltpu.VMEM((nrow, tb), F32), pltpu.VMEM((nrow, tb), F32)],
        compiler_params=_params(("arbitrary",)),
        name="peer_experts",
    )(hnt, u_b, vt_b, n1, e1, r2, e2, x1, gain_f)


def _pad_heads(w, heads, d, dp):
    lead = w.shape[:-1]
    w = w.reshape(*lead, heads, d)
    w = jnp.pad(w, [(0, 0)] * len(lead) + [(0, 0), (0, dp - d)])
    return w.reshape(*lead, heads * dp)


def _layer_params(l, w_in, w_out, rwkv_mu, rwkv_w_up, rwkv_w_bias, rwkv_a_up, rwkv_a_bias, rwkv_g_up, rwkv_k_k,
                  rwkv_k_a, rwkv_r_k, rwkv_ln_w, rwkv_ln_b, gla_gk_up, gla_gk_bias, gla_norm_w, peer_w_q,
                  peer_sub_keys, peer_u, peer_v):
    wi = w_in[l]
    g0 = RET_IN + RWKV_IN
    qk, vw = GLA_HEADS * GLA_DK, GLA_HEADS * GLA_DV
    gq = _pad_heads(wi[:, g0:g0 + qk], GLA_HEADS, GLA_DK, GLA_DKP)
    gkk = _pad_heads(wi[:, g0 + qk:g0 + 2 * qk], GLA_HEADS, GLA_DK, GLA_DKP)
    gv = _pad_heads(wi[:, g0 + 2 * qk:g0 + 2 * qk + vw], GLA_HEADS, GLA_DV, GLA_DVP)
    ggk = jnp.pad(wi[:, g0 + 2 * qk + vw:g0 + 2 * qk + vw + GLA_GATE_LORA], ((0, 0), (0, LANES - GLA_GATE_LORA)))
    gg = _pad_heads(wi[:, g0 + 2 * qk + vw + GLA_GATE_LORA:], GLA_HEADS, GLA_DV, GLA_DVP)
    w_in_p = _bf(jnp.concatenate([wi[:, :g0], gq, gkk, gv, ggk, gg], axis=1))
    wo = w_out[l]
    m0 = RET_W + RWKV_W
    wo_gla = jnp.pad(wo[m0:].reshape(GLA_HEADS, GLA_DV, D_MODEL), ((0, 0), (0, GLA_DVP - GLA_DV), (0, 0)))
    w_out_p = _bf(jnp.concatenate([wo[:m0], wo_gla.reshape(GLA_VW, D_MODEL)], axis=0))
    zeros = jnp.zeros((DECAY_LORA, RWKV_W), F32)
    rw = dict(
        mu=rwkv_mu[l][None, :],
        wwa=jnp.concatenate([jnp.concatenate([rwkv_w_up[l], zeros], axis=1),
                             jnp.concatenate([zeros, rwkv_a_up[l]], axis=1)], axis=0),
        w_bias=rwkv_w_bias[l][None, :], a_bias=rwkv_a_bias[l][None, :], g_up=_bf(rwkv_g_up[l]),
        k_k=rwkv_k_k[l][None, :], k_a=rwkv_k_a[l][None, :], r_k=rwkv_r_k[l].reshape(1, RWKV_W),
        ln_w=rwkv_ln_w[l][None, :], ln_b=rwkv_ln_b[l][None, :])
    gl = dict(
        gk_up=jnp.pad(_pad_heads(gla_gk_up[l], GLA_HEADS, GLA_DK, GLA_DKP), ((0, LANES - GLA_GATE_LORA), (0, 0))),
        gk_bias=_pad_heads(gla_gk_bias[l][None, :], GLA_HEADS, GLA_DK, GLA_DKP),
        norm_w=_pad_heads(gla_norm_w[l][None, :], GLA_HEADS, GLA_DV, GLA_DVP))
    wq = peer_w_q[l].reshape(D_MODEL, PEER_HEADS, 2, PEER_QHALF)
    wq_t = _bf(jnp.transpose(wq, (2, 1, 3, 0)).reshape(2 * PEER_HEADS * PEER_QHALF, D_MODEL))
    keys = _bf(jnp.transpose(peer_sub_keys[l], (1, 0, 2, 3)))
    return dict(w_in=w_in_p, w_out=w_out_p, rwkv=rw, gla=gl, wq_t=wq_t, keys=keys, u=_bf(peer_u[l]),
                vt=_bf(peer_v[l].T))


def _layer(x2, b, s, pr, gain_mix, gain_ffn, gain_final, final_norm):
    z_ret, z_rwkv, z_gla = _norm_proj(x2, gain_mix, pr["w_in"])
    o_ret = _retention(z_ret.reshape(b, s, RET_IN)).reshape(b * s, RET_W)
    o_rwkv = _rwkv(z_rwkv.reshape(b, s, RWKV_IN), pr["rwkv"]).reshape(b * s, RWKV_W)
    o_gla = _gla(z_gla.reshape(b, s, GLA_IN), pr["gla"]).reshape(b * s, GLA_VW)
    x1, hnt = _out_proj(x2, o_ret, o_rwkv, o_gla, pr["w_out"], gain_ffn)
    n1, e1, r2, e2 = _route(hnt, pr["wq_t"], pr["keys"])
    return _experts(hnt, pr["u"], pr["vt"], n1, e1, r2, e2, x1, gain_final, final_norm)


def kernel(x, norm_mix, norm_ffn, norm_final, w_in, w_out, rwkv_mu, rwkv_w_up, rwkv_w_bias, rwkv_a_up, rwkv_a_bias, rwkv_g_up, rwkv_k_k, rwkv_k_a, rwkv_r_k, rwkv_ln_w, rwkv_ln_b, gla_gk_up, gla_gk_bias, gla_norm_w, peer_w_q, peer_sub_keys, peer_u, peer_v):
    b, s, d = x.shape
    x2 = x.reshape(b * s, d)
    gain_final = norm_final[None, :]
    for l in range(DEPTH):
        pr = _layer_params(l, w_in, w_out, rwkv_mu, rwkv_w_up, rwkv_w_bias, rwkv_a_up, rwkv_a_bias, rwkv_g_up,
                           rwkv_k_k, rwkv_k_a, rwkv_r_k, rwkv_ln_w, rwkv_ln_b, gla_gk_up, gla_gk_bias, gla_norm_w,
                           peer_w_q, peer_sub_keys, peer_u, peer_v)
        x2 = _layer(x2, b, s, pr, norm_mix[l][None, :], norm_ffn[l][None, :], gain_final, l == DEPTH - 1)
    return x2.reshape(b, s, d)
```

```python
import functools

import numpy as np
import jax
import jax.numpy as jnp
from jax import lax
from jax.experimental import pallas as pl
from jax.experimental.pallas import tpu as pltpu

F32 = jnp.float32
BF16 = jnp.bfloat16
HI = lax.Precision.HIGHEST

D_MODEL = 1024
DEPTH = 2
NORM_EPS = 1e-6
HEAD_NORM_EPS = 1e-5
RET_HEADS, RET_DK, RET_DV, ROPE_BASE = 4, 64, 64, 10000.0
RET_W = RET_HEADS * RET_DK
RET_CHUNK = 128
RWKV_HEADS, RWKV_HEAD = 6, 64
RWKV_W = RWKV_HEADS * RWKV_HEAD
DECAY_LORA, AAA_LORA, GATE_LORA = 64, 64, 128
RWKV_GN_EPS = 64e-5
RWKV_CHUNK = 64
RWKV_SUB = 16
RWKV_IN = 3 * RWKV_W + DECAY_LORA + AAA_LORA + GATE_LORA
GLA_HEADS, GLA_DK, GLA_DV, GLA_GATE_LORA = 4, 48, 96, 16
GLA_GATE_NORMALIZER = 16.0
GLA_DKP, GLA_DVP = 64, 128
GLA_QW = GLA_HEADS * GLA_DKP
GLA_VW = GLA_HEADS * GLA_DVP
GLA_IN = 2 * GLA_QW + GLA_VW + 128 + GLA_VW
GLA_SUB = 16
GLA_BLOCK = 128
RET_IN = 4 * RET_W
D_INP = RET_IN + RWKV_IN + GLA_IN
D_MIXP = RET_W + RWKV_W + GLA_VW
PEER_HEADS, PEER_NKEYS, PEER_QHALF, PEER_TOPK = 8, 128, 128, 16
PEER_NEXPERTS = PEER_NKEYS * PEER_NKEYS
LANES = 128
SUBLANES = 8
VMEM_LIMIT = 56 * 1024 * 1024


def _params(sem):
    return pltpu.CompilerParams(dimension_semantics=sem, vmem_limit_bytes=VMEM_LIMIT)


def _dot(a, b, prec=None):
    return jnp.dot(a, b, precision=prec, preferred_element_type=F32)


def _dot_nt(a, b, prec=None):
    return lax.dot_general(a, b, (((1,), (1,)), ((), ())), precision=prec, preferred_element_type=F32)


def _dot_tn(a, b, prec=None):
    return lax.dot_general(a, b, (((0,), (0,)), ((), ())), precision=prec, preferred_element_type=F32)


def _bf(x):
    return x.astype(BF16)


def _iota(shape, dim):
    return lax.broadcasted_iota(jnp.int32, shape, dim)


def _same_block(shape, rblk, cblk):
    r = _iota(shape, 0) >> (rblk.bit_length() - 1)
    c = _iota(shape, 1) >> (cblk.bit_length() - 1)
    return r == c


def _full(shape):
    n = len(shape)
    return pl.BlockSpec(shape, lambda *_: (0,) * n)


def _block_ones(n, blk):
    i = np.arange(n) // blk
    return jnp.asarray((i[:, None] == i[None, :]).astype(np.float32))


def _tril_ones(n):
    return jnp.asarray(np.tril(np.ones((n, n), np.float32)))


def _norm_proj_kernel(x_ref, g_ref, w_ref, zr_ref, zk_ref, zg_ref):
    x = x_ref[...]
    y = x * lax.rsqrt(jnp.mean(x * x, axis=-1, keepdims=True) + NORM_EPS) * g_ref[...]
    yb = _bf(y)
    zr_ref[...] = _dot(yb, w_ref[:, 0:RET_IN])
    zk_ref[...] = _dot(yb, w_ref[:, RET_IN:RET_IN + RWKV_IN])
    zg_ref[...] = _dot(yb, w_ref[:, RET_IN + RWKV_IN:D_INP])


def _norm_proj(x2, gain, w_in_p, tm=512):
    t = x2.shape[0]
    return pl.pallas_call(
        _norm_proj_kernel,
        grid=(t // tm,),
        in_specs=[pl.BlockSpec((tm, D_MODEL), lambda i: (i, 0)), _full((1, D_MODEL)), _full((D_MODEL, D_INP))],
        out_specs=[pl.BlockSpec((tm, RET_IN), lambda i: (i, 0)), pl.BlockSpec((tm, RWKV_IN), lambda i: (i, 0)),
                   pl.BlockSpec((tm, GLA_IN), lambda i: (i, 0))],
        out_shape=[jax.ShapeDtypeStruct((t, RET_IN), F32), jax.ShapeDtypeStruct((t, RWKV_IN), F32),
                   jax.ShapeDtypeStruct((t, GLA_IN), F32)],
        compiler_params=_params(("arbitrary",)),
        name="norm_proj",
    )(x2, gain, w_in_p)


def _ret_kernel(z_ref, cos_ref, sin_ref, dmat_ref, qdec_ref, kdec_ref, cdec_ref, ones_ref, o_ref, st_ref):
    c = RET_CHUNK

    @pl.when(pl.program_id(1) == 0)
    def _():
        st_ref[...] = jnp.zeros_like(st_ref)

    z = z_ref[0]
    q, k, v, g = (z[:, i * RET_W:(i + 1) * RET_W] for i in range(4))
    cos, sin = cos_ref[...], sin_ref[...]
    first_half = (_iota((c, RET_W), 1) & (RET_DK - 1)) < RET_DK // 2

    def rot(t):
        return jnp.where(first_half, -pltpu.roll(t, RET_W - RET_DK // 2, 1), pltpu.roll(t, RET_DK // 2, 1))

    q = q * cos + rot(q) * sin
    k = (k * cos + rot(k) * sin) * (RET_DK ** -0.5)
    qd = q * qdec_ref[...]
    kd = k * kdec_ref[...]
    head0 = _iota((c, LANES), 1) < RET_DK
    bd = _same_block((LANES, LANES), RET_DV, RET_DK)
    outs = []
    for p in range(RET_HEADS // 2):
        sl = slice(p * LANES, (p + 1) * LANES)
        qp, kp, vp = q[:, sl], _bf(k[:, sl]), v[:, sl]
        o = None
        for hh in range(2):
            mh = head0 if hh == 0 else jnp.logical_not(head0)
            s = _dot_nt(_bf(jnp.where(mh, qp, 0.0)), kp) * dmat_ref[2 * p + hh]
            oh = _dot(_bf(s), _bf(jnp.where(mh, vp, 0.0)))
            o = oh if o is None else o + oh
        st = st_ref[p]
        o = o + _dot_nt(_bf(qd[:, sl]), _bf(st))
        upd = _dot_tn(_bf(vp), _bf(kd[:, sl]))
        st_ref[p] = st * cdec_ref[:, sl] + jnp.where(bd, upd, 0.0)
        outs.append(o)
    o = jnp.concatenate(outs, axis=1)
    ms = _dot(o * o, ones_ref[...], HI) * (1.0 / RET_DV)
    o = o * lax.rsqrt(ms + HEAD_NORM_EPS)
    o_ref[0] = o * (g * jax.nn.sigmoid(g))


def _retention_consts(s):
    c = RET_CHUNK
    log_gamma = np.log1p(-np.exp(np.linspace(np.log(1.0 / 32.0), np.log(1.0 / 512.0), RET_HEADS)))
    idx = np.arange(c, dtype=np.float64)
    diff = idx[:, None] - idx[None, :]
    dmat = np.where(diff >= 0, np.exp(log_gamma[:, None, None] * np.where(diff >= 0, diff, 0.0)), 0.0)
    lg_lane = np.repeat(log_gamma, RET_DK)[None, :]
    qdec = np.exp(lg_lane * (idx[:, None] + 1.0))
    kdec = np.exp(lg_lane * (c - 1.0 - idx[:, None]))
    cdec = np.exp(lg_lane * c)
    half = RET_DK // 2
    inv = ROPE_BASE ** (-np.arange(half, dtype=np.float64) / half)
    ang = np.arange(s, dtype=np.float64)[:, None] * inv[None, :]
    cos = np.tile(np.cos(ang), (1, 2 * RET_HEADS))
    sin = np.tile(np.sin(ang), (1, 2 * RET_HEADS))
    f = lambda a: jnp.asarray(a.astype(np.float32))
    return f(cos), f(sin), f(dmat), f(qdec), f(kdec), f(cdec)


def _retention(z_ret):
    b, s, _ = z_ret.shape
    c = RET_CHUNK
    cos, sin, dmat, qdec, kdec, cdec = _retention_consts(s)
    return pl.pallas_call(
        _ret_kernel,
        grid=(b, s // c),
        in_specs=[pl.BlockSpec((1, c, RET_IN), lambda i, j: (i, j, 0)),
                  pl.BlockSpec((c, RET_W), lambda i, j: (j, 0)), pl.BlockSpec((c, RET_W), lambda i, j: (j, 0)),
                  _full((RET_HEADS, c, c)), _full((c, RET_W)), _full((c, RET_W)), _full((1, RET_W)),
                  _full((RET_W, RET_W))],
        out_specs=pl.BlockSpec((1, c, RET_W), lambda i, j: (i, j, 0)),
        out_shape=jax.ShapeDtypeStruct((b, s, RET_W), F32),
        scratch_shapes=[pltpu.VMEM((RET_HEADS // 2, LANES, LANES), F32)],
        compiler_params=_params(("arbitrary", "arbitrary")),
        name="retention",
    )(z_ret, cos, sin, dmat, qdec, kdec, cdec, _block_ones(RET_W, RET_DV))


RWKV_BLOCK = 256
INV_PASSES = 3


def _split(x, parts):
    out = []
    for _ in range(parts - 1):
        h = _bf(x)
        out.append(h)
        x = x - h.astype(F32)
    out.append(_bf(x))
    return out


def _dot_exact_lhs(a_bf, x, parts):
    acc = None
    for p in _split(x, parts):
        t = _dot(a_bf, p)
        acc = t if acc is None else acc + t
    return acc


def _dot_exact_rhs(x, b_bf, parts):
    acc = None
    for p in _split(x, parts):
        t = _dot(p, b_bf)
        acc = t if acc is None else acc + t
    return acc


def _dot_f32(a, b, passes):
    if passes == 1:
        return _dot(_bf(a), _bf(b))
    if passes == 3:
        ah, al = _split(a, 2)
        bh, bl = _split(b, 2)
        return _dot(ah, bh) + (_dot(ah, bl) + _dot(al, bh))
    return _dot(a, b, HI)


def _rwkv_kernel(z_ref, mu_ref, wwa_ref, wb_ref, ab_ref, gup_ref, kk_ref, ka_ref, rk_ref, lnw_ref, lnb_ref,
                 ones_ref, ltri_ref, o_ref, st_ref, carry_ref):
    c = RWKV_CHUNK
    tb = RWKV_BLOCK
    w_ = RWKV_W

    @pl.when(pl.program_id(1) == 0)
    def _():
        st_ref[...] = jnp.zeros_like(st_ref)
        carry_ref[...] = jnp.zeros_like(carry_ref)

    f = z_ref[0]
    f_prev = jnp.where(_iota((tb, RWKV_IN), 0) == 0, carry_ref[...], pltpu.roll(f, 1, 0))
    carry_ref[...] = f[tb - 1:tb, :]
    f = f + (f_prev - f) * mu_ref[...]
    r, k, v = f[:, 0:w_], f[:, w_:2 * w_], f[:, 2 * w_:3 * w_]
    wa, gd = f[:, 3 * w_:3 * w_ + LANES], f[:, 3 * w_ + LANES:RWKV_IN]
    lane = _iota((tb, LANES), 1)
    proj = _dot_f32(jnp.where(lane < DECAY_LORA, jnp.tanh(wa), wa), wwa_ref[...], 3)
    wlog = -jax.nn.softplus(-(wb_ref[...] + proj[:, 0:w_])) - 0.5
    ld = -jnp.exp(wlog)
    a = jax.nn.sigmoid(ab_ref[...] + proj[:, w_:2 * w_])
    g = _dot(_bf(jax.nn.sigmoid(gd)), gup_ref[...])
    ones = ones_ref[...]
    kk = k * kk_ref[...]
    kk = kk * lax.rsqrt(jnp.maximum(_dot_exact_rhs(kk * kk, ones, 2), 1e-24))
    k = k * (1.0 + (a - 1.0) * ka_ref[...])
    cum = _dot_exact_lhs(ltri_ref[...], ld, 3)
    e_neg = jnp.exp(-cum)
    b = kk * a
    a_t_all = -kk * jnp.exp(cum - ld)
    b_t_all, k_t_all, r_t_all = b * e_neg, k * e_neg, r * jnp.exp(cum)

    n2 = 2 * c
    ri, ci = _iota((n2, n2), 0), _iota((n2, n2), 1)
    same_head = _same_block((n2, n2), c, c)
    strict = jnp.logical_and(same_head, ri > ci)
    incl = jnp.logical_and(same_head, ri >= ci)
    sub = _same_block((n2, n2), RWKV_SUB, RWKV_SUB)
    eye = (ri == ci).astype(F32)
    bd = _same_block((LANES, LANES), RWKV_HEAD, RWKV_HEAD)
    head0 = _iota((c, LANES), 1) < RWKV_HEAD
    mm = functools.partial(_dot_f32, passes=INV_PASSES)
    mm1 = functools.partial(_dot_f32, passes=1)

    def stack(x):
        return jnp.concatenate([jnp.where(head0, x, 0.0), jnp.where(head0, 0.0, x)], axis=0)

    def fold(x):
        return x[:c] + x[c:]

    units = [(ch, p) for ch in range(tb // c) for p in range(RWKV_HEADS // 2)]
    each = lambda fn, *lists: [fn(*args) for args in zip(*lists)]
    rows_of = lambda ch: slice(ch * c, (ch + 1) * c)
    lanes_of = lambda p: slice(p * LANES, (p + 1) * LANES)
    tile = lambda t: [stack(t[rows_of(ch), lanes_of(p)]) for ch, p in units]
    a_s, r_s, b_s, k_s, v_s = tile(a_t_all), tile(r_t_all), tile(b_t_all), tile(k_t_all), tile(v)
    cl = [cum[(ch + 1) * c - 1:(ch + 1) * c, lanes_of(p)] for ch, p in units]
    e_rem = [jnp.exp(cl_u - cum[rows_of(ch), lanes_of(p)]) for cl_u, (ch, p) in zip(cl, units)]
    b_h = [b[rows_of(ch), lanes_of(p)] * e for e, (ch, p) in zip(e_rem, units)]
    k_h = [k[rows_of(ch), lanes_of(p)] * e for e, (ch, p) in zip(e_rem, units)]
    v_p = [v[rows_of(ch), lanes_of(p)] for ch, p in units]
    big = each(lambda x, y, z, w: _dot_nt(_bf(jnp.concatenate([x, y], axis=0)), _bf(jnp.concatenate([z, w], axis=0))),
               a_s, r_s, b_s, k_s)
    a_ab = [jnp.where(strict, t[:n2, :n2], 0.0) for t in big]
    a_ak = [jnp.where(strict, t[:n2, n2:], 0.0) for t in big]
    a_rb = [jnp.where(incl, t[n2:, :n2], 0.0) for t in big]
    a_rk = [jnp.where(incl, t[n2:, n2:], 0.0) for t in big]
    d1 = [jnp.where(sub, t, 0.0) for t in a_ab]
    lo = each(lambda x, y: x - y, a_ab, d1)
    d2 = each(mm, d1, d1)
    t_d = each(lambda x, y: mm(eye + x, eye + y), d1, d2)
    d4 = each(mm, d2, d2)
    t_d = each(lambda x, y: mm(x, eye + y), t_d, d4)
    d8 = each(mm, d4, d4)
    t_d = each(lambda x, y: mm(x, eye + y), t_d, d8)
    e1 = each(mm1, t_d, lo)
    e2 = each(mm1, e1, e1)
    t_inv = each(lambda x, y: mm1(eye + x, eye + y), e1, e2)
    t_inv = each(mm1, t_inv, t_d)
    akv = each(lambda x, y: _dot(_bf(x), _bf(y)), a_ak, v_s)
    wy = each(lambda t, x, y: mm1(t, jnp.concatenate([x, y], axis=1)), t_inv, a_s, akv)
    qo = each(lambda x, y: _dot(_bf(x), _bf(y)), a_rb, wy)
    rkv = each(lambda x, y: _dot(_bf(x), _bf(y)), a_rk, v_s)
    w_p = [fold(t[:, :LANES]) for t in wy]
    y_p = [fold(t[:, LANES:]) for t in wy]
    q_p = each(lambda x, y: fold(x + y[:, :LANES]), r_s, qo)
    ol_p = each(lambda x, y: fold(x[:, LANES:] + y), qo, rkv)
    gmat = each(lambda x, y: jnp.where(bd, _dot_tn(_bf(x), _bf(y)), 0.0), w_p, b_h)
    nmat = each(lambda y, vv, bb, kk_: jnp.where(bd, _dot_tn(_bf(jnp.concatenate([y, vv], axis=0)),
                                                             _bf(jnp.concatenate([bb, kk_], axis=0))), 0.0),
                y_p, v_p, b_h, k_h)
    pre = {u: (q_p[i], ol_p[i], gmat[i], nmat[i], jnp.exp(cl[i])) for i, u in enumerate(units)}

    outs = [[None] * (RWKV_HEADS // 2) for _ in range(tb // c)]
    st = [st_ref[p] for p in range(RWKV_HEADS // 2)]
    for ch in range(tb // c):
        for p in range(RWKV_HEADS // 2):
            q_u, ol_u, g_u, n_u, pc_u = pre[ch, p]
            stb = _bf(st[p])
            outs[ch][p] = _dot_nt(_bf(q_u), stb) + ol_u
            st[p] = st[p] * pc_u + _dot(stb, _bf(g_u)) + n_u
    for p in range(RWKV_HEADS // 2):
        st_ref[p] = st[p]
    o = jnp.concatenate([jnp.concatenate(row, axis=1) for row in outs], axis=0)
    inv_n = 1.0 / RWKV_HEAD
    d = o - _dot_exact_rhs(o, ones, 2) * inv_n
    var = _dot_exact_rhs(d * d, ones, 2) * inv_n
    o = d * lax.rsqrt(var + RWKV_GN_EPS) * lnw_ref[...] + lnb_ref[...]
    o = o + _dot_exact_rhs(r * k * rk_ref[...], ones, 2) * v
    o_ref[0] = o * g


def _rwkv(z_rwkv, pr):
    b, s, _ = z_rwkv.shape
    tb = RWKV_BLOCK
    row = lambda n: _full((1, n))
    ltri = np.kron(np.eye(tb // RWKV_CHUNK, dtype=np.float32), np.tril(np.ones((RWKV_CHUNK, RWKV_CHUNK), np.float32)))
    return pl.pallas_call(
        _rwkv_kernel,
        grid=(b, s // tb),
        in_specs=[pl.BlockSpec((1, tb, RWKV_IN), lambda i, j: (i, j, 0)), row(RWKV_IN), _full((LANES, 2 * RWKV_W)),
                  row(RWKV_W), row(RWKV_W), _full((GATE_LORA, RWKV_W)), row(RWKV_W), row(RWKV_W), row(RWKV_W),
                  row(RWKV_W), row(RWKV_W), _full((RWKV_W, RWKV_W)), _full((tb, tb))],
        out_specs=pl.BlockSpec((1, tb, RWKV_W), lambda i, j: (i, j, 0)),
        out_shape=jax.ShapeDtypeStruct((b, s, RWKV_W), F32),
        scratch_shapes=[pltpu.VMEM((RWKV_HEADS // 2, LANES, LANES), F32), pltpu.VMEM((1, RWKV_IN), F32)],
        compiler_params=_params(("arbitrary", "arbitrary")),
        name="rwkv7",
    )(z_rwkv, pr["mu"], pr["wwa"], pr["w_bias"], pr["a_bias"], pr["g_up"], pr["k_k"], pr["k_a"], pr["r_k"],
      pr["ln_w"], pr["ln_b"], _bf(_block_ones(RWKV_W, RWKV_HEAD)), jnp.asarray(ltri, dtype=BF16))


def _gla_kernel(z_ref, gkup_ref, gkb_ref, nw_ref, ind_ref, ones_ref, ltri_ref, o_ref, st_ref, x_ref):
    n = GLA_SUB
    tb = GLA_BLOCK
    nsub = tb // n

    @pl.when(pl.program_id(1) == 0)
    def _():
        st_ref[...] = jnp.zeros_like(st_ref)

    z = z_ref[0]
    q_all = z[:, 0:GLA_QW] * (GLA_DK ** -0.5)
    k_all = z[:, GLA_QW:2 * GLA_QW]
    v_all = z[:, 2 * GLA_QW:2 * GLA_QW + GLA_VW]
    gkd = z[:, 2 * GLA_QW + GLA_VW:2 * GLA_QW + GLA_VW + LANES]
    g = z[:, 2 * GLA_QW + GLA_VW + LANES:GLA_IN]
    gk = jax.nn.log_sigmoid(_dot_f32(gkd, gkup_ref[...], 3) + gkb_ref[...]) * (1.0 / GLA_GATE_NORMALIZER)
    bc_all = _dot_exact_lhs(ltri_ref[...], gk, 3)
    bd = _same_block((GLA_VW, GLA_QW), GLA_DVP, GLA_DKP)
    rowi = _iota((n, GLA_QW), 0)
    subs = range(nsub)
    rows = [slice(sc * n, (sc + 1) * n) for sc in subs]
    q = [q_all[r] for r in rows]
    k = [k_all[r] for r in rows]
    v = [v_all[r] for r in rows]
    bc = [bc_all[r] for r in rows]
    bl = [t[n - 1:n, :] for t in bc]
    for sc in subs:
        for j in range(n):
            xj = q[sc] * jnp.exp(jnp.minimum(bc[sc] - bc[sc][j:j + 1, :], 0.0)) * k[sc][j:j + 1, :]
            x_ref[(sc * n + j) * n:(sc * n + j + 1) * n, :] = jnp.where(rowi >= j, xj, 0.0)
    x = x_ref[...]
    xh = _bf(x)
    xl = _bf(x - xh.astype(F32))
    ind = ind_ref[...]
    e = _dot(xh, ind) + _dot(xl, ind)
    o_intra = []
    for sc in subs:
        acc = None
        for j in range(n):
            t = e[(sc * n + j) * n:(sc * n + j + 1) * n, :] * v[sc][j:j + 1, :]
            acc = t if acc is None else acc + t
        o_intra.append(acc)
    upd = [jnp.where(bd, _dot_tn(_bf(v[sc]), _bf(k[sc] * jnp.exp(bl[sc] - bc[sc]))), 0.0) for sc in subs]
    qe = [_bf(q[sc] * jnp.exp(bc[sc])) for sc in subs]
    st = st_ref[...]
    outs = []
    for sc in subs:
        outs.append(_dot_nt(qe[sc], _bf(st)) + o_intra[sc])
        st = st * jnp.exp(bl[sc]) + upd[sc]
    st_ref[...] = st
    o = jnp.concatenate(outs, axis=0)
    ms = _dot_exact_rhs(o * o, ones_ref[...], 2) * (1.0 / GLA_DV)
    o_ref[0] = o * lax.rsqrt(ms + HEAD_NORM_EPS) * nw_ref[...] * (g * jax.nn.sigmoid(g))


def _gla(z_gla, pr):
    b, s, _ = z_gla.shape
    tb = GLA_BLOCK
    hq = np.arange(GLA_QW) // GLA_DKP
    hv = np.arange(GLA_VW) // GLA_DVP
    ind = jnp.asarray((hq[:, None] == hv[None, :]).astype(np.float32), dtype=BF16)
    ltri = np.kron(np.eye(tb // GLA_SUB, dtype=np.float32), np.tril(np.ones((GLA_SUB, GLA_SUB), np.float32)))
    return pl.pallas_call(
        _gla_kernel,
        grid=(b, s // tb),
        in_specs=[pl.BlockSpec((1, tb, GLA_IN), lambda i, j: (i, j, 0)), _full((LANES, GLA_QW)), _full((1, GLA_QW)),
                  _full((1, GLA_VW)), _full((GLA_QW, GLA_VW)), _full((GLA_VW, GLA_VW)), _full((tb, tb))],
        out_specs=pl.BlockSpec((1, tb, GLA_VW), lambda i, j: (i, j, 0)),
        out_shape=jax.ShapeDtypeStruct((b, s, GLA_VW), F32),
        scratch_shapes=[pltpu.VMEM((GLA_VW, GLA_QW), F32), pltpu.VMEM((tb * GLA_SUB, GLA_QW), F32)],
        compiler_params=_params(("arbitrary", "arbitrary")),
        name="gla",
    )(z_gla, pr["gk_up"], pr["gk_bias"], pr["norm_w"], ind, _bf(_block_ones(GLA_VW, GLA_DVP)),
      jnp.asarray(ltri, dtype=BF16))


def _out_proj_kernel(x_ref, a_ref, b_ref, c_ref, w_ref, g_ref, x1_ref, hnt_ref):
    acc = _dot(_bf(a_ref[...]), w_ref[0:RET_W, :])
    acc += _dot(_bf(b_ref[...]), w_ref[RET_W:RET_W + RWKV_W, :])
    acc += _dot(_bf(c_ref[...]), w_ref[RET_W + RWKV_W:D_MIXP, :])
    x1 = x_ref[...] + acc
    x1_ref[...] = x1
    hn = x1 * lax.rsqrt(jnp.mean(x1 * x1, axis=-1, keepdims=True) + NORM_EPS) * g_ref[...]
    hnt_ref[...] = _bf(hn.T)


def _out_proj(x2, o_ret, o_rwkv, o_gla, w_out_p, gain, tm=512):
    t = x2.shape[0]
    blk = lambda n: pl.BlockSpec((tm, n), lambda i: (i, 0))
    return pl.pallas_call(
        _out_proj_kernel,
        grid=(t // tm,),
        in_specs=[blk(D_MODEL), blk(RET_W), blk(RWKV_W), blk(GLA_VW), _full((D_MIXP, D_MODEL)), _full((1, D_MODEL))],
        out_specs=[blk(D_MODEL), pl.BlockSpec((D_MODEL, tm), lambda i: (0, i))],
        out_shape=[jax.ShapeDtypeStruct((t, D_MODEL), F32), jax.ShapeDtypeStruct((D_MODEL, t), BF16)],
        compiler_params=_params(("arbitrary",)),
        name="out_proj",
    )(x2, o_ret, o_rwkv, o_gla, w_out_p, gain)


def _staircase_pairs():
    k = PEER_TOPK
    return [(a, b) for a in range(k) for b in range(k) if (a + 1) * (b + 1) <= k]


def _tree_max(xs):
    xs = list(xs)
    while len(xs) > 1:
        nxt = [jnp.maximum(xs[2 * i], xs[2 * i + 1]) for i in range(len(xs) // 2)]
        if len(xs) % 2:
            nxt.append(xs[-1])
        xs = nxt
    return xs[0]


def _route_kernel(hnt_ref, wq_ref, keys_ref, n1_ref, e1_ref, r2_ref, e2_ref, work_ref, s_ref, vals_ref, rank_ref):
    nh, nk, k_top = PEER_HEADS, PEER_NKEYS, PEER_TOPK
    qt = _bf(_dot(wq_ref[...], hnt_ref[...]))
    for p in range(2):
        for h in range(nh):
            r0 = (p * nh + h) * PEER_QHALF
            s = _dot(keys_ref[p, h], qt[r0:r0 + PEER_QHALF, :])
            work_ref[p, h] = s
            s_ref[p, h] = s
    rank_ref[...] = jnp.full(rank_ref.shape, float(k_top), F32)
    groups = nk // SUBLANES

    def extract(r, carry):
        rf = jnp.asarray(r, F32)
        for p in range(2):
            for h in range(nh):
                tiles = [work_ref[p, h, i * SUBLANES:(i + 1) * SUBLANES, :] for i in range(groups)]
                m = _tree_max(tiles)
                for sh in (4, 2, 1):
                    m = jnp.maximum(m, pltpu.roll(m, sh, 0))
                vals_ref[p, r, pl.ds(h, 1), :] = m[0:1, :]
                for i in range(groups):
                    rows = slice(i * SUBLANES, (i + 1) * SUBLANES)
                    is_max = tiles[i] == m
                    work_ref[p, h, rows, :] = jnp.where(is_max, -jnp.inf, tiles[i])
                    rank_ref[p, h, rows, :] = jnp.where(is_max, rf, rank_ref[p, h, rows, :])
        return carry

    lax.fori_loop(0, k_top, extract, 0)

    v1 = [vals_ref[0, r] for r in range(k_top)]
    v2 = [vals_ref[1, r] for r in range(k_top)]
    cand = {(a, b): v1[a] + v2[b] for a, b in _staircase_pairs()}
    work = list(cand.values())
    tau = None
    for it in range(k_top):
        tau = _tree_max(work)
        if it + 1 < k_top:
            work = [jnp.where(w == tau, -jnp.inf, w) for w in work]
    top = cand[(0, 0)]
    z = None
    for c in cand.values():
        zi = jnp.where(c >= tau, jnp.exp(c - top), 0.0)
        z = zi if z is None else z + zi
    scale2 = 0.5 / z
    n_of_rank = []
    for a in range(k_top):
        cnt = None
        for b in range(k_top // (a + 1)):
            ge = (cand[(a, b)] >= tau).astype(F32)
            cnt = ge if cnt is None else cnt + ge
        n_of_rank.append(cnt)
    m1, m2 = v1[0], v2[0]
    for h in range(nh):
        r1 = rank_ref[0, h]
        n1 = jnp.zeros_like(r1)
        for a in range(k_top):
            n1 = jnp.where(r1 == float(a), n_of_rank[a][h:h + 1, :], n1)
        n1_ref[h] = n1
        e1_ref[h] = jnp.exp(s_ref[0, h] - m1[h:h + 1, :])
        r2_ref[h] = pltpu.bitcast(_bf(rank_ref[1, h]), jnp.uint32)
        e2_ref[h] = pltpu.bitcast(_bf(jnp.exp(s_ref[1, h] - m2[h:h + 1, :]) * scale2[h:h + 1, :]), jnp.uint32)


def _route(hnt, wq_t, keys, tb=256):
    t = hnt.shape[1]
    nh, nk = PEER_HEADS, PEER_NKEYS
    out = pl.BlockSpec((nh, nk, tb), lambda i: (0, 0, i))
    packed = pl.BlockSpec((nh, nk // 2, tb), lambda i: (0, 0, i))
    return pl.pallas_call(
        _route_kernel,
        grid=(t // tb,),
        in_specs=[pl.BlockSpec((D_MODEL, tb), lambda i: (0, i)), _full((2 * nh * PEER_QHALF, D_MODEL)),
                  _full((2, nh, nk, PEER_QHALF))],
        out_specs=[out, out, packed, packed],
        out_shape=[jax.ShapeDtypeStruct((nh, nk, t), F32)] * 2 + [jax.ShapeDtypeStruct((nh, nk // 2, t), jnp.uint32)] * 2,
        scratch_shapes=[pltpu.VMEM((2, nh, nk, tb), F32), pltpu.VMEM((2, nh, nk, tb), F32),
                        pltpu.VMEM((2, PEER_TOPK, nh, tb), F32), pltpu.VMEM((2, nh, nk, tb), F32)],
        compiler_params=_params(("arbitrary",)),
        name="peer_route",
    )(hnt, wq_t, keys)


EXPERT_SLABS = 8
EXPERT_PAIR = 2
PIPE_LAG = 2


def _expert_kernel(hnt_ref, u_ref, vt_ref, n1_ref, e1_ref, r2_ref, e2_ref, x1_ref, gf_ref, o_ref, acc_ref, ht0, ht1,
                   act0, act1, rn_ref, re_ref, *, n_blocks, final_norm):
    s = pl.program_id(0)
    n_pairs = pl.num_programs(0) - PIPE_LAG
    nk = PEER_NKEYS
    tb = hnt_ref.shape[1]
    j2 = lax.rem(jnp.clip(s - 1, 0, n_pairs - 1), n_blocks)
    j3 = lax.rem(jnp.clip(s - PIPE_LAG, 0, n_pairs - 1), n_blocks)
    live = jnp.logical_and(s >= 1, s <= n_pairs).astype(F32)

    @pl.when(s == 0)
    def _():
        for r in (ht0, ht1, act0, act1):
            r[...] = jnp.zeros_like(r)

    @pl.when(j3 == 0)
    def _():
        acc_ref[...] = jnp.zeros_like(acc_ref)

    for h in range(PEER_HEADS):
        for k in range(EXPERT_SLABS):
            r = h * EXPERT_SLABS + k
            rn_ref[r:r + 1, :] = n1_ref[h, pl.ds(j2 * EXPERT_SLABS + k, 1), :]
            re_ref[r:r + 1, :] = e1_ref[h, pl.ds(j2 * EXPERT_SLABS + k, 1), :] * live

    def stages(ht_w, ht_r, act_w, act_r):
        half_w = 2 * LANES
        kc = 2 * LANES
        n_kc = D_MODEL // kc
        n_ec = (EXPERT_SLABS * nk) // kc

        def hidden_piece(half, c):
            cols = slice(half * half_w, (half + 1) * half_w)
            return _dot(u_ref[:, c * kc:(c + 1) * kc], hnt_ref[c * kc:(c + 1) * kc, cols])

        def project_piece(half, c):
            cols = slice(half * half_w, (half + 1) * half_w)
            return _dot(vt_ref[:, c * kc:(c + 1) * kc], act_r[c * kc:(c + 1) * kc, cols])

        def gate_unit(st, g0):
            cols = slice(st * LANES, (st + 1) * LANES)
            gates = [None] * EXPERT_PAIR
            for h in range(PEER_HEADS):
                r2 = pltpu.bitcast(r2_ref[h, :, cols], BF16)
                e2 = pltpu.bitcast(e2_ref[h, :, cols], BF16)
                for kk in range(EXPERT_PAIR):
                    r = h * EXPERT_SLABS + g0 + kk
                    n1 = _bf(rn_ref[r:r + 1, cols])
                    e1 = _bf(re_ref[r:r + 1, cols])
                    gh = jnp.where(r2 < n1, e2 * e1, jnp.zeros_like(e2))
                    gates[kk] = gh if gates[kk] is None else gates[kk] + gh
            for kk in range(EXPERT_PAIR):
                rows = slice((g0 + kk) * nk, (g0 + kk + 1) * nk)
                hk = ht_r[rows, cols]
                act_w[rows, cols] = _bf(hk * (1.0 + lax.erf(hk * (2.0 ** -0.5)))) * gates[kk]

        units = [(st, g0) for st in range(tb // LANES) for g0 in range(0, EXPERT_SLABS, EXPERT_PAIR)]
        per_slot = len(units) // (4 * max(n_kc, n_ec))
        hid = [None, None]
        prj = [None, None]
        ui = 0
        for half in range(2):
            for c in range(max(n_kc, n_ec)):
                if c < n_kc:
                    t = hidden_piece(half, c)
                    hid[half] = t if hid[half] is None else hid[half] + t
                for _ in range(per_slot):
                    gate_unit(*units[ui])
                    ui += 1
                if c < n_ec:
                    t = project_piece(half, c)
                    prj[half] = t if prj[half] is None else prj[half] + t
                for _ in range(per_slot):
                    gate_unit(*units[ui])
                    ui += 1
            cols = slice(half * half_w, (half + 1) * half_w)
            ht_w[:, cols] = hid[half]
            acc_ref[:, cols] += prj[half]

    parity = lax.rem(s, 2)

    @pl.when(parity == 0)
    def _():
        stages(ht0, ht1, act1, act0)

    @pl.when(parity == 1)
    def _():
        stages(ht1, ht0, act0, act1)

    @pl.when(jnp.logical_and(j3 == n_blocks - 1, s >= PIPE_LAG))
    def _():
        y = acc_ref[...].T + x1_ref[...]
        if final_norm:
            y = y * lax.rsqrt(jnp.mean(y * y, axis=-1, keepdims=True) + NORM_EPS) * gf_ref[...]
        o_ref[...] = y


def _experts(hnt, u_b, vt_b, n1, e1, r2, e2, x1, gain_f, final_norm, tb=512):
    t = hnt.shape[1]
    nh, nk = PEER_HEADS, PEER_NKEYS
    eb = EXPERT_SLABS * nk
    n_blocks = PEER_NEXPERTS // eb
    n_pairs = (t // tb) * n_blocks

    def pair(lag):
        def f(s):
            p = jnp.clip(s - lag, 0, n_pairs - 1)
            return p // n_blocks, lax.rem(p, n_blocks)
        return f

    tok = lambda lag: (lambda s: pair(lag)(s)[0])
    blk = lambda lag: (lambda s: pair(lag)(s)[1])
    routed = lambda rows: pl.BlockSpec((nh, rows, tb), lambda s: (0, 0, tok(1)(s)))
    nrow = PEER_HEADS * EXPERT_SLABS
    return pl.pallas_call(
        functools.partial(_expert_kernel, n_blocks=n_blocks, final_norm=final_norm),
        grid=(n_pairs + PIPE_LAG,),
        in_specs=[pl.BlockSpec((D_MODEL, tb), lambda s: (0, tok(0)(s))),
                  pl.BlockSpec((eb, D_MODEL), lambda s: (blk(0)(s), 0)),
                  pl.BlockSpec((D_MODEL, eb), lambda s: (0, blk(PIPE_LAG)(s))),
                  routed(nk), routed(nk), routed(nk // 2), routed(nk // 2),
                  pl.BlockSpec((tb, D_MODEL), lambda s: (tok(PIPE_LAG)(s), 0)), _full((1, D_MODEL))],
        out_specs=pl.BlockSpec((tb, D_MODEL), lambda s: (tok(PIPE_LAG)(s), 0)),
        out_shape=jax.ShapeDtypeStruct((t, D_MODEL), F32),
        scratch_shapes=[pltpu.VMEM((D_MODEL, tb), F32), pltpu.VMEM((eb, tb), F32), pltpu.VMEM((eb, tb), F32),
                        pltpu.VMEM((eb, tb), BF16), pltpu.VMEM((eb, tb), BF16),
                        pltpu.VMEM((nrow, tb), F32), pltpu.VMEM((nrow, tb), F32)],
        compiler_params=_params(("arbitrary",)),
        name="peer_experts",
    )(hnt, u_b, vt_b, n1, e1, r2, e2, x1, gain_f)


def _pad_heads(w, heads, d, dp):
    lead = w.shape[:-1]
    w = w.reshape(*lead, heads, d)
    w = jnp.pad(w, [(0, 0)] * len(lead) + [(0, 0), (0, dp - d)])
    return w.reshape(*lead, heads * dp)


def _layer_params(l, w_in, w_out, rwkv_mu, rwkv_w_up, rwkv_w_bias, rwkv_a_up, rwkv_a_bias, rwkv_g_up, rwkv_k_k,
                  rwkv_k_a, rwkv_r_k, rwkv_ln_w, rwkv_ln_b, gla_gk_up, gla_gk_bias, gla_norm_w, peer_w_q,
                  peer_sub_keys, peer_u, peer_v):
    wi = w_in[l]
    g0 = RET_IN + RWKV_IN
    qk, vw = GLA_HEADS * GLA_DK, GLA_HEADS * GLA_DV
    gq = _pad_heads(wi[:, g0:g0 + qk], GLA_HEADS, GLA_DK, GLA_DKP)
    gkk = _pad_heads(wi[:, g0 + qk:g0 + 2 * qk], GLA_HEADS, GLA_DK, GLA_DKP)
    gv = _pad_heads(wi[:, g0 + 2 * qk:g0 + 2 * qk + vw], GLA_HEADS, GLA_DV, GLA_DVP)
    ggk = jnp.pad(wi[:, g0 + 2 * qk + vw:g0 + 2 * qk + vw + GLA_GATE_LORA], ((0, 0), (0, LANES - GLA_GATE_LORA)))
    gg = _pad_heads(wi[:, g0 + 2 * qk + vw + GLA_GATE_LORA:], GLA_HEADS, GLA_DV, GLA_DVP)
    w_in_p = _bf(jnp.concatenate([wi[:, :g0], gq, gkk, gv, ggk, gg], axis=1))
    wo = w_out[l]
    m0 = RET_W + RWKV_W
    wo_gla = jnp.pad(wo[m0:].reshape(GLA_HEADS, GLA_DV, D_MODEL), ((0, 0), (0, GLA_DVP - GLA_DV), (0, 0)))
    w_out_p = _bf(jnp.concatenate([wo[:m0], wo_gla.reshape(GLA_VW, D_MODEL)], axis=0))
    zeros = jnp.zeros((DECAY_LORA, RWKV_W), F32)
    rw = dict(
        mu=rwkv_mu[l][None, :],
        wwa=jnp.concatenate([jnp.concatenate([rwkv_w_up[l], zeros], axis=1),
                             jnp.concatenate([zeros, rwkv_a_up[l]], axis=1)], axis=0),
        w_bias=rwkv_w_bias[l][None, :], a_bias=rwkv_a_bias[l][None, :], g_up=_bf(rwkv_g_up[l]),
        k_k=rwkv_k_k[l][None, :], k_a=rwkv_k_a[l][None, :], r_k=rwkv_r_k[l].reshape(1, RWKV_W),
        ln_w=rwkv_ln_w[l][None, :], ln_b=rwkv_ln_b[l][None, :])
    gl = dict(
        gk_up=jnp.pad(_pad_heads(gla_gk_up[l], GLA_HEADS, GLA_DK, GLA_DKP), ((0, LANES - GLA_GATE_LORA), (0, 0))),
        gk_bias=_pad_heads(gla_gk_bias[l][None, :], GLA_HEADS, GLA_DK, GLA_DKP),
        norm_w=_pad_heads(gla_norm_w[l][None, :], GLA_HEADS, GLA_DV, GLA_DVP))
    wq = peer_w_q[l].reshape(D_MODEL, PEER_HEADS, 2, PEER_QHALF)
    wq_t = _bf(jnp.transpose(wq, (2, 1, 3, 0)).reshape(2 * PEER_HEADS * PEER_QHALF, D_MODEL))
    keys = _bf(jnp.transpose(peer_sub_keys[l], (1, 0, 2, 3)))
    return dict(w_in=w_in_p, w_out=w_out_p, rwkv=rw, gla=gl, wq_t=wq_t, keys=keys, u=_bf(peer_u[l]),
                vt=_bf(peer_v[l].T))


def _layer(x2, b, s, pr, gain_mix, gain_ffn, gain_final, final_norm):
    z_ret, z_rwkv, z_gla = _norm_proj(x2, gain_mix, pr["w_in"])
    o_ret = _retention(z_ret.reshape(b, s, RET_IN)).reshape(b * s, RET_W)
    o_rwkv = _rwkv(z_rwkv.reshape(b, s, RWKV_IN), pr["rwkv"]).reshape(b * s, RWKV_W)
    o_gla = _gla(z_gla.reshape(b, s, GLA_IN), pr["gla"]).reshape(b * s, GLA_VW)
    x1, hnt = _out_proj(x2, o_ret, o_rwkv, o_gla, pr["w_out"], gain_ffn)
    n1, e1, r2, e2 = _route(hnt, pr["wq_t"], pr["keys"])
    return _experts(hnt, pr["u"], pr["vt"], n1, e1, r2, e2, x1, gain_final, final_norm)


def kernel(x, norm_mix, norm_ffn, norm_final, w_in, w_out, rwkv_mu, rwkv_w_up, rwkv_w_bias, rwkv_a_up, rwkv_a_bias, rwkv_g_up, rwkv_k_k, rwkv_k_a, rwkv_r_k, rwkv_ln_w, rwkv_ln_b, gla_gk_up, gla_gk_bias, gla_norm_w, peer_w_q, peer_sub_keys, peer_u, peer_v):
    b, s, d = x.shape
    x2 = x.reshape(b * s, d)
    gain_final = norm_final[None, :]
    for l in range(DEPTH):
        pr = _layer_params(l, w_in, w_out, rwkv_mu, rwkv_w_up, rwkv_w_bias, rwkv_a_up, rwkv_a_bias, rwkv_g_up,
                           rwkv_k_k, rwkv_k_a, rwkv_r_k, rwkv_ln_w, rwkv_ln_b, gla_gk_up, gla_gk_bias, gla_norm_w,
                           peer_w_q, peer_sub_keys, peer_u, peer_v)
        x2 = _layer(x2, b, s, pr, norm_mix[l][None, :], norm_ffn[l][None, :], gain_final, l == DEPTH - 1)
    return x2.reshape(b, s, d)
```

```python
import functools

import numpy as np
import jax
import jax.numpy as jnp
from jax import lax
from jax.experimental import pallas as pl
from jax.experimental.pallas import tpu as pltpu

F32 = jnp.float32
BF16 = jnp.bfloat16
HI = lax.Precision.HIGHEST

D_MODEL = 1024
DEPTH = 2
NORM_EPS = 1e-6
HEAD_NORM_EPS = 1e-5
RET_HEADS, RET_DK, RET_DV, ROPE_BASE = 4, 64, 64, 10000.0
RET_W = RET_HEADS * RET_DK
RET_CHUNK = 128
RWKV_HEADS, RWKV_HEAD = 6, 64
RWKV_W = RWKV_HEADS * RWKV_HEAD
DECAY_LORA, AAA_LORA, GATE_LORA = 64, 64, 128
RWKV_GN_EPS = 64e-5
RWKV_CHUNK = 64
RWKV_SUB = 16
RWKV_IN = 3 * RWKV_W + DECAY_LORA + AAA_LORA + GATE_LORA
GLA_HEADS, GLA_DK, GLA_DV, GLA_GATE_LORA = 4, 48, 96, 16
GLA_GATE_NORMALIZER = 16.0
GLA_DKP, GLA_DVP = 64, 128
GLA_QW = GLA_HEADS * GLA_DKP
GLA_VW = GLA_HEADS * GLA_DVP
GLA_IN = 2 * GLA_QW + GLA_VW + 128 + GLA_VW
GLA_SUB = 16
GLA_BLOCK = 256
RET_IN = 4 * RET_W
D_INP = RET_IN + RWKV_IN + GLA_IN
D_MIXP = RET_W + RWKV_W + GLA_VW
PEER_HEADS, PEER_NKEYS, PEER_QHALF, PEER_TOPK = 8, 128, 128, 16
PEER_NEXPERTS = PEER_NKEYS * PEER_NKEYS
LANES = 128
SUBLANES = 8
VMEM_LIMIT = 56 * 1024 * 1024


def _params(sem):
    return pltpu.CompilerParams(dimension_semantics=sem, vmem_limit_bytes=VMEM_LIMIT)


def _dot(a, b, prec=None):
    return jnp.dot(a, b, precision=prec, preferred_element_type=F32)


def _dot_nt(a, b, prec=None):
    return lax.dot_general(a, b, (((1,), (1,)), ((), ())), precision=prec, preferred_element_type=F32)


def _dot_tn(a, b, prec=None):
    return lax.dot_general(a, b, (((0,), (0,)), ((), ())), precision=prec, preferred_element_type=F32)


def _bf(x):
    return x.astype(BF16)


def _iota(shape, dim):
    return lax.broadcasted_iota(jnp.int32, shape, dim)


def _same_block(shape, rblk, cblk):
    r = _iota(shape, 0) >> (rblk.bit_length() - 1)
    c = _iota(shape, 1) >> (cblk.bit_length() - 1)
    return r == c


def _full(shape):
    n = len(shape)
    return pl.BlockSpec(shape, lambda *_: (0,) * n)


def _block_ones(n, blk):
    i = np.arange(n) // blk
    return jnp.asarray((i[:, None] == i[None, :]).astype(np.float32))


def _tril_ones(n):
    return jnp.asarray(np.tril(np.ones((n, n), np.float32)))


def _norm_proj_kernel(x_ref, g_ref, w_ref, zr_ref, zk_ref, zg_ref):
    x = x_ref[...]
    y = x * lax.rsqrt(jnp.mean(x * x, axis=-1, keepdims=True) + NORM_EPS) * g_ref[...]
    yb = _bf(y)
    zr_ref[...] = _dot(yb, w_ref[:, 0:RET_IN])
    zk_ref[...] = _dot(yb, w_ref[:, RET_IN:RET_IN + RWKV_IN])
    zg_ref[...] = _dot(yb, w_ref[:, RET_IN + RWKV_IN:D_INP])


def _norm_proj(x2, gain, w_in_p, tm=512):
    t = x2.shape[0]
    return pl.pallas_call(
        _norm_proj_kernel,
        grid=(t // tm,),
        in_specs=[pl.BlockSpec((tm, D_MODEL), lambda i: (i, 0)), _full((1, D_MODEL)), _full((D_MODEL, D_INP))],
        out_specs=[pl.BlockSpec((tm, RET_IN), lambda i: (i, 0)), pl.BlockSpec((tm, RWKV_IN), lambda i: (i, 0)),
                   pl.BlockSpec((tm, GLA_IN), lambda i: (i, 0))],
        out_shape=[jax.ShapeDtypeStruct((t, RET_IN), F32), jax.ShapeDtypeStruct((t, RWKV_IN), F32),
                   jax.ShapeDtypeStruct((t, GLA_IN), F32)],
        compiler_params=_params(("arbitrary",)),
        name="norm_proj",
    )(x2, gain, w_in_p)


def _ret_kernel(z_ref, cos_ref, sin_ref, dmat_ref, qdec_ref, kdec_ref, cdec_ref, ones_ref, o_ref, st_ref):
    c = RET_CHUNK

    @pl.when(pl.program_id(1) == 0)
    def _():
        st_ref[...] = jnp.zeros_like(st_ref)

    z = z_ref[0]
    q, k, v, g = (z[:, i * RET_W:(i + 1) * RET_W] for i in range(4))
    cos, sin = cos_ref[...], sin_ref[...]
    first_half = (_iota((c, RET_W), 1) & (RET_DK - 1)) < RET_DK // 2

    def rot(t):
        return jnp.where(first_half, -pltpu.roll(t, RET_W - RET_DK // 2, 1), pltpu.roll(t, RET_DK // 2, 1))

    q = q * cos + rot(q) * sin
    k = (k * cos + rot(k) * sin) * (RET_DK ** -0.5)
    qd = q * qdec_ref[...]
    kd = k * kdec_ref[...]
    head0 = _iota((c, LANES), 1) < RET_DK
    bd = _same_block((LANES, LANES), RET_DV, RET_DK)
    outs = []
    for p in range(RET_HEADS // 2):
        sl = slice(p * LANES, (p + 1) * LANES)
        qp, kp, vp = q[:, sl], _bf(k[:, sl]), v[:, sl]
        o = None
        for hh in range(2):
            mh = head0 if hh == 0 else jnp.logical_not(head0)
            s = _dot_nt(_bf(jnp.where(mh, qp, 0.0)), kp) * dmat_ref[2 * p + hh]
            oh = _dot(_bf(s), _bf(jnp.where(mh, vp, 0.0)))
            o = oh if o is None else o + oh
        st = st_ref[p]
        o = o + _dot_nt(_bf(qd[:, sl]), _bf(st))
        upd = _dot_tn(_bf(vp), _bf(kd[:, sl]))
        st_ref[p] = st * cdec_ref[:, sl] + jnp.where(bd, upd, 0.0)
        outs.append(o)
    o = jnp.concatenate(outs, axis=1)
    ms = _dot(o * o, ones_ref[...], HI) * (1.0 / RET_DV)
    o = o * lax.rsqrt(ms + HEAD_NORM_EPS)
    o_ref[0] = o * (g * jax.nn.sigmoid(g))


def _retention_consts(s):
    c = RET_CHUNK
    log_gamma = np.log1p(-np.exp(np.linspace(np.log(1.0 / 32.0), np.log(1.0 / 512.0), RET_HEADS)))
    idx = np.arange(c, dtype=np.float64)
    diff = idx[:, None] - idx[None, :]
    dmat = np.where(diff >= 0, np.exp(log_gamma[:, None, None] * np.where(diff >= 0, diff, 0.0)), 0.0)
    lg_lane = np.repeat(log_gamma, RET_DK)[None, :]
    qdec = np.exp(lg_lane * (idx[:, None] + 1.0))
    kdec = np.exp(lg_lane * (c - 1.0 - idx[:, None]))
    cdec = np.exp(lg_lane * c)
    half = RET_DK // 2
    inv = ROPE_BASE ** (-np.arange(half, dtype=np.float64) / half)
    ang = np.arange(s, dtype=np.float64)[:, None] * inv[None, :]
    cos = np.tile(np.cos(ang), (1, 2 * RET_HEADS))
    sin = np.tile(np.sin(ang), (1, 2 * RET_HEADS))
    f = lambda a: jnp.asarray(a.astype(np.float32))
    return f(cos), f(sin), f(dmat), f(qdec), f(kdec), f(cdec)


def _retention(z_ret):
    b, s, _ = z_ret.shape
    c = RET_CHUNK
    cos, sin, dmat, qdec, kdec, cdec = _retention_consts(s)
    return pl.pallas_call(
        _ret_kernel,
        grid=(b, s // c),
        in_specs=[pl.BlockSpec((1, c, RET_IN), lambda i, j: (i, j, 0)),
                  pl.BlockSpec((c, RET_W), lambda i, j: (j, 0)), pl.BlockSpec((c, RET_W), lambda i, j: (j, 0)),
                  _full((RET_HEADS, c, c)), _full((c, RET_W)), _full((c, RET_W)), _full((1, RET_W)),
                  _full((RET_W, RET_W))],
        out_specs=pl.BlockSpec((1, c, RET_W), lambda i, j: (i, j, 0)),
        out_shape=jax.ShapeDtypeStruct((b, s, RET_W), F32),
        scratch_shapes=[pltpu.VMEM((RET_HEADS // 2, LANES, LANES), F32)],
        compiler_params=_params(("arbitrary", "arbitrary")),
        name="retention",
    )(z_ret, cos, sin, dmat, qdec, kdec, cdec, _block_ones(RET_W, RET_DV))


RWKV_BLOCK = 256
INV_PASSES = 3


def _split(x, parts):
    out = []
    for _ in range(parts - 1):
        h = _bf(x)
        out.append(h)
        x = x - h.astype(F32)
    out.append(_bf(x))
    return out


def _dot_exact_lhs(a_bf, x, parts):
    acc = None
    for p in _split(x, parts):
        t = _dot(a_bf, p)
        acc = t if acc is None else acc + t
    return acc


def _dot_exact_rhs(x, b_bf, parts):
    acc = None
    for p in _split(x, parts):
        t = _dot(p, b_bf)
        acc = t if acc is None else acc + t
    return acc


def _dot_f32(a, b, passes):
    if passes == 1:
        return _dot(_bf(a), _bf(b))
    if passes == 3:
        ah, al = _split(a, 2)
        bh, bl = _split(b, 2)
        return _dot(ah, bh) + (_dot(ah, bl) + _dot(al, bh))
    return _dot(a, b, HI)


def _rwkv_kernel(z_ref, mu_ref, wwa_ref, wb_ref, ab_ref, gup_ref, kk_ref, ka_ref, rk_ref, lnw_ref, lnb_ref,
                 ones_ref, ltri_ref, o_ref, st_ref, carry_ref):
    c = RWKV_CHUNK
    tb = RWKV_BLOCK
    w_ = RWKV_W

    @pl.when(pl.program_id(1) == 0)
    def _():
        st_ref[...] = jnp.zeros_like(st_ref)
        carry_ref[...] = jnp.zeros_like(carry_ref)

    f = z_ref[0]
    f_prev = jnp.where(_iota((tb, RWKV_IN), 0) == 0, carry_ref[...], pltpu.roll(f, 1, 0))
    carry_ref[...] = f[tb - 1:tb, :]
    f = f + (f_prev - f) * mu_ref[...]
    r, k, v = f[:, 0:w_], f[:, w_:2 * w_], f[:, 2 * w_:3 * w_]
    wa, gd = f[:, 3 * w_:3 * w_ + LANES], f[:, 3 * w_ + LANES:RWKV_IN]
    lane = _iota((tb, LANES), 1)
    proj = _dot_f32(jnp.where(lane < DECAY_LORA, jnp.tanh(wa), wa), wwa_ref[...], 3)
    wlog = -jax.nn.softplus(-(wb_ref[...] + proj[:, 0:w_])) - 0.5
    ld = -jnp.exp(wlog)
    a = jax.nn.sigmoid(ab_ref[...] + proj[:, w_:2 * w_])
    g = _dot(_bf(jax.nn.sigmoid(gd)), gup_ref[...])
    ones = ones_ref[...]
    kk = k * kk_ref[...]
    kk = kk * lax.rsqrt(jnp.maximum(_dot_exact_rhs(kk * kk, ones, 2), 1e-24))
    k = k * (1.0 + (a - 1.0) * ka_ref[...])
    cum = _dot_exact_lhs(ltri_ref[...], ld, 3)
    e_neg = jnp.exp(-cum)
    b = kk * a
    a_t_all = -kk * jnp.exp(cum - ld)
    b_t_all, k_t_all, r_t_all = b * e_neg, k * e_neg, r * jnp.exp(cum)

    n2 = 2 * c
    ri, ci = _iota((n2, n2), 0), _iota((n2, n2), 1)
    same_head = _same_block((n2, n2), c, c)
    strict = jnp.logical_and(same_head, ri > ci)
    incl = jnp.logical_and(same_head, ri >= ci)
    sub = _same_block((n2, n2), RWKV_SUB, RWKV_SUB)
    eye = (ri == ci).astype(F32)
    bd = _same_block((LANES, LANES), RWKV_HEAD, RWKV_HEAD)
    head0 = _iota((c, LANES), 1) < RWKV_HEAD
    mm = functools.partial(_dot_f32, passes=INV_PASSES)
    mm1 = functools.partial(_dot_f32, passes=1)

    def stack(x):
        return jnp.concatenate([jnp.where(head0, x, 0.0), jnp.where(head0, 0.0, x)], axis=0)

    def fold(x):
        return x[:c] + x[c:]

    units = [(ch, p) for ch in range(tb // c) for p in range(RWKV_HEADS // 2)]
    each = lambda fn, *lists: [fn(*args) for args in zip(*lists)]
    rows_of = lambda ch: slice(ch * c, (ch + 1) * c)
    lanes_of = lambda p: slice(p * LANES, (p + 1) * LANES)
    tile = lambda t: [stack(t[rows_of(ch), lanes_of(p)]) for ch, p in units]
    a_s, r_s, b_s, k_s, v_s = tile(a_t_all), tile(r_t_all), tile(b_t_all), tile(k_t_all), tile(v)
    cl = [cum[(ch + 1) * c - 1:(ch + 1) * c, lanes_of(p)] for ch, p in units]
    e_rem = [jnp.exp(cl_u - cum[rows_of(ch), lanes_of(p)]) for cl_u, (ch, p) in zip(cl, units)]
    b_h = [b[rows_of(ch), lanes_of(p)] * e for e, (ch, p) in zip(e_rem, units)]
    k_h = [k[rows_of(ch), lanes_of(p)] * e for e, (ch, p) in zip(e_rem, units)]
    v_p = [v[rows_of(ch), lanes_of(p)] for ch, p in units]
    big = each(lambda x, y, z, w: _dot_nt(_bf(jnp.concatenate([x, y], axis=0)), _bf(jnp.concatenate([z, w], axis=0))),
               a_s, r_s, b_s, k_s)
    a_ab = [jnp.where(strict, t[:n2, :n2], 0.0) for t in big]
    a_ak = [jnp.where(strict, t[:n2, n2:], 0.0) for t in big]
    a_rb = [jnp.where(incl, t[n2:, :n2], 0.0) for t in big]
    a_rk = [jnp.where(incl, t[n2:, n2:], 0.0) for t in big]
    d1 = [jnp.where(sub, t, 0.0) for t in a_ab]
    lo = each(lambda x, y: x - y, a_ab, d1)
    d2 = each(mm, d1, d1)
    t_d = each(lambda x, y: mm(eye + x, eye + y), d1, d2)
    d4 = each(mm, d2, d2)
    t_d = each(lambda x, y: mm(x, eye + y), t_d, d4)
    d8 = each(mm, d4, d4)
    t_d = each(lambda x, y: mm(x, eye + y), t_d, d8)
    e1 = each(mm1, t_d, lo)
    e2 = each(mm1, e1, e1)
    t_inv = each(lambda x, y: mm1(eye + x, eye + y), e1, e2)
    t_inv = each(mm1, t_inv, t_d)
    akv = each(lambda x, y: _dot(_bf(x), _bf(y)), a_ak, v_s)
    wy = each(lambda t, x, y: mm1(t, jnp.concatenate([x, y], axis=1)), t_inv, a_s, akv)
    qo = each(lambda x, y: _dot(_bf(x), _bf(y)), a_rb, wy)
    rkv = each(lambda x, y: _dot(_bf(x), _bf(y)), a_rk, v_s)
    w_p = [fold(t[:, :LANES]) for t in wy]
    y_p = [fold(t[:, LANES:]) for t in wy]
    q_p = each(lambda x, y: fold(x + y[:, :LANES]), r_s, qo)
    ol_p = each(lambda x, y: fold(x[:, LANES:] + y), qo, rkv)
    gmat = each(lambda x, y: jnp.where(bd, _dot_tn(_bf(x), _bf(y)), 0.0), w_p, b_h)
    nmat = each(lambda y, vv, bb, kk_: jnp.where(bd, _dot_tn(_bf(jnp.concatenate([y, vv], axis=0)),
                                                             _bf(jnp.concatenate([bb, kk_], axis=0))), 0.0),
                y_p, v_p, b_h, k_h)
    pre = {u: (q_p[i], ol_p[i], gmat[i], nmat[i], jnp.exp(cl[i])) for i, u in enumerate(units)}

    outs = [[None] * (RWKV_HEADS // 2) for _ in range(tb // c)]
    st = [st_ref[p] for p in range(RWKV_HEADS // 2)]
    for ch in range(tb // c):
        for p in range(RWKV_HEADS // 2):
            q_u, ol_u, g_u, n_u, pc_u = pre[ch, p]
            stb = _bf(st[p])
            outs[ch][p] = _dot_nt(_bf(q_u), stb) + ol_u
            st[p] = st[p] * pc_u + _dot(stb, _bf(g_u)) + n_u
    for p in range(RWKV_HEADS // 2):
        st_ref[p] = st[p]
    o = jnp.concatenate([jnp.concatenate(row, axis=1) for row in outs], axis=0)
    inv_n = 1.0 / RWKV_HEAD
    d = o - _dot_exact_rhs(o, ones, 2) * inv_n
    var = _dot_exact_rhs(d * d, ones, 2) * inv_n
    o = d * lax.rsqrt(var + RWKV_GN_EPS) * lnw_ref[...] + lnb_ref[...]
    o = o + _dot_exact_rhs(r * k * rk_ref[...], ones, 2) * v
    o_ref[0] = o * g


def _rwkv(z_rwkv, pr):
    b, s, _ = z_rwkv.shape
    tb = RWKV_BLOCK
    row = lambda n: _full((1, n))
    ltri = np.kron(np.eye(tb // RWKV_CHUNK, dtype=np.float32), np.tril(np.ones((RWKV_CHUNK, RWKV_CHUNK), np.float32)))
    return pl.pallas_call(
        _rwkv_kernel,
        grid=(b, s // tb),
        in_specs=[pl.BlockSpec((1, tb, RWKV_IN), lambda i, j: (i, j, 0)), row(RWKV_IN), _full((LANES, 2 * RWKV_W)),
                  row(RWKV_W), row(RWKV_W), _full((GATE_LORA, RWKV_W)), row(RWKV_W), row(RWKV_W), row(RWKV_W),
                  row(RWKV_W), row(RWKV_W), _full((RWKV_W, RWKV_W)), _full((tb, tb))],
        out_specs=pl.BlockSpec((1, tb, RWKV_W), lambda i, j: (i, j, 0)),
        out_shape=jax.ShapeDtypeStruct((b, s, RWKV_W), F32),
        scratch_shapes=[pltpu.VMEM((RWKV_HEADS // 2, LANES, LANES), F32), pltpu.VMEM((1, RWKV_IN), F32)],
        compiler_params=_params(("arbitrary", "arbitrary")),
        name="rwkv7",
    )(z_rwkv, pr["mu"], pr["wwa"], pr["w_bias"], pr["a_bias"], pr["g_up"], pr["k_k"], pr["k_a"], pr["r_k"],
      pr["ln_w"], pr["ln_b"], _bf(_block_ones(RWKV_W, RWKV_HEAD)), jnp.asarray(ltri, dtype=BF16))


def _gla_kernel(z_ref, gkup_ref, gkb_ref, nw_ref, ind_ref, ones_ref, ltri_ref, o_ref, st_ref, x_ref):
    n = GLA_SUB
    tb = GLA_BLOCK
    nsub = tb // n

    @pl.when(pl.program_id(1) == 0)
    def _():
        st_ref[...] = jnp.zeros_like(st_ref)

    z = z_ref[0]
    q_all = z[:, 0:GLA_QW] * (GLA_DK ** -0.5)
    k_all = z[:, GLA_QW:2 * GLA_QW]
    v_all = z[:, 2 * GLA_QW:2 * GLA_QW + GLA_VW]
    gkd = z[:, 2 * GLA_QW + GLA_VW:2 * GLA_QW + GLA_VW + LANES]
    g = z[:, 2 * GLA_QW + GLA_VW + LANES:GLA_IN]
    gk = jax.nn.log_sigmoid(_dot_f32(gkd, gkup_ref[...], 3) + gkb_ref[...]) * (1.0 / GLA_GATE_NORMALIZER)
    bc_all = _dot_exact_lhs(ltri_ref[...], gk, 3)
    bd = _same_block((GLA_VW, GLA_QW), GLA_DVP, GLA_DKP)
    rowi = _iota((n, GLA_QW), 0)
    subs = range(nsub)
    rows = [slice(sc * n, (sc + 1) * n) for sc in subs]
    q = [q_all[r] for r in rows]
    k = [k_all[r] for r in rows]
    v = [v_all[r] for r in rows]
    bc = [bc_all[r] for r in rows]
    bl = [t[n - 1:n, :] for t in bc]
    for sc in subs:
        for j in range(n):
            xj = q[sc] * jnp.exp(jnp.minimum(bc[sc] - bc[sc][j:j + 1, :], 0.0)) * k[sc][j:j + 1, :]
            x_ref[(sc * n + j) * n:(sc * n + j + 1) * n, :] = jnp.where(rowi >= j, xj, 0.0)
    x = x_ref[...]
    xh = _bf(x)
    xl = _bf(x - xh.astype(F32))
    ind = ind_ref[...]
    e = _dot(xh, ind) + _dot(xl, ind)
    o_intra = []
    for sc in subs:
        acc = None
        for j in range(n):
            t = e[(sc * n + j) * n:(sc * n + j + 1) * n, :] * v[sc][j:j + 1, :]
            acc = t if acc is None else acc + t
        o_intra.append(acc)
    upd = [jnp.where(bd, _dot_tn(_bf(v[sc]), _bf(k[sc] * jnp.exp(bl[sc] - bc[sc]))), 0.0) for sc in subs]
    qe = [_bf(q[sc] * jnp.exp(bc[sc])) for sc in subs]
    st = st_ref[...]
    outs = []
    for sc in subs:
        outs.append(_dot_nt(qe[sc], _bf(st)) + o_intra[sc])
        st = st * jnp.exp(bl[sc]) + upd[sc]
    st_ref[...] = st
    o = jnp.concatenate(outs, axis=0)
    ms = _dot_exact_rhs(o * o, ones_ref[...], 2) * (1.0 / GLA_DV)
    o_ref[0] = o * lax.rsqrt(ms + HEAD_NORM_EPS) * nw_ref[...] * (g * jax.nn.sigmoid(g))


def _gla(z_gla, pr):
    b, s, _ = z_gla.shape
    tb = GLA_BLOCK
    hq = np.arange(GLA_QW) // GLA_DKP
    hv = np.arange(GLA_VW) // GLA_DVP
    ind = jnp.asarray((hq[:, None] == hv[None, :]).astype(np.float32), dtype=BF16)
    ltri = np.kron(np.eye(tb // GLA_SUB, dtype=np.float32), np.tril(np.ones((GLA_SUB, GLA_SUB), np.float32)))
    return pl.pallas_call(
        _gla_kernel,
        grid=(b, s // tb),
        in_specs=[pl.BlockSpec((1, tb, GLA_IN), lambda i, j: (i, j, 0)), _full((LANES, GLA_QW)), _full((1, GLA_QW)),
                  _full((1, GLA_VW)), _full((GLA_QW, GLA_VW)), _full((GLA_VW, GLA_VW)), _full((tb, tb))],
        out_specs=pl.BlockSpec((1, tb, GLA_VW), lambda i, j: (i, j, 0)),
        out_shape=jax.ShapeDtypeStruct((b, s, GLA_VW), F32),
        scratch_shapes=[pltpu.VMEM((GLA_VW, GLA_QW), F32), pltpu.VMEM((tb * GLA_SUB, GLA_QW), F32)],
        compiler_params=_params(("arbitrary", "arbitrary")),
        name="gla",
    )(z_gla, pr["gk_up"], pr["gk_bias"], pr["norm_w"], ind, _bf(_block_ones(GLA_VW, GLA_DVP)),
      jnp.asarray(ltri, dtype=BF16))


def _out_proj_kernel(x_ref, a_ref, b_ref, c_ref, w_ref, g_ref, x1_ref, hnt_ref):
    acc = _dot(_bf(a_ref[...]), w_ref[0:RET_W, :])
    acc += _dot(_bf(b_ref[...]), w_ref[RET_W:RET_W + RWKV_W, :])
    acc += _dot(_bf(c_ref[...]), w_ref[RET_W + RWKV_W:D_MIXP, :])
    x1 = x_ref[...] + acc
    x1_ref[...] = x1
    hn = x1 * lax.rsqrt(jnp.mean(x1 * x1, axis=-1, keepdims=True) + NORM_EPS) * g_ref[...]
    hnt_ref[...] = _bf(hn.T)


def _out_proj(x2, o_ret, o_rwkv, o_gla, w_out_p, gain, tm=512):
    t = x2.shape[0]
    blk = lambda n: pl.BlockSpec((tm, n), lambda i: (i, 0))
    return pl.pallas_call(
        _out_proj_kernel,
        grid=(t // tm,),
        in_specs=[blk(D_MODEL), blk(RET_W), blk(RWKV_W), blk(GLA_VW), _full((D_MIXP, D_MODEL)), _full((1, D_MODEL))],
        out_specs=[blk(D_MODEL), pl.BlockSpec((D_MODEL, tm), lambda i: (0, i))],
        out_shape=[jax.ShapeDtypeStruct((t, D_MODEL), F32), jax.ShapeDtypeStruct((D_MODEL, t), BF16)],
        compiler_params=_params(("arbitrary",)),
        name="out_proj",
    )(x2, o_ret, o_rwkv, o_gla, w_out_p, gain)


def _staircase_pairs():
    k = PEER_TOPK
    return [(a, b) for a in range(k) for b in range(k) if (a + 1) * (b + 1) <= k]


def _tree_max(xs):
    xs = list(xs)
    while len(xs) > 1:
        nxt = [jnp.maximum(xs[2 * i], xs[2 * i + 1]) for i in range(len(xs) // 2)]
        if len(xs) % 2:
            nxt.append(xs[-1])
        xs = nxt
    return xs[0]


def _route_kernel(hnt_ref, wq_ref, keys_ref, n1_ref, e1_ref, r2_ref, e2_ref, work_ref, s_ref, vals_ref, rank_ref):
    nh, nk, k_top = PEER_HEADS, PEER_NKEYS, PEER_TOPK
    qt = _bf(_dot(wq_ref[...], hnt_ref[...]))
    for p in range(2):
        for h in range(nh):
            r0 = (p * nh + h) * PEER_QHALF
            s = _dot(keys_ref[p, h], qt[r0:r0 + PEER_QHALF, :])
            work_ref[p, h] = s
            s_ref[p, h] = s
    rank_ref[...] = jnp.full(rank_ref.shape, float(k_top), F32)
    groups = nk // SUBLANES

    def extract(r, carry):
        rf = jnp.asarray(r, F32)
        for p in range(2):
            for h in range(nh):
                tiles = [work_ref[p, h, i * SUBLANES:(i + 1) * SUBLANES, :] for i in range(groups)]
                m = _tree_max(tiles)
                for sh in (4, 2, 1):
                    m = jnp.maximum(m, pltpu.roll(m, sh, 0))
                vals_ref[p, r, pl.ds(h, 1), :] = m[0:1, :]
                for i in range(groups):
                    rows = slice(i * SUBLANES, (i + 1) * SUBLANES)
                    is_max = tiles[i] == m
                    work_ref[p, h, rows, :] = jnp.where(is_max, -jnp.inf, tiles[i])
                    rank_ref[p, h, rows, :] = jnp.where(is_max, rf, rank_ref[p, h, rows, :])
        return carry

    lax.fori_loop(0, k_top, extract, 0)

    v1 = [vals_ref[0, r] for r in range(k_top)]
    v2 = [vals_ref[1, r] for r in range(k_top)]
    cand = {(a, b): v1[a] + v2[b] for a, b in _staircase_pairs()}
    work = list(cand.values())
    tau = None
    for it in range(k_top):
        tau = _tree_max(work)
        if it + 1 < k_top:
            work = [jnp.where(w == tau, -jnp.inf, w) for w in work]
    top = cand[(0, 0)]
    z = None
    for c in cand.values():
        zi = jnp.where(c >= tau, jnp.exp(c - top), 0.0)
        z = zi if z is None else z + zi
    scale2 = 0.5 / z
    n_of_rank = []
    for a in range(k_top):
        cnt = None
        for b in range(k_top // (a + 1)):
            ge = (cand[(a, b)] >= tau).astype(F32)
            cnt = ge if cnt is None else cnt + ge
        n_of_rank.append(cnt)
    m1, m2 = v1[0], v2[0]
    for h in range(nh):
        r1 = rank_ref[0, h]
        n1 = jnp.zeros_like(r1)
        for a in range(k_top):
            n1 = jnp.where(r1 == float(a), n_of_rank[a][h:h + 1, :], n1)
        n1_ref[h] = n1
        e1_ref[h] = jnp.exp(s_ref[0, h] - m1[h:h + 1, :])
        r2_ref[h] = pltpu.bitcast(_bf(rank_ref[1, h]), jnp.uint32)
        e2_ref[h] = pltpu.bitcast(_bf(jnp.exp(s_ref[1, h] - m2[h:h + 1, :]) * scale2[h:h + 1, :]), jnp.uint32)


def _route(hnt, wq_t, keys, tb=256):
    t = hnt.shape[1]
    nh, nk = PEER_HEADS, PEER_NKEYS
    out = pl.BlockSpec((nh, nk, tb), lambda i: (0, 0, i))
    packed = pl.BlockSpec((nh, nk // 2, tb), lambda i: (0, 0, i))
    return pl.pallas_call(
        _route_kernel,
        grid=(t // tb,),
        in_specs=[pl.BlockSpec((D_MODEL, tb), lambda i: (0, i)), _full((2 * nh * PEER_QHALF, D_MODEL)),
                  _full((2, nh, nk, PEER_QHALF))],
        out_specs=[out, out, packed, packed],
        out_shape=[jax.ShapeDtypeStruct((nh, nk, t), F32)] * 2 + [jax.ShapeDtypeStruct((nh, nk // 2, t), jnp.uint32)] * 2,
        scratch_shapes=[pltpu.VMEM((2, nh, nk, tb), F32), pltpu.VMEM((2, nh, nk, tb), F32),
                        pltpu.VMEM((2, PEER_TOPK, nh, tb), F32), pltpu.VMEM((2, nh, nk, tb), F32)],
        compiler_params=_params(("arbitrary",)),
        name="peer_route",
    )(hnt, wq_t, keys)


EXPERT_SLABS = 8
EXPERT_PAIR = 2
PIPE_LAG = 2
ROW_SPLIT = 2


def _expert_kernel(hnt_ref, u_ref, vt_ref, n1_ref, e1_ref, r2_ref, e2_ref, x1_ref, gf_ref, o_ref, acc_ref, ht0, ht1,
                   act0, act1, rn_ref, re_ref, *, n_blocks, final_norm):
    s = pl.program_id(0)
    n_pairs = pl.num_programs(0) - PIPE_LAG
    nk = PEER_NKEYS
    tb = hnt_ref.shape[1]
    j2 = lax.rem(jnp.clip(s - 1, 0, n_pairs - 1), n_blocks)
    j3 = lax.rem(jnp.clip(s - PIPE_LAG, 0, n_pairs - 1), n_blocks)
    live = jnp.logical_and(s >= 1, s <= n_pairs).astype(F32)

    @pl.when(s == 0)
    def _():
        for r in (ht0, ht1, act0, act1):
            r[...] = jnp.zeros_like(r)

    @pl.when(j3 == 0)
    def _():
        acc_ref[...] = jnp.zeros_like(acc_ref)

    for h in range(PEER_HEADS):
        for k in range(EXPERT_SLABS):
            r = h * EXPERT_SLABS + k
            rn_ref[r:r + 1, :] = n1_ref[h, pl.ds(j2 * EXPERT_SLABS + k, 1), :]
            re_ref[r:r + 1, :] = e1_ref[h, pl.ds(j2 * EXPERT_SLABS + k, 1), :] * live

    def stages(ht_w, ht_r, act_w, act_r):
        half_w = 2 * LANES
        kc = 2 * LANES
        n_kc = D_MODEL // kc
        n_ec = (EXPERT_SLABS * nk) // kc

        assert n_kc == n_ec
        rws = (EXPERT_SLABS * nk) // ROW_SPLIT
        mws = D_MODEL // ROW_SPLIT
        hid = {}
        prj = {}

        def hidden_piece(half, c, rs):
            cols = slice(half * half_w, (half + 1) * half_w)
            t = _dot(u_ref[rs * rws:(rs + 1) * rws, c * kc:(c + 1) * kc], hnt_ref[c * kc:(c + 1) * kc, cols])
            hid[half, rs] = t if c == 0 else hid[half, rs] + t

        def project_piece(half, c, rs):
            cols = slice(half * half_w, (half + 1) * half_w)
            t = _dot(vt_ref[rs * mws:(rs + 1) * mws, c * kc:(c + 1) * kc], act_r[c * kc:(c + 1) * kc, cols])
            prj[half, rs] = t if c == 0 else prj[half, rs] + t

        gates = {}

        def gate_part(st, g0, heads, last):
            cols = slice(st * LANES, (st + 1) * LANES)
            for h in heads:
                r2 = pltpu.bitcast(r2_ref[h, :, cols], BF16)
                e2 = pltpu.bitcast(e2_ref[h, :, cols], BF16)
                for kk in range(EXPERT_PAIR):
                    r = h * EXPERT_SLABS + g0 + kk
                    n1 = _bf(rn_ref[r:r + 1, cols])
                    e1 = _bf(re_ref[r:r + 1, cols])
                    gh = jnp.where(r2 < n1, e2 * e1, jnp.zeros_like(e2))
                    gates[st, g0 + kk] = gh if h == 0 else gates[st, g0 + kk] + gh
            if last:
                for kk in range(EXPERT_PAIR):
                    rows = slice((g0 + kk) * nk, (g0 + kk + 1) * nk)
                    hk = ht_r[rows, cols]
                    act_w[rows, cols] = _bf(hk * (1.0 + lax.erf(hk * (2.0 ** -0.5)))) * gates[st, g0 + kk]

        hh = PEER_HEADS // 2
        vpu_work = [functools.partial(gate_part, st, g0, heads, last)
                    for st in range(tb // LANES) for g0 in range(0, EXPERT_SLABS, EXPERT_PAIR)
                    for heads, last in ((range(0, hh), False), (range(hh, PEER_HEADS), True))]
        mxu_work = [functools.partial(piece, half, c, rs)
                    for half in range(2) for c in range(n_kc) for rs in range(ROW_SPLIT)
                    for piece in (hidden_piece, project_piece)]
        assert len(mxu_work) == len(vpu_work)
        for m, v_ in zip(mxu_work, vpu_work):
            m()
            v_()
        for half in range(2):
            cols = slice(half * half_w, (half + 1) * half_w)
            for rs in range(ROW_SPLIT):
                ht_w[rs * rws:(rs + 1) * rws, cols] = hid[half, rs]
                acc_ref[rs * mws:(rs + 1) * mws, cols] += prj[half, rs]

    parity = lax.rem(s, 2)

    @pl.when(parity == 0)
    def _():
        stages(ht0, ht1, act1, act0)

    @pl.when(parity == 1)
    def _():
        stages(ht1, ht0, act0, act1)

    @pl.when(jnp.logical_and(j3 == n_blocks - 1, s >= PIPE_LAG))
    def _():
        y = acc_ref[...].T + x1_ref[...]
        if final_norm:
            y = y * lax.rsqrt(jnp.mean(y * y, axis=-1, keepdims=True) + NORM_EPS) * gf_ref[...]
        o_ref[...] = y


def _experts(hnt, u_b, vt_b, n1, e1, r2, e2, x1, gain_f, final_norm, tb=512):
    t = hnt.shape[1]
    nh, nk = PEER_HEADS, PEER_NKEYS
    eb = EXPERT_SLABS * nk
    n_blocks = PEER_NEXPERTS // eb
    n_pairs = (t // tb) * n_blocks

    def pair(lag):
        def f(s):
            p = jnp.clip(s - lag, 0, n_pairs - 1)
            return p // n_blocks, lax.rem(p, n_blocks)
        return f

    tok = lambda lag: (lambda s: pair(lag)(s)[0])
    blk = lambda lag: (lambda s: pair(lag)(s)[1])
    routed = lambda rows: pl.BlockSpec((nh, rows, tb), lambda s: (0, 0, tok(1)(s)))
    nrow = PEER_HEADS * EXPERT_SLABS
    return pl.pallas_call(
        functools.partial(_expert_kernel, n_blocks=n_blocks, final_norm=final_norm),
        grid=(n_pairs + PIPE_LAG,),
        in_specs=[pl.BlockSpec((D_MODEL, tb), lambda s: (0, tok(0)(s))),
                  pl.BlockSpec((eb, D_MODEL), lambda s: (blk(0)(s), 0)),
                  pl.BlockSpec((D_MODEL, eb), lambda s: (0, blk(PIPE_LAG)(s))),
                  routed(nk), routed(nk), routed(nk // 2), routed(nk // 2),
                  pl.BlockSpec((tb, D_MODEL), lambda s: (tok(PIPE_LAG)(s), 0)), _full((1, D_MODEL))],
        out_specs=pl.BlockSpec((tb, D_MODEL), lambda s: (tok(PIPE_LAG)(s), 0)),
        out_shape=jax.ShapeDtypeStruct((t, D_MODEL), F32),
        scratch_shapes=[pltpu.VMEM((D_MODEL, tb), F32), pltpu.VMEM((eb, tb), F32), pltpu.VMEM((eb, tb), F32),
                        pltpu.VMEM((eb, tb), BF16), pltpu.VMEM((eb, tb), BF16),
                        pltpu.VMEM((nrow, tb), F32), pltpu.VMEM((nrow, tb), F32)],
        compiler_params=_params(("arbitrary",)),
        name="peer_experts",
    )(hnt, u_b, vt_b, n1, e1, r2, e2, x1, gain_f)


def _pad_heads(w, heads, d, dp):
    lead = w.shape[:-1]
    w = w.reshape(*lead, heads, d)
    w = jnp.pad(w, [(0, 0)] * len(lead) + [(0, 0), (0, dp - d)])
    return w.reshape(*lead, heads * dp)


def _layer_params(l, w_in, w_out, rwkv_mu, rwkv_w_up, rwkv_w_bias, rwkv_a_up, rwkv_a_bias, rwkv_g_up, rwkv_k_k,
                  rwkv_k_a, rwkv_r_k, rwkv_ln_w, rwkv_ln_b, gla_gk_up, gla_gk_bias, gla_norm_w, peer_w_q,
                  peer_sub_keys, peer_u, peer_v):
    wi = w_in[l]
    g0 = RET_IN + RWKV_IN
    qk, vw = GLA_HEADS * GLA_DK, GLA_HEADS * GLA_DV
    gq = _pad_heads(wi[:, g0:g0 + qk], GLA_HEADS, GLA_DK, GLA_DKP)
    gkk = _pad_heads(wi[:, g0 + qk:g0 + 2 * qk], GLA_HEADS, GLA_DK, GLA_DKP)
    gv = _pad_heads(wi[:, g0 + 2 * qk:g0 + 2 * qk + vw], GLA_HEADS, GLA_DV, GLA_DVP)
    ggk = jnp.pad(wi[:, g0 + 2 * qk + vw:g0 + 2 * qk + vw + GLA_GATE_LORA], ((0, 0), (0, LANES - GLA_GATE_LORA)))
    gg = _pad_heads(wi[:, g0 + 2 * qk + vw + GLA_GATE_LORA:], GLA_HEADS, GLA_DV, GLA_DVP)
    w_in_p = _bf(jnp.concatenate([wi[:, :g0], gq, gkk, gv, ggk, gg], axis=1))
    wo = w_out[l]
    m0 = RET_W + RWKV_W
    wo_gla = jnp.pad(wo[m0:].reshape(GLA_HEADS, GLA_DV, D_MODEL), ((0, 0), (0, GLA_DVP - GLA_DV), (0, 0)))
    w_out_p = _bf(jnp.concatenate([wo[:m0], wo_gla.reshape(GLA_VW, D_MODEL)], axis=0))
    zeros = jnp.zeros((DECAY_LORA, RWKV_W), F32)
    rw = dict(
        mu=rwkv_mu[l][None, :],
        wwa=jnp.concatenate([jnp.concatenate([rwkv_w_up[l], zeros], axis=1),
                             jnp.concatenate([zeros, rwkv_a_up[l]], axis=1)], axis=0),
        w_bias=rwkv_w_bias[l][None, :], a_bias=rwkv_a_bias[l][None, :], g_up=_bf(rwkv_g_up[l]),
        k_k=rwkv_k_k[l][None, :], k_a=rwkv_k_a[l][None, :], r_k=rwkv_r_k[l].reshape(1, RWKV_W),
        ln_w=rwkv_ln_w[l][None, :], ln_b=rwkv_ln_b[l][None, :])
    gl = dict(
        gk_up=jnp.pad(_pad_heads(gla_gk_up[l], GLA_HEADS, GLA_DK, GLA_DKP), ((0, LANES - GLA_GATE_LORA), (0, 0))),
        gk_bias=_pad_heads(gla_gk_bias[l][None, :], GLA_HEADS, GLA_DK, GLA_DKP),
        norm_w=_pad_heads(gla_norm_w[l][None, :], GLA_HEADS, GLA_DV, GLA_DVP))
    wq = peer_w_q[l].reshape(D_MODEL, PEER_HEADS, 2, PEER_QHALF)
    wq_t = _bf(jnp.transpose(wq, (2, 1, 3, 0)).reshape(2 * PEER_HEADS * PEER_QHALF, D_MODEL))
    keys = _bf(jnp.transpose(peer_sub_keys[l], (1, 0, 2, 3)))
    return dict(w_in=w_in_p, w_out=w_out_p, rwkv=rw, gla=gl, wq_t=wq_t, keys=keys, u=_bf(peer_u[l]),
                vt=_bf(peer_v[l].T))


def _layer(x2, b, s, pr, gain_mix, gain_ffn, gain_final, final_norm):
    z_ret, z_rwkv, z_gla = _norm_proj(x2, gain_mix, pr["w_in"])
    o_ret = _retention(z_ret.reshape(b, s, RET_IN)).reshape(b * s, RET_W)
    o_rwkv = _rwkv(z_rwkv.reshape(b, s, RWKV_IN), pr["rwkv"]).reshape(b * s, RWKV_W)
    o_gla = _gla(z_gla.reshape(b, s, GLA_IN), pr["gla"]).reshape(b * s, GLA_VW)
    x1, hnt = _out_proj(x2, o_ret, o_rwkv, o_gla, pr["w_out"], gain_ffn)
    n1, e1, r2, e2 = _route(hnt, pr["wq_t"], pr["keys"])
    return _experts(hnt, pr["u"], pr["vt"], n1, e1, r2, e2, x1, gain_final, final_norm)


def kernel(x, norm_mix, norm_ffn, norm_final, w_in, w_out, rwkv_mu, rwkv_w_up, rwkv_w_bias, rwkv_a_up, rwkv_a_bias, rwkv_g_up, rwkv_k_k, rwkv_k_a, rwkv_r_k, rwkv_ln_w, rwkv_ln_b, gla_gk_up, gla_gk_bias, gla_norm_w, peer_w_q, peer_sub_keys, peer_u, peer_v):
    b, s, d = x.shape
    x2 = x.reshape(b * s, d)
    gain_final = norm_final[None, :]
    for l in range(DEPTH):
        pr = _layer_params(l, w_in, w_out, rwkv_mu, rwkv_w_up, rwkv_w_bias, rwkv_a_up, rwkv_a_bias, rwkv_g_up,
                           rwkv_k_k, rwkv_k_a, rwkv_r_k, rwkv_ln_w, rwkv_ln_b, gla_gk_up, gla_gk_bias, gla_norm_w,
                           peer_w_q, peer_sub_keys, peer_u, peer_v)
        x2 = _layer(x2, b, s, pr, norm_mix[l][None, :], norm_ffn[l][None, :], gain_final, l == DEPTH - 1)
    return x2.reshape(b, s, d)
```

```python
import functools

import numpy as np
import jax
import jax.numpy as jnp
from jax import lax
from jax.experimental import pallas as pl
from jax.experimental.pallas import tpu as pltpu

F32 = jnp.float32
BF16 = jnp.bfloat16
HI = lax.Precision.HIGHEST

D_MODEL = 1024
DEPTH = 2
NORM_EPS = 1e-6
HEAD_NORM_EPS = 1e-5
RET_HEADS, RET_DK, RET_DV, ROPE_BASE = 4, 64, 64, 10000.0
RET_W = RET_HEADS * RET_DK
RET_CHUNK = 128
RWKV_HEADS, RWKV_HEAD = 6, 64
RWKV_W = RWKV_HEADS * RWKV_HEAD
DECAY_LORA, AAA_LORA, GATE_LORA = 64, 64, 128
RWKV_GN_EPS = 64e-5
RWKV_CHUNK = 64
RWKV_SUB = 16
RWKV_IN = 3 * RWKV_W + DECAY_LORA + AAA_LORA + GATE_LORA
GLA_HEADS, GLA_DK, GLA_DV, GLA_GATE_LORA = 4, 48, 96, 16
GLA_GATE_NORMALIZER = 16.0
GLA_DKP, GLA_DVP = 64, 128
GLA_QW = GLA_HEADS * GLA_DKP
GLA_VW = GLA_HEADS * GLA_DVP
GLA_IN = 2 * GLA_QW + GLA_VW + 128 + GLA_VW
GLA_SUB = 16
GLA_BLOCK = 256
RET_IN = 4 * RET_W
D_INP = RET_IN + RWKV_IN + GLA_IN
D_MIXP = RET_W + RWKV_W + GLA_VW
PEER_HEADS, PEER_NKEYS, PEER_QHALF, PEER_TOPK = 8, 128, 128, 16
PEER_NEXPERTS = PEER_NKEYS * PEER_NKEYS
LANES = 128
SUBLANES = 8
VMEM_LIMIT = 56 * 1024 * 1024


def _params(sem):
    return pltpu.CompilerParams(dimension_semantics=sem, vmem_limit_bytes=VMEM_LIMIT)


def _dot(a, b, prec=None):
    return jnp.dot(a, b, precision=prec, preferred_element_type=F32)


def _dot_nt(a, b, prec=None):
    return lax.dot_general(a, b, (((1,), (1,)), ((), ())), precision=prec, preferred_element_type=F32)


def _dot_tn(a, b, prec=None):
    return lax.dot_general(a, b, (((0,), (0,)), ((), ())), precision=prec, preferred_element_type=F32)


def _bf(x):
    return x.astype(BF16)


def _iota(shape, dim):
    return lax.broadcasted_iota(jnp.int32, shape, dim)


def _same_block(shape, rblk, cblk):
    r = _iota(shape, 0) >> (rblk.bit_length() - 1)
    c = _iota(shape, 1) >> (cblk.bit_length() - 1)
    return r == c


def _full(shape):
    n = len(shape)
    return pl.BlockSpec(shape, lambda *_: (0,) * n)


def _block_ones(n, blk):
    i = np.arange(n) // blk
    return jnp.asarray((i[:, None] == i[None, :]).astype(np.float32))


def _tril_ones(n):
    return jnp.asarray(np.tril(np.ones((n, n), np.float32)))


def _norm_proj_kernel(x_ref, g_ref, w_ref, zr_ref, zk_ref, zg_ref):
    x = x_ref[...]
    y = x * lax.rsqrt(jnp.mean(x * x, axis=-1, keepdims=True) + NORM_EPS) * g_ref[...]
    yb = _bf(y)
    zr_ref[...] = _dot(yb, w_ref[:, 0:RET_IN])
    zk_ref[...] = _dot(yb, w_ref[:, RET_IN:RET_IN + RWKV_IN])
    zg_ref[...] = _dot(yb, w_ref[:, RET_IN + RWKV_IN:D_INP])


def _norm_proj(x2, gain, w_in_p, tm=512):
    t = x2.shape[0]
    return pl.pallas_call(
        _norm_proj_kernel,
        grid=(t // tm,),
        in_specs=[pl.BlockSpec((tm, D_MODEL), lambda i: (i, 0)), _full((1, D_MODEL)), _full((D_MODEL, D_INP))],
        out_specs=[pl.BlockSpec((tm, RET_IN), lambda i: (i, 0)), pl.BlockSpec((tm, RWKV_IN), lambda i: (i, 0)),
                   pl.BlockSpec((tm, GLA_IN), lambda i: (i, 0))],
        out_shape=[jax.ShapeDtypeStruct((t, RET_IN), F32), jax.ShapeDtypeStruct((t, RWKV_IN), F32),
                   jax.ShapeDtypeStruct((t, GLA_IN), F32)],
        compiler_params=_params(("arbitrary",)),
        name="norm_proj",
    )(x2, gain, w_in_p)


def _ret_kernel(z_ref, cos_ref, sin_ref, dmat_ref, qdec_ref, kdec_ref, cdec_ref, ones_ref, o_ref, st_ref):
    c = RET_CHUNK

    @pl.when(pl.program_id(1) == 0)
    def _():
        st_ref[...] = jnp.zeros_like(st_ref)

    z = z_ref[0]
    q, k, v, g = (z[:, i * RET_W:(i + 1) * RET_W] for i in range(4))
    cos, sin = cos_ref[...], sin_ref[...]
    first_half = (_iota((c, RET_W), 1) & (RET_DK - 1)) < RET_DK // 2

    def rot(t):
        return jnp.where(first_half, -pltpu.roll(t, RET_W - RET_DK // 2, 1), pltpu.roll(t, RET_DK // 2, 1))

    q = q * cos + rot(q) * sin
    k = (k * cos + rot(k) * sin) * (RET_DK ** -0.5)
    qd = q * qdec_ref[...]
    kd = k * kdec_ref[...]
    head0 = _iota((c, LANES), 1) < RET_DK
    bd = _same_block((LANES, LANES), RET_DV, RET_DK)
    outs = []
    for p in range(RET_HEADS // 2):
        sl = slice(p * LANES, (p + 1) * LANES)
        qp, kp, vp = q[:, sl], _bf(k[:, sl]), v[:, sl]
        o = None
        for hh in range(2):
            mh = head0 if hh == 0 else jnp.logical_not(head0)
            s = _dot_nt(_bf(jnp.where(mh, qp, 0.0)), kp) * dmat_ref[2 * p + hh]
            oh = _dot(_bf(s), _bf(jnp.where(mh, vp, 0.0)))
            o = oh if o is None else o + oh
        st = st_ref[p]
        o = o + _dot_nt(_bf(qd[:, sl]), _bf(st))
        upd = _dot_tn(_bf(vp), _bf(kd[:, sl]))
        st_ref[p] = st * cdec_ref[:, sl] + jnp.where(bd, upd, 0.0)
        outs.append(o)
    o = jnp.concatenate(outs, axis=1)
    ms = _dot(o * o, ones_ref[...], HI) * (1.0 / RET_DV)
    o = o * lax.rsqrt(ms + HEAD_NORM_EPS)
    o_ref[0] = o * (g * jax.nn.sigmoid(g))


def _retention_consts(s):
    c = RET_CHUNK
    log_gamma = np.log1p(-np.exp(np.linspace(np.log(1.0 / 32.0), np.log(1.0 / 512.0), RET_HEADS)))
    idx = np.arange(c, dtype=np.float64)
    diff = idx[:, None] - idx[None, :]
    dmat = np.where(diff >= 0, np.exp(log_gamma[:, None, None] * np.where(diff >= 0, diff, 0.0)), 0.0)
    lg_lane = np.repeat(log_gamma, RET_DK)[None, :]
    qdec = np.exp(lg_lane * (idx[:, None] + 1.0))
    kdec = np.exp(lg_lane * (c - 1.0 - idx[:, None]))
    cdec = np.exp(lg_lane * c)
    half = RET_DK // 2
    inv = ROPE_BASE ** (-np.arange(half, dtype=np.float64) / half)
    ang = np.arange(s, dtype=np.float64)[:, None] * inv[None, :]
    cos = np.tile(np.cos(ang), (1, 2 * RET_HEADS))
    sin = np.tile(np.sin(ang), (1, 2 * RET_HEADS))
    f = lambda a: jnp.asarray(a.astype(np.float32))
    return f(cos), f(sin), f(dmat), f(qdec), f(kdec), f(cdec)


def _retention(z_ret):
    b, s, _ = z_ret.shape
    c = RET_CHUNK
    cos, sin, dmat, qdec, kdec, cdec = _retention_consts(s)
    return pl.pallas_call(
        _ret_kernel,
        grid=(b, s // c),
        in_specs=[pl.BlockSpec((1, c, RET_IN), lambda i, j: (i, j, 0)),
                  pl.BlockSpec((c, RET_W), lambda i, j: (j, 0)), pl.BlockSpec((c, RET_W), lambda i, j: (j, 0)),
                  _full((RET_HEADS, c, c)), _full((c, RET_W)), _full((c, RET_W)), _full((1, RET_W)),
                  _full((RET_W, RET_W))],
        out_specs=pl.BlockSpec((1, c, RET_W), lambda i, j: (i, j, 0)),
        out_shape=jax.ShapeDtypeStruct((b, s, RET_W), F32),
        scratch_shapes=[pltpu.VMEM((RET_HEADS // 2, LANES, LANES), F32)],
        compiler_params=_params(("arbitrary", "arbitrary")),
        name="retention",
    )(z_ret, cos, sin, dmat, qdec, kdec, cdec, _block_ones(RET_W, RET_DV))


RWKV_BLOCK = 256
INV_PASSES = 3


def _split(x, parts):
    out = []
    for _ in range(parts - 1):
        h = _bf(x)
        out.append(h)
        x = x - h.astype(F32)
    out.append(_bf(x))
    return out


def _dot_exact_lhs(a_bf, x, parts):
    acc = None
    for p in _split(x, parts):
        t = _dot(a_bf, p)
        acc = t if acc is None else acc + t
    return acc


def _dot_exact_rhs(x, b_bf, parts):
    acc = None
    for p in _split(x, parts):
        t = _dot(p, b_bf)
        acc = t if acc is None else acc + t
    return acc


def _dot_f32(a, b, passes):
    if passes == 1:
        return _dot(_bf(a), _bf(b))
    if passes == 3:
        ah, al = _split(a, 2)
        bh, bl = _split(b, 2)
        return _dot(ah, bh) + (_dot(ah, bl) + _dot(al, bh))
    return _dot(a, b, HI)


def _rwkv_kernel(z_ref, mu_ref, wwa_ref, wb_ref, ab_ref, gup_ref, kk_ref, ka_ref, rk_ref, lnw_ref, lnb_ref,
                 ones_ref, ltri_ref, o_ref, st_ref, carry_ref):
    c = RWKV_CHUNK
    tb = RWKV_BLOCK
    w_ = RWKV_W

    @pl.when(pl.program_id(1) == 0)
    def _():
        st_ref[...] = jnp.zeros_like(st_ref)
        carry_ref[...] = jnp.zeros_like(carry_ref)

    f = z_ref[0]
    f_prev = jnp.where(_iota((tb, RWKV_IN), 0) == 0, carry_ref[...], pltpu.roll(f, 1, 0))
    carry_ref[...] = f[tb - 1:tb, :]
    f = f + (f_prev - f) * mu_ref[...]
    r, k, v = f[:, 0:w_], f[:, w_:2 * w_], f[:, 2 * w_:3 * w_]
    wa, gd = f[:, 3 * w_:3 * w_ + LANES], f[:, 3 * w_ + LANES:RWKV_IN]
    lane = _iota((tb, LANES), 1)
    proj = _dot_f32(jnp.where(lane < DECAY_LORA, jnp.tanh(wa), wa), wwa_ref[...], 3)
    wlog = -jax.nn.softplus(-(wb_ref[...] + proj[:, 0:w_])) - 0.5
    ld = -jnp.exp(wlog)
    a = jax.nn.sigmoid(ab_ref[...] + proj[:, w_:2 * w_])
    g = _dot(_bf(jax.nn.sigmoid(gd)), gup_ref[...])
    ones = ones_ref[...]
    kk = k * kk_ref[...]
    kk = kk * lax.rsqrt(jnp.maximum(_dot_exact_rhs(kk * kk, ones, 2), 1e-24))
    k = k * (1.0 + (a - 1.0) * ka_ref[...])
    cum = _dot_exact_lhs(ltri_ref[...], ld, 3)
    e_neg = jnp.exp(-cum)
    b = kk * a
    a_t_all = -kk * jnp.exp(cum - ld)
    b_t_all, k_t_all, r_t_all = b * e_neg, k * e_neg, r * jnp.exp(cum)

    n2 = 2 * c
    ri, ci = _iota((n2, n2), 0), _iota((n2, n2), 1)
    same_head = _same_block((n2, n2), c, c)
    strict = jnp.logical_and(same_head, ri > ci)
    incl = jnp.logical_and(same_head, ri >= ci)
    sub = _same_block((n2, n2), RWKV_SUB, RWKV_SUB)
    eye = (ri == ci).astype(F32)
    bd = _same_block((LANES, LANES), RWKV_HEAD, RWKV_HEAD)
    head0 = _iota((c, LANES), 1) < RWKV_HEAD
    mm = functools.partial(_dot_f32, passes=INV_PASSES)
    mm1 = functools.partial(_dot_f32, passes=1)

    def stack(x):
        return jnp.concatenate([jnp.where(head0, x, 0.0), jnp.where(head0, 0.0, x)], axis=0)

    def fold(x):
        return x[:c] + x[c:]

    units = [(ch, p) for ch in range(tb // c) for p in range(RWKV_HEADS // 2)]
    each = lambda fn, *lists: [fn(*args) for args in zip(*lists)]
    rows_of = lambda ch: slice(ch * c, (ch + 1) * c)
    lanes_of = lambda p: slice(p * LANES, (p + 1) * LANES)
    tile = lambda t: [stack(t[rows_of(ch), lanes_of(p)]) for ch, p in units]
    a_s, r_s, b_s, k_s, v_s = tile(a_t_all), tile(r_t_all), tile(b_t_all), tile(k_t_all), tile(v)
    cl = [cum[(ch + 1) * c - 1:(ch + 1) * c, lanes_of(p)] for ch, p in units]
    e_rem = [jnp.exp(cl_u - cum[rows_of(ch), lanes_of(p)]) for cl_u, (ch, p) in zip(cl, units)]
    b_h = [b[rows_of(ch), lanes_of(p)] * e for e, (ch, p) in zip(e_rem, units)]
    k_h = [k[rows_of(ch), lanes_of(p)] * e for e, (ch, p) in zip(e_rem, units)]
    v_p = [v[rows_of(ch), lanes_of(p)] for ch, p in units]
    big = each(lambda x, y, z, w: _dot_nt(_bf(jnp.concatenate([x, y], axis=0)), _bf(jnp.concatenate([z, w], axis=0))),
               a_s, r_s, b_s, k_s)
    a_ab = [jnp.where(strict, t[:n2, :n2], 0.0) for t in big]
    a_ak = [jnp.where(strict, t[:n2, n2:], 0.0) for t in big]
    a_rb = [jnp.where(incl, t[n2:, :n2], 0.0) for t in big]
    a_rk = [jnp.where(incl, t[n2:, n2:], 0.0) for t in big]
    d1 = [jnp.where(sub, t, 0.0) for t in a_ab]
    lo = each(lambda x, y: x - y, a_ab, d1)
    d2 = each(mm, d1, d1)
    t_d = each(lambda x, y: mm(eye + x, eye + y), d1, d2)
    d4 = each(mm, d2, d2)
    t_d = each(lambda x, y: mm(x, eye + y), t_d, d4)
    d8 = each(mm, d4, d4)
    t_d = each(lambda x, y: mm(x, eye + y), t_d, d8)
    e1 = each(mm1, t_d, lo)
    e2 = each(mm1, e1, e1)
    t_inv = each(lambda x, y: mm1(eye + x, eye + y), e1, e2)
    t_inv = each(mm1, t_inv, t_d)
    akv = each(lambda x, y: _dot(_bf(x), _bf(y)), a_ak, v_s)
    wy = each(lambda t, x, y: mm1(t, jnp.concatenate([x, y], axis=1)), t_inv, a_s, akv)
    qo = each(lambda x, y: _dot(_bf(x), _bf(y)), a_rb, wy)
    rkv = each(lambda x, y: _dot(_bf(x), _bf(y)), a_rk, v_s)
    w_p = [fold(t[:, :LANES]) for t in wy]
    y_p = [fold(t[:, LANES:]) for t in wy]
    q_p = each(lambda x, y: fold(x + y[:, :LANES]), r_s, qo)
    ol_p = each(lambda x, y: fold(x[:, LANES:] + y), qo, rkv)
    gmat = each(lambda x, y: jnp.where(bd, _dot_tn(_bf(x), _bf(y)), 0.0), w_p, b_h)
    nmat = each(lambda y, vv, bb, kk_: jnp.where(bd, _dot_tn(_bf(jnp.concatenate([y, vv], axis=0)),
                                                             _bf(jnp.concatenate([bb, kk_], axis=0))), 0.0),
                y_p, v_p, b_h, k_h)
    pre = {u: (q_p[i], ol_p[i], gmat[i], nmat[i], jnp.exp(cl[i])) for i, u in enumerate(units)}

    outs = [[None] * (RWKV_HEADS // 2) for _ in range(tb // c)]
    st = [st_ref[p] for p in range(RWKV_HEADS // 2)]
    for ch in range(tb // c):
        for p in range(RWKV_HEADS // 2):
            q_u, ol_u, g_u, n_u, pc_u = pre[ch, p]
            stb = _bf(st[p])
            outs[ch][p] = _dot_nt(_bf(q_u), stb) + ol_u
            st[p] = st[p] * pc_u + _dot(stb, _bf(g_u)) + n_u
    for p in range(RWKV_HEADS // 2):
        st_ref[p] = st[p]
    o = jnp.concatenate([jnp.concatenate(row, axis=1) for row in outs], axis=0)
    inv_n = 1.0 / RWKV_HEAD
    d = o - _dot_exact_rhs(o, ones, 2) * inv_n
    var = _dot_exact_rhs(d * d, ones, 2) * inv_n
    o = d * lax.rsqrt(var + RWKV_GN_EPS) * lnw_ref[...] + lnb_ref[...]
    o = o + _dot_exact_rhs(r * k * rk_ref[...], ones, 2) * v
    o_ref[0] = o * g


def _rwkv(z_rwkv, pr):
    b, s, _ = z_rwkv.shape
    tb = RWKV_BLOCK
    row = lambda n: _full((1, n))
    ltri = np.kron(np.eye(tb // RWKV_CHUNK, dtype=np.float32), np.tril(np.ones((RWKV_CHUNK, RWKV_CHUNK), np.float32)))
    return pl.pallas_call(
        _rwkv_kernel,
        grid=(b, s // tb),
        in_specs=[pl.BlockSpec((1, tb, RWKV_IN), lambda i, j: (i, j, 0)), row(RWKV_IN), _full((LANES, 2 * RWKV_W)),
                  row(RWKV_W), row(RWKV_W), _full((GATE_LORA, RWKV_W)), row(RWKV_W), row(RWKV_W), row(RWKV_W),
                  row(RWKV_W), row(RWKV_W), _full((RWKV_W, RWKV_W)), _full((tb, tb))],
        out_specs=pl.BlockSpec((1, tb, RWKV_W), lambda i, j: (i, j, 0)),
        out_shape=jax.ShapeDtypeStruct((b, s, RWKV_W), F32),
        scratch_shapes=[pltpu.VMEM((RWKV_HEADS // 2, LANES, LANES), F32), pltpu.VMEM((1, RWKV_IN), F32)],
        compiler_params=_params(("arbitrary", "arbitrary")),
        name="rwkv7",
    )(z_rwkv, pr["mu"], pr["wwa"], pr["w_bias"], pr["a_bias"], pr["g_up"], pr["k_k"], pr["k_a"], pr["r_k"],
      pr["ln_w"], pr["ln_b"], _bf(_block_ones(RWKV_W, RWKV_HEAD)), jnp.asarray(ltri, dtype=BF16))


def _gla_kernel(z_ref, gkup_ref, gkb_ref, nw_ref, ind_ref, ones_ref, ltri_ref, o_ref, st_ref, x_ref):
    n = GLA_SUB
    tb = GLA_BLOCK
    nsub = tb // n

    @pl.when(pl.program_id(1) == 0)
    def _():
        st_ref[...] = jnp.zeros_like(st_ref)

    z = z_ref[0]
    q_all = z[:, 0:GLA_QW] * (GLA_DK ** -0.5)
    k_all = z[:, GLA_QW:2 * GLA_QW]
    v_all = z[:, 2 * GLA_QW:2 * GLA_QW + GLA_VW]
    gkd = z[:, 2 * GLA_QW + GLA_VW:2 * GLA_QW + GLA_VW + LANES]
    g = z[:, 2 * GLA_QW + GLA_VW + LANES:GLA_IN]
    gk = jax.nn.log_sigmoid(_dot_f32(gkd, gkup_ref[...], 3) + gkb_ref[...]) * (1.0 / GLA_GATE_NORMALIZER)
    bc_all = _dot_exact_lhs(ltri_ref[...], gk, 3)
    bd = _same_block((GLA_VW, GLA_QW), GLA_DVP, GLA_DKP)
    rowi = _iota((n, GLA_QW), 0)
    subs = range(nsub)
    rows = [slice(sc * n, (sc + 1) * n) for sc in subs]
    q = [q_all[r] for r in rows]
    k = [k_all[r] for r in rows]
    v = [v_all[r] for r in rows]
    bc = [bc_all[r] for r in rows]
    bl = [t[n - 1:n, :] for t in bc]
    for sc in subs:
        for j in range(n):
            xj = q[sc] * jnp.exp(jnp.minimum(bc[sc] - bc[sc][j:j + 1, :], 0.0)) * k[sc][j:j + 1, :]
            x_ref[(sc * n + j) * n:(sc * n + j + 1) * n, :] = jnp.where(rowi >= j, xj, 0.0)
    x = x_ref[...]
    xh = _bf(x)
    xl = _bf(x - xh.astype(F32))
    ind = ind_ref[...]
    e = _dot(xh, ind) + _dot(xl, ind)
    o_intra = []
    for sc in subs:
        acc = None
        for j in range(n):
            t = e[(sc * n + j) * n:(sc * n + j + 1) * n, :] * v[sc][j:j + 1, :]
            acc = t if acc is None else acc + t
        o_intra.append(acc)
    upd = [jnp.where(bd, _dot_tn(_bf(v[sc]), _bf(k[sc] * jnp.exp(bl[sc] - bc[sc]))), 0.0) for sc in subs]
    qe = [_bf(q[sc] * jnp.exp(bc[sc])) for sc in subs]
    st = st_ref[...]
    outs = []
    for sc in subs:
        outs.append(_dot_nt(qe[sc], _bf(st)) + o_intra[sc])
        st = st * jnp.exp(bl[sc]) + upd[sc]
    st_ref[...] = st
    o = jnp.concatenate(outs, axis=0)
    ms = _dot_exact_rhs(o * o, ones_ref[...], 2) * (1.0 / GLA_DV)
    o_ref[0] = o * lax.rsqrt(ms + HEAD_NORM_EPS) * nw_ref[...] * (g * jax.nn.sigmoid(g))


def _gla(z_gla, pr):
    b, s, _ = z_gla.shape
    tb = GLA_BLOCK
    hq = np.arange(GLA_QW) // GLA_DKP
    hv = np.arange(GLA_VW) // GLA_DVP
    ind = jnp.asarray((hq[:, None] == hv[None, :]).astype(np.float32), dtype=BF16)
    ltri = np.kron(np.eye(tb // GLA_SUB, dtype=np.float32), np.tril(np.ones((GLA_SUB, GLA_SUB), np.float32)))
    return pl.pallas_call(
        _gla_kernel,
        grid=(b, s // tb),
        in_specs=[pl.BlockSpec((1, tb, GLA_IN), lambda i, j: (i, j, 0)), _full((LANES, GLA_QW)), _full((1, GLA_QW)),
                  _full((1, GLA_VW)), _full((GLA_QW, GLA_VW)), _full((GLA_VW, GLA_VW)), _full((tb, tb))],
        out_specs=pl.BlockSpec((1, tb, GLA_VW), lambda i, j: (i, j, 0)),
        out_shape=jax.ShapeDtypeStruct((b, s, GLA_VW), F32),
        scratch_shapes=[pltpu.VMEM((GLA_VW, GLA_QW), F32), pltpu.VMEM((tb * GLA_SUB, GLA_QW), F32)],
        compiler_params=_params(("arbitrary", "arbitrary")),
        name="gla",
    )(z_gla, pr["gk_up"], pr["gk_bias"], pr["norm_w"], ind, _bf(_block_ones(GLA_VW, GLA_DVP)),
      jnp.asarray(ltri, dtype=BF16))


def _out_proj_kernel(x_ref, a_ref, b_ref, c_ref, w_ref, g_ref, x1_ref, hnt_ref):
    acc = _dot(_bf(a_ref[...]), w_ref[0:RET_W, :])
    acc += _dot(_bf(b_ref[...]), w_ref[RET_W:RET_W + RWKV_W, :])
    acc += _dot(_bf(c_ref[...]), w_ref[RET_W + RWKV_W:D_MIXP, :])
    x1 = x_ref[...] + acc
    x1_ref[...] = x1
    hn = x1 * lax.rsqrt(jnp.mean(x1 * x1, axis=-1, keepdims=True) + NORM_EPS) * g_ref[...]
    hnt_ref[...] = _bf(hn.T)


def _out_proj(x2, o_ret, o_rwkv, o_gla, w_out_p, gain, tm=512):
    t = x2.shape[0]
    blk = lambda n: pl.BlockSpec((tm, n), lambda i: (i, 0))
    return pl.pallas_call(
        _out_proj_kernel,
        grid=(t // tm,),
        in_specs=[blk(D_MODEL), blk(RET_W), blk(RWKV_W), blk(GLA_VW), _full((D_MIXP, D_MODEL)), _full((1, D_MODEL))],
        out_specs=[blk(D_MODEL), pl.BlockSpec((D_MODEL, tm), lambda i: (0, i))],
        out_shape=[jax.ShapeDtypeStruct((t, D_MODEL), F32), jax.ShapeDtypeStruct((D_MODEL, t), BF16)],
        compiler_params=_params(("arbitrary",)),
        name="out_proj",
    )(x2, o_ret, o_rwkv, o_gla, w_out_p, gain)


def _staircase_pairs():
    k = PEER_TOPK
    return [(a, b) for a in range(k) for b in range(k) if (a + 1) * (b + 1) <= k]


def _tree_max(xs):
    xs = list(xs)
    while len(xs) > 1:
        nxt = [jnp.maximum(xs[2 * i], xs[2 * i + 1]) for i in range(len(xs) // 2)]
        if len(xs) % 2:
            nxt.append(xs[-1])
        xs = nxt
    return xs[0]


def _route_kernel(hnt_ref, wq_ref, keys_ref, n1_ref, e1_ref, r2_ref, e2_ref, work_ref, s_ref, vals_ref, rank_ref):
    nh, nk, k_top = PEER_HEADS, PEER_NKEYS, PEER_TOPK
    qt = _bf(_dot(wq_ref[...], hnt_ref[...]))
    for p in range(2):
        for h in range(nh):
            r0 = (p * nh + h) * PEER_QHALF
            s = _dot(keys_ref[p, h], qt[r0:r0 + PEER_QHALF, :])
            work_ref[p, h] = s
            s_ref[p, h] = s
    rank_ref[...] = jnp.full(rank_ref.shape, float(k_top), F32)
    groups = nk // SUBLANES

    def extract(r, carry):
        rf = jnp.asarray(r, F32)
        for p in range(2):
            for h in range(nh):
                tiles = [work_ref[p, h, i * SUBLANES:(i + 1) * SUBLANES, :] for i in range(groups)]
                m = _tree_max(tiles)
                for sh in (4, 2, 1):
                    m = jnp.maximum(m, pltpu.roll(m, sh, 0))
                vals_ref[p, r, pl.ds(h, 1), :] = m[0:1, :]
                for i in range(groups):
                    rows = slice(i * SUBLANES, (i + 1) * SUBLANES)
                    is_max = tiles[i] == m
                    work_ref[p, h, rows, :] = jnp.where(is_max, -jnp.inf, tiles[i])
                    rank_ref[p, h, rows, :] = jnp.where(is_max, rf, rank_ref[p, h, rows, :])
        return carry

    lax.fori_loop(0, k_top, extract, 0)

    v1 = [vals_ref[0, r] for r in range(k_top)]
    v2 = [vals_ref[1, r] for r in range(k_top)]
    cand = {(a, b): v1[a] + v2[b] for a, b in _staircase_pairs()}
    work = list(cand.values())
    tau = None
    for it in range(k_top):
        tau = _tree_max(work)
        if it + 1 < k_top:
            work = [jnp.where(w == tau, -jnp.inf, w) for w in work]
    top = cand[(0, 0)]
    z = None
    for c in cand.values():
        zi = jnp.where(c >= tau, jnp.exp(c - top), 0.0)
        z = zi if z is None else z + zi
    scale2 = 0.5 / z
    n_of_rank = []
    for a in range(k_top):
        cnt = None
        for b in range(k_top // (a + 1)):
            ge = (cand[(a, b)] >= tau).astype(F32)
            cnt = ge if cnt is None else cnt + ge
        n_of_rank.append(cnt)
    m1, m2 = v1[0], v2[0]
    for h in range(nh):
        r1 = rank_ref[0, h]
        n1 = jnp.zeros_like(r1)
        for a in range(k_top):
            n1 = jnp.where(r1 == float(a), n_of_rank[a][h:h + 1, :], n1)
        n1_ref[h] = n1
        e1_ref[h] = jnp.exp(s_ref[0, h] - m1[h:h + 1, :])
        r2_ref[h] = pltpu.bitcast(_bf(rank_ref[1, h]), jnp.uint32)
        e2_ref[h] = pltpu.bitcast(_bf(jnp.exp(s_ref[1, h] - m2[h:h + 1, :]) * scale2[h:h + 1, :]), jnp.uint32)


def _route(hnt, wq_t, keys, tb=256):
    t = hnt.shape[1]
    nh, nk = PEER_HEADS, PEER_NKEYS
    out = pl.BlockSpec((nh, nk, tb), lambda i: (0, 0, i))
    packed = pl.BlockSpec((nh, nk // 2, tb), lambda i: (0, 0, i))
    return pl.pallas_call(
        _route_kernel,
        grid=(t // tb,),
        in_specs=[pl.BlockSpec((D_MODEL, tb), lambda i: (0, i)), _full((2 * nh * PEER_QHALF, D_MODEL)),
                  _full((2, nh, nk, PEER_QHALF))],
        out_specs=[out, out, packed, packed],
        out_shape=[jax.ShapeDtypeStruct((nh, nk, t), F32)] * 2 + [jax.ShapeDtypeStruct((nh, nk // 2, t), jnp.uint32)] * 2,
        scratch_shapes=[pltpu.VMEM((2, nh, nk, tb), F32), pltpu.VMEM((2, nh, nk, tb), F32),
                        pltpu.VMEM((2, PEER_TOPK, nh, tb), F32), pltpu.VMEM((2, nh, nk, tb), F32)],
        compiler_params=_params(("arbitrary",)),
        name="peer_route",
    )(hnt, wq_t, keys)


EXPERT_SLABS = 8
EXPERT_PAIR = 2
PIPE_LAG = 2
ROW_SPLIT = 2


def _expert_kernel(hnt_ref, u_ref, vt_ref, n1_ref, e1_ref, r2_ref, e2_ref, x1_ref, gf_ref, o_ref, acc_ref, ht_ref,
                   act_ref, rn_ref, re_ref, actn_ref, *, n_blocks, final_norm):
    s = pl.program_id(0)
    n_pairs = pl.num_programs(0) - PIPE_LAG
    nk = PEER_NKEYS
    tb = hnt_ref.shape[1]
    j2 = lax.rem(jnp.clip(s - 1, 0, n_pairs - 1), n_blocks)
    j3 = lax.rem(jnp.clip(s - PIPE_LAG, 0, n_pairs - 1), n_blocks)
    live = jnp.logical_and(s >= 1, s <= n_pairs).astype(F32)

    @pl.when(s == 0)
    def _():
        ht_ref[...] = jnp.zeros_like(ht_ref)
        actn_ref[...] = jnp.zeros_like(actn_ref)

    @pl.when(j3 == 0)
    def _():
        acc_ref[...] = jnp.zeros_like(acc_ref)

    act_ref[...] = actn_ref[...]
    for h in range(PEER_HEADS):
        for k in range(EXPERT_SLABS):
            r = h * EXPERT_SLABS + k
            rn_ref[r:r + 1, :] = n1_ref[h, pl.ds(j2 * EXPERT_SLABS + k, 1), :]
            re_ref[r:r + 1, :] = e1_ref[h, pl.ds(j2 * EXPERT_SLABS + k, 1), :] * live

    def stages():
        half_w = 2 * LANES
        rws = (EXPERT_SLABS * nk) // ROW_SPLIT
        mws = D_MODEL // ROW_SPLIT
        hid = {}
        prj = {}

        def hidden_piece(half, rs):
            cols = slice(half * half_w, (half + 1) * half_w)
            hid[half, rs] = _dot(u_ref[rs * rws:(rs + 1) * rws, :], hnt_ref[:, cols])

        def project_piece(half, rs):
            cols = slice(half * half_w, (half + 1) * half_w)
            prj[half, rs] = _dot(vt_ref[rs * mws:(rs + 1) * mws, :], act_ref[:, cols])

        gates = {}

        def gate_part(st, g0, heads, last):
            cols = slice(st * LANES, (st + 1) * LANES)
            for h in heads:
                r2 = pltpu.bitcast(r2_ref[h, :, cols], BF16)
                e2 = pltpu.bitcast(e2_ref[h, :, cols], BF16)
                for kk in range(EXPERT_PAIR):
                    r = h * EXPERT_SLABS + g0 + kk
                    n1 = _bf(rn_ref[r:r + 1, cols])
                    e1 = _bf(re_ref[r:r + 1, cols])
                    gh = jnp.where(r2 < n1, e2 * e1, jnp.zeros_like(e2))
                    gates[st, g0 + kk] = gh if h == 0 else gates[st, g0 + kk] + gh
            if last:
                for kk in range(EXPERT_PAIR):
                    rows = slice((g0 + kk) * nk, (g0 + kk + 1) * nk)
                    hk = ht_ref[rows, cols]
                    actn_ref[rows, cols] = _bf(hk * (1.0 + lax.erf(hk * (2.0 ** -0.5)))) * gates[st, g0 + kk]

        hh = PEER_HEADS // 2
        mxu_work = [functools.partial(piece, half, rs) for half in range(2) for rs in range(ROW_SPLIT)
                    for piece in (hidden_piece, project_piece)]
        vpu_work = [functools.partial(gate_part, st, g0, heads, last)
                    for st in range(tb // LANES) for g0 in range(0, EXPERT_SLABS, EXPERT_PAIR)
                    for heads, last in ((range(0, hh), False), (range(hh, PEER_HEADS), True))]
        per = len(vpu_work) // len(mxu_work)
        for i, m in enumerate(mxu_work):
            m()
            for v_ in vpu_work[i * per:(i + 1) * per]:
                v_()
        for half in range(2):
            cols = slice(half * half_w, (half + 1) * half_w)
            for rs in range(ROW_SPLIT):
                ht_ref[rs * rws:(rs + 1) * rws, cols] = hid[half, rs]
                acc_ref[rs * mws:(rs + 1) * mws, cols] += prj[half, rs]

    stages()

    @pl.when(jnp.logical_and(j3 == n_blocks - 1, s >= PIPE_LAG))
    def _():
        y = acc_ref[...].T + x1_ref[...]
        if final_norm:
            y = y * lax.rsqrt(jnp.mean(y * y, axis=-1, keepdims=True) + NORM_EPS) * gf_ref[...]
        o_ref[...] = y


def _experts(hnt, u_b, vt_b, n1, e1, r2, e2, x1, gain_f, final_norm, tb=512):
    t = hnt.shape[1]
    nh, nk = PEER_HEADS, PEER_NKEYS
    eb = EXPERT_SLABS * nk
    n_blocks = PEER_NEXPERTS // eb
    n_pairs = (t // tb) * n_blocks

    def pair(lag):
        def f(s):
            p = jnp.clip(s - lag, 0, n_pairs - 1)
            return p // n_blocks, lax.rem(p, n_blocks)
        return f

    tok = lambda lag: (lambda s: pair(lag)(s)[0])
    blk = lambda lag: (lambda s: pair(lag)(s)[1])
    routed = lambda rows: pl.BlockSpec((nh, rows, tb), lambda s: (0, 0, tok(1)(s)))
    nrow = PEER_HEADS * EXPERT_SLABS
    return pl.pallas_call(
        functools.partial(_expert_kernel, n_blocks=n_blocks, final_norm=final_norm),
        grid=(n_pairs + PIPE_LAG,),
        in_specs=[pl.BlockSpec((D_MODEL, tb), lambda s: (0, tok(0)(s))),
                  pl.BlockSpec((eb, D_MODEL), lambda s: (blk(0)(s), 0)),
                  pl.BlockSpec((D_MODEL, eb), lambda s: (0, blk(PIPE_LAG)(s))),
                  routed(nk), routed(nk), routed(nk // 2), routed(nk // 2),
                  pl.BlockSpec((tb, D_MODEL), lambda s: (tok(PIPE_LAG)(s), 0)), _full((1, D_MODEL))],
        out_specs=pl.BlockSpec((tb, D_MODEL), lambda s: (tok(PIPE_LAG)(s), 0)),
        out_shape=jax.ShapeDtypeStruct((t, D_MODEL), F32),
        scratch_shapes=[pltpu.VMEM((D_MODEL, tb), F32), pltpu.VMEM((eb, tb), F32), pltpu.VMEM((eb, tb), BF16),
                        pltpu.VMEM((nrow, tb), F32), pltpu.VMEM((nrow, tb), F32), pltpu.VMEM((eb, tb), BF16)],
        compiler_params=_params(("arbitrary",)),
        name="peer_experts",
    )(hnt, u_b, vt_b, n1, e1, r2, e2, x1, gain_f)


def _pad_heads(w, heads, d, dp):
    lead = w.shape[:-1]
    w = w.reshape(*lead, heads, d)
    w = jnp.pad(w, [(0, 0)] * len(lead) + [(0, 0), (0, dp - d)])
    return w.reshape(*lead, heads * dp)


def _layer_params(l, w_in, w_out, rwkv_mu, rwkv_w_up, rwkv_w_bias, rwkv_a_up, rwkv_a_bias, rwkv_g_up, rwkv_k_k,
                  rwkv_k_a, rwkv_r_k, rwkv_ln_w, rwkv_ln_b, gla_gk_up, gla_gk_bias, gla_norm_w, peer_w_q,
                  peer_sub_keys, peer_u, peer_v):
    wi = w_in[l]
    g0 = RET_IN + RWKV_IN
    qk, vw = GLA_HEADS * GLA_DK, GLA_HEADS * GLA_DV
    gq = _pad_heads(wi[:, g0:g0 + qk], GLA_HEADS, GLA_DK, GLA_DKP)
    gkk = _pad_heads(wi[:, g0 + qk:g0 + 2 * qk], GLA_HEADS, GLA_DK, GLA_DKP)
    gv = _pad_heads(wi[:, g0 + 2 * qk:g0 + 2 * qk + vw], GLA_HEADS, GLA_DV, GLA_DVP)
    ggk = jnp.pad(wi[:, g0 + 2 * qk + vw:g0 + 2 * qk + vw + GLA_GATE_LORA], ((0, 0), (0, LANES - GLA_GATE_LORA)))
    gg = _pad_heads(wi[:, g0 + 2 * qk + vw + GLA_GATE_LORA:], GLA_HEADS, GLA_DV, GLA_DVP)
    w_in_p = _bf(jnp.concatenate([wi[:, :g0], gq, gkk, gv, ggk, gg], axis=1))
    wo = w_out[l]
    m0 = RET_W + RWKV_W
    wo_gla = jnp.pad(wo[m0:].reshape(GLA_HEADS, GLA_DV, D_MODEL), ((0, 0), (0, GLA_DVP - GLA_DV), (0, 0)))
    w_out_p = _bf(jnp.concatenate([wo[:m0], wo_gla.reshape(GLA_VW, D_MODEL)], axis=0))
    zeros = jnp.zeros((DECAY_LORA, RWKV_W), F32)
    rw = dict(
        mu=rwkv_mu[l][None, :],
        wwa=jnp.concatenate([jnp.concatenate([rwkv_w_up[l], zeros], axis=1),
                             jnp.concatenate([zeros, rwkv_a_up[l]], axis=1)], axis=0),
        w_bias=rwkv_w_bias[l][None, :], a_bias=rwkv_a_bias[l][None, :], g_up=_bf(rwkv_g_up[l]),
        k_k=rwkv_k_k[l][None, :], k_a=rwkv_k_a[l][None, :], r_k=rwkv_r_k[l].reshape(1, RWKV_W),
        ln_w=rwkv_ln_w[l][None, :], ln_b=rwkv_ln_b[l][None, :])
    gl = dict(
        gk_up=jnp.pad(_pad_heads(gla_gk_up[l], GLA_HEADS, GLA_DK, GLA_DKP), ((0, LANES - GLA_GATE_LORA), (0, 0))),
        gk_bias=_pad_heads(gla_gk_bias[l][None, :], GLA_HEADS, GLA_DK, GLA_DKP),
        norm_w=_pad_heads(gla_norm_w[l][None, :], GLA_HEADS, GLA_DV, GLA_DVP))
    wq = peer_w_q[l].reshape(D_MODEL, PEER_HEADS, 2, PEER_QHALF)
    wq_t = _bf(jnp.transpose(wq, (2, 1, 3, 0)).reshape(2 * PEER_HEADS * PEER_QHALF, D_MODEL))
    keys = _bf(jnp.transpose(peer_sub_keys[l], (1, 0, 2, 3)))
    return dict(w_in=w_in_p, w_out=w_out_p, rwkv=rw, gla=gl, wq_t=wq_t, keys=keys, u=_bf(peer_u[l]),
                vt=_bf(peer_v[l].T))


def _layer(x2, b, s, pr, gain_mix, gain_ffn, gain_final, final_norm):
    z_ret, z_rwkv, z_gla = _norm_proj(x2, gain_mix, pr["w_in"])
    o_ret = _retention(z_ret.reshape(b, s, RET_IN)).reshape(b * s, RET_W)
    o_rwkv = _rwkv(z_rwkv.reshape(b, s, RWKV_IN), pr["rwkv"]).reshape(b * s, RWKV_W)
    o_gla = _gla(z_gla.reshape(b, s, GLA_IN), pr["gla"]).reshape(b * s, GLA_VW)
    x1, hnt = _out_proj(x2, o_ret, o_rwkv, o_gla, pr["w_out"], gain_ffn)
    n1, e1, r2, e2 = _route(hnt, pr["wq_t"], pr["keys"])
    return _experts(hnt, pr["u"], pr["vt"], n1, e1, r2, e2, x1, gain_final, final_norm)


def kernel(x, norm_mix, norm_ffn, norm_final, w_in, w_out, rwkv_mu, rwkv_w_up, rwkv_w_bias, rwkv_a_up, rwkv_a_bias, rwkv_g_up, rwkv_k_k, rwkv_k_a, rwkv_r_k, rwkv_ln_w, rwkv_ln_b, gla_gk_up, gla_gk_bias, gla_norm_w, peer_w_q, peer_sub_keys, peer_u, peer_v):
    b, s, d = x.shape
    x2 = x.reshape(b * s, d)
    gain_final = norm_final[None, :]
    for l in range(DEPTH):
        pr = _layer_params(l, w_in, w_out, rwkv_mu, rwkv_w_up, rwkv_w_bias, rwkv_a_up, rwkv_a_bias, rwkv_g_up,
                           rwkv_k_k, rwkv_k_a, rwkv_r_k, rwkv_ln_w, rwkv_ln_b, gla_gk_up, gla_gk_bias, gla_norm_w,
                           peer_w_q, peer_sub_keys, peer_u, peer_v)
        x2 = _layer(x2, b, s, pr, norm_mix[l][None, :], norm_ffn[l][None, :], gain_final, l == DEPTH - 1)
    return x2.reshape(b, s, d)
```

```python
import functools

import numpy as np
import jax
import jax.numpy as jnp
from jax import lax
from jax.experimental import pallas as pl
from jax.experimental.pallas import tpu as pltpu

F32 = jnp.float32
BF16 = jnp.bfloat16
HI = lax.Precision.HIGHEST

D_MODEL = 1024
DEPTH = 2
NORM_EPS = 1e-6
HEAD_NORM_EPS = 1e-5
RET_HEADS, RET_DK, RET_DV, ROPE_BASE = 4, 64, 64, 10000.0
RET_W = RET_HEADS * RET_DK
RET_CHUNK = 128
RWKV_HEADS, RWKV_HEAD = 6, 64
RWKV_W = RWKV_HEADS * RWKV_HEAD
DECAY_LORA, AAA_LORA, GATE_LORA = 64, 64, 128
RWKV_GN_EPS = 64e-5
RWKV_CHUNK = 64
RWKV_SUB = 16
RWKV_IN = 3 * RWKV_W + DECAY_LORA + AAA_LORA + GATE_LORA
GLA_HEADS, GLA_DK, GLA_DV, GLA_GATE_LORA = 4, 48, 96, 16
GLA_GATE_NORMALIZER = 16.0
GLA_DKP, GLA_DVP = 64, 128
GLA_QW = GLA_HEADS * GLA_DKP
GLA_VW = GLA_HEADS * GLA_DVP
GLA_IN = 2 * GLA_QW + GLA_VW + 128 + GLA_VW
GLA_SUB = 16
GLA_BLOCK = 256
RET_IN = 4 * RET_W
D_INP = RET_IN + RWKV_IN + GLA_IN
D_MIXP = RET_W + RWKV_W + GLA_VW
PEER_HEADS, PEER_NKEYS, PEER_QHALF, PEER_TOPK = 8, 128, 128, 16
PEER_NEXPERTS = PEER_NKEYS * PEER_NKEYS
LANES = 128
SUBLANES = 8
VMEM_LIMIT = 56 * 1024 * 1024


def _params(sem):
    return pltpu.CompilerParams(dimension_semantics=sem, vmem_limit_bytes=VMEM_LIMIT)


def _dot(a, b, prec=None):
    return jnp.dot(a, b, precision=prec, preferred_element_type=F32)


def _dot_nt(a, b, prec=None):
    return lax.dot_general(a, b, (((1,), (1,)), ((), ())), precision=prec, preferred_element_type=F32)


def _dot_tn(a, b, prec=None):
    return lax.dot_general(a, b, (((0,), (0,)), ((), ())), precision=prec, preferred_element_type=F32)


def _bf(x):
    return x.astype(BF16)


def _iota(shape, dim):
    return lax.broadcasted_iota(jnp.int32, shape, dim)


def _same_block(shape, rblk, cblk):
    r = _iota(shape, 0) >> (rblk.bit_length() - 1)
    c = _iota(shape, 1) >> (cblk.bit_length() - 1)
    return r == c


def _full(shape):
    n = len(shape)
    return pl.BlockSpec(shape, lambda *_: (0,) * n)


def _block_ones(n, blk):
    i = np.arange(n) // blk
    return jnp.asarray((i[:, None] == i[None, :]).astype(np.float32))


def _tril_ones(n):
    return jnp.asarray(np.tril(np.ones((n, n), np.float32)))


def _norm_proj_kernel(x_ref, g_ref, w_ref, zr_ref, zk_ref, zg_ref):
    x = x_ref[...]
    y = x * lax.rsqrt(jnp.mean(x * x, axis=-1, keepdims=True) + NORM_EPS) * g_ref[...]
    yb = _bf(y)
    zr_ref[...] = _dot(yb, w_ref[:, 0:RET_IN])
    zk_ref[...] = _dot(yb, w_ref[:, RET_IN:RET_IN + RWKV_IN])
    zg_ref[...] = _dot(yb, w_ref[:, RET_IN + RWKV_IN:D_INP])


def _norm_proj(x2, gain, w_in_p, tm=512):
    t = x2.shape[0]
    return pl.pallas_call(
        _norm_proj_kernel,
        grid=(t // tm,),
        in_specs=[pl.BlockSpec((tm, D_MODEL), lambda i: (i, 0)), _full((1, D_MODEL)), _full((D_MODEL, D_INP))],
        out_specs=[pl.BlockSpec((tm, RET_IN), lambda i: (i, 0)), pl.BlockSpec((tm, RWKV_IN), lambda i: (i, 0)),
                   pl.BlockSpec((tm, GLA_IN), lambda i: (i, 0))],
        out_shape=[jax.ShapeDtypeStruct((t, RET_IN), F32), jax.ShapeDtypeStruct((t, RWKV_IN), F32),
                   jax.ShapeDtypeStruct((t, GLA_IN), F32)],
        compiler_params=_params(("arbitrary",)),
        name="norm_proj",
    )(x2, gain, w_in_p)


RET_BLOCK = 512


def _ret_kernel(z_ref, cos_ref, sin_ref, dmat_ref, qdec_ref, kdec_ref, cdec_ref, ones_ref, o_ref, st_ref):
    c = RET_CHUNK
    tb = RET_BLOCK

    @pl.when(pl.program_id(1) == 0)
    def _():
        st_ref[...] = jnp.zeros_like(st_ref)

    z = z_ref[0]
    q, k, v, g = (z[:, i * RET_W:(i + 1) * RET_W] for i in range(4))
    cos, sin = cos_ref[...], sin_ref[...]
    first_half = (_iota((tb, RET_W), 1) & (RET_DK - 1)) < RET_DK // 2

    def rot(t):
        return jnp.where(first_half, -pltpu.roll(t, RET_W - RET_DK // 2, 1), pltpu.roll(t, RET_DK // 2, 1))

    q = q * cos + rot(q) * sin
    k = (k * cos + rot(k) * sin) * (RET_DK ** -0.5)
    qd = q * qdec_ref[...]
    kd = k * kdec_ref[...]
    head0 = _iota((c, LANES), 1) < RET_DK
    bd = _same_block((LANES, LANES), RET_DV, RET_DK)
    units = [(ch, p) for ch in range(tb // c) for p in range(RET_HEADS // 2)]
    tile = lambda t, u: t[u[0] * c:(u[0] + 1) * c, u[1] * LANES:(u[1] + 1) * LANES]
    kb = [_bf(tile(k, u)) for u in units]
    intra = []
    for hh in range(2):
        mh = head0 if hh == 0 else jnp.logical_not(head0)
        sc = [_dot_nt(_bf(jnp.where(mh, tile(q, u), 0.0)), kb[i]) * dmat_ref[2 * u[1] + hh] for i, u in enumerate(units)]
        intra.append([_dot(_bf(s_), _bf(jnp.where(mh, tile(v, u), 0.0))) for s_, u in zip(sc, units)])
    upd = [jnp.where(bd, _dot_tn(_bf(tile(v, u)), _bf(tile(kd, u))), 0.0) for u in units]
    qdb = [_bf(tile(qd, u)) for u in units]
    st = [st_ref[p] for p in range(RET_HEADS // 2)]
    rows = []
    for ch in range(tb // c):
        row = []
        for p in range(RET_HEADS // 2):
            i = units.index((ch, p))
            row.append(intra[0][i] + intra[1][i] + _dot_nt(qdb[i], _bf(st[p])))
            st[p] = st[p] * cdec_ref[:, p * LANES:(p + 1) * LANES] + upd[i]
        rows.append(jnp.concatenate(row, axis=1))
    for p in range(RET_HEADS // 2):
        st_ref[p] = st[p]
    o = jnp.concatenate(rows, axis=0)
    ms = _dot_exact_rhs(o * o, ones_ref[...], 2) * (1.0 / RET_DV)
    o = o * lax.rsqrt(ms + HEAD_NORM_EPS)
    o_ref[0] = o * (g * jax.nn.sigmoid(g))


def _retention_consts(s):
    c = RET_CHUNK
    reps = RET_BLOCK // c
    log_gamma = np.log1p(-np.exp(np.linspace(np.log(1.0 / 32.0), np.log(1.0 / 512.0), RET_HEADS)))
    idx = np.arange(c, dtype=np.float64)
    diff = idx[:, None] - idx[None, :]
    dmat = np.where(diff >= 0, np.exp(log_gamma[:, None, None] * np.where(diff >= 0, diff, 0.0)), 0.0)
    lg_lane = np.repeat(log_gamma, RET_DK)[None, :]
    qdec = np.tile(np.exp(lg_lane * (idx[:, None] + 1.0)), (reps, 1))
    kdec = np.tile(np.exp(lg_lane * (c - 1.0 - idx[:, None])), (reps, 1))
    cdec = np.exp(lg_lane * c)
    half = RET_DK // 2
    inv = ROPE_BASE ** (-np.arange(half, dtype=np.float64) / half)
    ang = np.arange(s, dtype=np.float64)[:, None] * inv[None, :]
    cos = np.tile(np.cos(ang), (1, 2 * RET_HEADS))
    sin = np.tile(np.sin(ang), (1, 2 * RET_HEADS))
    f = lambda a: jnp.asarray(a.astype(np.float32))
    return f(cos), f(sin), f(dmat), f(qdec), f(kdec), f(cdec)


def _retention(z_ret):
    b, s, _ = z_ret.shape
    c = RET_CHUNK
    tb = RET_BLOCK
    cos, sin, dmat, qdec, kdec, cdec = _retention_consts(s)
    return pl.pallas_call(
        _ret_kernel,
        grid=(b, s // tb),
        in_specs=[pl.BlockSpec((1, tb, RET_IN), lambda i, j: (i, j, 0)),
                  pl.BlockSpec((tb, RET_W), lambda i, j: (j, 0)), pl.BlockSpec((tb, RET_W), lambda i, j: (j, 0)),
                  _full((RET_HEADS, c, c)), _full((tb, RET_W)), _full((tb, RET_W)), _full((1, RET_W)),
                  _full((RET_W, RET_W))],
        out_specs=pl.BlockSpec((1, tb, RET_W), lambda i, j: (i, j, 0)),
        out_shape=jax.ShapeDtypeStruct((b, s, RET_W), F32),
        scratch_shapes=[pltpu.VMEM((RET_HEADS // 2, LANES, LANES), F32)],
        compiler_params=_params(("arbitrary", "arbitrary")),
        name="retention",
    )(z_ret, cos, sin, dmat, qdec, kdec, cdec, _bf(_block_ones(RET_W, RET_DV)))


RWKV_BLOCK = 256
INV_PASSES = 3


def _split(x, parts):
    out = []
    for _ in range(parts - 1):
        h = _bf(x)
        out.append(h)
        x = x - h.astype(F32)
    out.append(_bf(x))
    return out


def _dot_exact_lhs(a_bf, x, parts):
    acc = None
    for p in _split(x, parts):
        t = _dot(a_bf, p)
        acc = t if acc is None else acc + t
    return acc


def _dot_exact_rhs(x, b_bf, parts):
    acc = None
    for p in _split(x, parts):
        t = _dot(p, b_bf)
        acc = t if acc is None else acc + t
    return acc


def _dot_f32(a, b, passes):
    if passes == 1:
        return _dot(_bf(a), _bf(b))
    if passes == 3:
        ah, al = _split(a, 2)
        bh, bl = _split(b, 2)
        return _dot(ah, bh) + (_dot(ah, bl) + _dot(al, bh))
    return _dot(a, b, HI)


def _rwkv_kernel(z_ref, mu_ref, wwa_ref, wb_ref, ab_ref, gup_ref, kk_ref, ka_ref, rk_ref, lnw_ref, lnb_ref,
                 ones_ref, ltri_ref, o_ref, st_ref, carry_ref):
    c = RWKV_CHUNK
    tb = RWKV_BLOCK
    w_ = RWKV_W

    @pl.when(pl.program_id(1) == 0)
    def _():
        st_ref[...] = jnp.zeros_like(st_ref)
        carry_ref[...] = jnp.zeros_like(carry_ref)

    f = z_ref[0]
    f_prev = jnp.where(_iota((tb, RWKV_IN), 0) == 0, carry_ref[...], pltpu.roll(f, 1, 0))
    carry_ref[...] = f[tb - 1:tb, :]
    f = f + (f_prev - f) * mu_ref[...]
    r, k, v = f[:, 0:w_], f[:, w_:2 * w_], f[:, 2 * w_:3 * w_]
    wa, gd = f[:, 3 * w_:3 * w_ + LANES], f[:, 3 * w_ + LANES:RWKV_IN]
    lane = _iota((tb, LANES), 1)
    proj = _dot_f32(jnp.where(lane < DECAY_LORA, jnp.tanh(wa), wa), wwa_ref[...], 3)
    wlog = -jax.nn.softplus(-(wb_ref[...] + proj[:, 0:w_])) - 0.5
    ld = -jnp.exp(wlog)
    a = jax.nn.sigmoid(ab_ref[...] + proj[:, w_:2 * w_])
    g = _dot(_bf(jax.nn.sigmoid(gd)), gup_ref[...])
    ones = ones_ref[...]
    kk = k * kk_ref[...]
    kk = kk * lax.rsqrt(jnp.maximum(_dot_exact_rhs(kk * kk, ones, 2), 1e-24))
    k = k * (1.0 + (a - 1.0) * ka_ref[...])
    cum = _dot_exact_lhs(ltri_ref[...], ld, 3)
    e_neg = jnp.exp(-cum)
    b = kk * a
    a_t_all = -kk * jnp.exp(cum - ld)
    b_t_all, k_t_all, r_t_all = b * e_neg, k * e_neg, r * jnp.exp(cum)

    n2 = 2 * c
    ri, ci = _iota((n2, n2), 0), _iota((n2, n2), 1)
    same_head = _same_block((n2, n2), c, c)
    strict = jnp.logical_and(same_head, ri > ci)
    incl = jnp.logical_and(same_head, ri >= ci)
    sub = _same_block((n2, n2), RWKV_SUB, RWKV_SUB)
    eye = (ri == ci).astype(F32)
    bd = _same_block((LANES, LANES), RWKV_HEAD, RWKV_HEAD)
    head0 = _iota((c, LANES), 1) < RWKV_HEAD
    mm = functools.partial(_dot_f32, passes=INV_PASSES)
    mm1 = functools.partial(_dot_f32, passes=1)

    def stack(x):
        return jnp.concatenate([jnp.where(head0, x, 0.0), jnp.where(head0, 0.0, x)], axis=0)

    def fold(x):
        return x[:c] + x[c:]

    units = [(ch, p) for ch in range(tb // c) for p in range(RWKV_HEADS // 2)]
    each = lambda fn, *lists: [fn(*args) for args in zip(*lists)]
    rows_of = lambda ch: slice(ch * c, (ch + 1) * c)
    lanes_of = lambda p: slice(p * LANES, (p + 1) * LANES)
    tile = lambda t: [stack(t[rows_of(ch), lanes_of(p)]) for ch, p in units]
    a_s, r_s, b_s, k_s, v_s = tile(a_t_all), tile(r_t_all), tile(b_t_all), tile(k_t_all), tile(v)
    cl = [cum[(ch + 1) * c - 1:(ch + 1) * c, lanes_of(p)] for ch, p in units]
    e_rem = [jnp.exp(cl_u - cum[rows_of(ch), lanes_of(p)]) for cl_u, (ch, p) in zip(cl, units)]
    b_h = [b[rows_of(ch), lanes_of(p)] * e for e, (ch, p) in zip(e_rem, units)]
    k_h = [k[rows_of(ch), lanes_of(p)] * e for e, (ch, p) in zip(e_rem, units)]
    v_p = [v[rows_of(ch), lanes_of(p)] for ch, p in units]
    big = each(lambda x, y, z, w: _dot_nt(_bf(jnp.concatenate([x, y], axis=0)), _bf(jnp.concatenate([z, w], axis=0))),
               a_s, r_s, b_s, k_s)
    a_ab = [jnp.where(strict, t[:n2, :n2], 0.0) for t in big]
    a_ak = [jnp.where(strict, t[:n2, n2:], 0.0) for t in big]
    a_rb = [jnp.where(incl, t[n2:, :n2], 0.0) for t in big]
    a_rk = [jnp.where(incl, t[n2:, n2:], 0.0) for t in big]
    d1 = [jnp.where(sub, t, 0.0) for t in a_ab]
    lo = each(lambda x, y: x - y, a_ab, d1)
    d2 = each(mm, d1, d1)
    t_d = each(lambda x, y: mm1(eye + x, eye + y), d1, d2)
    d4 = each(mm, d2, d2)
    t_d = each(lambda x, y: mm1(x, eye + y), t_d, d4)
    d8 = each(mm, d4, d4)
    t_d = each(lambda x, y: mm1(x, eye + y), t_d, d8)
    e1 = each(mm1, t_d, lo)
    e2 = each(mm1, e1, e1)
    t_inv = each(lambda x, y: mm1(eye + x, eye + y), e1, e2)
    t_inv = each(mm1, t_inv, t_d)
    akv = each(lambda x, y: _dot(_bf(x), _bf(y)), a_ak, v_s)
    wy = each(lambda t, x, y: mm1(t, jnp.concatenate([x, y], axis=1)), t_inv, a_s, akv)
    qo = each(lambda x, y: _dot(_bf(x), _bf(y)), a_rb, wy)
    rkv = each(lambda x, y: _dot(_bf(x), _bf(y)), a_rk, v_s)
    w_p = [fold(t[:, :LANES]) for t in wy]
    y_p = [fold(t[:, LANES:]) for t in wy]
    q_p = each(lambda x, y: fold(x + y[:, :LANES]), r_s, qo)
    ol_p = each(lambda x, y: fold(x[:, LANES:] + y), qo, rkv)
    gmat = each(lambda x, y: jnp.where(bd, _dot_tn(_bf(x), _bf(y)), 0.0), w_p, b_h)
    nmat = each(lambda y, vv, bb, kk_: jnp.where(bd, _dot_tn(_bf(jnp.concatenate([y, vv], axis=0)),
                                                             _bf(jnp.concatenate([bb, kk_], axis=0))), 0.0),
                y_p, v_p, b_h, k_h)
    pre = {u: (q_p[i], ol_p[i], gmat[i], nmat[i], jnp.exp(cl[i])) for i, u in enumerate(units)}

    outs = [[None] * (RWKV_HEADS // 2) for _ in range(tb // c)]
    st = [st_ref[p] for p in range(RWKV_HEADS // 2)]
    for ch in range(tb // c):
        for p in range(RWKV_HEADS // 2):
            q_u, ol_u, g_u, n_u, pc_u = pre[ch, p]
            stb = _bf(st[p])
            outs[ch][p] = _dot_nt(_bf(q_u), stb) + ol_u
            st[p] = st[p] * pc_u + _dot(stb, _bf(g_u)) + n_u
    for p in range(RWKV_HEADS // 2):
        st_ref[p] = st[p]
    o = jnp.concatenate([jnp.concatenate(row, axis=1) for row in outs], axis=0)
    inv_n = 1.0 / RWKV_HEAD
    d = o - _dot_exact_rhs(o, ones, 2) * inv_n
    var = _dot_exact_rhs(d * d, ones, 2) * inv_n
    o = d * lax.rsqrt(var + RWKV_GN_EPS) * lnw_ref[...] + lnb_ref[...]
    o = o + _dot_exact_rhs(r * k * rk_ref[...], ones, 2) * v
    o_ref[0] = o * g


def _rwkv(z_rwkv, pr):
    b, s, _ = z_rwkv.shape
    tb = RWKV_BLOCK
    row = lambda n: _full((1, n))
    ltri = np.kron(np.eye(tb // RWKV_CHUNK, dtype=np.float32), np.tril(np.ones((RWKV_CHUNK, RWKV_CHUNK), np.float32)))
    return pl.pallas_call(
        _rwkv_kernel,
        grid=(b, s // tb),
        in_specs=[pl.BlockSpec((1, tb, RWKV_IN), lambda i, j: (i, j, 0)), row(RWKV_IN), _full((LANES, 2 * RWKV_W)),
                  row(RWKV_W), row(RWKV_W), _full((GATE_LORA, RWKV_W)), row(RWKV_W), row(RWKV_W), row(RWKV_W),
                  row(RWKV_W), row(RWKV_W), _full((RWKV_W, RWKV_W)), _full((tb, tb))],
        out_specs=pl.BlockSpec((1, tb, RWKV_W), lambda i, j: (i, j, 0)),
        out_shape=jax.ShapeDtypeStruct((b, s, RWKV_W), F32),
        scratch_shapes=[pltpu.VMEM((RWKV_HEADS // 2, LANES, LANES), F32), pltpu.VMEM((1, RWKV_IN), F32)],
        compiler_params=_params(("arbitrary", "arbitrary")),
        name="rwkv7",
    )(z_rwkv, pr["mu"], pr["wwa"], pr["w_bias"], pr["a_bias"], pr["g_up"], pr["k_k"], pr["k_a"], pr["r_k"],
      pr["ln_w"], pr["ln_b"], _bf(_block_ones(RWKV_W, RWKV_HEAD)), jnp.asarray(ltri, dtype=BF16))


def _gla_kernel(z_ref, gkup_ref, gkb_ref, nw_ref, ind_ref, ones_ref, ltri_ref, o_ref, st_ref, x_ref):
    n = GLA_SUB
    tb = GLA_BLOCK
    nsub = tb // n

    @pl.when(pl.program_id(1) == 0)
    def _():
        st_ref[...] = jnp.zeros_like(st_ref)

    z = z_ref[0]
    q_all = z[:, 0:GLA_QW] * (GLA_DK ** -0.5)
    k_all = z[:, GLA_QW:2 * GLA_QW]
    v_all = z[:, 2 * GLA_QW:2 * GLA_QW + GLA_VW]
    gkd = z[:, 2 * GLA_QW + GLA_VW:2 * GLA_QW + GLA_VW + LANES]
    g = z[:, 2 * GLA_QW + GLA_VW + LANES:GLA_IN]
    gk = jax.nn.log_sigmoid(_dot_f32(gkd, gkup_ref[...], 3) + gkb_ref[...]) * (1.0 / GLA_GATE_NORMALIZER)
    bc_all = _dot_exact_lhs(ltri_ref[...], gk, 3)
    bd = _same_block((GLA_VW, GLA_QW), GLA_DVP, GLA_DKP)
    rowi = _iota((n, GLA_QW), 0)
    subs = range(nsub)
    rows = [slice(sc * n, (sc + 1) * n) for sc in subs]
    q = [q_all[r] for r in rows]
    k = [k_all[r] for r in rows]
    v = [v_all[r] for r in rows]
    bc = [bc_all[r] for r in rows]
    bl = [t[n - 1:n, :] for t in bc]
    for sc in subs:
        for j in range(n):
            xj = q[sc] * jnp.exp(jnp.minimum(bc[sc] - bc[sc][j:j + 1, :], 0.0)) * k[sc][j:j + 1, :]
            x_ref[(sc * n + j) * n:(sc * n + j + 1) * n, :] = jnp.where(rowi >= j, xj, 0.0)
    x = x_ref[...]
    xh = _bf(x)
    xl = _bf(x - xh.astype(F32))
    ind = ind_ref[...]
    e = _dot(xh, ind) + _dot(xl, ind)
    o_intra = []
    for sc in subs:
        acc = None
        for j in range(n):
            t = e[(sc * n + j) * n:(sc * n + j + 1) * n, :] * v[sc][j:j + 1, :]
            acc = t if acc is None else acc + t
        o_intra.append(acc)
    upd = [jnp.where(bd, _dot_tn(_bf(v[sc]), _bf(k[sc] * jnp.exp(bl[sc] - bc[sc]))), 0.0) for sc in subs]
    qe = [_bf(q[sc] * jnp.exp(bc[sc])) for sc in subs]
    st = st_ref[...]
    outs = []
    for sc in subs:
        outs.append(_dot_nt(qe[sc], _bf(st)) + o_intra[sc])
        st = st * jnp.exp(bl[sc]) + upd[sc]
    st_ref[...] = st
    o = jnp.concatenate(outs, axis=0)
    ms = _dot_exact_rhs(o * o, ones_ref[...], 2) * (1.0 / GLA_DV)
    o_ref[0] = o * lax.rsqrt(ms + HEAD_NORM_EPS) * nw_ref[...] * (g * jax.nn.sigmoid(g))


def _gla(z_gla, pr):
    b, s, _ = z_gla.shape
    tb = GLA_BLOCK
    hq = np.arange(GLA_QW) // GLA_DKP
    hv = np.arange(GLA_VW) // GLA_DVP
    ind = jnp.asarray((hq[:, None] == hv[None, :]).astype(np.float32), dtype=BF16)
    ltri = np.kron(np.eye(tb // GLA_SUB, dtype=np.float32), np.tril(np.ones((GLA_SUB, GLA_SUB), np.float32)))
    return pl.pallas_call(
        _gla_kernel,
        grid=(b, s // tb),
        in_specs=[pl.BlockSpec((1, tb, GLA_IN), lambda i, j: (i, j, 0)), _full((LANES, GLA_QW)), _full((1, GLA_QW)),
                  _full((1, GLA_VW)), _full((GLA_QW, GLA_VW)), _full((GLA_VW, GLA_VW)), _full((tb, tb))],
        out_specs=pl.BlockSpec((1, tb, GLA_VW), lambda i, j: (i, j, 0)),
        out_shape=jax.ShapeDtypeStruct((b, s, GLA_VW), F32),
        scratch_shapes=[pltpu.VMEM((GLA_VW, GLA_QW), F32), pltpu.VMEM((tb * GLA_SUB, GLA_QW), F32)],
        compiler_params=_params(("arbitrary", "arbitrary")),
        name="gla",
    )(z_gla, pr["gk_up"], pr["gk_bias"], pr["norm_w"], ind, _bf(_block_ones(GLA_VW, GLA_DVP)),
      jnp.asarray(ltri, dtype=BF16))


def _out_proj_kernel(x_ref, a_ref, b_ref, c_ref, w_ref, g_ref, x1_ref, hnt_ref):
    acc = _dot(_bf(a_ref[...]), w_ref[0:RET_W, :])
    acc += _dot(_bf(b_ref[...]), w_ref[RET_W:RET_W + RWKV_W, :])
    acc += _dot(_bf(c_ref[...]), w_ref[RET_W + RWKV_W:D_MIXP, :])
    x1 = x_ref[...] + acc
    x1_ref[...] = x1
    hn = x1 * lax.rsqrt(jnp.mean(x1 * x1, axis=-1, keepdims=True) + NORM_EPS) * g_ref[...]
    hnt_ref[...] = _bf(hn.T)


def _out_proj(x2, o_ret, o_rwkv, o_gla, w_out_p, gain, tm=512):
    t = x2.shape[0]
    blk = lambda n: pl.BlockSpec((tm, n), lambda i: (i, 0))
    return pl.pallas_call(
        _out_proj_kernel,
        grid=(t // tm,),
        in_specs=[blk(D_MODEL), blk(RET_W), blk(RWKV_W), blk(GLA_VW), _full((D_MIXP, D_MODEL)), _full((1, D_MODEL))],
        out_specs=[blk(D_MODEL), pl.BlockSpec((D_MODEL, tm), lambda i: (0, i))],
        out_shape=[jax.ShapeDtypeStruct((t, D_MODEL), F32), jax.ShapeDtypeStruct((D_MODEL, t), BF16)],
        compiler_params=_params(("arbitrary",)),
        name="out_proj",
    )(x2, o_ret, o_rwkv, o_gla, w_out_p, gain)


def _staircase_pairs():
    k = PEER_TOPK
    return [(a, b) for a in range(k) for b in range(k) if (a + 1) * (b + 1) <= k]


def _tree_max(xs):
    xs = list(xs)
    while len(xs) > 1:
        nxt = [jnp.maximum(xs[2 * i], xs[2 * i + 1]) for i in range(len(xs) // 2)]
        if len(xs) % 2:
            nxt.append(xs[-1])
        xs = nxt
    return xs[0]


def _route_kernel(hnt_ref, wq_ref, keys_ref, n1_ref, e1_ref, r2_ref, e2_ref, work_ref, s_ref, vals_ref, rank_ref):
    nh, nk, k_top = PEER_HEADS, PEER_NKEYS, PEER_TOPK
    qt = _bf(_dot(wq_ref[...], hnt_ref[...]))
    for p in range(2):
        for h in range(nh):
            r0 = (p * nh + h) * PEER_QHALF
            s = _dot(keys_ref[p, h], qt[r0:r0 + PEER_QHALF, :])
            work_ref[p, h] = s
            s_ref[p, h] = s
    rank_ref[...] = jnp.full(rank_ref.shape, float(k_top), F32)
    groups = nk // SUBLANES

    def extract(r, carry):
        rf = jnp.asarray(r, F32)
        for p in range(2):
            for h in range(nh):
                tiles = [work_ref[p, h, i * SUBLANES:(i + 1) * SUBLANES, :] for i in range(groups)]
                m = _tree_max(tiles)
                for sh in (4, 2, 1):
                    m = jnp.maximum(m, pltpu.roll(m, sh, 0))
                vals_ref[p, r, pl.ds(h, 1), :] = m[0:1, :]
                for i in range(groups):
                    rows = slice(i * SUBLANES, (i + 1) * SUBLANES)
                    is_max = tiles[i] == m
                    work_ref[p, h, rows, :] = jnp.where(is_max, -jnp.inf, tiles[i])
                    rank_ref[p, h, rows, :] = jnp.where(is_max, rf, rank_ref[p, h, rows, :])
        return carry

    lax.fori_loop(0, k_top, extract, 0)

    v1 = [vals_ref[0, r] for r in range(k_top)]
    v2 = [vals_ref[1, r] for r in range(k_top)]
    cand = {(a, b): v1[a] + v2[b] for a, b in _staircase_pairs()}
    work = list(cand.values())
    tau = None
    for it in range(k_top):
        tau = _tree_max(work)
        if it + 1 < k_top:
            work = [jnp.where(w == tau, -jnp.inf, w) for w in work]
    top = cand[(0, 0)]
    z = None
    for c in cand.values():
        zi = jnp.where(c >= tau, jnp.exp(c - top), 0.0)
        z = zi if z is None else z + zi
    scale2 = 0.5 / z
    n_of_rank = []
    for a in range(k_top):
        cnt = None
        for b in range(k_top // (a + 1)):
            ge = (cand[(a, b)] >= tau).astype(F32)
            cnt = ge if cnt is None else cnt + ge
        n_of_rank.append(cnt)
    m1, m2 = v1[0], v2[0]
    for h in range(nh):
        r1 = rank_ref[0, h]
        n1 = jnp.zeros_like(r1)
        for a in range(k_top):
            n1 = jnp.where(r1 == float(a), n_of_rank[a][h:h + 1, :], n1)
        n1_ref[h] = n1
        e1_ref[h] = jnp.exp(s_ref[0, h] - m1[h:h + 1, :])
        r2_ref[h] = pltpu.bitcast(_bf(rank_ref[1, h]), jnp.uint32)
        e2_ref[h] = pltpu.bitcast(_bf(jnp.exp(s_ref[1, h] - m2[h:h + 1, :]) * scale2[h:h + 1, :]), jnp.uint32)


def _route(hnt, wq_t, keys, tb=256):
    t = hnt.shape[1]
    nh, nk = PEER_HEADS, PEER_NKEYS
    out = pl.BlockSpec((nh, nk, tb), lambda i: (0, 0, i))
    packed = pl.BlockSpec((nh, nk // 2, tb), lambda i: (0, 0, i))
    return pl.pallas_call(
        _route_kernel,
        grid=(t // tb,),
        in_specs=[pl.BlockSpec((D_MODEL, tb), lambda i: (0, i)), _full((2 * nh * PEER_QHALF, D_MODEL)),
                  _full((2, nh, nk, PEER_QHALF))],
        out_specs=[out, out, packed, packed],
        out_shape=[jax.ShapeDtypeStruct((nh, nk, t), F32)] * 2 + [jax.ShapeDtypeStruct((nh, nk // 2, t), jnp.uint32)] * 2,
        scratch_shapes=[pltpu.VMEM((2, nh, nk, tb), F32), pltpu.VMEM((2, nh, nk, tb), F32),
                        pltpu.VMEM((2, PEER_TOPK, nh, tb), F32), pltpu.VMEM((2, nh, nk, tb), F32)],
        compiler_params=_params(("arbitrary",)),
        name="peer_route",
    )(hnt, wq_t, keys)


EXPERT_SLABS = 8
EXPERT_PAIR = 2
PIPE_LAG = 2
ROW_SPLIT = 2


def _expert_kernel(hnt_ref, u_ref, vt_ref, n1_ref, e1_ref, r2_ref, e2_ref, x1_ref, gf_ref, o_ref, acc_ref, ht0, ht1,
                   act0, act1, rn_ref, re_ref, *, n_blocks, final_norm):
    s = pl.program_id(0)
    n_pairs = pl.num_programs(0) - PIPE_LAG
    nk = PEER_NKEYS
    tb = hnt_ref.shape[1]
    j2 = lax.rem(jnp.clip(s - 1, 0, n_pairs - 1), n_blocks)
    j3 = lax.rem(jnp.clip(s - PIPE_LAG, 0, n_pairs - 1), n_blocks)
    live = jnp.logical_and(s >= 1, s <= n_pairs).astype(F32)

    @pl.when(s == 0)
    def _():
        for r in (ht0, ht1, act0, act1):
            r[...] = jnp.zeros_like(r)

    @pl.when(j3 == 0)
    def _():
        acc_ref[...] = jnp.zeros_like(acc_ref)

    for h in range(PEER_HEADS):
        for k in range(EXPERT_SLABS):
            r = h * EXPERT_SLABS + k
            rn_ref[r:r + 1, :] = n1_ref[h, pl.ds(j2 * EXPERT_SLABS + k, 1), :]
            re_ref[r:r + 1, :] = e1_ref[h, pl.ds(j2 * EXPERT_SLABS + k, 1), :] * live

    def stages(ht_w, ht_r, act_w, act_r):
        half_w = 2 * LANES
        kc = 2 * LANES
        n_kc = D_MODEL // kc
        n_ec = (EXPERT_SLABS * nk) // kc

        assert n_kc == n_ec
        rws = (EXPERT_SLABS * nk) // ROW_SPLIT
        mws = D_MODEL // ROW_SPLIT
        hid = {}
        prj = {}

        def hidden_piece(half, c, rs):
            cols = slice(half * half_w, (half + 1) * half_w)
            t = _dot(u_ref[rs * rws:(rs + 1) * rws, c * kc:(c + 1) * kc], hnt_ref[c * kc:(c + 1) * kc, cols])
            hid[half, rs] = t if c == 0 else hid[half, rs] + t

        def project_piece(half, c, rs):
            cols = slice(half * half_w, (half + 1) * half_w)
            t = _dot(vt_ref[rs * mws:(rs + 1) * mws, c * kc:(c + 1) * kc], act_r[c * kc:(c + 1) * kc, cols])
            prj[half, rs] = t if c == 0 else prj[half, rs] + t

        gates = {}

        def gate_part(st, g0, heads, last):
            cols = slice(st * LANES, (st + 1) * LANES)
            for h in heads:
                r2 = pltpu.bitcast(r2_ref[h, :, cols], BF16)
                e2 = pltpu.bitcast(e2_ref[h, :, cols], BF16)
                for kk in range(EXPERT_PAIR):
                    r = h * EXPERT_SLABS + g0 + kk
                    n1 = _bf(rn_ref[r:r + 1, cols])
                    e1 = _bf(re_ref[r:r + 1, cols])
                    gh = jnp.where(r2 < n1, e2 * e1, jnp.zeros_like(e2))
                    gates[st, g0 + kk] = gh if h == 0 else gates[st, g0 + kk] + gh
            if last:
                for kk in range(EXPERT_PAIR):
                    rows = slice((g0 + kk) * nk, (g0 + kk + 1) * nk)
                    hk = ht_r[rows, cols]
                    act_w[rows, cols] = _bf(hk * (1.0 + lax.erf(hk * (2.0 ** -0.5)))) * gates[st, g0 + kk]

        hh = PEER_HEADS // 2
        vpu_work = [functools.partial(gate_part, st, g0, heads, last)
                    for st in range(tb // LANES) for g0 in range(0, EXPERT_SLABS, EXPERT_PAIR)
                    for heads, last in ((range(0, hh), False), (range(hh, PEER_HEADS), True))]
        mxu_work = [functools.partial(piece, half, c, rs)
                    for half in range(2) for c in range(n_kc) for rs in range(ROW_SPLIT)
                    for piece in (hidden_piece, project_piece)]
        assert len(mxu_work) == len(vpu_work)
        for m, v_ in zip(mxu_work, vpu_work):
            m()
            v_()
        for half in range(2):
            cols = slice(half * half_w, (half + 1) * half_w)
            for rs in range(ROW_SPLIT):
                ht_w[rs * rws:(rs + 1) * rws, cols] = hid[half, rs]
                acc_ref[rs * mws:(rs + 1) * mws, cols] += prj[half, rs]

    parity = lax.rem(s, 2)

    @pl.when(parity == 0)
    def _():
        stages(ht0, ht1, act1, act0)

    @pl.when(parity == 1)
    def _():
        stages(ht1, ht0, act0, act1)

    @pl.when(jnp.logical_and(j3 == n_blocks - 1, s >= PIPE_LAG))
    def _():
        y = acc_ref[...].T + x1_ref[...]
        if final_norm:
            y = y * lax.rsqrt(jnp.mean(y * y, axis=-1, keepdims=True) + NORM_EPS) * gf_ref[...]
        o_ref[...] = y


def _experts(hnt, u_b, vt_b, n1, e1, r2, e2, x1, gain_f, final_norm, tb=512):
    t = hnt.shape[1]
    nh, nk = PEER_HEADS, PEER_NKEYS
    eb = EXPERT_SLABS * nk
    n_blocks = PEER_NEXPERTS // eb
    n_pairs = (t // tb) * n_blocks

    def pair(lag):
        def f(s):
            p = jnp.clip(s - lag, 0, n_pairs - 1)
            return p // n_blocks, lax.rem(p, n_blocks)
        return f

    tok = lambda lag: (lambda s: pair(lag)(s)[0])
    blk = lambda lag: (lambda s: pair(lag)(s)[1])
    routed = lambda rows: pl.BlockSpec((nh, rows, tb), lambda s: (0, 0, tok(1)(s)))
    nrow = PEER_HEADS * EXPERT_SLABS
    return pl.pallas_call(
        functools.partial(_expert_kernel, n_blocks=n_blocks, final_norm=final_norm),
        grid=(n_pairs + PIPE_LAG,),
        in_specs=[pl.BlockSpec((D_MODEL, tb), lambda s: (0, tok(0)(s))),
                  pl.BlockSpec((eb, D_MODEL), lambda s: (blk(0)(s), 0)),
                  pl.BlockSpec((D_MODEL, eb), lambda s: (0, blk(PIPE_LAG)(s))),
                  routed(nk), routed(nk), routed(nk // 2), routed(nk // 2),
                  pl.BlockSpec((tb, D_MODEL), lambda s: (tok(PIPE_LAG)(s), 0)), _full((1, D_MODEL))],
        out_specs=pl.BlockSpec((tb, D_MODEL), lambda s: (tok(PIPE_LAG)(s), 0)),
        out_shape=jax.ShapeDtypeStruct((t, D_MODEL), F32),
        scratch_shapes=[pltpu.VMEM((D_MODEL, tb), F32), pltpu.VMEM((eb, tb), F32), pltpu.VMEM((eb, tb), F32),
                        pltpu.VMEM((eb, tb), BF16), pltpu.VMEM((eb, tb), BF16),
                        pltpu.VMEM((nrow, tb), F32), pltpu.VMEM((nrow, tb), F32)],
        compiler_params=_params(("arbitrary",)),
        name="peer_experts",
    )(hnt, u_b, vt_b, n1, e1, r2, e2, x1, gain_f)


def _pad_heads(w, heads, d, dp):
    lead = w.shape[:-1]
    w = w.reshape(*lead, heads, d)
    w = jnp.pad(w, [(0, 0)] * len(lead) + [(0, 0), (0, dp - d)])
    return w.reshape(*lead, heads * dp)


def _layer_params(l, w_in, w_out, rwkv_mu, rwkv_w_up, rwkv_w_bias, rwkv_a_up, rwkv_a_bias, rwkv_g_up, rwkv_k_k,
                  rwkv_k_a, rwkv_r_k, rwkv_ln_w, rwkv_ln_b, gla_gk_up, gla_gk_bias, gla_norm_w, peer_w_q,
                  peer_sub_keys, peer_u, peer_v):
    wi = w_in[l]
    g0 = RET_IN + RWKV_IN
    qk, vw = GLA_HEADS * GLA_DK, GLA_HEADS * GLA_DV
    gq = _pad_heads(wi[:, g0:g0 + qk], GLA_HEADS, GLA_DK, GLA_DKP)
    gkk = _pad_heads(wi[:, g0 + qk:g0 + 2 * qk], GLA_HEADS, GLA_DK, GLA_DKP)
    gv = _pad_heads(wi[:, g0 + 2 * qk:g0 + 2 * qk + vw], GLA_HEADS, GLA_DV, GLA_DVP)
    ggk = jnp.pad(wi[:, g0 + 2 * qk + vw:g0 + 2 * qk + vw + GLA_GATE_LORA], ((0, 0), (0, LANES - GLA_GATE_LORA)))
    gg = _pad_heads(wi[:, g0 + 2 * qk + vw + GLA_GATE_LORA:], GLA_HEADS, GLA_DV, GLA_DVP)
    w_in_p = _bf(jnp.concatenate([wi[:, :g0], gq, gkk, gv, ggk, gg], axis=1))
    wo = w_out[l]
    m0 = RET_W + RWKV_W
    wo_gla = jnp.pad(wo[m0:].reshape(GLA_HEADS, GLA_DV, D_MODEL), ((0, 0), (0, GLA_DVP - GLA_DV), (0, 0)))
    w_out_p = _bf(jnp.concatenate([wo[:m0], wo_gla.reshape(GLA_VW, D_MODEL)], axis=0))
    zeros = jnp.zeros((DECAY_LORA, RWKV_W), F32)
    rw = dict(
        mu=rwkv_mu[l][None, :],
        wwa=jnp.concatenate([jnp.concatenate([rwkv_w_up[l], zeros], axis=1),
                             jnp.concatenate([zeros, rwkv_a_up[l]], axis=1)], axis=0),
        w_bias=rwkv_w_bias[l][None, :], a_bias=rwkv_a_bias[l][None, :], g_up=_bf(rwkv_g_up[l]),
        k_k=rwkv_k_k[l][None, :], k_a=rwkv_k_a[l][None, :], r_k=rwkv_r_k[l].reshape(1, RWKV_W),
        ln_w=rwkv_ln_w[l][None, :], ln_b=rwkv_ln_b[l][None, :])
    gl = dict(
        gk_up=jnp.pad(_pad_heads(gla_gk_up[l], GLA_HEADS, GLA_DK, GLA_DKP), ((0, LANES - GLA_GATE_LORA), (0, 0))),
        gk_bias=_pad_heads(gla_gk_bias[l][None, :], GLA_HEADS, GLA_DK, GLA_DKP),
        norm_w=_pad_heads(gla_norm_w[l][None, :], GLA_HEADS, GLA_DV, GLA_DVP))
    wq = peer_w_q[l].reshape(D_MODEL, PEER_HEADS, 2, PEER_QHALF)
    wq_t = _bf(jnp.transpose(wq, (2, 1, 3, 0)).reshape(2 * PEER_HEADS * PEER_QHALF, D_MODEL))
    keys = _bf(jnp.transpose(peer_sub_keys[l], (1, 0, 2, 3)))
    return dict(w_in=w_in_p, w_out=w_out_p, rwkv=rw, gla=gl, wq_t=wq_t, keys=keys, u=_bf(peer_u[l]),
                vt=_bf(peer_v[l].T))


def _layer(x2, b, s, pr, gain_mix, gain_ffn, gain_final, final_norm):
    z_ret, z_rwkv, z_gla = _norm_proj(x2, gain_mix, pr["w_in"])
    o_ret = _retention(z_ret.reshape(b, s, RET_IN)).reshape(b * s, RET_W)
    o_rwkv = _rwkv(z_rwkv.reshape(b, s, RWKV_IN), pr["rwkv"]).reshape(b * s, RWKV_W)
    o_gla = _gla(z_gla.reshape(b, s, GLA_IN), pr["gla"]).reshape(b * s, GLA_VW)
    x1, hnt = _out_proj(x2, o_ret, o_rwkv, o_gla, pr["w_out"], gain_ffn)
    n1, e1, r2, e2 = _route(hnt, pr["wq_t"], pr["keys"])
    return _experts(hnt, pr["u"], pr["vt"], n1, e1, r2, e2, x1, gain_final, final_norm)


def kernel(x, norm_mix, norm_ffn, norm_final, w_in, w_out, rwkv_mu, rwkv_w_up, rwkv_w_bias, rwkv_a_up, rwkv_a_bias, rwkv_g_up, rwkv_k_k, rwkv_k_a, rwkv_r_k, rwkv_ln_w, rwkv_ln_b, gla_gk_up, gla_gk_bias, gla_norm_w, peer_w_q, peer_sub_keys, peer_u, peer_v):
    b, s, d = x.shape
    x2 = x.reshape(b * s, d)
    gain_final = norm_final[None, :]
    for l in range(DEPTH):
        pr = _layer_params(l, w_in, w_out, rwkv_mu, rwkv_w_up, rwkv_w_bias, rwkv_a_up, rwkv_a_bias, rwkv_g_up,
                           rwkv_k_k, rwkv_k_a, rwkv_r_k, rwkv_ln_w, rwkv_ln_b, gla_gk_up, gla_gk_bias, gla_norm_w,
                           peer_w_q, peer_sub_keys, peer_u, peer_v)
        x2 = _layer(x2, b, s, pr, norm_mix[l][None, :], norm_ffn[l][None, :], gain_final, l == DEPTH - 1)
    return x2.reshape(b, s, d)
```

```python
import functools

import numpy as np
import jax
import jax.numpy as jnp
from jax import lax
from jax.experimental import pallas as pl
from jax.experimental.pallas import tpu as pltpu

F32 = jnp.float32
BF16 = jnp.bfloat16
HI = lax.Precision.HIGHEST

D_MODEL = 1024
DEPTH = 2
NORM_EPS = 1e-6
HEAD_NORM_EPS = 1e-5
RET_HEADS, RET_DK, RET_DV, ROPE_BASE = 4, 64, 64, 10000.0
RET_W = RET_HEADS * RET_DK
RET_CHUNK = 128
RWKV_HEADS, RWKV_HEAD = 6, 64
RWKV_W = RWKV_HEADS * RWKV_HEAD
DECAY_LORA, AAA_LORA, GATE_LORA = 64, 64, 128
RWKV_GN_EPS = 64e-5
RWKV_CHUNK = 64
RWKV_SUB = 16
RWKV_IN = 3 * RWKV_W + DECAY_LORA + AAA_LORA + GATE_LORA
GLA_HEADS, GLA_DK, GLA_DV, GLA_GATE_LORA = 4, 48, 96, 16
GLA_GATE_NORMALIZER = 16.0
GLA_DKP, GLA_DVP = 64, 128
GLA_QW = GLA_HEADS * GLA_DKP
GLA_VW = GLA_HEADS * GLA_DVP
GLA_IN = 2 * GLA_QW + GLA_VW + 128 + GLA_VW
GLA_SUB = 16
GLA_BLOCK = 256
RET_IN = 4 * RET_W
D_INP = RET_IN + RWKV_IN + GLA_IN
D_MIXP = RET_W + RWKV_W + GLA_VW
PEER_HEADS, PEER_NKEYS, PEER_QHALF, PEER_TOPK = 8, 128, 128, 16
PEER_NEXPERTS = PEER_NKEYS * PEER_NKEYS
LANES = 128
SUBLANES = 8
VMEM_LIMIT = 56 * 1024 * 1024


def _params(sem):
    return pltpu.CompilerParams(dimension_semantics=sem, vmem_limit_bytes=VMEM_LIMIT)


def _dot(a, b, prec=None):
    return jnp.dot(a, b, precision=prec, preferred_element_type=F32)


def _dot_nt(a, b, prec=None):
    return lax.dot_general(a, b, (((1,), (1,)), ((), ())), precision=prec, preferred_element_type=F32)


def _dot_tn(a, b, prec=None):
    return lax.dot_general(a, b, (((0,), (0,)), ((), ())), precision=prec, preferred_element_type=F32)


def _bf(x):
    return x.astype(BF16)


def _pack_rows(x):
    m, n = x.shape
    return lax.bitcast_convert_type(jnp.swapaxes(x.reshape(m // 2, 2, n), -1, -2), jnp.uint32)


def _iota(shape, dim):
    return lax.broadcasted_iota(jnp.int32, shape, dim)


def _same_block(shape, rblk, cblk):
    r = _iota(shape, 0) >> (rblk.bit_length() - 1)
    c = _iota(shape, 1) >> (cblk.bit_length() - 1)
    return r == c


def _full(shape):
    n = len(shape)
    return pl.BlockSpec(shape, lambda *_: (0,) * n)


def _block_ones(n, blk):
    i = np.arange(n) // blk
    return jnp.asarray((i[:, None] == i[None, :]).astype(np.float32))


def _tril_ones(n):
    return jnp.asarray(np.tril(np.ones((n, n), np.float32)))


def _norm_proj_kernel(x_ref, g_ref, w_ref, zr_ref, zk_ref, zg_ref):
    x = x_ref[...]
    y = x * lax.rsqrt(jnp.mean(x * x, axis=-1, keepdims=True) + NORM_EPS) * g_ref[...]
    yb = _bf(y)
    zr_ref[...] = _dot(yb, w_ref[:, 0:RET_IN])
    zk_ref[...] = _dot(yb, w_ref[:, RET_IN:RET_IN + RWKV_IN])
    zg_ref[...] = _dot(yb, w_ref[:, RET_IN + RWKV_IN:D_INP])


def _norm_proj(x2, gain, w_in_p, tm=512):
    t = x2.shape[0]
    return pl.pallas_call(
        _norm_proj_kernel,
        grid=(t // tm,),
        in_specs=[pl.BlockSpec((tm, D_MODEL), lambda i: (i, 0)), _full((1, D_MODEL)), _full((D_MODEL, D_INP))],
        out_specs=[pl.BlockSpec((tm, RET_IN), lambda i: (i, 0)), pl.BlockSpec((tm, RWKV_IN), lambda i: (i, 0)),
                   pl.BlockSpec((tm, GLA_IN), lambda i: (i, 0))],
        out_shape=[jax.ShapeDtypeStruct((t, RET_IN), F32), jax.ShapeDtypeStruct((t, RWKV_IN), F32),
                   jax.ShapeDtypeStruct((t, GLA_IN), F32)],
        compiler_params=_params(("arbitrary",)),
        name="norm_proj",
    )(x2, gain, w_in_p)


RET_BLOCK = 512


def _ret_kernel(z_ref, cos_ref, sin_ref, dmat_ref, qdec_ref, kdec_ref, cdec_ref, ones_ref, o_ref, st_ref):
    c = RET_CHUNK
    tb = RET_BLOCK

    @pl.when(pl.program_id(1) == 0)
    def _():
        st_ref[...] = jnp.zeros_like(st_ref)

    z = z_ref[0]
    q, k, v, g = (z[:, i * RET_W:(i + 1) * RET_W] for i in range(4))
    cos, sin = cos_ref[...], sin_ref[...]
    first_half = (_iota((tb, RET_W), 1) & (RET_DK - 1)) < RET_DK // 2

    def rot(t):
        return jnp.where(first_half, -pltpu.roll(t, RET_W - RET_DK // 2, 1), pltpu.roll(t, RET_DK // 2, 1))

    q = q * cos + rot(q) * sin
    k = (k * cos + rot(k) * sin) * (RET_DK ** -0.5)
    qd = q * qdec_ref[...]
    kd = k * kdec_ref[...]
    head0 = _iota((c, LANES), 1) < RET_DK
    bd = _same_block((LANES, LANES), RET_DV, RET_DK)
    units = [(ch, p) for ch in range(tb // c) for p in range(RET_HEADS // 2)]
    tile = lambda t, u: t[u[0] * c:(u[0] + 1) * c, u[1] * LANES:(u[1] + 1) * LANES]
    kb = [_bf(tile(k, u)) for u in units]
    intra = []
    for hh in range(2):
        mh = head0 if hh == 0 else jnp.logical_not(head0)
        sc = [_dot_nt(_bf(jnp.where(mh, tile(q, u), 0.0)), kb[i]) * dmat_ref[2 * u[1] + hh] for i, u in enumerate(units)]
        intra.append([_dot(_bf(s_), _bf(jnp.where(mh, tile(v, u), 0.0))) for s_, u in zip(sc, units)])
    upd = [jnp.where(bd, _dot_tn(_bf(tile(v, u)), _bf(tile(kd, u))), 0.0) for u in units]
    qdb = [_bf(tile(qd, u)) for u in units]
    st = [st_ref[p] for p in range(RET_HEADS // 2)]
    rows = []
    for ch in range(tb // c):
        row = []
        for p in range(RET_HEADS // 2):
            i = units.index((ch, p))
            row.append(intra[0][i] + intra[1][i] + _dot_nt(qdb[i], _bf(st[p])))
            st[p] = st[p] * cdec_ref[:, p * LANES:(p + 1) * LANES] + upd[i]
        rows.append(jnp.concatenate(row, axis=1))
    for p in range(RET_HEADS // 2):
        st_ref[p] = st[p]
    o = jnp.concatenate(rows, axis=0)
    ms = _dot_exact_rhs(o * o, ones_ref[...], 2) * (1.0 / RET_DV)
    o = o * lax.rsqrt(ms + HEAD_NORM_EPS)
    o_ref[0] = o * (g * jax.nn.sigmoid(g))


def _retention_consts(s):
    c = RET_CHUNK
    reps = RET_BLOCK // c
    log_gamma = np.log1p(-np.exp(np.linspace(np.log(1.0 / 32.0), np.log(1.0 / 512.0), RET_HEADS)))
    idx = np.arange(c, dtype=np.float64)
    diff = idx[:, None] - idx[None, :]
    dmat = np.where(diff >= 0, np.exp(log_gamma[:, None, None] * np.where(diff >= 0, diff, 0.0)), 0.0)
    lg_lane = np.repeat(log_gamma, RET_DK)[None, :]
    qdec = np.tile(np.exp(lg_lane * (idx[:, None] + 1.0)), (reps, 1))
    kdec = np.tile(np.exp(lg_lane * (c - 1.0 - idx[:, None])), (reps, 1))
    cdec = np.exp(lg_lane * c)
    half = RET_DK // 2
    inv = ROPE_BASE ** (-np.arange(half, dtype=np.float64) / half)
    ang = np.arange(s, dtype=np.float64)[:, None] * inv[None, :]
    cos = np.tile(np.cos(ang), (1, 2 * RET_HEADS))
    sin = np.tile(np.sin(ang), (1, 2 * RET_HEADS))
    f = lambda a: jnp.asarray(a.astype(np.float32))
    return f(cos), f(sin), f(dmat), f(qdec), f(kdec), f(cdec)


def _retention(z_ret):
    b, s, _ = z_ret.shape
    c = RET_CHUNK
    tb = RET_BLOCK
    cos, sin, dmat, qdec, kdec, cdec = _retention_consts(s)
    return pl.pallas_call(
        _ret_kernel,
        grid=(b, s // tb),
        in_specs=[pl.BlockSpec((1, tb, RET_IN), lambda i, j: (i, j, 0)),
                  pl.BlockSpec((tb, RET_W), lambda i, j: (j, 0)), pl.BlockSpec((tb, RET_W), lambda i, j: (j, 0)),
                  _full((RET_HEADS, c, c)), _full((tb, RET_W)), _full((tb, RET_W)), _full((1, RET_W)),
                  _full((RET_W, RET_W))],
        out_specs=pl.BlockSpec((1, tb, RET_W), lambda i, j: (i, j, 0)),
        out_shape=jax.ShapeDtypeStruct((b, s, RET_W), F32),
        scratch_shapes=[pltpu.VMEM((RET_HEADS // 2, LANES, LANES), F32)],
        compiler_params=_params(("arbitrary", "arbitrary")),
        name="retention",
    )(z_ret, cos, sin, dmat, qdec, kdec, cdec, _bf(_block_ones(RET_W, RET_DV)))


RWKV_BLOCK = 256
INV_PASSES = 3


def _split(x, parts):
    out = []
    for _ in range(parts - 1):
        h = _bf(x)
        out.append(h)
        x = x - h.astype(F32)
    out.append(_bf(x))
    return out


def _dot_exact_lhs(a_bf, x, parts):
    acc = None
    for p in _split(x, parts):
        t = _dot(a_bf, p)
        acc = t if acc is None else acc + t
    return acc


def _dot_exact_rhs(x, b_bf, parts):
    acc = None
    for p in _split(x, parts):
        t = _dot(p, b_bf)
        acc = t if acc is None else acc + t
    return acc


def _dot_f32(a, b, passes):
    if passes == 1:
        return _dot(_bf(a), _bf(b))
    if passes == 3:
        ah, al = _split(a, 2)
        bh, bl = _split(b, 2)
        return _dot(ah, bh) + (_dot(ah, bl) + _dot(al, bh))
    return _dot(a, b, HI)


def _rwkv_kernel(z_ref, mu_ref, wwa_ref, wb_ref, ab_ref, gup_ref, kk_ref, ka_ref, rk_ref, lnw_ref, lnb_ref,
                 ones_ref, ltri_ref, o_ref, st_ref, carry_ref):
    c = RWKV_CHUNK
    tb = RWKV_BLOCK
    w_ = RWKV_W

    @pl.when(pl.program_id(1) == 0)
    def _():
        st_ref[...] = jnp.zeros_like(st_ref)
        carry_ref[...] = jnp.zeros_like(carry_ref)

    f = z_ref[0]
    f_prev = jnp.where(_iota((tb, RWKV_IN), 0) == 0, carry_ref[...], pltpu.roll(f, 1, 0))
    carry_ref[...] = f[tb - 1:tb, :]
    f = f + (f_prev - f) * mu_ref[...]
    r, k, v = f[:, 0:w_], f[:, w_:2 * w_], f[:, 2 * w_:3 * w_]
    wa, gd = f[:, 3 * w_:3 * w_ + LANES], f[:, 3 * w_ + LANES:RWKV_IN]
    lane = _iota((tb, LANES), 1)
    proj = _dot_f32(jnp.where(lane < DECAY_LORA, jnp.tanh(wa), wa), wwa_ref[...], 3)
    wlog = -jax.nn.softplus(-(wb_ref[...] + proj[:, 0:w_])) - 0.5
    ld = -jnp.exp(wlog)
    a = jax.nn.sigmoid(ab_ref[...] + proj[:, w_:2 * w_])
    g = _dot(_bf(jax.nn.sigmoid(gd)), gup_ref[...])
    ones = ones_ref[...]
    kk = k * kk_ref[...]
    kk = kk * lax.rsqrt(jnp.maximum(_dot_exact_rhs(kk * kk, ones, 2), 1e-24))
    k = k * (1.0 + (a - 1.0) * ka_ref[...])
    cum = _dot_exact_lhs(ltri_ref[...], ld, 3)
    e_neg = jnp.exp(-cum)
    b = kk * a
    a_t_all = -kk * jnp.exp(cum - ld)
    b_t_all, k_t_all, r_t_all = b * e_neg, k * e_neg, r * jnp.exp(cum)

    n2 = 2 * c
    ri, ci = _iota((n2, n2), 0), _iota((n2, n2), 1)
    same_head = _same_block((n2, n2), c, c)
    strict = jnp.logical_and(same_head, ri > ci)
    incl = jnp.logical_and(same_head, ri >= ci)
    sub = _same_block((n2, n2), RWKV_SUB, RWKV_SUB)
    eye = (ri == ci).astype(F32)
    bd = _same_block((LANES, LANES), RWKV_HEAD, RWKV_HEAD)
    head0 = _iota((c, LANES), 1) < RWKV_HEAD
    mm = functools.partial(_dot_f32, passes=INV_PASSES)
    mm1 = functools.partial(_dot_f32, passes=1)

    def stack(x):
        return jnp.concatenate([jnp.where(head0, x, 0.0), jnp.where(head0, 0.0, x)], axis=0)

    def fold(x):
        return x[:c] + x[c:]

    units = [(ch, p) for ch in range(tb // c) for p in range(RWKV_HEADS // 2)]
    each = lambda fn, *lists: [fn(*args) for args in zip(*lists)]
    rows_of = lambda ch: slice(ch * c, (ch + 1) * c)
    lanes_of = lambda p: slice(p * LANES, (p + 1) * LANES)
    tile = lambda t: [stack(t[rows_of(ch), lanes_of(p)]) for ch, p in units]
    a_s, r_s, b_s, k_s, v_s = tile(a_t_all), tile(r_t_all), tile(b_t_all), tile(k_t_all), tile(v)
    cl = [cum[(ch + 1) * c - 1:(ch + 1) * c, lanes_of(p)] for ch, p in units]
    e_rem = [jnp.exp(cl_u - cum[rows_of(ch), lanes_of(p)]) for cl_u, (ch, p) in zip(cl, units)]
    b_h = [b[rows_of(ch), lanes_of(p)] * e for e, (ch, p) in zip(e_rem, units)]
    k_h = [k[rows_of(ch), lanes_of(p)] * e for e, (ch, p) in zip(e_rem, units)]
    v_p = [v[rows_of(ch), lanes_of(p)] for ch, p in units]
    big = each(lambda x, y, z, w: _dot_nt(_bf(jnp.concatenate([x, y], axis=0)), _bf(jnp.concatenate([z, w], axis=0))),
               a_s, r_s, b_s, k_s)
    a_ab = [jnp.where(strict, t[:n2, :n2], 0.0) for t in big]
    a_ak = [jnp.where(strict, t[:n2, n2:], 0.0) for t in big]
    a_rb = [jnp.where(incl, t[n2:, :n2], 0.0) for t in big]
    a_rk = [jnp.where(incl, t[n2:, n2:], 0.0) for t in big]
    d1 = [jnp.where(sub, t, 0.0) for t in a_ab]
    lo = each(lambda x, y: x - y, a_ab, d1)
    d2 = each(mm, d1, d1)
    t_d = each(lambda x, y: mm1(eye + x, eye + y), d1, d2)
    d4 = each(mm, d2, d2)
    t_d = each(lambda x, y: mm1(x, eye + y), t_d, d4)
    d8 = each(mm, d4, d4)
    t_d = each(lambda x, y: mm1(x, eye + y), t_d, d8)
    e1 = each(mm1, t_d, lo)
    e2 = each(mm1, e1, e1)
    t_inv = each(lambda x, y: mm1(eye + x, eye + y), e1, e2)
    t_inv = each(mm1, t_inv, t_d)
    akv = each(lambda x, y: _dot(_bf(x), _bf(y)), a_ak, v_s)
    wy = each(lambda t, x, y: mm1(t, jnp.concatenate([x, y], axis=1)), t_inv, a_s, akv)
    qo = each(lambda x, y: _dot(_bf(x), _bf(y)), a_rb, wy)
    rkv = each(lambda x, y: _dot(_bf(x), _bf(y)), a_rk, v_s)
    w_p = [fold(t[:, :LANES]) for t in wy]
    y_p = [fold(t[:, LANES:]) for t in wy]
    q_p = each(lambda x, y: fold(x + y[:, :LANES]), r_s, qo)
    ol_p = each(lambda x, y: fold(x[:, LANES:] + y), qo, rkv)
    gmat = each(lambda x, y: jnp.where(bd, _dot_tn(_bf(x), _bf(y)), 0.0), w_p, b_h)
    nmat = each(lambda y, vv, bb, kk_: jnp.where(bd, _dot_tn(_bf(jnp.concatenate([y, vv], axis=0)),
                                                             _bf(jnp.concatenate([bb, kk_], axis=0))), 0.0),
                y_p, v_p, b_h, k_h)
    pre = {u: (q_p[i], ol_p[i], gmat[i], nmat[i], jnp.exp(cl[i])) for i, u in enumerate(units)}

    outs = [[None] * (RWKV_HEADS // 2) for _ in range(tb // c)]
    st = [st_ref[p] for p in range(RWKV_HEADS // 2)]
    for ch in range(tb // c):
        for p in range(RWKV_HEADS // 2):
            q_u, ol_u, g_u, n_u, pc_u = pre[ch, p]
            stb = _bf(st[p])
            outs[ch][p] = _dot_nt(_bf(q_u), stb) + ol_u
            st[p] = st[p] * pc_u + _dot(stb, _bf(g_u)) + n_u
    for p in range(RWKV_HEADS // 2):
        st_ref[p] = st[p]
    o = jnp.concatenate([jnp.concatenate(row, axis=1) for row in outs], axis=0)
    inv_n = 1.0 / RWKV_HEAD
    d = o - _dot_exact_rhs(o, ones, 2) * inv_n
    var = _dot_exact_rhs(d * d, ones, 2) * inv_n
    o = d * lax.rsqrt(var + RWKV_GN_EPS) * lnw_ref[...] + lnb_ref[...]
    o = o + _dot_exact_rhs(r * k * rk_ref[...], ones, 2) * v
    o_ref[0] = o * g


def _rwkv(z_rwkv, pr):
    b, s, _ = z_rwkv.shape
    tb = RWKV_BLOCK
    row = lambda n: _full((1, n))
    ltri = np.kron(np.eye(tb // RWKV_CHUNK, dtype=np.float32), np.tril(np.ones((RWKV_CHUNK, RWKV_CHUNK), np.float32)))
    return pl.pallas_call(
        _rwkv_kernel,
        grid=(b, s // tb),
        in_specs=[pl.BlockSpec((1, tb, RWKV_IN), lambda i, j: (i, j, 0)), row(RWKV_IN), _full((LANES, 2 * RWKV_W)),
                  row(RWKV_W), row(RWKV_W), _full((GATE_LORA, RWKV_W)), row(RWKV_W), row(RWKV_W), row(RWKV_W),
                  row(RWKV_W), row(RWKV_W), _full((RWKV_W, RWKV_W)), _full((tb, tb))],
        out_specs=pl.BlockSpec((1, tb, RWKV_W), lambda i, j: (i, j, 0)),
        out_shape=jax.ShapeDtypeStruct((b, s, RWKV_W), F32),
        scratch_shapes=[pltpu.VMEM((RWKV_HEADS // 2, LANES, LANES), F32), pltpu.VMEM((1, RWKV_IN), F32)],
        compiler_params=_params(("arbitrary", "arbitrary")),
        name="rwkv7",
    )(z_rwkv, pr["mu"], pr["wwa"], pr["w_bias"], pr["a_bias"], pr["g_up"], pr["k_k"], pr["k_a"], pr["r_k"],
      pr["ln_w"], pr["ln_b"], _bf(_block_ones(RWKV_W, RWKV_HEAD)), jnp.asarray(ltri, dtype=BF16))


def _gla_kernel(z_ref, gkup_ref, gkb_ref, nw_ref, ind_ref, ones_ref, ltri_ref, o_ref, st_ref, x_ref):
    n = GLA_SUB
    tb = GLA_BLOCK
    nsub = tb // n

    @pl.when(pl.program_id(1) == 0)
    def _():
        st_ref[...] = jnp.zeros_like(st_ref)

    z = z_ref[0]
    q_all = z[:, 0:GLA_QW] * (GLA_DK ** -0.5)
    k_all = z[:, GLA_QW:2 * GLA_QW]
    v_all = z[:, 2 * GLA_QW:2 * GLA_QW + GLA_VW]
    gkd = z[:, 2 * GLA_QW + GLA_VW:2 * GLA_QW + GLA_VW + LANES]
    g = z[:, 2 * GLA_QW + GLA_VW + LANES:GLA_IN]
    gk = jax.nn.log_sigmoid(_dot_f32(gkd, gkup_ref[...], 3) + gkb_ref[...]) * (1.0 / GLA_GATE_NORMALIZER)
    bc_all = _dot_exact_lhs(ltri_ref[...], gk, 3)
    bd = _same_block((GLA_VW, GLA_QW), GLA_DVP, GLA_DKP)
    rowi = _iota((n, GLA_QW), 0)
    subs = range(nsub)
    rows = [slice(sc * n, (sc + 1) * n) for sc in subs]
    q = [q_all[r] for r in rows]
    k = [k_all[r] for r in rows]
    v = [v_all[r] for r in rows]
    bc = [bc_all[r] for r in rows]
    bl = [t[n - 1:n, :] for t in bc]
    for sc in subs:
        for j in range(n):
            xj = q[sc] * jnp.exp(jnp.minimum(bc[sc] - bc[sc][j:j + 1, :], 0.0)) * k[sc][j:j + 1, :]
            x_ref[(sc * n + j) * n:(sc * n + j + 1) * n, :] = jnp.where(rowi >= j, xj, 0.0)
    x = x_ref[...]
    xh = _bf(x)
    xl = _bf(x - xh.astype(F32))
    ind = ind_ref[...]
    e = _dot(xh, ind) + _dot(xl, ind)
    o_intra = []
    for sc in subs:
        acc = None
        for j in range(n):
            t = e[(sc * n + j) * n:(sc * n + j + 1) * n, :] * v[sc][j:j + 1, :]
            acc = t if acc is None else acc + t
        o_intra.append(acc)
    upd = [jnp.where(bd, _dot_tn(_bf(v[sc]), _bf(k[sc] * jnp.exp(bl[sc] - bc[sc]))), 0.0) for sc in subs]
    qe = [_bf(q[sc] * jnp.exp(bc[sc])) for sc in subs]
    st = st_ref[...]
    outs = []
    for sc in subs:
        outs.append(_dot_nt(qe[sc], _bf(st)) + o_intra[sc])
        st = st * jnp.exp(bl[sc]) + upd[sc]
    st_ref[...] = st
    o = jnp.concatenate(outs, axis=0)
    ms = _dot_exact_rhs(o * o, ones_ref[...], 2) * (1.0 / GLA_DV)
    o_ref[0] = o * lax.rsqrt(ms + HEAD_NORM_EPS) * nw_ref[...] * (g * jax.nn.sigmoid(g))


def _gla(z_gla, pr):
    b, s, _ = z_gla.shape
    tb = GLA_BLOCK
    hq = np.arange(GLA_QW) // GLA_DKP
    hv = np.arange(GLA_VW) // GLA_DVP
    ind = jnp.asarray((hq[:, None] == hv[None, :]).astype(np.float32), dtype=BF16)
    ltri = np.kron(np.eye(tb // GLA_SUB, dtype=np.float32), np.tril(np.ones((GLA_SUB, GLA_SUB), np.float32)))
    return pl.pallas_call(
        _gla_kernel,
        grid=(b, s // tb),
        in_specs=[pl.BlockSpec((1, tb, GLA_IN), lambda i, j: (i, j, 0)), _full((LANES, GLA_QW)), _full((1, GLA_QW)),
                  _full((1, GLA_VW)), _full((GLA_QW, GLA_VW)), _full((GLA_VW, GLA_VW)), _full((tb, tb))],
        out_specs=pl.BlockSpec((1, tb, GLA_VW), lambda i, j: (i, j, 0)),
        out_shape=jax.ShapeDtypeStruct((b, s, GLA_VW), F32),
        scratch_shapes=[pltpu.VMEM((GLA_VW, GLA_QW), F32), pltpu.VMEM((tb * GLA_SUB, GLA_QW), F32)],
        compiler_params=_params(("arbitrary", "arbitrary")),
        name="gla",
    )(z_gla, pr["gk_up"], pr["gk_bias"], pr["norm_w"], ind, _bf(_block_ones(GLA_VW, GLA_DVP)),
      jnp.asarray(ltri, dtype=BF16))


def _out_proj_kernel(x_ref, a_ref, b_ref, c_ref, w_ref, g_ref, x1_ref, hnt_ref):
    acc = _dot(_bf(a_ref[...]), w_ref[0:RET_W, :])
    acc += _dot(_bf(b_ref[...]), w_ref[RET_W:RET_W + RWKV_W, :])
    acc += _dot(_bf(c_ref[...]), w_ref[RET_W + RWKV_W:D_MIXP, :])
    x1 = x_ref[...] + acc
    x1_ref[...] = x1
    hn = x1 * lax.rsqrt(jnp.mean(x1 * x1, axis=-1, keepdims=True) + NORM_EPS) * g_ref[...]
    hnt_ref[...] = pltpu.bitcast(_bf(hn.T), jnp.uint32)


def _out_proj(x2, o_ret, o_rwkv, o_gla, w_out_p, gain, tm=512):
    t = x2.shape[0]
    blk = lambda n: pl.BlockSpec((tm, n), lambda i: (i, 0))
    return pl.pallas_call(
        _out_proj_kernel,
        grid=(t // tm,),
        in_specs=[blk(D_MODEL), blk(RET_W), blk(RWKV_W), blk(GLA_VW), _full((D_MIXP, D_MODEL)), _full((1, D_MODEL))],
        out_specs=[blk(D_MODEL), pl.BlockSpec((D_MODEL // 2, tm), lambda i: (0, i))],
        out_shape=[jax.ShapeDtypeStruct((t, D_MODEL), F32), jax.ShapeDtypeStruct((D_MODEL // 2, t), jnp.uint32)],
        compiler_params=_params(("arbitrary",)),
        name="out_proj",
    )(x2, o_ret, o_rwkv, o_gla, w_out_p, gain)


def _staircase_pairs():
    k = PEER_TOPK
    return [(a, b) for a in range(k) for b in range(k) if (a + 1) * (b + 1) <= k]


def _tree_max(xs):
    xs = list(xs)
    while len(xs) > 1:
        nxt = [jnp.maximum(xs[2 * i], xs[2 * i + 1]) for i in range(len(xs) // 2)]
        if len(xs) % 2:
            nxt.append(xs[-1])
        xs = nxt
    return xs[0]


def _route_kernel(hnt_ref, wq_ref, keys_ref, n1_ref, e1_ref, r2_ref, e2_ref, work_ref, s_ref, vals_ref, rank_ref):
    nh, nk, k_top = PEER_HEADS, PEER_NKEYS, PEER_TOPK
    qt = _bf(_dot(wq_ref[...], pltpu.bitcast(hnt_ref[...], BF16)))
    for p in range(2):
        for h in range(nh):
            r0 = (p * nh + h) * PEER_QHALF
            s = _dot(keys_ref[p, h], qt[r0:r0 + PEER_QHALF, :])
            work_ref[p, h] = s
            s_ref[p, h] = s
    rank_ref[...] = jnp.full(rank_ref.shape, float(k_top), F32)
    groups = nk // SUBLANES

    def extract(r, carry):
        rf = jnp.asarray(r, F32)
        for p in range(2):
            for h in range(nh):
                tiles = [work_ref[p, h, i * SUBLANES:(i + 1) * SUBLANES, :] for i in range(groups)]
                m = _tree_max(tiles)
                for sh in (4, 2, 1):
                    m = jnp.maximum(m, pltpu.roll(m, sh, 0))
                vals_ref[p, r, pl.ds(h, 1), :] = m[0:1, :]
                for i in range(groups):
                    rows = slice(i * SUBLANES, (i + 1) * SUBLANES)
                    is_max = tiles[i] == m
                    work_ref[p, h, rows, :] = jnp.where(is_max, -jnp.inf, tiles[i])
                    rank_ref[p, h, rows, :] = jnp.where(is_max, rf, rank_ref[p, h, rows, :])
        return carry

    lax.fori_loop(0, k_top, extract, 0)

    v1 = [vals_ref[0, r] for r in range(k_top)]
    v2 = [vals_ref[1, r] for r in range(k_top)]
    cand = {(a, b): v1[a] + v2[b] for a, b in _staircase_pairs()}
    work = list(cand.values())
    tau = None
    for it in range(k_top):
        tau = _tree_max(work)
        if it + 1 < k_top:
            work = [jnp.where(w == tau, -jnp.inf, w) for w in work]
    top = cand[(0, 0)]
    z = None
    for c in cand.values():
        zi = jnp.where(c >= tau, jnp.exp(c - top), 0.0)
        z = zi if z is None else z + zi
    scale2 = 0.5 / z
    n_of_rank = []
    for a in range(k_top):
        cnt = None
        for b in range(k_top // (a + 1)):
            ge = (cand[(a, b)] >= tau).astype(F32)
            cnt = ge if cnt is None else cnt + ge
        n_of_rank.append(cnt)
    m1, m2 = v1[0], v2[0]
    for h in range(nh):
        r1 = rank_ref[0, h]
        n1 = jnp.zeros_like(r1)
        for a in range(k_top):
            n1 = jnp.where(r1 == float(a), n_of_rank[a][h:h + 1, :], n1)
        n1_ref[h] = n1
        e1_ref[h] = jnp.exp(s_ref[0, h] - m1[h:h + 1, :])
        r2_ref[h] = pltpu.bitcast(_bf(rank_ref[1, h]), jnp.uint32)
        e2_ref[h] = pltpu.bitcast(_bf(jnp.exp(s_ref[1, h] - m2[h:h + 1, :]) * scale2[h:h + 1, :]), jnp.uint32)


def _route(hnt, wq_t, keys, tb=256):
    t = hnt.shape[1]
    nh, nk = PEER_HEADS, PEER_NKEYS
    out = pl.BlockSpec((nh, nk, tb), lambda i: (0, 0, i))
    packed = pl.BlockSpec((nh, nk // 2, tb), lambda i: (0, 0, i))
    return pl.pallas_call(
        _route_kernel,
        grid=(t // tb,),
        in_specs=[pl.BlockSpec((D_MODEL // 2, tb), lambda i: (0, i)), _full((2 * nh * PEER_QHALF, D_MODEL)),
                  _full((2, nh, nk, PEER_QHALF))],
        out_specs=[out, out, packed, packed],
        out_shape=[jax.ShapeDtypeStruct((nh, nk, t), F32)] * 2 + [jax.ShapeDtypeStruct((nh, nk // 2, t), jnp.uint32)] * 2,
        scratch_shapes=[pltpu.VMEM((2, nh, nk, tb), F32), pltpu.VMEM((2, nh, nk, tb), F32),
                        pltpu.VMEM((2, PEER_TOPK, nh, tb), F32), pltpu.VMEM((2, nh, nk, tb), F32)],
        compiler_params=_params(("arbitrary",)),
        name="peer_route",
    )(hnt, wq_t, keys)


EXPERT_SLABS = 8
EXPERT_PAIR = 2
PIPE_LAG = 2
ROW_SPLIT = 2


def _expert_kernel(hnt_ref, u_ref, vt_ref, n1_ref, e1_ref, r2_ref, e2_ref, x1_ref, gf_ref, o_ref, acc_ref, ht0, ht1,
                   act0, act1, rn_ref, re_ref, *, n_blocks, final_norm):
    s = pl.program_id(0)
    n_pairs = pl.num_programs(0) - PIPE_LAG
    nk = PEER_NKEYS
    tb = hnt_ref.shape[1]
    j2 = lax.rem(jnp.clip(s - 1, 0, n_pairs - 1), n_blocks)
    j3 = lax.rem(jnp.clip(s - PIPE_LAG, 0, n_pairs - 1), n_blocks)
    live = jnp.logical_and(s >= 1, s <= n_pairs).astype(F32)

    @pl.when(s == 0)
    def _():
        for r in (ht0, ht1, act0, act1):
            r[...] = jnp.zeros_like(r)

    @pl.when(j3 == 0)
    def _():
        acc_ref[...] = jnp.zeros_like(acc_ref)

    for h in range(PEER_HEADS):
        for k in range(EXPERT_SLABS):
            r = h * EXPERT_SLABS + k
            rn_ref[r:r + 1, :] = n1_ref[h, pl.ds(j2 * EXPERT_SLABS + k, 1), :]
            re_ref[r:r + 1, :] = e1_ref[h, pl.ds(j2 * EXPERT_SLABS + k, 1), :] * live

    def stages(ht_w, ht_r, act_w, act_r):
        half_w = 2 * LANES
        kc = 2 * LANES
        n_kc = D_MODEL // kc
        n_ec = (EXPERT_SLABS * nk) // kc

        assert n_kc == n_ec
        rws = (EXPERT_SLABS * nk) // ROW_SPLIT
        mws = D_MODEL // ROW_SPLIT
        hid = {}
        prj = {}

        def hidden_piece(half, c, rs):
            cols = slice(half * half_w, (half + 1) * half_w)
            t = _dot(pltpu.bitcast(u_ref[rs * rws // 2:(rs + 1) * rws // 2, c * kc:(c + 1) * kc], BF16),
                     pltpu.bitcast(hnt_ref[c * kc // 2:(c + 1) * kc // 2, cols], BF16))
            hid[half, rs] = t if c == 0 else hid[half, rs] + t

        def project_piece(half, c, rs):
            cols = slice(half * half_w, (half + 1) * half_w)
            t = _dot(pltpu.bitcast(vt_ref[rs * mws // 2:(rs + 1) * mws // 2, c * kc:(c + 1) * kc], BF16),
                     act_r[c * kc:(c + 1) * kc, cols])
            prj[half, rs] = t if c == 0 else prj[half, rs] + t

        gates = {}

        def gate_part(st, g0, heads, last):
            cols = slice(st * LANES, (st + 1) * LANES)
            for h in heads:
                r2 = pltpu.bitcast(r2_ref[h, :, cols], BF16)
                e2 = pltpu.bitcast(e2_ref[h, :, cols], BF16)
                for kk in range(EXPERT_PAIR):
                    r = h * EXPERT_SLABS + g0 + kk
                    n1 = _bf(rn_ref[r:r + 1, cols])
                    e1 = _bf(re_ref[r:r + 1, cols])
                    gh = jnp.where(r2 < n1, e2 * e1, jnp.zeros_like(e2))
                    gates[st, g0 + kk] = gh if h == 0 else gates[st, g0 + kk] + gh
            if last:
                for kk in range(EXPERT_PAIR):
                    rows = slice((g0 + kk) * nk, (g0 + kk + 1) * nk)
                    hk = ht_r[rows, cols]
                    act_w[rows, cols] = _bf(hk * (1.0 + lax.erf(hk * (2.0 ** -0.5)))) * gates[st, g0 + kk]

        hh = PEER_HEADS // 2
        vpu_work = [functools.partial(gate_part, st, g0, heads, last)
                    for st in range(tb // LANES) for g0 in range(0, EXPERT_SLABS, EXPERT_PAIR)
                    for heads, last in ((range(0, hh), False), (range(hh, PEER_HEADS), True))]
        mxu_work = [functools.partial(piece, half, c, rs)
                    for half in range(2) for c in range(n_kc) for rs in range(ROW_SPLIT)
                    for piece in (hidden_piece, project_piece)]
        assert len(mxu_work) == len(vpu_work)
        for m, v_ in zip(mxu_work, vpu_work):
            m()
            v_()
        for half in range(2):
            cols = slice(half * half_w, (half + 1) * half_w)
            for rs in range(ROW_SPLIT):
                ht_w[rs * rws:(rs + 1) * rws, cols] = hid[half, rs]
                acc_ref[rs * mws:(rs + 1) * mws, cols] += prj[half, rs]

    parity = lax.rem(s, 2)

    @pl.when(parity == 0)
    def _():
        stages(ht0, ht1, act1, act0)

    @pl.when(parity == 1)
    def _():
        stages(ht1, ht0, act0, act1)

    @pl.when(jnp.logical_and(j3 == n_blocks - 1, s >= PIPE_LAG))
    def _():
        y = acc_ref[...].T + x1_ref[...]
        if final_norm:
            y = y * lax.rsqrt(jnp.mean(y * y, axis=-1, keepdims=True) + NORM_EPS) * gf_ref[...]
        o_ref[...] = y


def _experts(hnt, u_b, vt_b, n1, e1, r2, e2, x1, gain_f, final_norm, tb=512):
    t = hnt.shape[1]
    nh, nk = PEER_HEADS, PEER_NKEYS
    eb = EXPERT_SLABS * nk
    n_blocks = PEER_NEXPERTS // eb
    n_pairs = (t // tb) * n_blocks

    def pair(lag):
        def f(s):
            p = jnp.clip(s - lag, 0, n_pairs - 1)
            return p // n_blocks, lax.rem(p, n_blocks)
        return f

    tok = lambda lag: (lambda s: pair(lag)(s)[0])
    blk = lambda lag: (lambda s: pair(lag)(s)[1])
    routed = lambda rows: pl.BlockSpec((nh, rows, tb), lambda s: (0, 0, tok(1)(s)))
    nrow = PEER_HEADS * EXPERT_SLABS
    return pl.pallas_call(
        functools.partial(_expert_kernel, n_blocks=n_blocks, final_norm=final_norm),
        grid=(n_pairs + PIPE_LAG,),
        in_specs=[pl.BlockSpec((D_MODEL // 2, tb), lambda s: (0, tok(0)(s))),
                  pl.BlockSpec((eb // 2, D_MODEL), lambda s: (blk(0)(s), 0)),
                  pl.BlockSpec((D_MODEL // 2, eb), lambda s: (0, blk(PIPE_LAG)(s))),
                  routed(nk), routed(nk), routed(nk // 2), routed(nk // 2),
                  pl.BlockSpec((tb, D_MODEL), lambda s: (tok(PIPE_LAG)(s), 0)), _full((1, D_MODEL))],
        out_specs=pl.BlockSpec((tb, D_MODEL), lambda s: (tok(PIPE_LAG)(s), 0)),
        out_shape=jax.ShapeDtypeStruct((t, D_MODEL), F32),
        scratch_shapes=[pltpu.VMEM((D_MODEL, tb), F32), pltpu.VMEM((eb, tb), F32), pltpu.VMEM((eb, tb), F32),
                        pltpu.VMEM((eb, tb), BF16), pltpu.VMEM((eb, tb), BF16),
                        pltpu.VMEM((nrow, tb), F32), pltpu.VMEM((nrow, tb), F32)],
        compiler_params=_params(("arbitrary",)),
        name="peer_experts",
    )(hnt, u_b, vt_b, n1, e1, r2, e2, x1, gain_f)


def _pad_heads(w, heads, d, dp):
    lead = w.shape[:-1]
    w = w.reshape(*lead, heads, d)
    w = jnp.pad(w, [(0, 0)] * len(lead) + [(0, 0), (0, dp - d)])
    return w.reshape(*lead, heads * dp)


def _layer_params(l, w_in, w_out, rwkv_mu, rwkv_w_up, rwkv_w_bias, rwkv_a_up, rwkv_a_bias, rwkv_g_up, rwkv_k_k,
                  rwkv_k_a, rwkv_r_k, rwkv_ln_w, rwkv_ln_b, gla_gk_up, gla_gk_bias, gla_norm_w, peer_w_q,
                  peer_sub_keys, peer_u, peer_v):
    wi = w_in[l]
    g0 = RET_IN + RWKV_IN
    qk, vw = GLA_HEADS * GLA_DK, GLA_HEADS * GLA_DV
    gq = _pad_heads(wi[:, g0:g0 + qk], GLA_HEADS, GLA_DK, GLA_DKP)
    gkk = _pad_heads(wi[:, g0 + qk:g0 + 2 * qk], GLA_HEADS, GLA_DK, GLA_DKP)
    gv = _pad_heads(wi[:, g0 + 2 * qk:g0 + 2 * qk + vw], GLA_HEADS, GLA_DV, GLA_DVP)
    ggk = jnp.pad(wi[:, g0 + 2 * qk + vw:g0 + 2 * qk + vw + GLA_GATE_LORA], ((0, 0), (0, LANES - GLA_GATE_LORA)))
    gg = _pad_heads(wi[:, g0 + 2 * qk + vw + GLA_GATE_LORA:], GLA_HEADS, GLA_DV, GLA_DVP)
    w_in_p = _bf(jnp.concatenate([wi[:, :g0], gq, gkk, gv, ggk, gg], axis=1))
    wo = w_out[l]
    m0 = RET_W + RWKV_W
    wo_gla = jnp.pad(wo[m0:].reshape(GLA_HEADS, GLA_DV, D_MODEL), ((0, 0), (0, GLA_DVP - GLA_DV), (0, 0)))
    w_out_p = _bf(jnp.concatenate([wo[:m0], wo_gla.reshape(GLA_VW, D_MODEL)], axis=0))
    zeros = jnp.zeros((DECAY_LORA, RWKV_W), F32)
    rw = dict(
        mu=rwkv_mu[l][None, :],
        wwa=jnp.concatenate([jnp.concatenate([rwkv_w_up[l], zeros], axis=1),
                             jnp.concatenate([zeros, rwkv_a_up[l]], axis=1)], axis=0),
        w_bias=rwkv_w_bias[l][None, :], a_bias=rwkv_a_bias[l][None, :], g_up=_bf(rwkv_g_up[l]),
        k_k=rwkv_k_k[l][None, :], k_a=rwkv_k_a[l][None, :], r_k=rwkv_r_k[l].reshape(1, RWKV_W),
        ln_w=rwkv_ln_w[l][None, :], ln_b=rwkv_ln_b[l][None, :])
    gl = dict(
        gk_up=jnp.pad(_pad_heads(gla_gk_up[l], GLA_HEADS, GLA_DK, GLA_DKP), ((0, LANES - GLA_GATE_LORA), (0, 0))),
        gk_bias=_pad_heads(gla_gk_bias[l][None, :], GLA_HEADS, GLA_DK, GLA_DKP),
        norm_w=_pad_heads(gla_norm_w[l][None, :], GLA_HEADS, GLA_DV, GLA_DVP))
    wq = peer_w_q[l].reshape(D_MODEL, PEER_HEADS, 2, PEER_QHALF)
    wq_t = _bf(jnp.transpose(wq, (2, 1, 3, 0)).reshape(2 * PEER_HEADS * PEER_QHALF, D_MODEL))
    keys = _bf(jnp.transpose(peer_sub_keys[l], (1, 0, 2, 3)))
    return dict(w_in=w_in_p, w_out=w_out_p, rwkv=rw, gla=gl, wq_t=wq_t, keys=keys, u=_pack_rows(_bf(peer_u[l])),
                vt=_pack_rows(_bf(peer_v[l].T)))


def _layer(x2, b, s, pr, gain_mix, gain_ffn, gain_final, final_norm):
    z_ret, z_rwkv, z_gla = _norm_proj(x2, gain_mix, pr["w_in"])
    o_ret = _retention(z_ret.reshape(b, s, RET_IN)).reshape(b * s, RET_W)
    o_rwkv = _rwkv(z_rwkv.reshape(b, s, RWKV_IN), pr["rwkv"]).reshape(b * s, RWKV_W)
    o_gla = _gla(z_gla.reshape(b, s, GLA_IN), pr["gla"]).reshape(b * s, GLA_VW)
    x1, hnt = _out_proj(x2, o_ret, o_rwkv, o_gla, pr["w_out"], gain_ffn)
    n1, e1, r2, e2 = _route(hnt, pr["wq_t"], pr["keys"])
    return _experts(hnt, pr["u"], pr["vt"], n1, e1, r2, e2, x1, gain_final, final_norm)


def kernel(x, norm_mix, norm_ffn, norm_final, w_in, w_out, rwkv_mu, rwkv_w_up, rwkv_w_bias, rwkv_a_up, rwkv_a_bias, rwkv_g_up, rwkv_k_k, rwkv_k_a, rwkv_r_k, rwkv_ln_w, rwkv_ln_b, gla_gk_up, gla_gk_bias, gla_norm_w, peer_w_q, peer_sub_keys, peer_u, peer_v):
    b, s, d = x.shape
    x2 = x.reshape(b * s, d)
    gain_final = norm_final[None, :]
    for l in range(DEPTH):
        pr = _layer_params(l, w_in, w_out, rwkv_mu, rwkv_w_up, rwkv_w_bias, rwkv_a_up, rwkv_a_bias, rwkv_g_up,
                           rwkv_k_k, rwkv_k_a, rwkv_r_k, rwkv_ln_w, rwkv_ln_b, gla_gk_up, gla_gk_bias, gla_norm_w,
                           peer_w_q, peer_sub_keys, peer_u, peer_v)
        x2 = _layer(x2, b, s, pr, norm_mix[l][None, :], norm_ffn[l][None, :], gain_final, l == DEPTH - 1)
    return x2.reshape(b, s, d)
```

```python
import functools

import numpy as np
import jax
import jax.numpy as jnp
from jax import lax
from jax.experimental import pallas as pl
from jax.experimental.pallas import tpu as pltpu

F32 = jnp.float32
BF16 = jnp.bfloat16
HI = lax.Precision.HIGHEST

D_MODEL = 1024
DEPTH = 2
NORM_EPS = 1e-6
HEAD_NORM_EPS = 1e-5
RET_HEADS, RET_DK, RET_DV, ROPE_BASE = 4, 64, 64, 10000.0
RET_W = RET_HEADS * RET_DK
RET_CHUNK = 128
RWKV_HEADS, RWKV_HEAD = 6, 64
RWKV_W = RWKV_HEADS * RWKV_HEAD
DECAY_LORA, AAA_LORA, GATE_LORA = 64, 64, 128
RWKV_GN_EPS = 64e-5
RWKV_CHUNK = 64
RWKV_SUB = 16
RWKV_IN = 3 * RWKV_W + DECAY_LORA + AAA_LORA + GATE_LORA
GLA_HEADS, GLA_DK, GLA_DV, GLA_GATE_LORA = 4, 48, 96, 16
GLA_GATE_NORMALIZER = 16.0
GLA_DKP, GLA_DVP = 64, 128
GLA_QW = GLA_HEADS * GLA_DKP
GLA_VW = GLA_HEADS * GLA_DVP
GLA_IN = 2 * GLA_QW + GLA_VW + 128 + GLA_VW
GLA_SUB = 16
GLA_BLOCK = 256
RET_IN = 4 * RET_W
D_INP = RET_IN + RWKV_IN + GLA_IN
D_MIXP = RET_W + RWKV_W + GLA_VW
PEER_HEADS, PEER_NKEYS, PEER_QHALF, PEER_TOPK = 8, 128, 128, 16
PEER_NEXPERTS = PEER_NKEYS * PEER_NKEYS
LANES = 128
SUBLANES = 8
VMEM_LIMIT = 56 * 1024 * 1024


def _params(sem):
    return pltpu.CompilerParams(dimension_semantics=sem, vmem_limit_bytes=VMEM_LIMIT)


def _dot(a, b, prec=None):
    return jnp.dot(a, b, precision=prec, preferred_element_type=F32)


def _dot_nt(a, b, prec=None):
    return lax.dot_general(a, b, (((1,), (1,)), ((), ())), precision=prec, preferred_element_type=F32)


def _dot_tn(a, b, prec=None):
    return lax.dot_general(a, b, (((0,), (0,)), ((), ())), precision=prec, preferred_element_type=F32)


def _bf(x):
    return x.astype(BF16)


def _iota(shape, dim):
    return lax.broadcasted_iota(jnp.int32, shape, dim)


def _same_block(shape, rblk, cblk):
    r = _iota(shape, 0) >> (rblk.bit_length() - 1)
    c = _iota(shape, 1) >> (cblk.bit_length() - 1)
    return r == c


def _full(shape):
    n = len(shape)
    return pl.BlockSpec(shape, lambda *_: (0,) * n)


def _block_ones(n, blk):
    i = np.arange(n) // blk
    return jnp.asarray((i[:, None] == i[None, :]).astype(np.float32))


def _tril_ones(n):
    return jnp.asarray(np.tril(np.ones((n, n), np.float32)))


def _norm_proj_kernel(x_ref, g_ref, w_ref, zr_ref, zk_ref, zg_ref):
    x = x_ref[...]
    y = x * lax.rsqrt(jnp.mean(x * x, axis=-1, keepdims=True) + NORM_EPS) * g_ref[...]
    yb = _bf(y)
    zr_ref[...] = _dot(yb, w_ref[:, 0:RET_IN])
    zk_ref[...] = _dot(yb, w_ref[:, RET_IN:RET_IN + RWKV_IN])
    zg_ref[...] = _dot(yb, w_ref[:, RET_IN + RWKV_IN:D_INP])


def _norm_proj(x2, gain, w_in_p, tm=512):
    t = x2.shape[0]
    return pl.pallas_call(
        _norm_proj_kernel,
        grid=(t // tm,),
        in_specs=[pl.BlockSpec((tm, D_MODEL), lambda i: (i, 0)), _full((1, D_MODEL)), _full((D_MODEL, D_INP))],
        out_specs=[pl.BlockSpec((tm, RET_IN), lambda i: (i, 0)), pl.BlockSpec((tm, RWKV_IN), lambda i: (i, 0)),
                   pl.BlockSpec((tm, GLA_IN), lambda i: (i, 0))],
        out_shape=[jax.ShapeDtypeStruct((t, RET_IN), F32), jax.ShapeDtypeStruct((t, RWKV_IN), F32),
                   jax.ShapeDtypeStruct((t, GLA_IN), F32)],
        compiler_params=_params(("arbitrary",)),
        name="norm_proj",
    )(x2, gain, w_in_p)


RET_BLOCK = 512


def _ret_kernel(z_ref, cos_ref, sin_ref, dmat_ref, qdec_ref, kdec_ref, cdec_ref, ones_ref, o_ref, st_ref):
    c = RET_CHUNK
    tb = RET_BLOCK

    @pl.when(pl.program_id(1) == 0)
    def _():
        st_ref[...] = jnp.zeros_like(st_ref)

    z = z_ref[0]
    q, k, v, g = (z[:, i * RET_W:(i + 1) * RET_W] for i in range(4))
    cos, sin = cos_ref[...], sin_ref[...]
    first_half = (_iota((tb, RET_W), 1) & (RET_DK - 1)) < RET_DK // 2

    def rot(t):
        return jnp.where(first_half, -pltpu.roll(t, RET_W - RET_DK // 2, 1), pltpu.roll(t, RET_DK // 2, 1))

    q = q * cos + rot(q) * sin
    k = (k * cos + rot(k) * sin) * (RET_DK ** -0.5)
    qd = q * qdec_ref[...]
    kd = k * kdec_ref[...]
    head0 = _iota((c, LANES), 1) < RET_DK
    bd = _same_block((LANES, LANES), RET_DV, RET_DK)
    units = [(ch, p) for ch in range(tb // c) for p in range(RET_HEADS // 2)]
    tile = lambda t, u: t[u[0] * c:(u[0] + 1) * c, u[1] * LANES:(u[1] + 1) * LANES]
    kb = [_bf(tile(k, u)) for u in units]
    intra = []
    for hh in range(2):
        mh = head0 if hh == 0 else jnp.logical_not(head0)
        sc = [_dot_nt(_bf(jnp.where(mh, tile(q, u), 0.0)), kb[i]) * dmat_ref[2 * u[1] + hh] for i, u in enumerate(units)]
        intra.append([_dot(_bf(s_), _bf(jnp.where(mh, tile(v, u), 0.0))) for s_, u in zip(sc, units)])
    upd = [jnp.where(bd, _dot_tn(_bf(tile(v, u)), _bf(tile(kd, u))), 0.0) for u in units]
    qdb = [_bf(tile(qd, u)) for u in units]
    st = [st_ref[p] for p in range(RET_HEADS // 2)]
    rows = []
    for ch in range(tb // c):
        row = []
        for p in range(RET_HEADS // 2):
            i = units.index((ch, p))
            row.append(intra[0][i] + intra[1][i] + _dot_nt(qdb[i], _bf(st[p])))
            st[p] = st[p] * cdec_ref[:, p * LANES:(p + 1) * LANES] + upd[i]
        rows.append(jnp.concatenate(row, axis=1))
    for p in range(RET_HEADS // 2):
        st_ref[p] = st[p]
    o = jnp.concatenate(rows, axis=0)
    ms = _dot_exact_rhs(o * o, ones_ref[...], 2) * (1.0 / RET_DV)
    o = o * lax.rsqrt(ms + HEAD_NORM_EPS)
    o_ref[0] = o * (g * jax.nn.sigmoid(g))


def _retention_consts(s):
    c = RET_CHUNK
    reps = RET_BLOCK // c
    log_gamma = np.log1p(-np.exp(np.linspace(np.log(1.0 / 32.0), np.log(1.0 / 512.0), RET_HEADS)))
    idx = np.arange(c, dtype=np.float64)
    diff = idx[:, None] - idx[None, :]
    dmat = np.where(diff >= 0, np.exp(log_gamma[:, None, None] * np.where(diff >= 0, diff, 0.0)), 0.0)
    lg_lane = np.repeat(log_gamma, RET_DK)[None, :]
    qdec = np.tile(np.exp(lg_lane * (idx[:, None] + 1.0)), (reps, 1))
    kdec = np.tile(np.exp(lg_lane * (c - 1.0 - idx[:, None])), (reps, 1))
    cdec = np.exp(lg_lane * c)
    half = RET_DK // 2
    inv = ROPE_BASE ** (-np.arange(half, dtype=np.float64) / half)
    ang = np.arange(s, dtype=np.float64)[:, None] * inv[None, :]
    cos = np.tile(np.cos(ang), (1, 2 * RET_HEADS))
    sin = np.tile(np.sin(ang), (1, 2 * RET_HEADS))
    f = lambda a: jnp.asarray(a.astype(np.float32))
    return f(cos), f(sin), f(dmat), f(qdec), f(kdec), f(cdec)


def _retention(z_ret):
    b, s, _ = z_ret.shape
    c = RET_CHUNK
    tb = RET_BLOCK
    cos, sin, dmat, qdec, kdec, cdec = _retention_consts(s)
    return pl.pallas_call(
        _ret_kernel,
        grid=(b, s // tb),
        in_specs=[pl.BlockSpec((1, tb, RET_IN), lambda i, j: (i, j, 0)),
                  pl.BlockSpec((tb, RET_W), lambda i, j: (j, 0)), pl.BlockSpec((tb, RET_W), lambda i, j: (j, 0)),
                  _full((RET_HEADS, c, c)), _full((tb, RET_W)), _full((tb, RET_W)), _full((1, RET_W)),
                  _full((RET_W, RET_W))],
        out_specs=pl.BlockSpec((1, tb, RET_W), lambda i, j: (i, j, 0)),
        out_shape=jax.ShapeDtypeStruct((b, s, RET_W), F32),
        scratch_shapes=[pltpu.VMEM((RET_HEADS // 2, LANES, LANES), F32)],
        compiler_params=_params(("arbitrary", "arbitrary")),
        name="retention",
    )(z_ret, cos, sin, dmat, qdec, kdec, cdec, _bf(_block_ones(RET_W, RET_DV)))


RWKV_BLOCK = 256
INV_PASSES = 3


def _split(x, parts):
    out = []
    for _ in range(parts - 1):
        h = _bf(x)
        out.append(h)
        x = x - h.astype(F32)
    out.append(_bf(x))
    return out


def _dot_exact_lhs(a_bf, x, parts):
    acc = None
    for p in _split(x, parts):
        t = _dot(a_bf, p)
        acc = t if acc is None else acc + t
    return acc


def _dot_exact_rhs(x, b_bf, parts):
    acc = None
    for p in _split(x, parts):
        t = _dot(p, b_bf)
        acc = t if acc is None else acc + t
    return acc


def _dot_f32(a, b, passes):
    if passes == 1:
        return _dot(_bf(a), _bf(b))
    if passes == 3:
        ah, al = _split(a, 2)
        bh, bl = _split(b, 2)
        return _dot(ah, bh) + (_dot(ah, bl) + _dot(al, bh))
    return _dot(a, b, HI)


def _rwkv_kernel(z_ref, mu_ref, wwa_ref, wb_ref, ab_ref, gup_ref, kk_ref, ka_ref, rk_ref, lnw_ref, lnb_ref,
                 ones_ref, ltri_ref, o_ref, st_ref, carry_ref):
    c = RWKV_CHUNK
    tb = RWKV_BLOCK
    w_ = RWKV_W

    @pl.when(pl.program_id(1) == 0)
    def _():
        st_ref[...] = jnp.zeros_like(st_ref)
        carry_ref[...] = jnp.zeros_like(carry_ref)

    f = z_ref[0]
    f_prev = jnp.where(_iota((tb, RWKV_IN), 0) == 0, carry_ref[...], pltpu.roll(f, 1, 0))
    carry_ref[...] = f[tb - 1:tb, :]
    f = f + (f_prev - f) * mu_ref[...]
    r, k, v = f[:, 0:w_], f[:, w_:2 * w_], f[:, 2 * w_:3 * w_]
    wa, gd = f[:, 3 * w_:3 * w_ + LANES], f[:, 3 * w_ + LANES:RWKV_IN]
    lane = _iota((tb, LANES), 1)
    proj = _dot_f32(jnp.where(lane < DECAY_LORA, jnp.tanh(wa), wa), wwa_ref[...], 3)
    wlog = -jax.nn.softplus(-(wb_ref[...] + proj[:, 0:w_])) - 0.5
    ld = -jnp.exp(wlog)
    a = jax.nn.sigmoid(ab_ref[...] + proj[:, w_:2 * w_])
    g = _dot(_bf(jax.nn.sigmoid(gd)), gup_ref[...])
    ones = ones_ref[...]
    kk = k * kk_ref[...]
    kk = kk * lax.rsqrt(jnp.maximum(_dot_exact_rhs(kk * kk, ones, 2), 1e-24))
    k = k * (1.0 + (a - 1.0) * ka_ref[...])
    cum = _dot_exact_lhs(ltri_ref[...], ld, 3)
    e_neg = jnp.exp(-cum)
    b = kk * a
    a_t_all = -kk * jnp.exp(cum - ld)
    b_t_all, k_t_all, r_t_all = b * e_neg, k * e_neg, r * jnp.exp(cum)

    n2 = 2 * c
    ri, ci = _iota((n2, n2), 0), _iota((n2, n2), 1)
    same_head = _same_block((n2, n2), c, c)
    strict = jnp.logical_and(same_head, ri > ci)
    incl = jnp.logical_and(same_head, ri >= ci)
    sub = _same_block((n2, n2), RWKV_SUB, RWKV_SUB)
    eye = (ri == ci).astype(F32)
    bd = _same_block((LANES, LANES), RWKV_HEAD, RWKV_HEAD)
    head0 = _iota((c, LANES), 1) < RWKV_HEAD
    mm = functools.partial(_dot_f32, passes=INV_PASSES)
    mm1 = functools.partial(_dot_f32, passes=1)

    def stack(x):
        return jnp.concatenate([jnp.where(head0, x, 0.0), jnp.where(head0, 0.0, x)], axis=0)

    def fold(x):
        return x[:c] + x[c:]

    units = [(ch, p) for ch in range(tb // c) for p in range(RWKV_HEADS // 2)]
    each = lambda fn, *lists: [fn(*args) for args in zip(*lists)]
    rows_of = lambda ch: slice(ch * c, (ch + 1) * c)
    lanes_of = lambda p: slice(p * LANES, (p + 1) * LANES)
    tile = lambda t: [stack(t[rows_of(ch), lanes_of(p)]) for ch, p in units]
    a_s, r_s, b_s, k_s, v_s = tile(a_t_all), tile(r_t_all), tile(b_t_all), tile(k_t_all), tile(v)
    cl = [cum[(ch + 1) * c - 1:(ch + 1) * c, lanes_of(p)] for ch, p in units]
    e_rem = [jnp.exp(cl_u - cum[rows_of(ch), lanes_of(p)]) for cl_u, (ch, p) in zip(cl, units)]
    b_h = [b[rows_of(ch), lanes_of(p)] * e for e, (ch, p) in zip(e_rem, units)]
    k_h = [k[rows_of(ch), lanes_of(p)] * e for e, (ch, p) in zip(e_rem, units)]
    v_p = [v[rows_of(ch), lanes_of(p)] for ch, p in units]
    big = each(lambda x, y, z, w: _dot_nt(_bf(jnp.concatenate([x, y], axis=0)), _bf(jnp.concatenate([z, w], axis=0))),
               a_s, r_s, b_s, k_s)
    a_ab = [jnp.where(strict, t[:n2, :n2], 0.0) for t in big]
    a_ak = [jnp.where(strict, t[:n2, n2:], 0.0) for t in big]
    a_rb = [jnp.where(incl, t[n2:, :n2], 0.0) for t in big]
    a_rk = [jnp.where(incl, t[n2:, n2:], 0.0) for t in big]
    d1 = [jnp.where(sub, t, 0.0) for t in a_ab]
    lo = each(lambda x, y: x - y, a_ab, d1)
    d2 = each(mm, d1, d1)
    t_d = each(lambda x, y: mm1(eye + x, eye + y), d1, d2)
    d4 = each(mm, d2, d2)
    t_d = each(lambda x, y: mm1(x, eye + y), t_d, d4)
    d8 = each(mm, d4, d4)
    t_d = each(lambda x, y: mm1(x, eye + y), t_d, d8)
    e1 = each(mm1, t_d, lo)
    e2 = each(mm1, e1, e1)
    t_inv = each(lambda x, y: mm1(eye + x, eye + y), e1, e2)
    t_inv = each(mm1, t_inv, t_d)
    akv = each(lambda x, y: _dot(_bf(x), _bf(y)), a_ak, v_s)
    wy = each(lambda t, x, y: mm1(t, jnp.concatenate([x, y], axis=1)), t_inv, a_s, akv)
    qo = each(lambda x, y: _dot(_bf(x), _bf(y)), a_rb, wy)
    rkv = each(lambda x, y: _dot(_bf(x), _bf(y)), a_rk, v_s)
    w_p = [fold(t[:, :LANES]) for t in wy]
    y_p = [fold(t[:, LANES:]) for t in wy]
    q_p = each(lambda x, y: fold(x + y[:, :LANES]), r_s, qo)
    ol_p = each(lambda x, y: fold(x[:, LANES:] + y), qo, rkv)
    gmat = each(lambda x, y: jnp.where(bd, _dot_tn(_bf(x), _bf(y)), 0.0), w_p, b_h)
    nmat = each(lambda y, vv, bb, kk_: jnp.where(bd, _dot_tn(_bf(jnp.concatenate([y, vv], axis=0)),
                                                             _bf(jnp.concatenate([bb, kk_], axis=0))), 0.0),
                y_p, v_p, b_h, k_h)
    pre = {u: (q_p[i], ol_p[i], gmat[i], nmat[i], jnp.exp(cl[i])) for i, u in enumerate(units)}

    outs = [[None] * (RWKV_HEADS // 2) for _ in range(tb // c)]
    st = [st_ref[p] for p in range(RWKV_HEADS // 2)]
    for ch in range(tb // c):
        for p in range(RWKV_HEADS // 2):
            q_u, ol_u, g_u, n_u, pc_u = pre[ch, p]
            stb = _bf(st[p])
            outs[ch][p] = _dot_nt(_bf(q_u), stb) + ol_u
            st[p] = st[p] * pc_u + _dot(stb, _bf(g_u)) + n_u
    for p in range(RWKV_HEADS // 2):
        st_ref[p] = st[p]
    o = jnp.concatenate([jnp.concatenate(row, axis=1) for row in outs], axis=0)
    inv_n = 1.0 / RWKV_HEAD
    d = o - _dot_exact_rhs(o, ones, 2) * inv_n
    var = _dot_exact_rhs(d * d, ones, 2) * inv_n
    o = d * lax.rsqrt(var + RWKV_GN_EPS) * lnw_ref[...] + lnb_ref[...]
    o = o + _dot_exact_rhs(r * k * rk_ref[...], ones, 2) * v
    o_ref[0] = o * g


def _rwkv(z_rwkv, pr):
    b, s, _ = z_rwkv.shape
    tb = RWKV_BLOCK
    row = lambda n: _full((1, n))
    ltri = np.kron(np.eye(tb // RWKV_CHUNK, dtype=np.float32), np.tril(np.ones((RWKV_CHUNK, RWKV_CHUNK), np.float32)))
    return pl.pallas_call(
        _rwkv_kernel,
        grid=(b, s // tb),
        in_specs=[pl.BlockSpec((1, tb, RWKV_IN), lambda i, j: (i, j, 0)), row(RWKV_IN), _full((LANES, 2 * RWKV_W)),
                  row(RWKV_W), row(RWKV_W), _full((GATE_LORA, RWKV_W)), row(RWKV_W), row(RWKV_W), row(RWKV_W),
                  row(RWKV_W), row(RWKV_W), _full((RWKV_W, RWKV_W)), _full((tb, tb))],
        out_specs=pl.BlockSpec((1, tb, RWKV_W), lambda i, j: (i, j, 0)),
        out_shape=jax.ShapeDtypeStruct((b, s, RWKV_W), F32),
        scratch_shapes=[pltpu.VMEM((RWKV_HEADS // 2, LANES, LANES), F32), pltpu.VMEM((1, RWKV_IN), F32)],
        compiler_params=_params(("arbitrary", "arbitrary")),
        name="rwkv7",
    )(z_rwkv, pr["mu"], pr["wwa"], pr["w_bias"], pr["a_bias"], pr["g_up"], pr["k_k"], pr["k_a"], pr["r_k"],
      pr["ln_w"], pr["ln_b"], _bf(_block_ones(RWKV_W, RWKV_HEAD)), jnp.asarray(ltri, dtype=BF16))


def _gla_kernel(z_ref, gkup_ref, gkb_ref, nw_ref, ind_ref, ones_ref, ltri_ref, o_ref, st_ref, x_ref):
    n = GLA_SUB
    tb = GLA_BLOCK
    nsub = tb // n

    @pl.when(pl.program_id(1) == 0)
    def _():
        st_ref[...] = jnp.zeros_like(st_ref)

    z = z_ref[0]
    q_all = z[:, 0:GLA_QW] * (GLA_DK ** -0.5)
    k_all = z[:, GLA_QW:2 * GLA_QW]
    v_all = z[:, 2 * GLA_QW:2 * GLA_QW + GLA_VW]
    gkd = z[:, 2 * GLA_QW + GLA_VW:2 * GLA_QW + GLA_VW + LANES]
    g = z[:, 2 * GLA_QW + GLA_VW + LANES:GLA_IN]
    gk = jax.nn.log_sigmoid(_dot_f32(gkd, gkup_ref[...], 3) + gkb_ref[...]) * (1.0 / GLA_GATE_NORMALIZER)
    bc_all = _dot_exact_lhs(ltri_ref[...], gk, 3)
    bd = _same_block((GLA_VW, GLA_QW), GLA_DVP, GLA_DKP)
    rowi = _iota((n, GLA_QW), 0)
    subs = range(nsub)
    rows = [slice(sc * n, (sc + 1) * n) for sc in subs]
    q = [q_all[r] for r in rows]
    k = [k_all[r] for r in rows]
    v = [v_all[r] for r in rows]
    bc = [bc_all[r] for r in rows]
    bl = [t[n - 1:n, :] for t in bc]
    for sc in subs:
        for j in range(n):
            xj = q[sc] * jnp.exp(jnp.minimum(bc[sc] - bc[sc][j:j + 1, :], 0.0)) * k[sc][j:j + 1, :]
            x_ref[(sc * n + j) * n:(sc * n + j + 1) * n, :] = jnp.where(rowi >= j, xj, 0.0)
    x = x_ref[...]
    xh = _bf(x)
    xl = _bf(x - xh.astype(F32))
    ind = ind_ref[...]
    e = _dot(xh, ind) + _dot(xl, ind)
    o_intra = []
    for sc in subs:
        acc = None
        for j in range(n):
            t = e[(sc * n + j) * n:(sc * n + j + 1) * n, :] * v[sc][j:j + 1, :]
            acc = t if acc is None else acc + t
        o_intra.append(acc)
    upd = [jnp.where(bd, _dot_tn(_bf(v[sc]), _bf(k[sc] * jnp.exp(bl[sc] - bc[sc]))), 0.0) for sc in subs]
    qe = [_bf(q[sc] * jnp.exp(bc[sc])) for sc in subs]
    st = st_ref[...]
    outs = []
    for sc in subs:
        outs.append(_dot_nt(qe[sc], _bf(st)) + o_intra[sc])
        st = st * jnp.exp(bl[sc]) + upd[sc]
    st_ref[...] = st
    o = jnp.concatenate(outs, axis=0)
    ms = _dot_exact_rhs(o * o, ones_ref[...], 2) * (1.0 / GLA_DV)
    o_ref[0] = o * lax.rsqrt(ms + HEAD_NORM_EPS) * nw_ref[...] * (g * jax.nn.sigmoid(g))


def _gla(z_gla, pr):
    b, s, _ = z_gla.shape
    tb = GLA_BLOCK
    hq = np.arange(GLA_QW) // GLA_DKP
    hv = np.arange(GLA_VW) // GLA_DVP
    ind = jnp.asarray((hq[:, None] == hv[None, :]).astype(np.float32), dtype=BF16)
    ltri = np.kron(np.eye(tb // GLA_SUB, dtype=np.float32), np.tril(np.ones((GLA_SUB, GLA_SUB), np.float32)))
    return pl.pallas_call(
        _gla_kernel,
        grid=(b, s // tb),
        in_specs=[pl.BlockSpec((1, tb, GLA_IN), lambda i, j: (i, j, 0)), _full((LANES, GLA_QW)), _full((1, GLA_QW)),
                  _full((1, GLA_VW)), _full((GLA_QW, GLA_VW)), _full((GLA_VW, GLA_VW)), _full((tb, tb))],
        out_specs=pl.BlockSpec((1, tb, GLA_VW), lambda i, j: (i, j, 0)),
        out_shape=jax.ShapeDtypeStruct((b, s, GLA_VW), F32),
        scratch_shapes=[pltpu.VMEM((GLA_VW, GLA_QW), F32), pltpu.VMEM((tb * GLA_SUB, GLA_QW), F32)],
        compiler_params=_params(("arbitrary", "arbitrary")),
        name="gla",
    )(z_gla, pr["gk_up"], pr["gk_bias"], pr["norm_w"], ind, _bf(_block_ones(GLA_VW, GLA_DVP)),
      jnp.asarray(ltri, dtype=BF16))


def _out_proj_kernel(x_ref, a_ref, b_ref, c_ref, w_ref, g_ref, x1_ref, hnt_ref):
    acc = _dot(_bf(a_ref[...]), w_ref[0:RET_W, :])
    acc += _dot(_bf(b_ref[...]), w_ref[RET_W:RET_W + RWKV_W, :])
    acc += _dot(_bf(c_ref[...]), w_ref[RET_W + RWKV_W:D_MIXP, :])
    x1 = x_ref[...] + acc
    x1_ref[...] = x1
    hn = x1 * lax.rsqrt(jnp.mean(x1 * x1, axis=-1, keepdims=True) + NORM_EPS) * g_ref[...]
    hnt_ref[...] = pltpu.bitcast(_bf(hn.T), jnp.uint32)


def _out_proj(x2, o_ret, o_rwkv, o_gla, w_out_p, gain, tm=512):
    t = x2.shape[0]
    blk = lambda n: pl.BlockSpec((tm, n), lambda i: (i, 0))
    return pl.pallas_call(
        _out_proj_kernel,
        grid=(t // tm,),
        in_specs=[blk(D_MODEL), blk(RET_W), blk(RWKV_W), blk(GLA_VW), _full((D_MIXP, D_MODEL)), _full((1, D_MODEL))],
        out_specs=[blk(D_MODEL), pl.BlockSpec((D_MODEL // 2, tm), lambda i: (0, i))],
        out_shape=[jax.ShapeDtypeStruct((t, D_MODEL), F32), jax.ShapeDtypeStruct((D_MODEL // 2, t), jnp.uint32)],
        compiler_params=_params(("arbitrary",)),
        name="out_proj",
    )(x2, o_ret, o_rwkv, o_gla, w_out_p, gain)


def _staircase_pairs():
    k = PEER_TOPK
    return [(a, b) for a in range(k) for b in range(k) if (a + 1) * (b + 1) <= k]


def _tree_max(xs):
    xs = list(xs)
    while len(xs) > 1:
        nxt = [jnp.maximum(xs[2 * i], xs[2 * i + 1]) for i in range(len(xs) // 2)]
        if len(xs) % 2:
            nxt.append(xs[-1])
        xs = nxt
    return xs[0]


def _route_kernel(hnt_ref, wq_ref, keys_ref, n1_ref, e1_ref, r2_ref, e2_ref, work_ref, s_ref, vals_ref, rank_ref):
    nh, nk, k_top = PEER_HEADS, PEER_NKEYS, PEER_TOPK
    qt = _bf(_dot(wq_ref[...], pltpu.bitcast(hnt_ref[...], BF16)))
    for p in range(2):
        for h in range(nh):
            r0 = (p * nh + h) * PEER_QHALF
            s = _dot(keys_ref[p, h], qt[r0:r0 + PEER_QHALF, :])
            work_ref[p, h] = s
            s_ref[p, h] = s
    rank_ref[...] = jnp.full(rank_ref.shape, float(k_top), F32)
    groups = nk // SUBLANES

    def extract(r, carry):
        rf = jnp.asarray(r, F32)
        for p in range(2):
            for h in range(nh):
                tiles = [work_ref[p, h, i * SUBLANES:(i + 1) * SUBLANES, :] for i in range(groups)]
                m = _tree_max(tiles)
                for sh in (4, 2, 1):
                    m = jnp.maximum(m, pltpu.roll(m, sh, 0))
                vals_ref[p, r, pl.ds(h, 1), :] = m[0:1, :]
                for i in range(groups):
                    rows = slice(i * SUBLANES, (i + 1) * SUBLANES)
                    is_max = tiles[i] == m
                    work_ref[p, h, rows, :] = jnp.where(is_max, -jnp.inf, tiles[i])
                    rank_ref[p, h, rows, :] = jnp.where(is_max, rf, rank_ref[p, h, rows, :])
        return carry

    lax.fori_loop(0, k_top, extract, 0)

    v1 = [vals_ref[0, r] for r in range(k_top)]
    v2 = [vals_ref[1, r] for r in range(k_top)]
    cand = {(a, b): v1[a] + v2[b] for a, b in _staircase_pairs()}
    work = list(cand.values())
    tau = None
    for it in range(k_top):
        tau = _tree_max(work)
        if it + 1 < k_top:
            work = [jnp.where(w == tau, -jnp.inf, w) for w in work]
    top = cand[(0, 0)]
    z = None
    for c in cand.values():
        zi = jnp.where(c >= tau, jnp.exp(c - top), 0.0)
        z = zi if z is None else z + zi
    scale2 = 0.5 / z
    n_of_rank = []
    for a in range(k_top):
        cnt = None
        for b in range(k_top // (a + 1)):
            ge = (cand[(a, b)] >= tau).astype(F32)
            cnt = ge if cnt is None else cnt + ge
        n_of_rank.append(cnt)
    m1, m2 = v1[0], v2[0]
    for h in range(nh):
        r1 = rank_ref[0, h]
        n1 = jnp.zeros_like(r1)
        for a in range(k_top):
            n1 = jnp.where(r1 == float(a), n_of_rank[a][h:h + 1, :], n1)
        n1_ref[h] = n1
        e1_ref[h] = jnp.exp(s_ref[0, h] - m1[h:h + 1, :])
        r2_ref[h] = pltpu.bitcast(_bf(rank_ref[1, h]), jnp.uint32)
        e2_ref[h] = pltpu.bitcast(_bf(jnp.exp(s_ref[1, h] - m2[h:h + 1, :]) * scale2[h:h + 1, :]), jnp.uint32)


def _route(hnt, wq_t, keys, tb=256):
    t = hnt.shape[1]
    nh, nk = PEER_HEADS, PEER_NKEYS
    out = pl.BlockSpec((nh, nk, tb), lambda i: (0, 0, i))
    packed = pl.BlockSpec((nh, nk // 2, tb), lambda i: (0, 0, i))
    return pl.pallas_call(
        _route_kernel,
        grid=(t // tb,),
        in_specs=[pl.BlockSpec((D_MODEL // 2, tb), lambda i: (0, i)), _full((2 * nh * PEER_QHALF, D_MODEL)),
                  _full((2, nh, nk, PEER_QHALF))],
        out_specs=[out, out, packed, packed],
        out_shape=[jax.ShapeDtypeStruct((nh, nk, t), F32)] * 2 + [jax.ShapeDtypeStruct((nh, nk // 2, t), jnp.uint32)] * 2,
        scratch_shapes=[pltpu.VMEM((2, nh, nk, tb), F32), pltpu.VMEM((2, nh, nk, tb), F32),
                        pltpu.VMEM((2, PEER_TOPK, nh, tb), F32), pltpu.VMEM((2, nh, nk, tb), F32)],
        compiler_params=_params(("arbitrary",)),
        name="peer_route",
    )(hnt, wq_t, keys)


EXPERT_SLABS = 8
EXPERT_PAIR = 2
PIPE_LAG = 2
ROW_SPLIT = 2


def _expert_kernel(hnt_ref, u_ref, vt_ref, n1_ref, e1_ref, r2_ref, e2_ref, x1_ref, gf_ref, o_ref, acc_ref, ht0, ht1,
                   act0, act1, rn_ref, re_ref, *, n_blocks, final_norm):
    s = pl.program_id(0)
    n_pairs = pl.num_programs(0) - PIPE_LAG
    nk = PEER_NKEYS
    tb = hnt_ref.shape[1]
    j2 = lax.rem(jnp.clip(s - 1, 0, n_pairs - 1), n_blocks)
    j3 = lax.rem(jnp.clip(s - PIPE_LAG, 0, n_pairs - 1), n_blocks)
    live = jnp.logical_and(s >= 1, s <= n_pairs).astype(F32)

    @pl.when(s == 0)
    def _():
        for r in (ht0, ht1, act0, act1):
            r[...] = jnp.zeros_like(r)

    @pl.when(j3 == 0)
    def _():
        acc_ref[...] = jnp.zeros_like(acc_ref)

    for h in range(PEER_HEADS):
        for k in range(EXPERT_SLABS):
            r = h * EXPERT_SLABS + k
            rn_ref[r:r + 1, :] = n1_ref[h, pl.ds(j2 * EXPERT_SLABS + k, 1), :]
            re_ref[r:r + 1, :] = e1_ref[h, pl.ds(j2 * EXPERT_SLABS + k, 1), :] * live

    def stages(ht_w, ht_r, act_w, act_r):
        half_w = 2 * LANES
        kc = 2 * LANES
        n_kc = D_MODEL // kc
        n_ec = (EXPERT_SLABS * nk) // kc

        assert n_kc == n_ec
        rws = (EXPERT_SLABS * nk) // ROW_SPLIT
        mws = D_MODEL // ROW_SPLIT
        hid = {}
        prj = {}

        def hidden_piece(half, c, rs):
            cols = slice(half * half_w, (half + 1) * half_w)
            t = _dot(pltpu.bitcast(u_ref[rs * rws // 2:(rs + 1) * rws // 2, c * kc:(c + 1) * kc], BF16),
                     pltpu.bitcast(hnt_ref[c * kc // 2:(c + 1) * kc // 2, cols], BF16))
            hid[half, rs] = t if c == 0 else hid[half, rs] + t

        def project_piece(half, c, rs):
            cols = slice(half * half_w, (half + 1) * half_w)
            t = _dot(pltpu.bitcast(vt_ref[rs * mws // 2:(rs + 1) * mws // 2, c * kc:(c + 1) * kc], BF16),
                     act_r[c * kc:(c + 1) * kc, cols])
            prj[half, rs] = t if c == 0 else prj[half, rs] + t

        gates = {}

        def gate_part(st, g0, heads, last):
            cols = slice(st * LANES, (st + 1) * LANES)
            for h in heads:
                r2 = pltpu.bitcast(r2_ref[h, :, cols], BF16)
                e2 = pltpu.bitcast(e2_ref[h, :, cols], BF16)
                for kk in range(EXPERT_PAIR):
                    r = h * EXPERT_SLABS + g0 + kk
                    n1 = _bf(rn_ref[r:r + 1, cols])
                    e1 = _bf(re_ref[r:r + 1, cols])
                    gh = jnp.where(r2 < n1, e2 * e1, jnp.zeros_like(e2))
                    gates[st, g0 + kk] = gh if h == 0 else gates[st, g0 + kk] + gh
            if last:
                for kk in range(EXPERT_PAIR):
                    rows = slice((g0 + kk) * nk, (g0 + kk + 1) * nk)
                    hk = ht_r[rows, cols]
                    act_w[rows, cols] = _bf(hk * (1.0 + lax.erf(hk * (2.0 ** -0.5)))) * gates[st, g0 + kk]

        hh = PEER_HEADS // 2
        vpu_work = [functools.partial(gate_part, st, g0, heads, last)
                    for st in range(tb // LANES) for g0 in range(0, EXPERT_SLABS, EXPERT_PAIR)
                    for heads, last in ((range(0, hh), False), (range(hh, PEER_HEADS), True))]
        mxu_work = [functools.partial(piece, half, c, rs)
                    for half in range(2) for c in range(n_kc) for rs in range(ROW_SPLIT)
                    for piece in (hidden_piece, project_piece)]
        assert len(mxu_work) == len(vpu_work)
        for m, v_ in zip(mxu_work, vpu_work):
            m()
            v_()
        for half in range(2):
            cols = slice(half * half_w, (half + 1) * half_w)
            for rs in range(ROW_SPLIT):
                ht_w[rs * rws:(rs + 1) * rws, cols] = hid[half, rs]
                acc_ref[rs * mws:(rs + 1) * mws, cols] += prj[half, rs]

    parity = lax.rem(s, 2)

    @pl.when(parity == 0)
    def _():
        stages(ht0, ht1, act1, act0)

    @pl.when(parity == 1)
    def _():
        stages(ht1, ht0, act0, act1)

    @pl.when(jnp.logical_and(j3 == n_blocks - 1, s >= PIPE_LAG))
    def _():
        y = acc_ref[...].T + x1_ref[...]
        if final_norm:
            y = y * lax.rsqrt(jnp.mean(y * y, axis=-1, keepdims=True) + NORM_EPS) * gf_ref[...]
        o_ref[...] = y


def _experts(hnt, u_b, vt_b, n1, e1, r2, e2, x1, gain_f, final_norm, tb=512):
    t = hnt.shape[1]
    nh, nk = PEER_HEADS, PEER_NKEYS
    eb = EXPERT_SLABS * nk
    n_blocks = PEER_NEXPERTS // eb
    n_pairs = (t // tb) * n_blocks

    def pair(lag):
        def f(s):
            p = jnp.clip(s - lag, 0, n_pairs - 1)
            return p // n_blocks, lax.rem(p, n_blocks)
        return f

    tok = lambda lag: (lambda s: pair(lag)(s)[0])
    blk = lambda lag: (lambda s: pair(lag)(s)[1])
    routed = lambda rows: pl.BlockSpec((nh, rows, tb), lambda s: (0, 0, tok(1)(s)))
    nrow = PEER_HEADS * EXPERT_SLABS
    return pl.pallas_call(
        functools.partial(_expert_kernel, n_blocks=n_blocks, final_norm=final_norm),
        grid=(n_pairs + PIPE_LAG,),
        in_specs=[pl.BlockSpec((D_MODEL // 2, tb), lambda s: (0, tok(0)(s))),
                  pl.BlockSpec((eb // 2, D_MODEL), lambda s: (blk(0)(s), 0)),
                  pl.BlockSpec((D_MODEL // 2, eb), lambda s: (0, blk(PIPE_LAG)(s))),
                  routed(nk), routed(nk), routed(nk // 2), routed(nk // 2),
                  pl.BlockSpec((tb, D_MODEL), lambda s: (tok(PIPE_LAG)(s), 0)), _full((1, D_MODEL))],
        out_specs=pl.BlockSpec((tb, D_MODEL), lambda s: (tok(PIPE_LAG)(s), 0)),
        out_shape=jax.ShapeDtypeStruct((t, D_MODEL), F32),
        scratch_shapes=[pltpu.VMEM((D_MODEL, tb), F32), pltpu.VMEM((eb, tb), F32), pltpu.VMEM((eb, tb), F32),
                        pltpu.VMEM((eb, tb), BF16), pltpu.VMEM((eb, tb), BF16),
                        pltpu.VMEM((nrow, tb), F32), pltpu.VMEM((nrow, tb), F32)],
        compiler_params=_params(("arbitrary",)),
        name="peer_experts",
    )(hnt, u_b, vt_b, n1, e1, r2, e2, x1, gain_f)


def _pad_heads(w, heads, d, dp):
    lead = w.shape[:-1]
    w = w.reshape(*lead, heads, d)
    w = jnp.pad(w, [(0, 0)] * len(lead) + [(0, 0), (0, dp - d)])
    return w.reshape(*lead, heads * dp)


def _layer_params(l, w_in, w_out, rwkv_mu, rwkv_w_up, rwkv_w_bias, rwkv_a_up, rwkv_a_bias, rwkv_g_up, rwkv_k_k,
                  rwkv_k_a, rwkv_r_k, rwkv_ln_w, rwkv_ln_b, gla_gk_up, gla_gk_bias, gla_norm_w, peer_w_q,
                  peer_sub_keys, peer_u, peer_v):
    wi = w_in[l]
    g0 = RET_IN + RWKV_IN
    qk, vw = GLA_HEADS * GLA_DK, GLA_HEADS * GLA_DV
    gq = _pad_heads(wi[:, g0:g0 + qk], GLA_HEADS, GLA_DK, GLA_DKP)
    gkk = _pad_heads(wi[:, g0 + qk:g0 + 2 * qk], GLA_HEADS, GLA_DK, GLA_DKP)
    gv = _pad_heads(wi[:, g0 + 2 * qk:g0 + 2 * qk + vw], GLA_HEADS, GLA_DV, GLA_DVP)
    ggk = jnp.pad(wi[:, g0 + 2 * qk + vw:g0 + 2 * qk + vw + GLA_GATE_LORA], ((0, 0), (0, LANES - GLA_GATE_LORA)))
    gg = _pad_heads(wi[:, g0 + 2 * qk + vw + GLA_GATE_LORA:], GLA_HEADS, GLA_DV, GLA_DVP)
    w_in_p = _bf(jnp.concatenate([wi[:, :g0], gq, gkk, gv, ggk, gg], axis=1))
    wo = w_out[l]
    m0 = RET_W + RWKV_W
    wo_gla = jnp.pad(wo[m0:].reshape(GLA_HEADS, GLA_DV, D_MODEL), ((0, 0), (0, GLA_DVP - GLA_DV), (0, 0)))
    w_out_p = _bf(jnp.concatenate([wo[:m0], wo_gla.reshape(GLA_VW, D_MODEL)], axis=0))
    zeros = jnp.zeros((DECAY_LORA, RWKV_W), F32)
    rw = dict(
        mu=rwkv_mu[l][None, :],
        wwa=jnp.concatenate([jnp.concatenate([rwkv_w_up[l], zeros], axis=1),
                             jnp.concatenate([zeros, rwkv_a_up[l]], axis=1)], axis=0),
        w_bias=rwkv_w_bias[l][None, :], a_bias=rwkv_a_bias[l][None, :], g_up=_bf(rwkv_g_up[l]),
        k_k=rwkv_k_k[l][None, :], k_a=rwkv_k_a[l][None, :], r_k=rwkv_r_k[l].reshape(1, RWKV_W),
        ln_w=rwkv_ln_w[l][None, :], ln_b=rwkv_ln_b[l][None, :])
    gl = dict(
        gk_up=jnp.pad(_pad_heads(gla_gk_up[l], GLA_HEADS, GLA_DK, GLA_DKP), ((0, LANES - GLA_GATE_LORA), (0, 0))),
        gk_bias=_pad_heads(gla_gk_bias[l][None, :], GLA_HEADS, GLA_DK, GLA_DKP),
        norm_w=_pad_heads(gla_norm_w[l][None, :], GLA_HEADS, GLA_DV, GLA_DVP))
    wq = peer_w_q[l].reshape(D_MODEL, PEER_HEADS, 2, PEER_QHALF)
    wq_t = _bf(jnp.transpose(wq, (2, 1, 3, 0)).reshape(2 * PEER_HEADS * PEER_QHALF, D_MODEL))
    keys = _bf(jnp.transpose(peer_sub_keys[l], (1, 0, 2, 3)))
    u_p, vt_p = _pack_experts(peer_u[l], peer_v[l])
    return dict(w_in=w_in_p, w_out=w_out_p, rwkv=rw, gla=gl, wq_t=wq_t, keys=keys, u=u_p, vt=vt_p)


def _pack_experts_kernel(u_ref, v_ref, up_ref, vtp_ref):
    up_ref[...] = pltpu.bitcast(_bf(u_ref[...]), jnp.uint32)
    vtp_ref[...] = pltpu.bitcast(_bf(v_ref[...].T), jnp.uint32)


def _pack_experts(u, v, eb=1024):
    n = u.shape[0]
    return pl.pallas_call(
        _pack_experts_kernel,
        grid=(n // eb,),
        in_specs=[pl.BlockSpec((eb, D_MODEL), lambda i: (i, 0)), pl.BlockSpec((eb, D_MODEL), lambda i: (i, 0))],
        out_specs=[pl.BlockSpec((eb // 2, D_MODEL), lambda i: (i, 0)), pl.BlockSpec((D_MODEL // 2, eb), lambda i: (0, i))],
        out_shape=[jax.ShapeDtypeStruct((n // 2, D_MODEL), jnp.uint32),
                   jax.ShapeDtypeStruct((D_MODEL // 2, n), jnp.uint32)],
        compiler_params=_params(("arbitrary",)),
        name="pack_experts",
    )(u, v)


def _layer(x2, b, s, pr, gain_mix, gain_ffn, gain_final, final_norm):
    z_ret, z_rwkv, z_gla = _norm_proj(x2, gain_mix, pr["w_in"])
    o_ret = _retention(z_ret.reshape(b, s, RET_IN)).reshape(b * s, RET_W)
    o_rwkv = _rwkv(z_rwkv.reshape(b, s, RWKV_IN), pr["rwkv"]).reshape(b * s, RWKV_W)
    o_gla = _gla(z_gla.reshape(b, s, GLA_IN), pr["gla"]).reshape(b * s, GLA_VW)
    x1, hnt = _out_proj(x2, o_ret, o_rwkv, o_gla, pr["w_out"], gain_ffn)
    n1, e1, r2, e2 = _route(hnt, pr["wq_t"], pr["keys"])
    return _experts(hnt, pr["u"], pr["vt"], n1, e1, r2, e2, x1, gain_final, final_norm)


def kernel(x, norm_mix, norm_ffn, norm_final, w_in, w_out, rwkv_mu, rwkv_w_up, rwkv_w_bias, rwkv_a_up, rwkv_a_bias, rwkv_g_up, rwkv_k_k, rwkv_k_a, rwkv_r_k, rwkv_ln_w, rwkv_ln_b, gla_gk_up, gla_gk_bias, gla_norm_w, peer_w_q, peer_sub_keys, peer_u, peer_v):
    b, s, d = x.shape
    x2 = x.reshape(b * s, d)
    gain_final = norm_final[None, :]
    for l in range(DEPTH):
        pr = _layer_params(l, w_in, w_out, rwkv_mu, rwkv_w_up, rwkv_w_bias, rwkv_a_up, rwkv_a_bias, rwkv_g_up,
                           rwkv_k_k, rwkv_k_a, rwkv_r_k, rwkv_ln_w, rwkv_ln_b, gla_gk_up, gla_gk_bias, gla_norm_w,
                           peer_w_q, peer_sub_keys, peer_u, peer_v)
        x2 = _layer(x2, b, s, pr, norm_mix[l][None, :], norm_ffn[l][None, :], gain_final, l == DEPTH - 1)
    return x2.reshape(b, s, d)
```

```python
import functools

import numpy as np
import jax
import jax.numpy as jnp
from jax import lax
from jax.experimental import pallas as pl
from jax.experimental.pallas import tpu as pltpu

F32 = jnp.float32
BF16 = jnp.bfloat16
HI = lax.Precision.HIGHEST

D_MODEL = 1024
DEPTH = 2
NORM_EPS = 1e-6
HEAD_NORM_EPS = 1e-5
RET_HEADS, RET_DK, RET_DV, ROPE_BASE = 4, 64, 64, 10000.0
RET_W = RET_HEADS * RET_DK
RET_CHUNK = 128
RWKV_HEADS, RWKV_HEAD = 6, 64
RWKV_W = RWKV_HEADS * RWKV_HEAD
DECAY_LORA, AAA_LORA, GATE_LORA = 64, 64, 128
RWKV_GN_EPS = 64e-5
RWKV_CHUNK = 64
RWKV_SUB = 16
RWKV_IN = 3 * RWKV_W + DECAY_LORA + AAA_LORA + GATE_LORA
GLA_HEADS, GLA_DK, GLA_DV, GLA_GATE_LORA = 4, 48, 96, 16
GLA_GATE_NORMALIZER = 16.0
GLA_DKP, GLA_DVP = 64, 128
GLA_QW = GLA_HEADS * GLA_DKP
GLA_VW = GLA_HEADS * GLA_DVP
GLA_IN = 2 * GLA_QW + GLA_VW + 128 + GLA_VW
GLA_SUB = 16
GLA_BLOCK = 256
RET_IN = 4 * RET_W
D_INP = RET_IN + RWKV_IN + GLA_IN
D_MIXP = RET_W + RWKV_W + GLA_VW
PEER_HEADS, PEER_NKEYS, PEER_QHALF, PEER_TOPK = 8, 128, 128, 16
PEER_NEXPERTS = PEER_NKEYS * PEER_NKEYS
LANES = 128
SUBLANES = 8
VMEM_LIMIT = 56 * 1024 * 1024


def _params(sem):
    return pltpu.CompilerParams(dimension_semantics=sem, vmem_limit_bytes=VMEM_LIMIT)


def _dot(a, b, prec=None):
    return jnp.dot(a, b, precision=prec, preferred_element_type=F32)


def _dot_nt(a, b, prec=None):
    return lax.dot_general(a, b, (((1,), (1,)), ((), ())), precision=prec, preferred_element_type=F32)


def _dot_tn(a, b, prec=None):
    return lax.dot_general(a, b, (((0,), (0,)), ((), ())), precision=prec, preferred_element_type=F32)


def _bf(x):
    return x.astype(BF16)


def _iota(shape, dim):
    return lax.broadcasted_iota(jnp.int32, shape, dim)


def _same_block(shape, rblk, cblk):
    r = _iota(shape, 0) >> (rblk.bit_length() - 1)
    c = _iota(shape, 1) >> (cblk.bit_length() - 1)
    return r == c


def _full(shape):
    n = len(shape)
    return pl.BlockSpec(shape, lambda *_: (0,) * n)


def _block_ones(n, blk):
    i = np.arange(n) // blk
    return jnp.asarray((i[:, None] == i[None, :]).astype(np.float32))


def _tril_ones(n):
    return jnp.asarray(np.tril(np.ones((n, n), np.float32)))


def _norm_proj_kernel(x_ref, g_ref, w_ref, zr_ref, zk_ref, zg_ref):
    x = x_ref[...]
    y = x * lax.rsqrt(jnp.mean(x * x, axis=-1, keepdims=True) + NORM_EPS) * g_ref[...]
    yb = _bf(y)
    zr_ref[...] = _dot(yb, w_ref[:, 0:RET_IN])
    zk_ref[...] = _dot(yb, w_ref[:, RET_IN:RET_IN + RWKV_IN])
    zg_ref[...] = _dot(yb, w_ref[:, RET_IN + RWKV_IN:D_INP])


def _norm_proj(x2, gain, w_in_p, tm=512):
    t = x2.shape[0]
    return pl.pallas_call(
        _norm_proj_kernel,
        grid=(t // tm,),
        in_specs=[pl.BlockSpec((tm, D_MODEL), lambda i: (i, 0)), _full((1, D_MODEL)), _full((D_MODEL, D_INP))],
        out_specs=[pl.BlockSpec((tm, RET_IN), lambda i: (i, 0)), pl.BlockSpec((tm, RWKV_IN), lambda i: (i, 0)),
                   pl.BlockSpec((tm, GLA_IN), lambda i: (i, 0))],
        out_shape=[jax.ShapeDtypeStruct((t, RET_IN), F32), jax.ShapeDtypeStruct((t, RWKV_IN), F32),
                   jax.ShapeDtypeStruct((t, GLA_IN), F32)],
        compiler_params=_params(("arbitrary",)),
        name="norm_proj",
    )(x2, gain, w_in_p)


RET_BLOCK = 512


def _ret_kernel(z_ref, cos_ref, sin_ref, dmat_ref, qdec_ref, kdec_ref, cdec_ref, ones_ref, o_ref, st_ref):
    c = RET_CHUNK
    tb = RET_BLOCK

    @pl.when(pl.program_id(1) == 0)
    def _():
        st_ref[...] = jnp.zeros_like(st_ref)

    z = z_ref[0]
    q, k, v, g = (z[:, i * RET_W:(i + 1) * RET_W] for i in range(4))
    cos, sin = cos_ref[...], sin_ref[...]
    first_half = (_iota((tb, RET_W), 1) & (RET_DK - 1)) < RET_DK // 2

    def rot(t):
        return jnp.where(first_half, -pltpu.roll(t, RET_W - RET_DK // 2, 1), pltpu.roll(t, RET_DK // 2, 1))

    q = q * cos + rot(q) * sin
    k = (k * cos + rot(k) * sin) * (RET_DK ** -0.5)
    qd = q * qdec_ref[...]
    kd = k * kdec_ref[...]
    head0 = _iota((c, LANES), 1) < RET_DK
    bd = _same_block((LANES, LANES), RET_DV, RET_DK)
    units = [(ch, p) for ch in range(tb // c) for p in range(RET_HEADS // 2)]
    tile = lambda t, u: t[u[0] * c:(u[0] + 1) * c, u[1] * LANES:(u[1] + 1) * LANES]
    kb = [_bf(tile(k, u)) for u in units]
    intra = []
    for hh in range(2):
        mh = head0 if hh == 0 else jnp.logical_not(head0)
        sc = [_dot_nt(_bf(jnp.where(mh, tile(q, u), 0.0)), kb[i]) * dmat_ref[2 * u[1] + hh] for i, u in enumerate(units)]
        intra.append([_dot(_bf(s_), _bf(jnp.where(mh, tile(v, u), 0.0))) for s_, u in zip(sc, units)])
    upd = [jnp.where(bd, _dot_tn(_bf(tile(v, u)), _bf(tile(kd, u))), 0.0) for u in units]
    qdb = [_bf(tile(qd, u)) for u in units]
    st = [st_ref[p] for p in range(RET_HEADS // 2)]
    rows = []
    for ch in range(tb // c):
        row = []
        for p in range(RET_HEADS // 2):
            i = units.index((ch, p))
            row.append(intra[0][i] + intra[1][i] + _dot_nt(qdb[i], _bf(st[p])))
            st[p] = st[p] * cdec_ref[:, p * LANES:(p + 1) * LANES] + upd[i]
        rows.append(jnp.concatenate(row, axis=1))
    for p in range(RET_HEADS // 2):
        st_ref[p] = st[p]
    o = jnp.concatenate(rows, axis=0)
    ms = _dot_exact_rhs(o * o, ones_ref[...], 2) * (1.0 / RET_DV)
    o = o * lax.rsqrt(ms + HEAD_NORM_EPS)
    o_ref[0] = o * (g * jax.nn.sigmoid(g))


def _retention_consts(s):
    c = RET_CHUNK
    reps = RET_BLOCK // c
    log_gamma = np.log1p(-np.exp(np.linspace(np.log(1.0 / 32.0), np.log(1.0 / 512.0), RET_HEADS)))
    idx = np.arange(c, dtype=np.float64)
    diff = idx[:, None] - idx[None, :]
    dmat = np.where(diff >= 0, np.exp(log_gamma[:, None, None] * np.where(diff >= 0, diff, 0.0)), 0.0)
    lg_lane = np.repeat(log_gamma, RET_DK)[None, :]
    qdec = np.tile(np.exp(lg_lane * (idx[:, None] + 1.0)), (reps, 1))
    kdec = np.tile(np.exp(lg_lane * (c - 1.0 - idx[:, None])), (reps, 1))
    cdec = np.exp(lg_lane * c)
    half = RET_DK // 2
    inv = ROPE_BASE ** (-np.arange(half, dtype=np.float64) / half)
    ang = np.arange(s, dtype=np.float64)[:, None] * inv[None, :]
    cos = np.tile(np.cos(ang), (1, 2 * RET_HEADS))
    sin = np.tile(np.sin(ang), (1, 2 * RET_HEADS))
    f = lambda a: jnp.asarray(a.astype(np.float32))
    return f(cos), f(sin), f(dmat), f(qdec), f(kdec), f(cdec)


def _retention(z_ret):
    b, s, _ = z_ret.shape
    c = RET_CHUNK
    tb = RET_BLOCK
    cos, sin, dmat, qdec, kdec, cdec = _retention_consts(s)
    return pl.pallas_call(
        _ret_kernel,
        grid=(b, s // tb),
        in_specs=[pl.BlockSpec((1, tb, RET_IN), lambda i, j: (i, j, 0)),
                  pl.BlockSpec((tb, RET_W), lambda i, j: (j, 0)), pl.BlockSpec((tb, RET_W), lambda i, j: (j, 0)),
                  _full((RET_HEADS, c, c)), _full((tb, RET_W)), _full((tb, RET_W)), _full((1, RET_W)),
                  _full((RET_W, RET_W))],
        out_specs=pl.BlockSpec((1, tb, RET_W), lambda i, j: (i, j, 0)),
        out_shape=jax.ShapeDtypeStruct((b, s, RET_W), F32),
        scratch_shapes=[pltpu.VMEM((RET_HEADS // 2, LANES, LANES), F32)],
        compiler_params=_params(("arbitrary", "arbitrary")),
        name="retention",
    )(z_ret, cos, sin, dmat, qdec, kdec, cdec, _bf(_block_ones(RET_W, RET_DV)))


RWKV_BLOCK = 256
INV_PASSES = 3


def _split(x, parts):
    out = []
    for _ in range(parts - 1):
        h = _bf(x)
        out.append(h)
        x = x - h.astype(F32)
    out.append(_bf(x))
    return out


def _dot_exact_lhs(a_bf, x, parts):
    acc = None
    for p in _split(x, parts):
        t = _dot(a_bf, p)
        acc = t if acc is None else acc + t
    return acc


def _dot_exact_rhs(x, b_bf, parts):
    acc = None
    for p in _split(x, parts):
        t = _dot(p, b_bf)
        acc = t if acc is None else acc + t
    return acc


def _dot_f32(a, b, passes):
    if passes == 1:
        return _dot(_bf(a), _bf(b))
    if passes == 3:
        ah, al = _split(a, 2)
        bh, bl = _split(b, 2)
        return _dot(ah, bh) + (_dot(ah, bl) + _dot(al, bh))
    return _dot(a, b, HI)


def _rwkv_kernel(z_ref, mu_ref, wwa_ref, wb_ref, ab_ref, gup_ref, kk_ref, ka_ref, rk_ref, lnw_ref, lnb_ref,
                 ones_ref, ltri_ref, o_ref, st_ref, carry_ref):
    c = RWKV_CHUNK
    tb = RWKV_BLOCK
    w_ = RWKV_W

    @pl.when(pl.program_id(1) == 0)
    def _():
        st_ref[...] = jnp.zeros_like(st_ref)
        carry_ref[...] = jnp.zeros_like(carry_ref)

    f = z_ref[0]
    f_prev = jnp.where(_iota((tb, RWKV_IN), 0) == 0, carry_ref[...], pltpu.roll(f, 1, 0))
    carry_ref[...] = f[tb - 1:tb, :]
    f = f + (f_prev - f) * mu_ref[...]
    r, k, v = f[:, 0:w_], f[:, w_:2 * w_], f[:, 2 * w_:3 * w_]
    wa, gd = f[:, 3 * w_:3 * w_ + LANES], f[:, 3 * w_ + LANES:RWKV_IN]
    lane = _iota((tb, LANES), 1)
    proj = _dot_f32(jnp.where(lane < DECAY_LORA, jnp.tanh(wa), wa), wwa_ref[...], 3)
    wlog = -jax.nn.softplus(-(wb_ref[...] + proj[:, 0:w_])) - 0.5
    ld = -jnp.exp(wlog)
    a = jax.nn.sigmoid(ab_ref[...] + proj[:, w_:2 * w_])
    g = _dot(_bf(jax.nn.sigmoid(gd)), gup_ref[...])
    ones = ones_ref[...]
    kk = k * kk_ref[...]
    kk = kk * lax.rsqrt(jnp.maximum(_dot_exact_rhs(kk * kk, ones, 2), 1e-24))
    k = k * (1.0 + (a - 1.0) * ka_ref[...])
    cum = _dot_exact_lhs(ltri_ref[...], ld, 3)
    e_neg = jnp.exp(-cum)
    b = kk * a
    a_t_all = -kk * jnp.exp(cum - ld)
    b_t_all, k_t_all, r_t_all = b * e_neg, k * e_neg, r * jnp.exp(cum)

    n2 = 2 * c
    ri, ci = _iota((n2, n2), 0), _iota((n2, n2), 1)
    same_head = _same_block((n2, n2), c, c)
    strict = jnp.logical_and(same_head, ri > ci)
    incl = jnp.logical_and(same_head, ri >= ci)
    sub = _same_block((n2, n2), RWKV_SUB, RWKV_SUB)
    eye = (ri == ci).astype(F32)
    bd = _same_block((LANES, LANES), RWKV_HEAD, RWKV_HEAD)
    head0 = _iota((c, LANES), 1) < RWKV_HEAD
    mm = functools.partial(_dot_f32, passes=INV_PASSES)
    mm1 = functools.partial(_dot_f32, passes=1)

    def stack(x):
        return jnp.concatenate([jnp.where(head0, x, 0.0), jnp.where(head0, 0.0, x)], axis=0)

    def fold(x):
        return x[:c] + x[c:]

    units = [(ch, p) for ch in range(tb // c) for p in range(RWKV_HEADS // 2)]
    each = lambda fn, *lists: [fn(*args) for args in zip(*lists)]
    rows_of = lambda ch: slice(ch * c, (ch + 1) * c)
    lanes_of = lambda p: slice(p * LANES, (p + 1) * LANES)
    tile = lambda t: [stack(t[rows_of(ch), lanes_of(p)]) for ch, p in units]
    a_s, r_s, b_s, k_s, v_s = tile(a_t_all), tile(r_t_all), tile(b_t_all), tile(k_t_all), tile(v)
    cl = [cum[(ch + 1) * c - 1:(ch + 1) * c, lanes_of(p)] for ch, p in units]
    e_rem = [jnp.exp(cl_u - cum[rows_of(ch), lanes_of(p)]) for cl_u, (ch, p) in zip(cl, units)]
    b_h = [b[rows_of(ch), lanes_of(p)] * e for e, (ch, p) in zip(e_rem, units)]
    k_h = [k[rows_of(ch), lanes_of(p)] * e for e, (ch, p) in zip(e_rem, units)]
    v_p = [v[rows_of(ch), lanes_of(p)] for ch, p in units]
    big = each(lambda x, y, z, w: _dot_nt(_bf(jnp.concatenate([x, y], axis=0)), _bf(jnp.concatenate([z, w], axis=0))),
               a_s, r_s, b_s, k_s)
    a_ab = [jnp.where(strict, t[:n2, :n2], 0.0) for t in big]
    a_ak = [jnp.where(strict, t[:n2, n2:], 0.0) for t in big]
    a_rb = [jnp.where(incl, t[n2:, :n2], 0.0) for t in big]
    a_rk = [jnp.where(incl, t[n2:, n2:], 0.0) for t in big]
    d1 = [jnp.where(sub, t, 0.0) for t in a_ab]
    lo = each(lambda x, y: x - y, a_ab, d1)
    d2 = each(mm, d1, d1)
    t_d = each(lambda x, y: mm1(eye + x, eye + y), d1, d2)
    d4 = each(mm, d2, d2)
    t_d = each(lambda x, y: mm1(x, eye + y), t_d, d4)
    d8 = each(mm, d4, d4)
    t_d = each(lambda x, y: mm1(x, eye + y), t_d, d8)
    e1 = each(mm1, t_d, lo)
    e2 = each(mm1, e1, e1)
    t_inv = each(lambda x, y: mm1(eye + x, eye + y), e1, e2)
    t_inv = each(mm1, t_inv, t_d)
    akv = each(lambda x, y: _dot(_bf(x), _bf(y)), a_ak, v_s)
    wy = each(lambda t, x, y: mm1(t, jnp.concatenate([x, y], axis=1)), t_inv, a_s, akv)
    qo = each(lambda x, y: _dot(_bf(x), _bf(y)), a_rb, wy)
    rkv = each(lambda x, y: _dot(_bf(x), _bf(y)), a_rk, v_s)
    w_p = [fold(t[:, :LANES]) for t in wy]
    y_p = [fold(t[:, LANES:]) for t in wy]
    q_p = each(lambda x, y: fold(x + y[:, :LANES]), r_s, qo)
    ol_p = each(lambda x, y: fold(x[:, LANES:] + y), qo, rkv)
    gmat = each(lambda x, y: jnp.where(bd, _dot_tn(_bf(x), _bf(y)), 0.0), w_p, b_h)
    nmat = each(lambda y, vv, bb, kk_: jnp.where(bd, _dot_tn(_bf(jnp.concatenate([y, vv], axis=0)),
                                                             _bf(jnp.concatenate([bb, kk_], axis=0))), 0.0),
                y_p, v_p, b_h, k_h)
    pre = {u: (q_p[i], ol_p[i], gmat[i], nmat[i], jnp.exp(cl[i])) for i, u in enumerate(units)}

    outs = [[None] * (RWKV_HEADS // 2) for _ in range(tb // c)]
    st = [st_ref[p] for p in range(RWKV_HEADS // 2)]
    for ch in range(tb // c):
        for p in range(RWKV_HEADS // 2):
            q_u, ol_u, g_u, n_u, pc_u = pre[ch, p]
            stb = _bf(st[p])
            outs[ch][p] = _dot_nt(_bf(q_u), stb) + ol_u
            st[p] = st[p] * pc_u + _dot(stb, _bf(g_u)) + n_u
    for p in range(RWKV_HEADS // 2):
        st_ref[p] = st[p]
    o = jnp.concatenate([jnp.concatenate(row, axis=1) for row in outs], axis=0)
    inv_n = 1.0 / RWKV_HEAD
    d = o - _dot_exact_rhs(o, ones, 2) * inv_n
    var = _dot_exact_rhs(d * d, ones, 2) * inv_n
    o = d * lax.rsqrt(var + RWKV_GN_EPS) * lnw_ref[...] + lnb_ref[...]
    o = o + _dot_exact_rhs(r * k * rk_ref[...], ones, 2) * v
    o_ref[0] = o * g


def _rwkv(z_rwkv, pr):
    b, s, _ = z_rwkv.shape
    tb = RWKV_BLOCK
    row = lambda n: _full((1, n))
    ltri = np.kron(np.eye(tb // RWKV_CHUNK, dtype=np.float32), np.tril(np.ones((RWKV_CHUNK, RWKV_CHUNK), np.float32)))
    return pl.pallas_call(
        _rwkv_kernel,
        grid=(b, s // tb),
        in_specs=[pl.BlockSpec((1, tb, RWKV_IN), lambda i, j: (i, j, 0)), row(RWKV_IN), _full((LANES, 2 * RWKV_W)),
                  row(RWKV_W), row(RWKV_W), _full((GATE_LORA, RWKV_W)), row(RWKV_W), row(RWKV_W), row(RWKV_W),
                  row(RWKV_W), row(RWKV_W), _full((RWKV_W, RWKV_W)), _full((tb, tb))],
        out_specs=pl.BlockSpec((1, tb, RWKV_W), lambda i, j: (i, j, 0)),
        out_shape=jax.ShapeDtypeStruct((b, s, RWKV_W), F32),
        scratch_shapes=[pltpu.VMEM((RWKV_HEADS // 2, LANES, LANES), F32), pltpu.VMEM((1, RWKV_IN), F32)],
        compiler_params=_params(("arbitrary", "arbitrary")),
        name="rwkv7",
    )(z_rwkv, pr["mu"], pr["wwa"], pr["w_bias"], pr["a_bias"], pr["g_up"], pr["k_k"], pr["k_a"], pr["r_k"],
      pr["ln_w"], pr["ln_b"], _bf(_block_ones(RWKV_W, RWKV_HEAD)), jnp.asarray(ltri, dtype=BF16))


def _gla_kernel(z_ref, gkup_ref, gkb_ref, nw_ref, ind_ref, ones_ref, ltri_ref, o_ref, st_ref, x_ref):
    n = GLA_SUB
    tb = GLA_BLOCK
    nsub = tb // n

    @pl.when(pl.program_id(1) == 0)
    def _():
        st_ref[...] = jnp.zeros_like(st_ref)

    z = z_ref[0]
    q_all = z[:, 0:GLA_QW] * (GLA_DK ** -0.5)
    k_all = z[:, GLA_QW:2 * GLA_QW]
    v_all = z[:, 2 * GLA_QW:2 * GLA_QW + GLA_VW]
    gkd = z[:, 2 * GLA_QW + GLA_VW:2 * GLA_QW + GLA_VW + LANES]
    g = z[:, 2 * GLA_QW + GLA_VW + LANES:GLA_IN]
    gk = jax.nn.log_sigmoid(_dot_f32(gkd, gkup_ref[...], 3) + gkb_ref[...]) * (1.0 / GLA_GATE_NORMALIZER)
    bc_all = _dot_exact_lhs(ltri_ref[...], gk, 3)
    bd = _same_block((GLA_VW, GLA_QW), GLA_DVP, GLA_DKP)
    rowi = _iota((n, GLA_QW), 0)
    subs = range(nsub)
    rows = [slice(sc * n, (sc + 1) * n) for sc in subs]
    q = [q_all[r] for r in rows]
    k = [k_all[r] for r in rows]
    v = [v_all[r] for r in rows]
    bc = [bc_all[r] for r in rows]
    bl = [t[n - 1:n, :] for t in bc]
    for sc in subs:
        for j in range(n):
            xj = q[sc] * jnp.exp(jnp.minimum(bc[sc] - bc[sc][j:j + 1, :], 0.0)) * k[sc][j:j + 1, :]
            x_ref[(sc * n + j) * n:(sc * n + j + 1) * n, :] = jnp.where(rowi >= j, xj, 0.0)
    x = x_ref[...]
    xh = _bf(x)
    xl = _bf(x - xh.astype(F32))
    ind = ind_ref[...]
    e = _dot(xh, ind) + _dot(xl, ind)
    o_intra = []
    for sc in subs:
        acc = None
        for j in range(n):
            t = e[(sc * n + j) * n:(sc * n + j + 1) * n, :] * v[sc][j:j + 1, :]
            acc = t if acc is None else acc + t
        o_intra.append(acc)
    upd = [jnp.where(bd, _dot_tn(_bf(v[sc]), _bf(k[sc] * jnp.exp(bl[sc] - bc[sc]))), 0.0) for sc in subs]
    qe = [_bf(q[sc] * jnp.exp(bc[sc])) for sc in subs]
    st = st_ref[...]
    outs = []
    for sc in subs:
        outs.append(_dot_nt(qe[sc], _bf(st)) + o_intra[sc])
        st = st * jnp.exp(bl[sc]) + upd[sc]
    st_ref[...] = st
    o = jnp.concatenate(outs, axis=0)
    ms = _dot_exact_rhs(o * o, ones_ref[...], 2) * (1.0 / GLA_DV)
    o_ref[0] = o * lax.rsqrt(ms + HEAD_NORM_EPS) * nw_ref[...] * (g * jax.nn.sigmoid(g))


def _gla(z_gla, pr):
    b, s, _ = z_gla.shape
    tb = GLA_BLOCK
    hq = np.arange(GLA_QW) // GLA_DKP
    hv = np.arange(GLA_VW) // GLA_DVP
    ind = jnp.asarray((hq[:, None] == hv[None, :]).astype(np.float32), dtype=BF16)
    ltri = np.kron(np.eye(tb // GLA_SUB, dtype=np.float32), np.tril(np.ones((GLA_SUB, GLA_SUB), np.float32)))
    return pl.pallas_call(
        _gla_kernel,
        grid=(b, s // tb),
        in_specs=[pl.BlockSpec((1, tb, GLA_IN), lambda i, j: (i, j, 0)), _full((LANES, GLA_QW)), _full((1, GLA_QW)),
                  _full((1, GLA_VW)), _full((GLA_QW, GLA_VW)), _full((GLA_VW, GLA_VW)), _full((tb, tb))],
        out_specs=pl.BlockSpec((1, tb, GLA_VW), lambda i, j: (i, j, 0)),
        out_shape=jax.ShapeDtypeStruct((b, s, GLA_VW), F32),
        scratch_shapes=[pltpu.VMEM((GLA_VW, GLA_QW), F32), pltpu.VMEM((tb * GLA_SUB, GLA_QW), F32)],
        compiler_params=_params(("arbitrary", "arbitrary")),
        name="gla",
    )(z_gla, pr["gk_up"], pr["gk_bias"], pr["norm_w"], ind, _bf(_block_ones(GLA_VW, GLA_DVP)),
      jnp.asarray(ltri, dtype=BF16))


def _out_proj_kernel(x_ref, a_ref, b_ref, c_ref, w_ref, g_ref, x1_ref, hnt_ref):
    acc = _dot(_bf(a_ref[...]), w_ref[0:RET_W, :])
    acc += _dot(_bf(b_ref[...]), w_ref[RET_W:RET_W + RWKV_W, :])
    acc += _dot(_bf(c_ref[...]), w_ref[RET_W + RWKV_W:D_MIXP, :])
    x1 = x_ref[...] + acc
    x1_ref[...] = x1
    hn = x1 * lax.rsqrt(jnp.mean(x1 * x1, axis=-1, keepdims=True) + NORM_EPS) * g_ref[...]
    hnt_ref[...] = pltpu.bitcast(_bf(hn.T), jnp.uint32)


def _out_proj(x2, o_ret, o_rwkv, o_gla, w_out_p, gain, tm=512):
    t = x2.shape[0]
    blk = lambda n: pl.BlockSpec((tm, n), lambda i: (i, 0))
    return pl.pallas_call(
        _out_proj_kernel,
        grid=(t // tm,),
        in_specs=[blk(D_MODEL), blk(RET_W), blk(RWKV_W), blk(GLA_VW), _full((D_MIXP, D_MODEL)), _full((1, D_MODEL))],
        out_specs=[blk(D_MODEL), pl.BlockSpec((D_MODEL // 2, tm), lambda i: (0, i))],
        out_shape=[jax.ShapeDtypeStruct((t, D_MODEL), F32), jax.ShapeDtypeStruct((D_MODEL // 2, t), jnp.uint32)],
        compiler_params=_params(("arbitrary",)),
        name="out_proj",
    )(x2, o_ret, o_rwkv, o_gla, w_out_p, gain)


def _staircase_pairs():
    k = PEER_TOPK
    return [(a, b) for a in range(k) for b in range(k) if (a + 1) * (b + 1) <= k]


def _tree_max(xs):
    xs = list(xs)
    while len(xs) > 1:
        nxt = [jnp.maximum(xs[2 * i], xs[2 * i + 1]) for i in range(len(xs) // 2)]
        if len(xs) % 2:
            nxt.append(xs[-1])
        xs = nxt
    return xs[0]


def _route_kernel(hnt_ref, wq_ref, keys_ref, n1_ref, e1_ref, r2_ref, e2_ref, work_ref, s_ref, vals_ref, rank_ref):
    nh, nk, k_top = PEER_HEADS, PEER_NKEYS, PEER_TOPK
    qt = _bf(_dot(wq_ref[...], pltpu.bitcast(hnt_ref[...], BF16)))
    for p in range(2):
        for h in range(nh):
            r0 = (p * nh + h) * PEER_QHALF
            s = _dot(keys_ref[p, h], qt[r0:r0 + PEER_QHALF, :])
            work_ref[p, h] = s
            s_ref[p, h] = s
    rank_ref[...] = jnp.full(rank_ref.shape, float(k_top), F32)
    groups = nk // SUBLANES

    def extract(r, carry):
        rf = jnp.asarray(r, F32)
        for p in range(2):
            for h in range(nh):
                tiles = [work_ref[p, h, i * SUBLANES:(i + 1) * SUBLANES, :] for i in range(groups)]
                m = _tree_max(tiles)
                for sh in (4, 2, 1):
                    m = jnp.maximum(m, pltpu.roll(m, sh, 0))
                vals_ref[p, r, pl.ds(h, 1), :] = m[0:1, :]
                for i in range(groups):
                    rows = slice(i * SUBLANES, (i + 1) * SUBLANES)
                    is_max = tiles[i] == m
                    work_ref[p, h, rows, :] = jnp.where(is_max, -jnp.inf, tiles[i])
                    rank_ref[p, h, rows, :] = jnp.where(is_max, rf, rank_ref[p, h, rows, :])
        return carry

    lax.fori_loop(0, k_top, extract, 0)

    v1 = [vals_ref[0, r] for r in range(k_top)]
    v2 = [vals_ref[1, r] for r in range(k_top)]
    cand = {(a, b): v1[a] + v2[b] for a, b in _staircase_pairs()}
    work = list(cand.values())
    tau = None
    for it in range(k_top):
        tau = _tree_max(work)
        if it + 1 < k_top:
            work = [jnp.where(w == tau, -jnp.inf, w) for w in work]
    top = cand[(0, 0)]
    z = None
    for c in cand.values():
        zi = jnp.where(c >= tau, jnp.exp(c - top), 0.0)
        z = zi if z is None else z + zi
    scale2 = 0.5 / z
    n_of_rank = []
    for a in range(k_top):
        cnt = None
        for b in range(k_top // (a + 1)):
            ge = (cand[(a, b)] >= tau).astype(F32)
            cnt = ge if cnt is None else cnt + ge
        n_of_rank.append(cnt)
    m1, m2 = v1[0], v2[0]
    for h in range(nh):
        r1 = rank_ref[0, h]
        n1 = jnp.zeros_like(r1)
        for a in range(k_top):
            n1 = jnp.where(r1 == float(a), n_of_rank[a][h:h + 1, :], n1)
        n1_ref[h] = n1
        e1_ref[h] = jnp.exp(s_ref[0, h] - m1[h:h + 1, :])
        r2_ref[h] = pltpu.bitcast(_bf(rank_ref[1, h]), jnp.uint32)
        e2_ref[h] = pltpu.bitcast(_bf(jnp.exp(s_ref[1, h] - m2[h:h + 1, :]) * scale2[h:h + 1, :]), jnp.uint32)


def _route(hnt, wq_t, keys, tb=256):
    t = hnt.shape[1]
    nh, nk = PEER_HEADS, PEER_NKEYS
    out = pl.BlockSpec((nh, nk, tb), lambda i: (0, 0, i))
    packed = pl.BlockSpec((nh, nk // 2, tb), lambda i: (0, 0, i))
    return pl.pallas_call(
        _route_kernel,
        grid=(t // tb,),
        in_specs=[pl.BlockSpec((D_MODEL // 2, tb), lambda i: (0, i)), _full((2 * nh * PEER_QHALF, D_MODEL)),
                  _full((2, nh, nk, PEER_QHALF))],
        out_specs=[out, out, packed, packed],
        out_shape=[jax.ShapeDtypeStruct((nh, nk, t), F32)] * 2 + [jax.ShapeDtypeStruct((nh, nk // 2, t), jnp.uint32)] * 2,
        scratch_shapes=[pltpu.VMEM((2, nh, nk, tb), F32), pltpu.VMEM((2, nh, nk, tb), F32),
                        pltpu.VMEM((2, PEER_TOPK, nh, tb), F32), pltpu.VMEM((2, nh, nk, tb), F32)],
        compiler_params=_params(("arbitrary",)),
        name="peer_route",
    )(hnt, wq_t, keys)


EXPERT_SLABS = 8
EXPERT_PAIR = 2
PIPE_LAG = 2
ROW_SPLIT = 2


def _expert_kernel(hnt_ref, u_ref, vt_ref, n1_ref, e1_ref, r2_ref, e2_ref, x1_ref, gf_ref, o_ref, acc_ref, ht0, ht1,
                   act0, act1, rn_ref, re_ref, *, n_blocks, final_norm):
    s = pl.program_id(0)
    n_pairs = pl.num_programs(0) - PIPE_LAG
    nk = PEER_NKEYS
    tb = hnt_ref.shape[1]
    j2 = lax.rem(jnp.clip(s - 1, 0, n_pairs - 1), n_blocks)
    j3 = lax.rem(jnp.clip(s - PIPE_LAG, 0, n_pairs - 1), n_blocks)
    live = jnp.logical_and(s >= 1, s <= n_pairs).astype(F32)

    @pl.when(s == 0)
    def _():
        for r in (ht0, ht1, act0, act1):
            r[...] = jnp.zeros_like(r)

    @pl.when(j3 == 0)
    def _():
        acc_ref[...] = jnp.zeros_like(acc_ref)

    for h in range(PEER_HEADS):
        for k in range(EXPERT_SLABS):
            r = h * EXPERT_SLABS + k
            rn_ref[r:r + 1, :] = n1_ref[h, pl.ds(j2 * EXPERT_SLABS + k, 1), :]
            re_ref[r:r + 1, :] = e1_ref[h, pl.ds(j2 * EXPERT_SLABS + k, 1), :] * live

    def stages(ht_w, ht_r, act_w, act_r):
        half_w = 2 * LANES
        kc = 2 * LANES
        n_kc = D_MODEL // kc
        n_ec = (EXPERT_SLABS * nk) // kc

        assert n_kc == n_ec
        rws = (EXPERT_SLABS * nk) // ROW_SPLIT
        mws = D_MODEL // ROW_SPLIT
        hid = {}
        prj = {}

        def hidden_piece(half, c, rs):
            cols = slice(half * half_w, (half + 1) * half_w)
            t = _dot(pltpu.bitcast(u_ref[rs * rws // 2:(rs + 1) * rws // 2, c * kc:(c + 1) * kc], BF16),
                     pltpu.bitcast(hnt_ref[c * kc // 2:(c + 1) * kc // 2, cols], BF16))
            hid[half, rs] = t if c == 0 else hid[half, rs] + t

        def project_piece(half, c, rs):
            cols = slice(half * half_w, (half + 1) * half_w)
            t = _dot(pltpu.bitcast(vt_ref[rs * mws // 2:(rs + 1) * mws // 2, c * kc:(c + 1) * kc], BF16),
                     act_r[c * kc:(c + 1) * kc, cols])
            prj[half, rs] = t if c == 0 else prj[half, rs] + t

        gates = {}

        def gate_part(st, g0, heads, last):
            cols = slice(st * LANES, (st + 1) * LANES)
            for h in heads:
                r2 = pltpu.bitcast(r2_ref[h, :, cols], BF16)
                e2 = pltpu.bitcast(e2_ref[h, :, cols], BF16)
                for kk in range(EXPERT_PAIR):
                    r = h * EXPERT_SLABS + g0 + kk
                    n1 = _bf(rn_ref[r:r + 1, cols])
                    e1 = _bf(re_ref[r:r + 1, cols])
                    gh = jnp.where(r2 < n1, e2 * e1, jnp.zeros_like(e2))
                    gates[st, g0 + kk] = gh if h == 0 else gates[st, g0 + kk] + gh
            if last:
                for kk in range(EXPERT_PAIR):
                    rows = slice((g0 + kk) * nk, (g0 + kk + 1) * nk)
                    hk = ht_r[rows, cols]
                    act_w[rows, cols] = _bf(hk * (1.0 + lax.erf(hk * (2.0 ** -0.5)))) * gates[st, g0 + kk]

        hh = PEER_HEADS // 2
        vpu_work = [functools.partial(gate_part, st, g0, heads, last)
                    for st in range(tb // LANES) for g0 in range(0, EXPERT_SLABS, EXPERT_PAIR)
                    for heads, last in ((range(0, hh), False), (range(hh, PEER_HEADS), True))]
        mxu_work = [functools.partial(piece, half, c, rs)
                    for half in range(2) for c in range(n_kc) for rs in range(ROW_SPLIT)
                    for piece in (hidden_piece, project_piece)]
        assert len(mxu_work) == len(vpu_work)
        for m, v_ in zip(mxu_work, vpu_work):
            m()
            v_()
        for half in range(2):
            cols = slice(half * half_w, (half + 1) * half_w)
            for rs in range(ROW_SPLIT):
                ht_w[rs * rws:(rs + 1) * rws, cols] = hid[half, rs]
                acc_ref[rs * mws:(rs + 1) * mws, cols] += prj[half, rs]

    parity = lax.rem(s, 2)

    @pl.when(parity == 0)
    def _():
        stages(ht0, ht1, act1, act0)

    @pl.when(parity == 1)
    def _():
        stages(ht1, ht0, act0, act1)

    @pl.when(jnp.logical_and(j3 == n_blocks - 1, s >= PIPE_LAG))
    def _():
        y = acc_ref[...].T + x1_ref[...]
        if final_norm:
            y = y * lax.rsqrt(jnp.mean(y * y, axis=-1, keepdims=True) + NORM_EPS) * gf_ref[...]
        o_ref[...] = y


def _experts(hnt, u_b, vt_b, n1, e1, r2, e2, x1, gain_f, final_norm, tb=512):
    t = hnt.shape[1]
    nh, nk = PEER_HEADS, PEER_NKEYS
    eb = EXPERT_SLABS * nk
    n_blocks = PEER_NEXPERTS // eb
    n_pairs = (t // tb) * n_blocks

    def pair(lag):
        def f(s):
            p = jnp.clip(s - lag, 0, n_pairs - 1)
            return p // n_blocks, lax.rem(p, n_blocks)
        return f

    tok = lambda lag: (lambda s: pair(lag)(s)[0])
    blk = lambda lag: (lambda s: pair(lag)(s)[1])
    routed = lambda rows: pl.BlockSpec((nh, rows, tb), lambda s: (0, 0, tok(1)(s)))
    nrow = PEER_HEADS * EXPERT_SLABS
    return pl.pallas_call(
        functools.partial(_expert_kernel, n_blocks=n_blocks, final_norm=final_norm),
        grid=(n_pairs + PIPE_LAG,),
        in_specs=[pl.BlockSpec((D_MODEL // 2, tb), lambda s: (0, tok(0)(s))),
                  pl.BlockSpec((eb // 2, D_MODEL), lambda s: (blk(0)(s), 0)),
                  pl.BlockSpec((D_MODEL // 2, eb), lambda s: (0, blk(PIPE_LAG)(s))),
                  routed(nk), routed(nk), routed(nk // 2), routed(nk // 2),
                  pl.BlockSpec((tb, D_MODEL), lambda s: (tok(PIPE_LAG)(s), 0)), _full((1, D_MODEL))],
        out_specs=pl.BlockSpec((tb, D_MODEL), lambda s: (tok(PIPE_LAG)(s), 0)),
        out_shape=jax.ShapeDtypeStruct((t, D_MODEL), F32),
        scratch_shapes=[pltpu.VMEM((D_MODEL, tb), F32), pltpu.VMEM((eb, tb), F32), pltpu.VMEM((eb, tb), F32),
                        pltpu.VMEM((eb, tb), BF16), pltpu.VMEM((eb, tb), BF16),
                        pltpu.VMEM((nrow, tb), F32), pltpu.VMEM((nrow, tb), F32)],
        compiler_params=_params(("arbitrary",)),
        name="peer_experts",
    )(hnt, u_b, vt_b, n1, e1, r2, e2, x1, gain_f)


def _pad_heads(w, heads, d, dp):
    lead = w.shape[:-1]
    w = w.reshape(*lead, heads, d)
    w = jnp.pad(w, [(0, 0)] * len(lead) + [(0, 0), (0, dp - d)])
    return w.reshape(*lead, heads * dp)


def _layer_params(l, w_in, w_out, rwkv_mu, rwkv_w_up, rwkv_w_bias, rwkv_a_up, rwkv_a_bias, rwkv_g_up, rwkv_k_k,
                  rwkv_k_a, rwkv_r_k, rwkv_ln_w, rwkv_ln_b, gla_gk_up, gla_gk_bias, gla_norm_w, peer_w_q,
                  peer_sub_keys, peer_u, peer_v):
    wi = w_in[l]
    g0 = RET_IN + RWKV_IN
    qk, vw = GLA_HEADS * GLA_DK, GLA_HEADS * GLA_DV
    gq = _pad_heads(wi[:, g0:g0 + qk], GLA_HEADS, GLA_DK, GLA_DKP)
    gkk = _pad_heads(wi[:, g0 + qk:g0 + 2 * qk], GLA_HEADS, GLA_DK, GLA_DKP)
    gv = _pad_heads(wi[:, g0 + 2 * qk:g0 + 2 * qk + vw], GLA_HEADS, GLA_DV, GLA_DVP)
    ggk = jnp.pad(wi[:, g0 + 2 * qk + vw:g0 + 2 * qk + vw + GLA_GATE_LORA], ((0, 0), (0, LANES - GLA_GATE_LORA)))
    gg = _pad_heads(wi[:, g0 + 2 * qk + vw + GLA_GATE_LORA:], GLA_HEADS, GLA_DV, GLA_DVP)
    w_in_p = _bf(jnp.concatenate([wi[:, :g0], gq, gkk, gv, ggk, gg], axis=1))
    wo = w_out[l]
    m0 = RET_W + RWKV_W
    wo_gla = jnp.pad(wo[m0:].reshape(GLA_HEADS, GLA_DV, D_MODEL), ((0, 0), (0, GLA_DVP - GLA_DV), (0, 0)))
    w_out_p = _bf(jnp.concatenate([wo[:m0], wo_gla.reshape(GLA_VW, D_MODEL)], axis=0))
    zeros = jnp.zeros((DECAY_LORA, RWKV_W), F32)
    rw = dict(
        mu=rwkv_mu[l][None, :],
        wwa=jnp.concatenate([jnp.concatenate([rwkv_w_up[l], zeros], axis=1),
                             jnp.concatenate([zeros, rwkv_a_up[l]], axis=1)], axis=0),
        w_bias=rwkv_w_bias[l][None, :], a_bias=rwkv_a_bias[l][None, :], g_up=_bf(rwkv_g_up[l]),
        k_k=rwkv_k_k[l][None, :], k_a=rwkv_k_a[l][None, :], r_k=rwkv_r_k[l].reshape(1, RWKV_W),
        ln_w=rwkv_ln_w[l][None, :], ln_b=rwkv_ln_b[l][None, :])
    gl = dict(
        gk_up=jnp.pad(_pad_heads(gla_gk_up[l], GLA_HEADS, GLA_DK, GLA_DKP), ((0, LANES - GLA_GATE_LORA), (0, 0))),
        gk_bias=_pad_heads(gla_gk_bias[l][None, :], GLA_HEADS, GLA_DK, GLA_DKP),
        norm_w=_pad_heads(gla_norm_w[l][None, :], GLA_HEADS, GLA_DV, GLA_DVP))
    wq = peer_w_q[l].reshape(D_MODEL, PEER_HEADS, 2, PEER_QHALF)
    wq_t = _bf(jnp.transpose(wq, (2, 1, 3, 0)).reshape(2 * PEER_HEADS * PEER_QHALF, D_MODEL))
    keys = _bf(jnp.transpose(peer_sub_keys[l], (1, 0, 2, 3)))
    u_p, vt_p = _pack_experts(peer_u, peer_v, l)
    return dict(w_in=w_in_p, w_out=w_out_p, rwkv=rw, gla=gl, wq_t=wq_t, keys=keys, u=u_p, vt=vt_p)


def _pack_experts_kernel(u_ref, v_ref, up_ref, vtp_ref):
    up_ref[...] = pltpu.bitcast(_bf(u_ref[0]), jnp.uint32)
    vtp_ref[...] = pltpu.bitcast(_bf(v_ref[0].T), jnp.uint32)


def _pack_experts(u, v, l, eb=1024):
    n = u.shape[1]
    return pl.pallas_call(
        _pack_experts_kernel,
        grid=(n // eb,),
        in_specs=[pl.BlockSpec((1, eb, D_MODEL), lambda i: (l, i, 0)), pl.BlockSpec((1, eb, D_MODEL), lambda i: (l, i, 0))],
        out_specs=[pl.BlockSpec((eb // 2, D_MODEL), lambda i: (i, 0)), pl.BlockSpec((D_MODEL // 2, eb), lambda i: (0, i))],
        out_shape=[jax.ShapeDtypeStruct((n // 2, D_MODEL), jnp.uint32),
                   jax.ShapeDtypeStruct((D_MODEL // 2, n), jnp.uint32)],
        compiler_params=_params(("arbitrary",)),
        name="pack_experts",
    )(u, v)


def _layer(x2, b, s, pr, gain_mix, gain_ffn, gain_final, final_norm):
    z_ret, z_rwkv, z_gla = _norm_proj(x2, gain_mix, pr["w_in"])
    o_ret = _retention(z_ret.reshape(b, s, RET_IN)).reshape(b * s, RET_W)
    o_rwkv = _rwkv(z_rwkv.reshape(b, s, RWKV_IN), pr["rwkv"]).reshape(b * s, RWKV_W)
    o_gla = _gla(z_gla.reshape(b, s, GLA_IN), pr["gla"]).reshape(b * s, GLA_VW)
    x1, hnt = _out_proj(x2, o_ret, o_rwkv, o_gla, pr["w_out"], gain_ffn)
    n1, e1, r2, e2 = _route(hnt, pr["wq_t"], pr["keys"])
    return _experts(hnt, pr["u"], pr["vt"], n1, e1, r2, e2, x1, gain_final, final_norm)


def kernel(x, norm_mix, norm_ffn, norm_final, w_in, w_out, rwkv_mu, rwkv_w_up, rwkv_w_bias, rwkv_a_up, rwkv_a_bias, rwkv_g_up, rwkv_k_k, rwkv_k_a, rwkv_r_k, rwkv_ln_w, rwkv_ln_b, gla_gk_up, gla_gk_bias, gla_norm_w, peer_w_q, peer_sub_keys, peer_u, peer_v):
    b, s, d = x.shape
    x2 = x.reshape(b * s, d)
    gain_final = norm_final[None, :]
    for l in range(DEPTH):
        pr = _layer_params(l, w_in, w_out, rwkv_mu, rwkv_w_up, rwkv_w_bias, rwkv_a_up, rwkv_a_bias, rwkv_g_up,
                           rwkv_k_k, rwkv_k_a, rwkv_r_k, rwkv_ln_w, rwkv_ln_b, gla_gk_up, gla_gk_bias, gla_norm_w,
                           peer_w_q, peer_sub_keys, peer_u, peer_v)
        x2 = _layer(x2, b, s, pr, norm_mix[l][None, :], norm_ffn[l][None, :], gain_final, l == DEPTH - 1)
    return x2.reshape(b, s, d)
```

```python
import functools

import numpy as np
import jax
import jax.numpy as jnp
from jax import lax
from jax.experimental import pallas as pl
from jax.experimental.pallas import tpu as pltpu

F32 = jnp.float32
BF16 = jnp.bfloat16
HI = lax.Precision.HIGHEST

D_MODEL = 1024
DEPTH = 2
NORM_EPS = 1e-6
HEAD_NORM_EPS = 1e-5
RET_HEADS, RET_DK, RET_DV, ROPE_BASE = 4, 64, 64, 10000.0
RET_W = RET_HEADS * RET_DK
RET_CHUNK = 128
RWKV_HEADS, RWKV_HEAD = 6, 64
RWKV_W = RWKV_HEADS * RWKV_HEAD
DECAY_LORA, AAA_LORA, GATE_LORA = 64, 64, 128
RWKV_GN_EPS = 64e-5
RWKV_CHUNK = 64
RWKV_SUB = 16
RWKV_IN = 3 * RWKV_W + DECAY_LORA + AAA_LORA + GATE_LORA
GLA_HEADS, GLA_DK, GLA_DV, GLA_GATE_LORA = 4, 48, 96, 16
GLA_GATE_NORMALIZER = 16.0
GLA_DKP, GLA_DVP = 64, 128
GLA_QW = GLA_HEADS * GLA_DKP
GLA_VW = GLA_HEADS * GLA_DVP
GLA_IN = 2 * GLA_QW + GLA_VW + 128 + GLA_VW
GLA_SUB = 16
GLA_BLOCK = 256
RET_IN = 4 * RET_W
D_INP = RET_IN + RWKV_IN + GLA_IN
D_MIXP = RET_W + RWKV_W + GLA_VW
PEER_HEADS, PEER_NKEYS, PEER_QHALF, PEER_TOPK = 8, 128, 128, 16
PEER_NEXPERTS = PEER_NKEYS * PEER_NKEYS
LANES = 128
SUBLANES = 8
VMEM_LIMIT = 56 * 1024 * 1024


def _params(sem):
    return pltpu.CompilerParams(dimension_semantics=sem, vmem_limit_bytes=VMEM_LIMIT)


def _dot(a, b, prec=None):
    return jnp.dot(a, b, precision=prec, preferred_element_type=F32)


def _dot_nt(a, b, prec=None):
    return lax.dot_general(a, b, (((1,), (1,)), ((), ())), precision=prec, preferred_element_type=F32)


def _dot_tn(a, b, prec=None):
    return lax.dot_general(a, b, (((0,), (0,)), ((), ())), precision=prec, preferred_element_type=F32)


def _bf(x):
    return x.astype(BF16)


def _iota(shape, dim):
    return lax.broadcasted_iota(jnp.int32, shape, dim)


def _same_block(shape, rblk, cblk):
    r = _iota(shape, 0) >> (rblk.bit_length() - 1)
    c = _iota(shape, 1) >> (cblk.bit_length() - 1)
    return r == c


def _full(shape):
    n = len(shape)
    return pl.BlockSpec(shape, lambda *_: (0,) * n)


def _block_ones(n, blk):
    i = np.arange(n) // blk
    return jnp.asarray((i[:, None] == i[None, :]).astype(np.float32))


def _tril_ones(n):
    return jnp.asarray(np.tril(np.ones((n, n), np.float32)))


def _norm_proj_kernel(x_ref, g_ref, w_ref, zr_ref, zk_ref, zg_ref):
    x = x_ref[...]
    y = x * lax.rsqrt(jnp.mean(x * x, axis=-1, keepdims=True) + NORM_EPS) * g_ref[...]
    yb = _bf(y)
    zr_ref[...] = _dot(yb, w_ref[:, 0:RET_IN])
    zk_ref[...] = _dot(yb, w_ref[:, RET_IN:RET_IN + RWKV_IN])
    zg_ref[...] = _dot(yb, w_ref[:, RET_IN + RWKV_IN:D_INP])


def _norm_proj(x2, gain, w_in_p, tm=512):
    t = x2.shape[0]
    return pl.pallas_call(
        _norm_proj_kernel,
        grid=(t // tm,),
        in_specs=[pl.BlockSpec((tm, D_MODEL), lambda i: (i, 0)), _full((1, D_MODEL)), _full((D_MODEL, D_INP))],
        out_specs=[pl.BlockSpec((tm, RET_IN), lambda i: (i, 0)), pl.BlockSpec((tm, RWKV_IN), lambda i: (i, 0)),
                   pl.BlockSpec((tm, GLA_IN), lambda i: (i, 0))],
        out_shape=[jax.ShapeDtypeStruct((t, RET_IN), F32), jax.ShapeDtypeStruct((t, RWKV_IN), F32),
                   jax.ShapeDtypeStruct((t, GLA_IN), F32)],
        compiler_params=_params(("arbitrary",)),
        name="norm_proj",
    )(x2, gain, w_in_p)


RET_BLOCK = 512


def _ret_kernel(z_ref, cos_ref, sin_ref, dmat_ref, qdec_ref, kdec_ref, cdec_ref, ones_ref, o_ref, st_ref):
    c = RET_CHUNK
    tb = RET_BLOCK

    @pl.when(pl.program_id(1) == 0)
    def _():
        st_ref[...] = jnp.zeros_like(st_ref)

    z = z_ref[0]
    q, k, v, g = (z[:, i * RET_W:(i + 1) * RET_W] for i in range(4))
    cos, sin = cos_ref[...], sin_ref[...]
    first_half = (_iota((tb, RET_W), 1) & (RET_DK - 1)) < RET_DK // 2

    def rot(t):
        return jnp.where(first_half, -pltpu.roll(t, RET_W - RET_DK // 2, 1), pltpu.roll(t, RET_DK // 2, 1))

    q = q * cos + rot(q) * sin
    k = (k * cos + rot(k) * sin) * (RET_DK ** -0.5)
    qd = q * qdec_ref[...]
    kd = k * kdec_ref[...]
    head0 = _iota((c, LANES), 1) < RET_DK
    bd = _same_block((LANES, LANES), RET_DV, RET_DK)
    units = [(ch, p) for ch in range(tb // c) for p in range(RET_HEADS // 2)]
    tile = lambda t, u: t[u[0] * c:(u[0] + 1) * c, u[1] * LANES:(u[1] + 1) * LANES]
    kb = [_bf(tile(k, u)) for u in units]
    intra = []
    for hh in range(2):
        mh = head0 if hh == 0 else jnp.logical_not(head0)
        sc = [_dot_nt(_bf(jnp.where(mh, tile(q, u), 0.0)), kb[i]) * dmat_ref[2 * u[1] + hh] for i, u in enumerate(units)]
        intra.append([_dot(_bf(s_), _bf(jnp.where(mh, tile(v, u), 0.0))) for s_, u in zip(sc, units)])
    upd = [jnp.where(bd, _dot_tn(_bf(tile(v, u)), _bf(tile(kd, u))), 0.0) for u in units]
    qdb = [_bf(tile(qd, u)) for u in units]
    st = [st_ref[p] for p in range(RET_HEADS // 2)]
    rows = []
    for ch in range(tb // c):
        row = []
        for p in range(RET_HEADS // 2):
            i = units.index((ch, p))
            row.append(intra[0][i] + intra[1][i] + _dot_nt(qdb[i], _bf(st[p])))
            st[p] = st[p] * cdec_ref[:, p * LANES:(p + 1) * LANES] + upd[i]
        rows.append(jnp.concatenate(row, axis=1))
    for p in range(RET_HEADS // 2):
        st_ref[p] = st[p]
    o = jnp.concatenate(rows, axis=0)
    ms = _dot_exact_rhs(o * o, ones_ref[...], 2) * (1.0 / RET_DV)
    o = o * lax.rsqrt(ms + HEAD_NORM_EPS)
    o_ref[0] = o * (g * jax.nn.sigmoid(g))


def _retention_consts(s):
    c = RET_CHUNK
    reps = RET_BLOCK // c
    log_gamma = np.log1p(-np.exp(np.linspace(np.log(1.0 / 32.0), np.log(1.0 / 512.0), RET_HEADS)))
    idx = np.arange(c, dtype=np.float64)
    diff = idx[:, None] - idx[None, :]
    dmat = np.where(diff >= 0, np.exp(log_gamma[:, None, None] * np.where(diff >= 0, diff, 0.0)), 0.0)
    lg_lane = np.repeat(log_gamma, RET_DK)[None, :]
    qdec = np.tile(np.exp(lg_lane * (idx[:, None] + 1.0)), (reps, 1))
    kdec = np.tile(np.exp(lg_lane * (c - 1.0 - idx[:, None])), (reps, 1))
    cdec = np.exp(lg_lane * c)
    half = RET_DK // 2
    inv = ROPE_BASE ** (-np.arange(half, dtype=np.float64) / half)
    ang = np.arange(s, dtype=np.float64)[:, None] * inv[None, :]
    cos = np.tile(np.cos(ang), (1, 2 * RET_HEADS))
    sin = np.tile(np.sin(ang), (1, 2 * RET_HEADS))
    f = lambda a: jnp.asarray(a.astype(np.float32))
    return f(cos), f(sin), f(dmat), f(qdec), f(kdec), f(cdec)


def _retention(z_ret):
    b, s, _ = z_ret.shape
    c = RET_CHUNK
    tb = RET_BLOCK
    cos, sin, dmat, qdec, kdec, cdec = _retention_consts(s)
    return pl.pallas_call(
        _ret_kernel,
        grid=(b, s // tb),
        in_specs=[pl.BlockSpec((1, tb, RET_IN), lambda i, j: (i, j, 0)),
                  pl.BlockSpec((tb, RET_W), lambda i, j: (j, 0)), pl.BlockSpec((tb, RET_W), lambda i, j: (j, 0)),
                  _full((RET_HEADS, c, c)), _full((tb, RET_W)), _full((tb, RET_W)), _full((1, RET_W)),
                  _full((RET_W, RET_W))],
        out_specs=pl.BlockSpec((1, tb, RET_W), lambda i, j: (i, j, 0)),
        out_shape=jax.ShapeDtypeStruct((b, s, RET_W), F32),
        scratch_shapes=[pltpu.VMEM((RET_HEADS // 2, LANES, LANES), F32)],
        compiler_params=_params(("arbitrary", "arbitrary")),
        name="retention",
    )(z_ret, cos, sin, dmat, qdec, kdec, cdec, _bf(_block_ones(RET_W, RET_DV)))


RWKV_BLOCK = 512
INV_PASSES = 3


def _split(x, parts):
    out = []
    for _ in range(parts - 1):
        h = _bf(x)
        out.append(h)
        x = x - h.astype(F32)
    out.append(_bf(x))
    return out


def _dot_exact_lhs(a_bf, x, parts):
    acc = None
    for p in _split(x, parts):
        t = _dot(a_bf, p)
        acc = t if acc is None else acc + t
    return acc


def _dot_exact_rhs(x, b_bf, parts):
    acc = None
    for p in _split(x, parts):
        t = _dot(p, b_bf)
        acc = t if acc is None else acc + t
    return acc


def _dot_f32(a, b, passes):
    if passes == 1:
        return _dot(_bf(a), _bf(b))
    if passes == 3:
        ah, al = _split(a, 2)
        bh, bl = _split(b, 2)
        return _dot(ah, bh) + (_dot(ah, bl) + _dot(al, bh))
    return _dot(a, b, HI)


def _rwkv_kernel(z_ref, mu_ref, wwa_ref, wb_ref, ab_ref, gup_ref, kk_ref, ka_ref, rk_ref, lnw_ref, lnb_ref,
                 ones_ref, ltri_ref, o_ref, st_ref, carry_ref):
    c = RWKV_CHUNK
    tb = RWKV_BLOCK
    w_ = RWKV_W

    @pl.when(pl.program_id(1) == 0)
    def _():
        st_ref[...] = jnp.zeros_like(st_ref)
        carry_ref[...] = jnp.zeros_like(carry_ref)

    f = z_ref[0]
    f_prev = jnp.where(_iota((tb, RWKV_IN), 0) == 0, carry_ref[...], pltpu.roll(f, 1, 0))
    carry_ref[...] = f[tb - 1:tb, :]
    f = f + (f_prev - f) * mu_ref[...]
    r, k, v = f[:, 0:w_], f[:, w_:2 * w_], f[:, 2 * w_:3 * w_]
    wa, gd = f[:, 3 * w_:3 * w_ + LANES], f[:, 3 * w_ + LANES:RWKV_IN]
    lane = _iota((tb, LANES), 1)
    proj = _dot_f32(jnp.where(lane < DECAY_LORA, jnp.tanh(wa), wa), wwa_ref[...], 3)
    wlog = -jax.nn.softplus(-(wb_ref[...] + proj[:, 0:w_])) - 0.5
    ld = -jnp.exp(wlog)
    a = jax.nn.sigmoid(ab_ref[...] + proj[:, w_:2 * w_])
    g = _dot(_bf(jax.nn.sigmoid(gd)), gup_ref[...])
    ones = ones_ref[...]
    kk = k * kk_ref[...]
    kk = kk * lax.rsqrt(jnp.maximum(_dot_exact_rhs(kk * kk, ones, 2), 1e-24))
    k = k * (1.0 + (a - 1.0) * ka_ref[...])
    cum = _dot_exact_lhs(ltri_ref[...], ld, 3)
    e_neg = jnp.exp(-cum)
    b = kk * a
    a_t_all = -kk * jnp.exp(cum - ld)
    b_t_all, k_t_all, r_t_all = b * e_neg, k * e_neg, r * jnp.exp(cum)

    n2 = 2 * c
    ri, ci = _iota((n2, n2), 0), _iota((n2, n2), 1)
    same_head = _same_block((n2, n2), c, c)
    strict = jnp.logical_and(same_head, ri > ci)
    incl = jnp.logical_and(same_head, ri >= ci)
    sub = _same_block((n2, n2), RWKV_SUB, RWKV_SUB)
    eye = (ri == ci).astype(F32)
    bd = _same_block((LANES, LANES), RWKV_HEAD, RWKV_HEAD)
    head0 = _iota((c, LANES), 1) < RWKV_HEAD
    mm = functools.partial(_dot_f32, passes=INV_PASSES)
    mm1 = functools.partial(_dot_f32, passes=1)

    def stack(x):
        return jnp.concatenate([jnp.where(head0, x, 0.0), jnp.where(head0, 0.0, x)], axis=0)

    def fold(x):
        return x[:c] + x[c:]

    units = [(ch, p) for ch in range(tb // c) for p in range(RWKV_HEADS // 2)]
    each = lambda fn, *lists: [fn(*args) for args in zip(*lists)]
    rows_of = lambda ch: slice(ch * c, (ch + 1) * c)
    lanes_of = lambda p: slice(p * LANES, (p + 1) * LANES)
    tile = lambda t: [stack(t[rows_of(ch), lanes_of(p)]) for ch, p in units]
    a_s, r_s, b_s, k_s, v_s = tile(a_t_all), tile(r_t_all), tile(b_t_all), tile(k_t_all), tile(v)
    cl = [cum[(ch + 1) * c - 1:(ch + 1) * c, lanes_of(p)] for ch, p in units]
    e_rem = [jnp.exp(cl_u - cum[rows_of(ch), lanes_of(p)]) for cl_u, (ch, p) in zip(cl, units)]
    b_h = [b[rows_of(ch), lanes_of(p)] * e for e, (ch, p) in zip(e_rem, units)]
    k_h = [k[rows_of(ch), lanes_of(p)] * e for e, (ch, p) in zip(e_rem, units)]
    v_p = [v[rows_of(ch), lanes_of(p)] for ch, p in units]
    big = each(lambda x, y, z, w: _dot_nt(_bf(jnp.concatenate([x, y], axis=0)), _bf(jnp.concatenate([z, w], axis=0))),
               a_s, r_s, b_s, k_s)
    a_ab = [jnp.where(strict, t[:n2, :n2], 0.0) for t in big]
    a_ak = [jnp.where(strict, t[:n2, n2:], 0.0) for t in big]
    a_rb = [jnp.where(incl, t[n2:, :n2], 0.0) for t in big]
    a_rk = [jnp.where(incl, t[n2:, n2:], 0.0) for t in big]
    d1 = [jnp.where(sub, t, 0.0) for t in a_ab]
    lo = each(lambda x, y: x - y, a_ab, d1)
    d2 = each(mm, d1, d1)
    t_d = each(lambda x, y: mm1(eye + x, eye + y), d1, d2)
    d4 = each(mm, d2, d2)
    t_d = each(lambda x, y: mm1(x, eye + y), t_d, d4)
    d8 = each(mm, d4, d4)
    t_d = each(lambda x, y: mm1(x, eye + y), t_d, d8)
    e1 = each(mm1, t_d, lo)
    e2 = each(mm1, e1, e1)
    t_inv = each(lambda x, y: mm1(eye + x, eye + y), e1, e2)
    t_inv = each(mm1, t_inv, t_d)
    akv = each(lambda x, y: _dot(_bf(x), _bf(y)), a_ak, v_s)
    wy = each(lambda t, x, y: mm1(t, jnp.concatenate([x, y], axis=1)), t_inv, a_s, akv)
    qo = each(lambda x, y: _dot(_bf(x), _bf(y)), a_rb, wy)
    rkv = each(lambda x, y: _dot(_bf(x), _bf(y)), a_rk, v_s)
    w_p = [fold(t[:, :LANES]) for t in wy]
    y_p = [fold(t[:, LANES:]) for t in wy]
    q_p = each(lambda x, y: fold(x + y[:, :LANES]), r_s, qo)
    ol_p = each(lambda x, y: fold(x[:, LANES:] + y), qo, rkv)
    gmat = each(lambda x, y: jnp.where(bd, _dot_tn(_bf(x), _bf(y)), 0.0), w_p, b_h)
    nmat = each(lambda y, vv, bb, kk_: jnp.where(bd, _dot_tn(_bf(jnp.concatenate([y, vv], axis=0)),
                                                             _bf(jnp.concatenate([bb, kk_], axis=0))), 0.0),
                y_p, v_p, b_h, k_h)
    pre = {u: (q_p[i], ol_p[i], gmat[i], nmat[i], jnp.exp(cl[i])) for i, u in enumerate(units)}

    outs = [[None] * (RWKV_HEADS // 2) for _ in range(tb // c)]
    st = [st_ref[p] for p in range(RWKV_HEADS // 2)]
    for ch in range(tb // c):
        for p in range(RWKV_HEADS // 2):
            q_u, ol_u, g_u, n_u, pc_u = pre[ch, p]
            stb = _bf(st[p])
            outs[ch][p] = _dot_nt(_bf(q_u), stb) + ol_u
            st[p] = st[p] * pc_u + _dot(stb, _bf(g_u)) + n_u
    for p in range(RWKV_HEADS // 2):
        st_ref[p] = st[p]
    o = jnp.concatenate([jnp.concatenate(row, axis=1) for row in outs], axis=0)
    inv_n = 1.0 / RWKV_HEAD
    d = o - _dot_exact_rhs(o, ones, 2) * inv_n
    var = _dot_exact_rhs(d * d, ones, 2) * inv_n
    o = d * lax.rsqrt(var + RWKV_GN_EPS) * lnw_ref[...] + lnb_ref[...]
    o = o + _dot_exact_rhs(r * k * rk_ref[...], ones, 2) * v
    o_ref[0] = o * g


def _rwkv(z_rwkv, pr):
    b, s, _ = z_rwkv.shape
    tb = RWKV_BLOCK
    row = lambda n: _full((1, n))
    ltri = np.kron(np.eye(tb // RWKV_CHUNK, dtype=np.float32), np.tril(np.ones((RWKV_CHUNK, RWKV_CHUNK), np.float32)))
    return pl.pallas_call(
        _rwkv_kernel,
        grid=(b, s // tb),
        in_specs=[pl.BlockSpec((1, tb, RWKV_IN), lambda i, j: (i, j, 0)), row(RWKV_IN), _full((LANES, 2 * RWKV_W)),
                  row(RWKV_W), row(RWKV_W), _full((GATE_LORA, RWKV_W)), row(RWKV_W), row(RWKV_W), row(RWKV_W),
                  row(RWKV_W), row(RWKV_W), _full((RWKV_W, RWKV_W)), _full((tb, tb))],
        out_specs=pl.BlockSpec((1, tb, RWKV_W), lambda i, j: (i, j, 0)),
        out_shape=jax.ShapeDtypeStruct((b, s, RWKV_W), F32),
        scratch_shapes=[pltpu.VMEM((RWKV_HEADS // 2, LANES, LANES), F32), pltpu.VMEM((1, RWKV_IN), F32)],
        compiler_params=_params(("arbitrary", "arbitrary")),
        name="rwkv7",
    )(z_rwkv, pr["mu"], pr["wwa"], pr["w_bias"], pr["a_bias"], pr["g_up"], pr["k_k"], pr["k_a"], pr["r_k"],
      pr["ln_w"], pr["ln_b"], _bf(_block_ones(RWKV_W, RWKV_HEAD)), jnp.asarray(ltri, dtype=BF16))


def _gla_kernel(z_ref, gkup_ref, gkb_ref, nw_ref, ind_ref, ones_ref, ltri_ref, o_ref, st_ref, x_ref):
    n = GLA_SUB
    tb = GLA_BLOCK
    nsub = tb // n

    @pl.when(pl.program_id(1) == 0)
    def _():
        st_ref[...] = jnp.zeros_like(st_ref)

    z = z_ref[0]
    q_all = z[:, 0:GLA_QW] * (GLA_DK ** -0.5)
    k_all = z[:, GLA_QW:2 * GLA_QW]
    v_all = z[:, 2 * GLA_QW:2 * GLA_QW + GLA_VW]
    gkd = z[:, 2 * GLA_QW + GLA_VW:2 * GLA_QW + GLA_VW + LANES]
    g = z[:, 2 * GLA_QW + GLA_VW + LANES:GLA_IN]
    gk = jax.nn.log_sigmoid(_dot_f32(gkd, gkup_ref[...], 3) + gkb_ref[...]) * (1.0 / GLA_GATE_NORMALIZER)
    bc_all = _dot_exact_lhs(ltri_ref[...], gk, 3)
    bd = _same_block((GLA_VW, GLA_QW), GLA_DVP, GLA_DKP)
    rowi = _iota((n, GLA_QW), 0)
    subs = range(nsub)
    rows = [slice(sc * n, (sc + 1) * n) for sc in subs]
    q = [q_all[r] for r in rows]
    k = [k_all[r] for r in rows]
    v = [v_all[r] for r in rows]
    bc = [bc_all[r] for r in rows]
    bl = [t[n - 1:n, :] for t in bc]
    for sc in subs:
        for j in range(n):
            xj = q[sc] * jnp.exp(jnp.minimum(bc[sc] - bc[sc][j:j + 1, :], 0.0)) * k[sc][j:j + 1, :]
            x_ref[(sc * n + j) * n:(sc * n + j + 1) * n, :] = jnp.where(rowi >= j, xj, 0.0)
    e = _dot(_bf(x_ref[...]), ind_ref[...])
    o_intra = []
    for sc in subs:
        acc = None
        for j in range(n):
            t = e[(sc * n + j) * n:(sc * n + j + 1) * n, :] * v[sc][j:j + 1, :]
            acc = t if acc is None else acc + t
        o_intra.append(acc)
    upd = [jnp.where(bd, _dot_tn(_bf(v[sc]), _bf(k[sc] * jnp.exp(bl[sc] - bc[sc]))), 0.0) for sc in subs]
    qe = [_bf(q[sc] * jnp.exp(bc[sc])) for sc in subs]
    st = st_ref[...]
    outs = []
    for sc in subs:
        outs.append(_dot_nt(qe[sc], _bf(st)) + o_intra[sc])
        st = st * jnp.exp(bl[sc]) + upd[sc]
    st_ref[...] = st
    o = jnp.concatenate(outs, axis=0)
    ms = _dot_exact_rhs(o * o, ones_ref[...], 2) * (1.0 / GLA_DV)
    o_ref[0] = o * lax.rsqrt(ms + HEAD_NORM_EPS) * nw_ref[...] * (g * jax.nn.sigmoid(g))


def _gla(z_gla, pr):
    b, s, _ = z_gla.shape
    tb = GLA_BLOCK
    hq = np.arange(GLA_QW) // GLA_DKP
    hv = np.arange(GLA_VW) // GLA_DVP
    ind = jnp.asarray((hq[:, None] == hv[None, :]).astype(np.float32), dtype=BF16)
    ltri = np.kron(np.eye(tb // GLA_SUB, dtype=np.float32), np.tril(np.ones((GLA_SUB, GLA_SUB), np.float32)))
    return pl.pallas_call(
        _gla_kernel,
        grid=(b, s // tb),
        in_specs=[pl.BlockSpec((1, tb, GLA_IN), lambda i, j: (i, j, 0)), _full((LANES, GLA_QW)), _full((1, GLA_QW)),
                  _full((1, GLA_VW)), _full((GLA_QW, GLA_VW)), _full((GLA_VW, GLA_VW)), _full((tb, tb))],
        out_specs=pl.BlockSpec((1, tb, GLA_VW), lambda i, j: (i, j, 0)),
        out_shape=jax.ShapeDtypeStruct((b, s, GLA_VW), F32),
        scratch_shapes=[pltpu.VMEM((GLA_VW, GLA_QW), F32), pltpu.VMEM((tb * GLA_SUB, GLA_QW), F32)],
        compiler_params=_params(("arbitrary", "arbitrary")),
        name="gla",
    )(z_gla, pr["gk_up"], pr["gk_bias"], pr["norm_w"], ind, _bf(_block_ones(GLA_VW, GLA_DVP)),
      jnp.asarray(ltri, dtype=BF16))


def _out_proj_kernel(x_ref, a_ref, b_ref, c_ref, w_ref, g_ref, x1_ref, hnt_ref):
    acc = _dot(_bf(a_ref[...]), w_ref[0:RET_W, :])
    acc += _dot(_bf(b_ref[...]), w_ref[RET_W:RET_W + RWKV_W, :])
    acc += _dot(_bf(c_ref[...]), w_ref[RET_W + RWKV_W:D_MIXP, :])
    x1 = x_ref[...] + acc
    x1_ref[...] = x1
    hn = x1 * lax.rsqrt(jnp.mean(x1 * x1, axis=-1, keepdims=True) + NORM_EPS) * g_ref[...]
    hnt_ref[...] = pltpu.bitcast(_bf(hn.T), jnp.uint32)


def _out_proj(x2, o_ret, o_rwkv, o_gla, w_out_p, gain, tm=512):
    t = x2.shape[0]
    blk = lambda n: pl.BlockSpec((tm, n), lambda i: (i, 0))
    return pl.pallas_call(
        _out_proj_kernel,
        grid=(t // tm,),
        in_specs=[blk(D_MODEL), blk(RET_W), blk(RWKV_W), blk(GLA_VW), _full((D_MIXP, D_MODEL)), _full((1, D_MODEL))],
        out_specs=[blk(D_MODEL), pl.BlockSpec((D_MODEL // 2, tm), lambda i: (0, i))],
        out_shape=[jax.ShapeDtypeStruct((t, D_MODEL), F32), jax.ShapeDtypeStruct((D_MODEL // 2, t), jnp.uint32)],
        compiler_params=_params(("arbitrary",)),
        name="out_proj",
    )(x2, o_ret, o_rwkv, o_gla, w_out_p, gain)


def _staircase_pairs():
    k = PEER_TOPK
    return [(a, b) for a in range(k) for b in range(k) if (a + 1) * (b + 1) <= k]


def _tree_max(xs):
    xs = list(xs)
    while len(xs) > 1:
        nxt = [jnp.maximum(xs[2 * i], xs[2 * i + 1]) for i in range(len(xs) // 2)]
        if len(xs) % 2:
            nxt.append(xs[-1])
        xs = nxt
    return xs[0]


def _route_kernel(hnt_ref, wq_ref, keys_ref, n1_ref, e1_ref, r2_ref, e2_ref, work_ref, s_ref, vals_ref, rank_ref):
    nh, nk, k_top = PEER_HEADS, PEER_NKEYS, PEER_TOPK
    qt = _bf(_dot(wq_ref[...], pltpu.bitcast(hnt_ref[...], BF16)))
    for p in range(2):
        for h in range(nh):
            r0 = (p * nh + h) * PEER_QHALF
            s = _dot(keys_ref[p, h], qt[r0:r0 + PEER_QHALF, :])
            work_ref[p, h] = s
            s_ref[p, h] = s
    rank_ref[...] = jnp.full(rank_ref.shape, float(k_top), F32)
    groups = nk // SUBLANES

    def extract(r, carry):
        rf = jnp.asarray(r, F32)
        for p in range(2):
            for h in range(nh):
                tiles = [work_ref[p, h, i * SUBLANES:(i + 1) * SUBLANES, :] for i in range(groups)]
                m = _tree_max(tiles)
                for sh in (4, 2, 1):
                    m = jnp.maximum(m, pltpu.roll(m, sh, 0))
                vals_ref[p, r, pl.ds(h, 1), :] = m[0:1, :]
                for i in range(groups):
                    rows = slice(i * SUBLANES, (i + 1) * SUBLANES)
                    is_max = tiles[i] == m
                    work_ref[p, h, rows, :] = jnp.where(is_max, -jnp.inf, tiles[i])
                    rank_ref[p, h, rows, :] = jnp.where(is_max, rf, rank_ref[p, h, rows, :])
        return carry

    lax.fori_loop(0, k_top, extract, 0)

    v1 = [vals_ref[0, r] for r in range(k_top)]
    v2 = [vals_ref[1, r] for r in range(k_top)]
    cand = {(a, b): v1[a] + v2[b] for a, b in _staircase_pairs()}
    work = list(cand.values())
    tau = None
    for it in range(k_top):
        tau = _tree_max(work)
        if it + 1 < k_top:
            work = [jnp.where(w == tau, -jnp.inf, w) for w in work]
    top = cand[(0, 0)]
    z = None
    for c in cand.values():
        zi = jnp.where(c >= tau, jnp.exp(c - top), 0.0)
        z = zi if z is None else z + zi
    scale2 = 0.5 / z
    n_of_rank = []
    for a in range(k_top):
        cnt = None
        for b in range(k_top // (a + 1)):
            ge = (cand[(a, b)] >= tau).astype(F32)
            cnt = ge if cnt is None else cnt + ge
        n_of_rank.append(cnt)
    m1, m2 = v1[0], v2[0]
    for h in range(nh):
        r1 = rank_ref[0, h]
        n1 = jnp.zeros_like(r1)
        for a in range(k_top):
            n1 = jnp.where(r1 == float(a), n_of_rank[a][h:h + 1, :], n1)
        n1_ref[h] = n1
        e1_ref[h] = jnp.exp(s_ref[0, h] - m1[h:h + 1, :])
        r2_ref[h] = pltpu.bitcast(_bf(rank_ref[1, h]), jnp.uint32)
        e2_ref[h] = pltpu.bitcast(_bf(jnp.exp(s_ref[1, h] - m2[h:h + 1, :]) * scale2[h:h + 1, :]), jnp.uint32)


def _route(hnt, wq_t, keys, tb=256):
    t = hnt.shape[1]
    nh, nk = PEER_HEADS, PEER_NKEYS
    out = pl.BlockSpec((nh, nk, tb), lambda i: (0, 0, i))
    packed = pl.BlockSpec((nh, nk // 2, tb), lambda i: (0, 0, i))
    return pl.pallas_call(
        _route_kernel,
        grid=(t // tb,),
        in_specs=[pl.BlockSpec((D_MODEL // 2, tb), lambda i: (0, i)), _full((2 * nh * PEER_QHALF, D_MODEL)),
                  _full((2, nh, nk, PEER_QHALF))],
        out_specs=[out, out, packed, packed],
        out_shape=[jax.ShapeDtypeStruct((nh, nk, t), F32)] * 2 + [jax.ShapeDtypeStruct((nh, nk // 2, t), jnp.uint32)] * 2,
        scratch_shapes=[pltpu.VMEM((2, nh, nk, tb), F32), pltpu.VMEM((2, nh, nk, tb), F32),
                        pltpu.VMEM((2, PEER_TOPK, nh, tb), F32), pltpu.VMEM((2, nh, nk, tb), F32)],
        compiler_params=_params(("arbitrary",)),
        name="peer_route",
    )(hnt, wq_t, keys)


EXPERT_SLABS = 8
EXPERT_PAIR = 2
PIPE_LAG = 2
ROW_SPLIT = 2


def _expert_kernel(hnt_ref, u_ref, vt_ref, n1_ref, e1_ref, r2_ref, e2_ref, x1_ref, gf_ref, o_ref, acc_ref, ht0, ht1,
                   act0, act1, rn_ref, re_ref, *, n_blocks, final_norm):
    s = pl.program_id(0)
    n_pairs = pl.num_programs(0) - PIPE_LAG
    nk = PEER_NKEYS
    tb = hnt_ref.shape[1]
    j2 = lax.rem(jnp.clip(s - 1, 0, n_pairs - 1), n_blocks)
    j3 = lax.rem(jnp.clip(s - PIPE_LAG, 0, n_pairs - 1), n_blocks)
    live = jnp.logical_and(s >= 1, s <= n_pairs).astype(F32)

    @pl.when(s == 0)
    def _():
        for r in (ht0, ht1, act0, act1):
            r[...] = jnp.zeros_like(r)

    @pl.when(j3 == 0)
    def _():
        acc_ref[...] = jnp.zeros_like(acc_ref)

    for h in range(PEER_HEADS):
        for k in range(EXPERT_SLABS):
            r = h * EXPERT_SLABS + k
            rn_ref[r:r + 1, :] = n1_ref[h, pl.ds(j2 * EXPERT_SLABS + k, 1), :]
            re_ref[r:r + 1, :] = e1_ref[h, pl.ds(j2 * EXPERT_SLABS + k, 1), :] * live

    def stages(ht_w, ht_r, act_w, act_r):
        half_w = 2 * LANES
        kc = 2 * LANES
        n_kc = D_MODEL // kc
        n_ec = (EXPERT_SLABS * nk) // kc

        assert n_kc == n_ec
        rws = (EXPERT_SLABS * nk) // ROW_SPLIT
        mws = D_MODEL // ROW_SPLIT
        hid = {}
        prj = {}

        def hidden_piece(half, c, rs):
            cols = slice(half * half_w, (half + 1) * half_w)
            t = _dot(pltpu.bitcast(u_ref[rs * rws // 2:(rs + 1) * rws // 2, c * kc:(c + 1) * kc], BF16),
                     pltpu.bitcast(hnt_ref[c * kc // 2:(c + 1) * kc // 2, cols], BF16))
            hid[half, rs] = t if c == 0 else hid[half, rs] + t

        def project_piece(half, c, rs):
            cols = slice(half * half_w, (half + 1) * half_w)
            t = _dot(pltpu.bitcast(vt_ref[rs * mws // 2:(rs + 1) * mws // 2, c * kc:(c + 1) * kc], BF16),
                     act_r[c * kc:(c + 1) * kc, cols])
            prj[half, rs] = t if c == 0 else prj[half, rs] + t

        gates = {}

        def gate_part(st, g0, heads, last):
            cols = slice(st * LANES, (st + 1) * LANES)
            for h in heads:
                r2 = pltpu.bitcast(r2_ref[h, :, cols], BF16)
                e2 = pltpu.bitcast(e2_ref[h, :, cols], BF16)
                for kk in range(EXPERT_PAIR):
                    r = h * EXPERT_SLABS + g0 + kk
                    n1 = _bf(rn_ref[r:r + 1, cols])
                    e1 = _bf(re_ref[r:r + 1, cols])
                    gh = jnp.where(r2 < n1, e2 * e1, jnp.zeros_like(e2))
                    gates[st, g0 + kk] = gh if h == 0 else gates[st, g0 + kk] + gh
            if last:
                for kk in range(EXPERT_PAIR):
                    rows = slice((g0 + kk) * nk, (g0 + kk + 1) * nk)
                    hk = ht_r[rows, cols]
                    act_w[rows, cols] = _bf(hk * (1.0 + lax.erf(hk * (2.0 ** -0.5)))) * gates[st, g0 + kk]

        hh = PEER_HEADS // 2
        vpu_work = [functools.partial(gate_part, st, g0, heads, last)
                    for st in range(tb // LANES) for g0 in range(0, EXPERT_SLABS, EXPERT_PAIR)
                    for heads, last in ((range(0, hh), False), (range(hh, PEER_HEADS), True))]
        mxu_work = [functools.partial(piece, half, c, rs)
                    for half in range(2) for c in range(n_kc) for rs in range(ROW_SPLIT)
                    for piece in (hidden_piece, project_piece)]
        assert len(mxu_work) == len(vpu_work)
        for m, v_ in zip(mxu_work, vpu_work):
            m()
            v_()
        for half in range(2):
            cols = slice(half * half_w, (half + 1) * half_w)
            for rs in range(ROW_SPLIT):
                ht_w[rs * rws:(rs + 1) * rws, cols] = hid[half, rs]
                acc_ref[rs * mws:(rs + 1) * mws, cols] += prj[half, rs]

    parity = lax.rem(s, 2)

    @pl.when(parity == 0)
    def _():
        stages(ht0, ht1, act1, act0)

    @pl.when(parity == 1)
    def _():
        stages(ht1, ht0, act0, act1)

    @pl.when(jnp.logical_and(j3 == n_blocks - 1, s >= PIPE_LAG))
    def _():
        y = acc_ref[...].T + x1_ref[...]
        if final_norm:
            y = y * lax.rsqrt(jnp.mean(y * y, axis=-1, keepdims=True) + NORM_EPS) * gf_ref[...]
        o_ref[...] = y


def _experts(hnt, u_b, vt_b, n1, e1, r2, e2, x1, gain_f, final_norm, tb=512):
    t = hnt.shape[1]
    nh, nk = PEER_HEADS, PEER_NKEYS
    eb = EXPERT_SLABS * nk
    n_blocks = PEER_NEXPERTS // eb
    n_pairs = (t // tb) * n_blocks

    def pair(lag):
        def f(s):
            p = jnp.clip(s - lag, 0, n_pairs - 1)
            return p // n_blocks, lax.rem(p, n_blocks)
        return f

    tok = lambda lag: (lambda s: pair(lag)(s)[0])
    blk = lambda lag: (lambda s: pair(lag)(s)[1])
    routed = lambda rows: pl.BlockSpec((nh, rows, tb), lambda s: (0, 0, tok(1)(s)))
    nrow = PEER_HEADS * EXPERT_SLABS
    return pl.pallas_call(
        functools.partial(_expert_kernel, n_blocks=n_blocks, final_norm=final_norm),
        grid=(n_pairs + PIPE_LAG,),
        in_specs=[pl.BlockSpec((D_MODEL // 2, tb), lambda s: (0, tok(0)(s))),
                  pl.BlockSpec((eb // 2, D_MODEL), lambda s: (blk(0)(s), 0)),
                  pl.BlockSpec((D_MODEL // 2, eb), lambda s: (0, blk(PIPE_LAG)(s))),
                  routed(nk), routed(nk), routed(nk // 2), routed(nk // 2),
                  pl.BlockSpec((tb, D_MODEL), lambda s: (tok(PIPE_LAG)(s), 0)), _full((1, D_MODEL))],
        out_specs=pl.BlockSpec((tb, D_MODEL), lambda s: (tok(PIPE_LAG)(s), 0)),
        out_shape=jax.ShapeDtypeStruct((t, D_MODEL), F32),
        scratch_shapes=[pltpu.VMEM((D_MODEL, tb), F32), pltpu.VMEM((eb, tb), F32), pltpu.VMEM((eb, tb), F32),
                        pltpu.VMEM((eb, tb), BF16), pltpu.VMEM((eb, tb), BF16),
                        pltpu.VMEM((nrow, tb), F32), pltpu.VMEM((nrow, tb), F32)],
        compiler_params=_params(("arbitrary",)),
        name="peer_experts",
    )(hnt, u_b, vt_b, n1, e1, r2, e2, x1, gain_f)


def _pad_heads(w, heads, d, dp):
    lead = w.shape[:-1]
    w = w.reshape(*lead, heads, d)
    w = jnp.pad(w, [(0, 0)] * len(lead) + [(0, 0), (0, dp - d)])
    return w.reshape(*lead, heads * dp)


def _layer_params(l, w_in, w_out, rwkv_mu, rwkv_w_up, rwkv_w_bias, rwkv_a_up, rwkv_a_bias, rwkv_g_up, rwkv_k_k,
                  rwkv_k_a, rwkv_r_k, rwkv_ln_w, rwkv_ln_b, gla_gk_up, gla_gk_bias, gla_norm_w, peer_w_q,
                  peer_sub_keys, peer_u, peer_v):
    wi = w_in[l]
    g0 = RET_IN + RWKV_IN
    qk, vw = GLA_HEADS * GLA_DK, GLA_HEADS * GLA_DV
    gq = _pad_heads(wi[:, g0:g0 + qk], GLA_HEADS, GLA_DK, GLA_DKP)
    gkk = _pad_heads(wi[:, g0 + qk:g0 + 2 * qk], GLA_HEADS, GLA_DK, GLA_DKP)
    gv = _pad_heads(wi[:, g0 + 2 * qk:g0 + 2 * qk + vw], GLA_HEADS, GLA_DV, GLA_DVP)
    ggk = jnp.pad(wi[:, g0 + 2 * qk + vw:g0 + 2 * qk + vw + GLA_GATE_LORA], ((0, 0), (0, LANES - GLA_GATE_LORA)))
    gg = _pad_heads(wi[:, g0 + 2 * qk + vw + GLA_GATE_LORA:], GLA_HEADS, GLA_DV, GLA_DVP)
    w_in_p = _bf(jnp.concatenate([wi[:, :g0], gq, gkk, gv, ggk, gg], axis=1))
    wo = w_out[l]
    m0 = RET_W + RWKV_W
    wo_gla = jnp.pad(wo[m0:].reshape(GLA_HEADS, GLA_DV, D_MODEL), ((0, 0), (0, GLA_DVP - GLA_DV), (0, 0)))
    w_out_p = _bf(jnp.concatenate([wo[:m0], wo_gla.reshape(GLA_VW, D_MODEL)], axis=0))
    zeros = jnp.zeros((DECAY_LORA, RWKV_W), F32)
    rw = dict(
        mu=rwkv_mu[l][None, :],
        wwa=jnp.concatenate([jnp.concatenate([rwkv_w_up[l], zeros], axis=1),
                             jnp.concatenate([zeros, rwkv_a_up[l]], axis=1)], axis=0),
        w_bias=rwkv_w_bias[l][None, :], a_bias=rwkv_a_bias[l][None, :], g_up=_bf(rwkv_g_up[l]),
        k_k=rwkv_k_k[l][None, :], k_a=rwkv_k_a[l][None, :], r_k=rwkv_r_k[l].reshape(1, RWKV_W),
        ln_w=rwkv_ln_w[l][None, :], ln_b=rwkv_ln_b[l][None, :])
    gl = dict(
        gk_up=jnp.pad(_pad_heads(gla_gk_up[l], GLA_HEADS, GLA_DK, GLA_DKP), ((0, LANES - GLA_GATE_LORA), (0, 0))),
        gk_bias=_pad_heads(gla_gk_bias[l][None, :], GLA_HEADS, GLA_DK, GLA_DKP),
        norm_w=_pad_heads(gla_norm_w[l][None, :], GLA_HEADS, GLA_DV, GLA_DVP))
    wq = peer_w_q[l].reshape(D_MODEL, PEER_HEADS, 2, PEER_QHALF)
    wq_t = _bf(jnp.transpose(wq, (2, 1, 3, 0)).reshape(2 * PEER_HEADS * PEER_QHALF, D_MODEL))
    keys = _bf(jnp.transpose(peer_sub_keys[l], (1, 0, 2, 3)))
    u_p, vt_p = _pack_experts(peer_u, peer_v, l)
    return dict(w_in=w_in_p, w_out=w_out_p, rwkv=rw, gla=gl, wq_t=wq_t, keys=keys, u=u_p, vt=vt_p)


def _pack_experts_kernel(u_ref, v_ref, up_ref, vtp_ref):
    up_ref[...] = pltpu.bitcast(_bf(u_ref[0]), jnp.uint32)
    vtp_ref[...] = pltpu.bitcast(_bf(v_ref[0].T), jnp.uint32)


def _pack_experts(u, v, l, eb=1024):
    n = u.shape[1]
    return pl.pallas_call(
        _pack_experts_kernel,
        grid=(n // eb,),
        in_specs=[pl.BlockSpec((1, eb, D_MODEL), lambda i: (l, i, 0)), pl.BlockSpec((1, eb, D_MODEL), lambda i: (l, i, 0))],
        out_specs=[pl.BlockSpec((eb // 2, D_MODEL), lambda i: (i, 0)), pl.BlockSpec((D_MODEL // 2, eb), lambda i: (0, i))],
        out_shape=[jax.ShapeDtypeStruct((n // 2, D_MODEL), jnp.uint32),
                   jax.ShapeDtypeStruct((D_MODEL // 2, n), jnp.uint32)],
        compiler_params=_params(("arbitrary",)),
        name="pack_experts",
    )(u, v)


def _layer(x2, b, s, pr, gain_mix, gain_ffn, gain_final, final_norm):
    z_ret, z_rwkv, z_gla = _norm_proj(x2, gain_mix, pr["w_in"])
    o_ret = _retention(z_ret.reshape(b, s, RET_IN)).reshape(b * s, RET_W)
    o_rwkv = _rwkv(z_rwkv.reshape(b, s, RWKV_IN), pr["rwkv"]).reshape(b * s, RWKV_W)
    o_gla = _gla(z_gla.reshape(b, s, GLA_IN), pr["gla"]).reshape(b * s, GLA_VW)
    x1, hnt = _out_proj(x2, o_ret, o_rwkv, o_gla, pr["w_out"], gain_ffn)
    n1, e1, r2, e2 = _route(hnt, pr["wq_t"], pr["keys"])
    return _experts(hnt, pr["u"], pr["vt"], n1, e1, r2, e2, x1, gain_final, final_norm)


def kernel(x, norm_mix, norm_ffn, norm_final, w_in, w_out, rwkv_mu, rwkv_w_up, rwkv_w_bias, rwkv_a_up, rwkv_a_bias, rwkv_g_up, rwkv_k_k, rwkv_k_a, rwkv_r_k, rwkv_ln_w, rwkv_ln_b, gla_gk_up, gla_gk_bias, gla_norm_w, peer_w_q, peer_sub_keys, peer_u, peer_v):
    b, s, d = x.shape
    x2 = x.reshape(b * s, d)
    gain_final = norm_final[None, :]
    for l in range(DEPTH):
        pr = _layer_params(l, w_in, w_out, rwkv_mu, rwkv_w_up, rwkv_w_bias, rwkv_a_up, rwkv_a_bias, rwkv_g_up,
                           rwkv_k_k, rwkv_k_a, rwkv_r_k, rwkv_ln_w, rwkv_ln_b, gla_gk_up, gla_gk_bias, gla_norm_w,
                           peer_w_q, peer_sub_keys, peer_u, peer_v)
        x2 = _layer(x2, b, s, pr, norm_mix[l][None, :], norm_ffn[l][None, :], gain_final, l == DEPTH - 1)
    return x2.reshape(b, s, d)
```

```python
import functools

import numpy as np
import jax
import jax.numpy as jnp
from jax import lax
from jax.experimental import pallas as pl
from jax.experimental.pallas import tpu as pltpu

F32 = jnp.float32
BF16 = jnp.bfloat16
HI = lax.Precision.HIGHEST

D_MODEL = 1024
DEPTH = 2
NORM_EPS = 1e-6
HEAD_NORM_EPS = 1e-5
RET_HEADS, RET_DK, RET_DV, ROPE_BASE = 4, 64, 64, 10000.0
RET_W = RET_HEADS * RET_DK
RET_CHUNK = 128
RWKV_HEADS, RWKV_HEAD = 6, 64
RWKV_W = RWKV_HEADS * RWKV_HEAD
DECAY_LORA, AAA_LORA, GATE_LORA = 64, 64, 128
RWKV_GN_EPS = 64e-5
RWKV_CHUNK = 64
RWKV_SUB = 16
RWKV_IN = 3 * RWKV_W + DECAY_LORA + AAA_LORA + GATE_LORA
GLA_HEADS, GLA_DK, GLA_DV, GLA_GATE_LORA = 4, 48, 96, 16
GLA_GATE_NORMALIZER = 16.0
GLA_DKP, GLA_DVP = 64, 128
GLA_QW = GLA_HEADS * GLA_DKP
GLA_VW = GLA_HEADS * GLA_DVP
GLA_IN = 2 * GLA_QW + GLA_VW + 128 + GLA_VW
GLA_SUB = 16
GLA_BLOCK = 256
RET_IN = 4 * RET_W
D_INP = RET_IN + RWKV_IN + GLA_IN
D_MIXP = RET_W + RWKV_W + GLA_VW
PEER_HEADS, PEER_NKEYS, PEER_QHALF, PEER_TOPK = 8, 128, 128, 16
PEER_NEXPERTS = PEER_NKEYS * PEER_NKEYS
LANES = 128
SUBLANES = 8
VMEM_LIMIT = 56 * 1024 * 1024


def _params(sem):
    return pltpu.CompilerParams(dimension_semantics=sem, vmem_limit_bytes=VMEM_LIMIT)


def _dot(a, b, prec=None):
    return jnp.dot(a, b, precision=prec, preferred_element_type=F32)


def _dot_nt(a, b, prec=None):
    return lax.dot_general(a, b, (((1,), (1,)), ((), ())), precision=prec, preferred_element_type=F32)


def _dot_tn(a, b, prec=None):
    return lax.dot_general(a, b, (((0,), (0,)), ((), ())), precision=prec, preferred_element_type=F32)


def _bf(x):
    return x.astype(BF16)


def _iota(shape, dim):
    return lax.broadcasted_iota(jnp.int32, shape, dim)


def _same_block(shape, rblk, cblk):
    r = _iota(shape, 0) >> (rblk.bit_length() - 1)
    c = _iota(shape, 1) >> (cblk.bit_length() - 1)
    return r == c


def _full(shape):
    n = len(shape)
    return pl.BlockSpec(shape, lambda *_: (0,) * n)


def _block_ones(n, blk):
    i = np.arange(n) // blk
    return jnp.asarray((i[:, None] == i[None, :]).astype(np.float32))


def _tril_ones(n):
    return jnp.asarray(np.tril(np.ones((n, n), np.float32)))


def _norm_proj_kernel(x_ref, g_ref, w_ref, zr_ref, zk_ref, zg_ref):
    x = x_ref[...]
    y = x * lax.rsqrt(jnp.mean(x * x, axis=-1, keepdims=True) + NORM_EPS) * g_ref[...]
    yb = _bf(y)
    zr_ref[...] = _dot(yb, w_ref[:, 0:RET_IN])
    zk_ref[...] = _dot(yb, w_ref[:, RET_IN:RET_IN + RWKV_IN])
    zg_ref[...] = _dot(yb, w_ref[:, RET_IN + RWKV_IN:D_INP])


def _norm_proj(x2, gain, w_in_p, tm=512):
    t = x2.shape[0]
    return pl.pallas_call(
        _norm_proj_kernel,
        grid=(t // tm,),
        in_specs=[pl.BlockSpec((tm, D_MODEL), lambda i: (i, 0)), _full((1, D_MODEL)), _full((D_MODEL, D_INP))],
        out_specs=[pl.BlockSpec((tm, RET_IN), lambda i: (i, 0)), pl.BlockSpec((tm, RWKV_IN), lambda i: (i, 0)),
                   pl.BlockSpec((tm, GLA_IN), lambda i: (i, 0))],
        out_shape=[jax.ShapeDtypeStruct((t, RET_IN), F32), jax.ShapeDtypeStruct((t, RWKV_IN), F32),
                   jax.ShapeDtypeStruct((t, GLA_IN), F32)],
        compiler_params=_params(("arbitrary",)),
        name="norm_proj",
    )(x2, gain, w_in_p)


RET_BLOCK = 512


def _ret_kernel(z_ref, cos_ref, sin_ref, dmat_ref, qdec_ref, kdec_ref, cdec_ref, ones_ref, o_ref, st_ref):
    c = RET_CHUNK
    tb = RET_BLOCK

    @pl.when(pl.program_id(1) == 0)
    def _():
        st_ref[...] = jnp.zeros_like(st_ref)

    z = z_ref[0]
    q, k, v, g = (z[:, i * RET_W:(i + 1) * RET_W] for i in range(4))
    cos, sin = cos_ref[...], sin_ref[...]
    first_half = (_iota((tb, RET_W), 1) & (RET_DK - 1)) < RET_DK // 2

    def rot(t):
        return jnp.where(first_half, -pltpu.roll(t, RET_W - RET_DK // 2, 1), pltpu.roll(t, RET_DK // 2, 1))

    q = q * cos + rot(q) * sin
    k = (k * cos + rot(k) * sin) * (RET_DK ** -0.5)
    qd = q * qdec_ref[...]
    kd = k * kdec_ref[...]
    head0 = _iota((c, LANES), 1) < RET_DK
    bd = _same_block((LANES, LANES), RET_DV, RET_DK)
    units = [(ch, p) for ch in range(tb // c) for p in range(RET_HEADS // 2)]
    tile = lambda t, u: t[u[0] * c:(u[0] + 1) * c, u[1] * LANES:(u[1] + 1) * LANES]
    kb = [_bf(tile(k, u)) for u in units]
    intra = []
    for hh in range(2):
        mh = head0 if hh == 0 else jnp.logical_not(head0)
        sc = [_dot_nt(_bf(jnp.where(mh, tile(q, u), 0.0)), kb[i]) * dmat_ref[2 * u[1] + hh] for i, u in enumerate(units)]
        intra.append([_dot(_bf(s_), _bf(jnp.where(mh, tile(v, u), 0.0))) for s_, u in zip(sc, units)])
    upd = [jnp.where(bd, _dot_tn(_bf(tile(v, u)), _bf(tile(kd, u))), 0.0) for u in units]
    qdb = [_bf(tile(qd, u)) for u in units]
    st = [st_ref[p] for p in range(RET_HEADS // 2)]
    rows = []
    for ch in range(tb // c):
        row = []
        for p in range(RET_HEADS // 2):
            i = units.index((ch, p))
            row.append(intra[0][i] + intra[1][i] + _dot_nt(qdb[i], _bf(st[p])))
            st[p] = st[p] * cdec_ref[:, p * LANES:(p + 1) * LANES] + upd[i]
        rows.append(jnp.concatenate(row, axis=1))
    for p in range(RET_HEADS // 2):
        st_ref[p] = st[p]
    o = jnp.concatenate(rows, axis=0)
    ms = _dot_exact_rhs(o * o, ones_ref[...], 2) * (1.0 / RET_DV)
    o = o * lax.rsqrt(ms + HEAD_NORM_EPS)
    o_ref[0] = o * (g * jax.nn.sigmoid(g))


def _retention_consts(s):
    c = RET_CHUNK
    reps = RET_BLOCK // c
    log_gamma = np.log1p(-np.exp(np.linspace(np.log(1.0 / 32.0), np.log(1.0 / 512.0), RET_HEADS)))
    idx = np.arange(c, dtype=np.float64)
    diff = idx[:, None] - idx[None, :]
    dmat = np.where(diff >= 0, np.exp(log_gamma[:, None, None] * np.where(diff >= 0, diff, 0.0)), 0.0)
    lg_lane = np.repeat(log_gamma, RET_DK)[None, :]
    qdec = np.tile(np.exp(lg_lane * (idx[:, None] + 1.0)), (reps, 1))
    kdec = np.tile(np.exp(lg_lane * (c - 1.0 - idx[:, None])), (reps, 1))
    cdec = np.exp(lg_lane * c)
    half = RET_DK // 2
    inv = ROPE_BASE ** (-np.arange(half, dtype=np.float64) / half)
    ang = np.arange(s, dtype=np.float64)[:, None] * inv[None, :]
    cos = np.tile(np.cos(ang), (1, 2 * RET_HEADS))
    sin = np.tile(np.sin(ang), (1, 2 * RET_HEADS))
    f = lambda a: jnp.asarray(a.astype(np.float32))
    return f(cos), f(sin), f(dmat), f(qdec), f(kdec), f(cdec)


def _retention(z_ret):
    b, s, _ = z_ret.shape
    c = RET_CHUNK
    tb = RET_BLOCK
    cos, sin, dmat, qdec, kdec, cdec = _retention_consts(s)
    return pl.pallas_call(
        _ret_kernel,
        grid=(b, s // tb),
        in_specs=[pl.BlockSpec((1, tb, RET_IN), lambda i, j: (i, j, 0)),
                  pl.BlockSpec((tb, RET_W), lambda i, j: (j, 0)), pl.BlockSpec((tb, RET_W), lambda i, j: (j, 0)),
                  _full((RET_HEADS, c, c)), _full((tb, RET_W)), _full((tb, RET_W)), _full((1, RET_W)),
                  _full((RET_W, RET_W))],
        out_specs=pl.BlockSpec((1, tb, RET_W), lambda i, j: (i, j, 0)),
        out_shape=jax.ShapeDtypeStruct((b, s, RET_W), F32),
        scratch_shapes=[pltpu.VMEM((RET_HEADS // 2, LANES, LANES), F32)],
        compiler_params=_params(("arbitrary", "arbitrary")),
        name="retention",
    )(z_ret, cos, sin, dmat, qdec, kdec, cdec, _bf(_block_ones(RET_W, RET_DV)))


RWKV_BLOCK = 512
INV_PASSES = 3


def _split(x, parts):
    out = []
    for _ in range(parts - 1):
        h = _bf(x)
        out.append(h)
        x = x - h.astype(F32)
    out.append(_bf(x))
    return out


def _dot_exact_lhs(a_bf, x, parts):
    acc = None
    for p in _split(x, parts):
        t = _dot(a_bf, p)
        acc = t if acc is None else acc + t
    return acc


def _dot_exact_rhs(x, b_bf, parts):
    acc = None
    for p in _split(x, parts):
        t = _dot(p, b_bf)
        acc = t if acc is None else acc + t
    return acc


def _dot_f32(a, b, passes):
    if passes == 1:
        return _dot(_bf(a), _bf(b))
    if passes == 3:
        ah, al = _split(a, 2)
        bh, bl = _split(b, 2)
        return _dot(ah, bh) + (_dot(ah, bl) + _dot(al, bh))
    return _dot(a, b, HI)


def _rwkv_kernel(z_ref, mu_ref, wwa_ref, wb_ref, ab_ref, gup_ref, kk_ref, ka_ref, rk_ref, lnw_ref, lnb_ref,
                 ones_ref, ltri_ref, o_ref, st_ref, carry_ref):
    c = RWKV_CHUNK
    tb = RWKV_BLOCK
    w_ = RWKV_W

    @pl.when(pl.program_id(1) == 0)
    def _():
        st_ref[...] = jnp.zeros_like(st_ref)
        carry_ref[...] = jnp.zeros_like(carry_ref)

    f = z_ref[0]
    f_prev = jnp.where(_iota((tb, RWKV_IN), 0) == 0, carry_ref[...], pltpu.roll(f, 1, 0))
    carry_ref[...] = f[tb - 1:tb, :]
    f = f + (f_prev - f) * mu_ref[...]
    r, k, v = f[:, 0:w_], f[:, w_:2 * w_], f[:, 2 * w_:3 * w_]
    wa, gd = f[:, 3 * w_:3 * w_ + LANES], f[:, 3 * w_ + LANES:RWKV_IN]
    lane = _iota((tb, LANES), 1)
    proj = _dot_f32(jnp.where(lane < DECAY_LORA, jnp.tanh(wa), wa), wwa_ref[...], 3)
    wlog = -jax.nn.softplus(-(wb_ref[...] + proj[:, 0:w_])) - 0.5
    ld = -jnp.exp(wlog)
    a = jax.nn.sigmoid(ab_ref[...] + proj[:, w_:2 * w_])
    g = _dot(_bf(jax.nn.sigmoid(gd)), gup_ref[...])
    ones = ones_ref[...]
    kk = k * kk_ref[...]
    kk = kk * lax.rsqrt(jnp.maximum(_dot_exact_rhs(kk * kk, ones, 2), 1e-24))
    k = k * (1.0 + (a - 1.0) * ka_ref[...])
    cum = _dot_exact_lhs(ltri_ref[...], ld, 3)
    e_neg = jnp.exp(-cum)
    b = kk * a
    a_t_all = -kk * jnp.exp(cum - ld)
    b_t_all, k_t_all, r_t_all = b * e_neg, k * e_neg, r * jnp.exp(cum)

    n2 = 2 * c
    ri, ci = _iota((n2, n2), 0), _iota((n2, n2), 1)
    same_head = _same_block((n2, n2), c, c)
    strict = jnp.logical_and(same_head, ri > ci)
    incl = jnp.logical_and(same_head, ri >= ci)
    sub = _same_block((n2, n2), RWKV_SUB, RWKV_SUB)
    eye = (ri == ci).astype(F32)
    bd = _same_block((LANES, LANES), RWKV_HEAD, RWKV_HEAD)
    head0 = _iota((c, LANES), 1) < RWKV_HEAD
    mm = functools.partial(_dot_f32, passes=INV_PASSES)
    mm1 = functools.partial(_dot_f32, passes=1)

    def stack(x):
        return jnp.concatenate([jnp.where(head0, x, 0.0), jnp.where(head0, 0.0, x)], axis=0)

    def fold(x):
        return x[:c] + x[c:]

    units = [(ch, p) for ch in range(tb // c) for p in range(RWKV_HEADS // 2)]
    each = lambda fn, *lists: [fn(*args) for args in zip(*lists)]
    rows_of = lambda ch: slice(ch * c, (ch + 1) * c)
    lanes_of = lambda p: slice(p * LANES, (p + 1) * LANES)
    tile = lambda t: [stack(t[rows_of(ch), lanes_of(p)]) for ch, p in units]
    a_s, r_s, b_s, k_s, v_s = tile(a_t_all), tile(r_t_all), tile(b_t_all), tile(k_t_all), tile(v)
    cl = [cum[(ch + 1) * c - 1:(ch + 1) * c, lanes_of(p)] for ch, p in units]
    e_rem = [jnp.exp(cl_u - cum[rows_of(ch), lanes_of(p)]) for cl_u, (ch, p) in zip(cl, units)]
    b_h = [b[rows_of(ch), lanes_of(p)] * e for e, (ch, p) in zip(e_rem, units)]
    k_h = [k[rows_of(ch), lanes_of(p)] * e for e, (ch, p) in zip(e_rem, units)]
    v_p = [v[rows_of(ch), lanes_of(p)] for ch, p in units]
    big = each(lambda x, y, z, w: _dot_nt(_bf(jnp.concatenate([x, y], axis=0)), _bf(jnp.concatenate([z, w], axis=0))),
               a_s, r_s, b_s, k_s)
    a_ab = [jnp.where(strict, t[:n2, :n2], 0.0) for t in big]
    a_ak = [jnp.where(strict, t[:n2, n2:], 0.0) for t in big]
    a_rb = [jnp.where(incl, t[n2:, :n2], 0.0) for t in big]
    a_rk = [jnp.where(incl, t[n2:, n2:], 0.0) for t in big]
    d1 = [jnp.where(sub, t, 0.0) for t in a_ab]
    lo = each(lambda x, y: x - y, a_ab, d1)
    d2 = each(mm, d1, d1)
    t_d = each(lambda x, y: mm1(eye + x, eye + y), d1, d2)
    d4 = each(mm, d2, d2)
    t_d = each(lambda x, y: mm1(x, eye + y), t_d, d4)
    d8 = each(mm, d4, d4)
    t_d = each(lambda x, y: mm1(x, eye + y), t_d, d8)
    e1 = each(mm1, t_d, lo)
    e2 = each(mm1, e1, e1)
    t_inv = each(lambda x, y: mm1(eye + x, eye + y), e1, e2)
    t_inv = each(mm1, t_inv, t_d)
    akv = each(lambda x, y: _dot(_bf(x), _bf(y)), a_ak, v_s)
    wy = each(lambda t, x, y: mm1(t, jnp.concatenate([x, y], axis=1)), t_inv, a_s, akv)
    qo = each(lambda x, y: _dot(_bf(x), _bf(y)), a_rb, wy)
    rkv = each(lambda x, y: _dot(_bf(x), _bf(y)), a_rk, v_s)
    w_p = [fold(t[:, :LANES]) for t in wy]
    y_p = [fold(t[:, LANES:]) for t in wy]
    q_p = each(lambda x, y: fold(x + y[:, :LANES]), r_s, qo)
    ol_p = each(lambda x, y: fold(x[:, LANES:] + y), qo, rkv)
    gmat = each(lambda x, y: jnp.where(bd, _dot_tn(_bf(x), _bf(y)), 0.0), w_p, b_h)
    nmat = each(lambda y, vv, bb, kk_: jnp.where(bd, _dot_tn(_bf(jnp.concatenate([y, vv], axis=0)),
                                                             _bf(jnp.concatenate([bb, kk_], axis=0))), 0.0),
                y_p, v_p, b_h, k_h)
    pre = {u: (q_p[i], ol_p[i], gmat[i], nmat[i], jnp.exp(cl[i])) for i, u in enumerate(units)}

    outs = [[None] * (RWKV_HEADS // 2) for _ in range(tb // c)]
    st = [st_ref[p] for p in range(RWKV_HEADS // 2)]
    for ch in range(tb // c):
        for p in range(RWKV_HEADS // 2):
            q_u, ol_u, g_u, n_u, pc_u = pre[ch, p]
            stb = _bf(st[p])
            outs[ch][p] = _dot_nt(_bf(q_u), stb) + ol_u
            st[p] = st[p] * pc_u + _dot(stb, _bf(g_u)) + n_u
    for p in range(RWKV_HEADS // 2):
        st_ref[p] = st[p]
    o = jnp.concatenate([jnp.concatenate(row, axis=1) for row in outs], axis=0)
    inv_n = 1.0 / RWKV_HEAD
    d = o - _dot_exact_rhs(o, ones, 2) * inv_n
    var = _dot_exact_rhs(d * d, ones, 2) * inv_n
    o = d * lax.rsqrt(var + RWKV_GN_EPS) * lnw_ref[...] + lnb_ref[...]
    o = o + _dot_exact_rhs(r * k * rk_ref[...], ones, 2) * v
    o_ref[0] = o * g


def _rwkv(z_rwkv, pr):
    b, s, _ = z_rwkv.shape
    tb = RWKV_BLOCK
    row = lambda n: _full((1, n))
    ltri = np.kron(np.eye(tb // RWKV_CHUNK, dtype=np.float32), np.tril(np.ones((RWKV_CHUNK, RWKV_CHUNK), np.float32)))
    return pl.pallas_call(
        _rwkv_kernel,
        grid=(b, s // tb),
        in_specs=[pl.BlockSpec((1, tb, RWKV_IN), lambda i, j: (i, j, 0)), row(RWKV_IN), _full((LANES, 2 * RWKV_W)),
                  row(RWKV_W), row(RWKV_W), _full((GATE_LORA, RWKV_W)), row(RWKV_W), row(RWKV_W), row(RWKV_W),
                  row(RWKV_W), row(RWKV_W), _full((RWKV_W, RWKV_W)), _full((tb, tb))],
        out_specs=pl.BlockSpec((1, tb, RWKV_W), lambda i, j: (i, j, 0)),
        out_shape=jax.ShapeDtypeStruct((b, s, RWKV_W), F32),
        scratch_shapes=[pltpu.VMEM((RWKV_HEADS // 2, LANES, LANES), F32), pltpu.VMEM((1, RWKV_IN), F32)],
        compiler_params=_params(("arbitrary", "arbitrary")),
        name="rwkv7",
    )(z_rwkv, pr["mu"], pr["wwa"], pr["w_bias"], pr["a_bias"], pr["g_up"], pr["k_k"], pr["k_a"], pr["r_k"],
      pr["ln_w"], pr["ln_b"], _bf(_block_ones(RWKV_W, RWKV_HEAD)), jnp.asarray(ltri, dtype=BF16))


def _gla_kernel(z_ref, gkup_ref, gkb_ref, nw_ref, ind_ref, ones_ref, ltri_ref, o_ref, st_ref, x_ref):
    n = GLA_SUB
    tb = GLA_BLOCK
    nsub = tb // n

    @pl.when(pl.program_id(1) == 0)
    def _():
        st_ref[...] = jnp.zeros_like(st_ref)

    z = z_ref[0]
    q_all = z[:, 0:GLA_QW] * (GLA_DK ** -0.5)
    k_all = z[:, GLA_QW:2 * GLA_QW]
    v_all = z[:, 2 * GLA_QW:2 * GLA_QW + GLA_VW]
    gkd = z[:, 2 * GLA_QW + GLA_VW:2 * GLA_QW + GLA_VW + LANES]
    g = z[:, 2 * GLA_QW + GLA_VW + LANES:GLA_IN]
    gk = jax.nn.log_sigmoid(_dot_f32(gkd, gkup_ref[...], 3) + gkb_ref[...]) * (1.0 / GLA_GATE_NORMALIZER)
    bc_all = _dot_exact_lhs(ltri_ref[...], gk, 3)
    bd = _same_block((GLA_VW, GLA_QW), GLA_DVP, GLA_DKP)
    rowi = _iota((n, GLA_QW), 0)
    subs = range(nsub)
    rows = [slice(sc * n, (sc + 1) * n) for sc in subs]
    q = [q_all[r] for r in rows]
    k = [k_all[r] for r in rows]
    v = [v_all[r] for r in rows]
    bc = [bc_all[r] for r in rows]
    bl = [t[n - 1:n, :] for t in bc]
    for sc in subs:
        for j in range(n):
            xj = q[sc] * jnp.exp(jnp.minimum(bc[sc] - bc[sc][j:j + 1, :], 0.0)) * k[sc][j:j + 1, :]
            x_ref[(sc * n + j) * n:(sc * n + j + 1) * n, :] = jnp.where(rowi >= j, xj, 0.0)
    e = _dot(_bf(x_ref[...]), ind_ref[...])
    o_intra = []
    for sc in subs:
        acc = None
        for j in range(n):
            t = e[(sc * n + j) * n:(sc * n + j + 1) * n, :] * v[sc][j:j + 1, :]
            acc = t if acc is None else acc + t
        o_intra.append(acc)
    upd = [jnp.where(bd, _dot_tn(_bf(v[sc]), _bf(k[sc] * jnp.exp(bl[sc] - bc[sc]))), 0.0) for sc in subs]
    qe = [_bf(q[sc] * jnp.exp(bc[sc])) for sc in subs]
    st = st_ref[...]
    outs = []
    for sc in subs:
        outs.append(_dot_nt(qe[sc], _bf(st)) + o_intra[sc])
        st = st * jnp.exp(bl[sc]) + upd[sc]
    st_ref[...] = st
    o = jnp.concatenate(outs, axis=0)
    ms = _dot_exact_rhs(o * o, ones_ref[...], 2) * (1.0 / GLA_DV)
    o_ref[0] = o * lax.rsqrt(ms + HEAD_NORM_EPS) * nw_ref[...] * (g * jax.nn.sigmoid(g))


def _gla(z_gla, pr):
    b, s, _ = z_gla.shape
    tb = GLA_BLOCK
    hq = np.arange(GLA_QW) // GLA_DKP
    hv = np.arange(GLA_VW) // GLA_DVP
    ind = jnp.asarray((hq[:, None] == hv[None, :]).astype(np.float32), dtype=BF16)
    ltri = np.kron(np.eye(tb // GLA_SUB, dtype=np.float32), np.tril(np.ones((GLA_SUB, GLA_SUB), np.float32)))
    return pl.pallas_call(
        _gla_kernel,
        grid=(b, s // tb),
        in_specs=[pl.BlockSpec((1, tb, GLA_IN), lambda i, j: (i, j, 0)), _full((LANES, GLA_QW)), _full((1, GLA_QW)),
                  _full((1, GLA_VW)), _full((GLA_QW, GLA_VW)), _full((GLA_VW, GLA_VW)), _full((tb, tb))],
        out_specs=pl.BlockSpec((1, tb, GLA_VW), lambda i, j: (i, j, 0)),
        out_shape=jax.ShapeDtypeStruct((b, s, GLA_VW), F32),
        scratch_shapes=[pltpu.VMEM((GLA_VW, GLA_QW), F32), pltpu.VMEM((tb * GLA_SUB, GLA_QW), F32)],
        compiler_params=_params(("arbitrary", "arbitrary")),
        name="gla",
    )(z_gla, pr["gk_up"], pr["gk_bias"], pr["norm_w"], ind, _bf(_block_ones(GLA_VW, GLA_DVP)),
      jnp.asarray(ltri, dtype=BF16))


def _out_proj_kernel(x_ref, a_ref, b_ref, c_ref, w_ref, g_ref, x1_ref, hnt_ref):
    acc = _dot(_bf(a_ref[...]), w_ref[0:RET_W, :])
    acc += _dot(_bf(b_ref[...]), w_ref[RET_W:RET_W + RWKV_W, :])
    acc += _dot(_bf(c_ref[...]), w_ref[RET_W + RWKV_W:D_MIXP, :])
    x1 = x_ref[...] + acc
    x1_ref[...] = x1
    hn = x1 * lax.rsqrt(jnp.mean(x1 * x1, axis=-1, keepdims=True) + NORM_EPS) * g_ref[...]
    hnt_ref[...] = pltpu.bitcast(_bf(hn.T), jnp.uint32)


def _out_proj(x2, o_ret, o_rwkv, o_gla, w_out_p, gain, tm=512):
    t = x2.shape[0]
    blk = lambda n: pl.BlockSpec((tm, n), lambda i: (i, 0))
    return pl.pallas_call(
        _out_proj_kernel,
        grid=(t // tm,),
        in_specs=[blk(D_MODEL), blk(RET_W), blk(RWKV_W), blk(GLA_VW), _full((D_MIXP, D_MODEL)), _full((1, D_MODEL))],
        out_specs=[blk(D_MODEL), pl.BlockSpec((D_MODEL // 2, tm), lambda i: (0, i))],
        out_shape=[jax.ShapeDtypeStruct((t, D_MODEL), F32), jax.ShapeDtypeStruct((D_MODEL // 2, t), jnp.uint32)],
        compiler_params=_params(("arbitrary",)),
        name="out_proj",
    )(x2, o_ret, o_rwkv, o_gla, w_out_p, gain)


def _staircase_pairs():
    k = PEER_TOPK
    return [(a, b) for a in range(k) for b in range(k) if (a + 1) * (b + 1) <= k]


def _tree_max(xs):
    xs = list(xs)
    while len(xs) > 1:
        nxt = [jnp.maximum(xs[2 * i], xs[2 * i + 1]) for i in range(len(xs) // 2)]
        if len(xs) % 2:
            nxt.append(xs[-1])
        xs = nxt
    return xs[0]


def _route_kernel(hnt_ref, wq_ref, keys_ref, n1_ref, e1_ref, r2_ref, e2_ref, work_ref, s_ref, vals_ref, rank_ref):
    nh, nk, k_top = PEER_HEADS, PEER_NKEYS, PEER_TOPK
    qt = _bf(_dot(wq_ref[...], pltpu.bitcast(hnt_ref[...], BF16)))
    for p in range(2):
        for h in range(nh):
            r0 = (p * nh + h) * PEER_QHALF
            s = _dot(keys_ref[p, h], qt[r0:r0 + PEER_QHALF, :])
            work_ref[p, h] = s
            s_ref[p, h] = s
    rank_ref[...] = jnp.full(rank_ref.shape, float(k_top), F32)
    groups = nk // SUBLANES

    def extract(r, carry):
        rf = jnp.asarray(r, F32)
        for p in range(2):
            for h in range(nh):
                tiles = [work_ref[p, h, i * SUBLANES:(i + 1) * SUBLANES, :] for i in range(groups)]
                m = _tree_max(tiles)
                for sh in (4, 2, 1):
                    m = jnp.maximum(m, pltpu.roll(m, sh, 0))
                vals_ref[p, r, pl.ds(h, 1), :] = m[0:1, :]
                for i in range(groups):
                    rows = slice(i * SUBLANES, (i + 1) * SUBLANES)
                    is_max = tiles[i] == m
                    work_ref[p, h, rows, :] = jnp.where(is_max, -jnp.inf, tiles[i])
                    rank_ref[p, h, rows, :] = jnp.where(is_max, rf, rank_ref[p, h, rows, :])
        return carry

    lax.fori_loop(0, k_top, extract, 0)

    v1 = [vals_ref[0, r] for r in range(k_top)]
    v2 = [vals_ref[1, r] for r in range(k_top)]
    cand = {(a, b): v1[a] + v2[b] for a, b in _staircase_pairs()}
    work = list(cand.values())
    tau = None
    for it in range(k_top):
        tau = _tree_max(work)
        if it + 1 < k_top:
            work = [jnp.where(w == tau, -jnp.inf, w) for w in work]
    top = cand[(0, 0)]
    z = None
    for c in cand.values():
        zi = jnp.where(c >= tau, jnp.exp(c - top), 0.0)
        z = zi if z is None else z + zi
    scale2 = 0.5 / z
    n_of_rank = []
    for a in range(k_top):
        cnt = None
        for b in range(k_top // (a + 1)):
            ge = (cand[(a, b)] >= tau).astype(F32)
            cnt = ge if cnt is None else cnt + ge
        n_of_rank.append(cnt)
    m1, m2 = v1[0], v2[0]
    for h in range(nh):
        r1 = rank_ref[0, h]
        n1 = jnp.zeros_like(r1)
        for a in range(k_top):
            n1 = jnp.where(r1 == float(a), n_of_rank[a][h:h + 1, :], n1)
        n1_ref[h] = n1
        e1_ref[h] = jnp.exp(s_ref[0, h] - m1[h:h + 1, :])
        r2_ref[h] = pltpu.bitcast(_bf(rank_ref[1, h]), jnp.uint32)
        e2_ref[h] = pltpu.bitcast(_bf(jnp.exp(s_ref[1, h] - m2[h:h + 1, :]) * scale2[h:h + 1, :]), jnp.uint32)


def _route(hnt, wq_t, keys, tb=256):
    t = hnt.shape[1]
    nh, nk = PEER_HEADS, PEER_NKEYS
    out = pl.BlockSpec((nh, nk, tb), lambda i: (0, 0, i))
    packed = pl.BlockSpec((nh, nk // 2, tb), lambda i: (0, 0, i))
    return pl.pallas_call(
        _route_kernel,
        grid=(t // tb,),
        in_specs=[pl.BlockSpec((D_MODEL // 2, tb), lambda i: (0, i)), _full((2 * nh * PEER_QHALF, D_MODEL)),
                  _full((2, nh, nk, PEER_QHALF))],
        out_specs=[out, out, packed, packed],
        out_shape=[jax.ShapeDtypeStruct((nh, nk, t), F32)] * 2 + [jax.ShapeDtypeStruct((nh, nk // 2, t), jnp.uint32)] * 2,
        scratch_shapes=[pltpu.VMEM((2, nh, nk, tb), F32), pltpu.VMEM((2, nh, nk, tb), F32),
                        pltpu.VMEM((2, PEER_TOPK, nh, tb), F32), pltpu.VMEM((2, nh, nk, tb), F32)],
        compiler_params=_params(("arbitrary",)),
        name="peer_route",
    )(hnt, wq_t, keys)


EXPERT_SLABS = 8
EXPERT_PAIR = 2
PIPE_LAG = 2
ROW_SPLIT = 2


def _expert_kernel(hnt_ref, u_ref, vt_ref, n1_ref, e1_ref, r2_ref, e2_ref, x1_ref, gf_ref, o_ref, acc_ref, ht0, ht1,
                   act0, act1, *, n_blocks, final_norm):
    s = pl.program_id(0)
    n_pairs = pl.num_programs(0) - PIPE_LAG
    nk = PEER_NKEYS
    tb = hnt_ref.shape[1]
    j3 = lax.rem(jnp.clip(s - PIPE_LAG, 0, n_pairs - 1), n_blocks)

    @pl.when(s == 0)
    def _():
        for r in (ht0, ht1, act0, act1):
            r[...] = jnp.zeros_like(r)

    @pl.when(j3 == 0)
    def _():
        acc_ref[...] = jnp.zeros_like(acc_ref)

    def stages(ht_w, ht_r, act_w, act_r):
        half_w = 2 * LANES
        kc = 2 * LANES
        n_kc = D_MODEL // kc
        n_ec = (EXPERT_SLABS * nk) // kc

        assert n_kc == n_ec
        rws = (EXPERT_SLABS * nk) // ROW_SPLIT
        mws = D_MODEL // ROW_SPLIT
        hid = {}
        prj = {}

        def hidden_piece(half, c, rs):
            cols = slice(half * half_w, (half + 1) * half_w)
            t = _dot(pltpu.bitcast(u_ref[rs * rws // 2:(rs + 1) * rws // 2, c * kc:(c + 1) * kc], BF16),
                     pltpu.bitcast(hnt_ref[c * kc // 2:(c + 1) * kc // 2, cols], BF16))
            hid[half, rs] = t if c == 0 else hid[half, rs] + t

        def project_piece(half, c, rs):
            cols = slice(half * half_w, (half + 1) * half_w)
            t = _dot(pltpu.bitcast(vt_ref[rs * mws // 2:(rs + 1) * mws // 2, c * kc:(c + 1) * kc], BF16),
                     act_r[c * kc:(c + 1) * kc, cols])
            prj[half, rs] = t if c == 0 else prj[half, rs] + t

        gates = {}

        def gate_part(st, g0, heads, last):
            cols = slice(st * LANES, (st + 1) * LANES)
            for h in heads:
                r2 = pltpu.bitcast(r2_ref[h, :, cols], BF16)
                e2 = pltpu.bitcast(e2_ref[h, :, cols], BF16)
                for kk in range(EXPERT_PAIR):
                    n1 = _bf(n1_ref[h, g0 + kk:g0 + kk + 1, cols])
                    e1 = _bf(e1_ref[h, g0 + kk:g0 + kk + 1, cols])
                    gh = jnp.where(r2 < n1, e2 * e1, jnp.zeros_like(e2))
                    gates[st, g0 + kk] = gh if h == 0 else gates[st, g0 + kk] + gh
            if last:
                for kk in range(EXPERT_PAIR):
                    rows = slice((g0 + kk) * nk, (g0 + kk + 1) * nk)
                    hk = ht_r[rows, cols]
                    act_w[rows, cols] = _bf(hk * (1.0 + lax.erf(hk * (2.0 ** -0.5)))) * gates[st, g0 + kk]

        hh = PEER_HEADS // 2
        vpu_work = [functools.partial(gate_part, st, g0, heads, last)
                    for st in range(tb // LANES) for g0 in range(0, EXPERT_SLABS, EXPERT_PAIR)
                    for heads, last in ((range(0, hh), False), (range(hh, PEER_HEADS), True))]
        mxu_work = [functools.partial(piece, half, c, rs)
                    for half in range(2) for c in range(n_kc) for rs in range(ROW_SPLIT)
                    for piece in (hidden_piece, project_piece)]
        assert len(mxu_work) == len(vpu_work)
        for m, v_ in zip(mxu_work, vpu_work):
            m()
            v_()
        for half in range(2):
            cols = slice(half * half_w, (half + 1) * half_w)
            for rs in range(ROW_SPLIT):
                ht_w[rs * rws:(rs + 1) * rws, cols] = hid[half, rs]
                acc_ref[rs * mws:(rs + 1) * mws, cols] += prj[half, rs]

    parity = lax.rem(s, 2)

    @pl.when(parity == 0)
    def _():
        stages(ht0, ht1, act1, act0)

    @pl.when(parity == 1)
    def _():
        stages(ht1, ht0, act0, act1)

    @pl.when(jnp.logical_and(j3 == n_blocks - 1, s >= PIPE_LAG))
    def _():
        y = acc_ref[...].T + x1_ref[...]
        if final_norm:
            y = y * lax.rsqrt(jnp.mean(y * y, axis=-1, keepdims=True) + NORM_EPS) * gf_ref[...]
        o_ref[...] = y


def _experts(hnt, u_b, vt_b, n1, e1, r2, e2, x1, gain_f, final_norm, tb=512):
    t = hnt.shape[1]
    nh, nk = PEER_HEADS, PEER_NKEYS
    eb = EXPERT_SLABS * nk
    n_blocks = PEER_NEXPERTS // eb
    n_pairs = (t // tb) * n_blocks

    def pair(lag):
        def f(s):
            p = jnp.clip(s - lag, 0, n_pairs - 1)
            return p // n_blocks, lax.rem(p, n_blocks)
        return f

    tok = lambda lag: (lambda s: pair(lag)(s)[0])
    blk = lambda lag: (lambda s: pair(lag)(s)[1])
    routed = lambda rows: pl.BlockSpec((nh, rows, tb), lambda s: (0, 0, tok(1)(s)))
    slabs = pl.BlockSpec((nh, EXPERT_SLABS, tb), lambda s: (0, blk(1)(s), tok(1)(s)))
    return pl.pallas_call(
        functools.partial(_expert_kernel, n_blocks=n_blocks, final_norm=final_norm),
        grid=(n_pairs + PIPE_LAG,),
        in_specs=[pl.BlockSpec((D_MODEL // 2, tb), lambda s: (0, tok(0)(s))),
                  pl.BlockSpec((eb // 2, D_MODEL), lambda s: (blk(0)(s), 0)),
                  pl.BlockSpec((D_MODEL // 2, eb), lambda s: (0, blk(PIPE_LAG)(s))),
                  slabs, slabs, routed(nk // 2), routed(nk // 2),
                  pl.BlockSpec((tb, D_MODEL), lambda s: (tok(PIPE_LAG)(s), 0)), _full((1, D_MODEL))],
        out_specs=pl.BlockSpec((tb, D_MODEL), lambda s: (tok(PIPE_LAG)(s), 0)),
        out_shape=jax.ShapeDtypeStruct((t, D_MODEL), F32),
        scratch_shapes=[pltpu.VMEM((D_MODEL, tb), F32), pltpu.VMEM((eb, tb), F32), pltpu.VMEM((eb, tb), F32),
                        pltpu.VMEM((eb, tb), BF16), pltpu.VMEM((eb, tb), BF16)],
        compiler_params=_params(("arbitrary",)),
        name="peer_experts",
    )(hnt, u_b, vt_b, n1, e1, r2, e2, x1, gain_f)


def _pad_heads(w, heads, d, dp):
    lead = w.shape[:-1]
    w = w.reshape(*lead, heads, d)
    w = jnp.pad(w, [(0, 0)] * len(lead) + [(0, 0), (0, dp - d)])
    return w.reshape(*lead, heads * dp)


def _layer_params(l, w_in, w_out, rwkv_mu, rwkv_w_up, rwkv_w_bias, rwkv_a_up, rwkv_a_bias, rwkv_g_up, rwkv_k_k,
                  rwkv_k_a, rwkv_r_k, rwkv_ln_w, rwkv_ln_b, gla_gk_up, gla_gk_bias, gla_norm_w, peer_w_q,
                  peer_sub_keys, peer_u, peer_v):
    wi = w_in[l]
    g0 = RET_IN + RWKV_IN
    qk, vw = GLA_HEADS * GLA_DK, GLA_HEADS * GLA_DV
    gq = _pad_heads(wi[:, g0:g0 + qk], GLA_HEADS, GLA_DK, GLA_DKP)
    gkk = _pad_heads(wi[:, g0 + qk:g0 + 2 * qk], GLA_HEADS, GLA_DK, GLA_DKP)
    gv = _pad_heads(wi[:, g0 + 2 * qk:g0 + 2 * qk + vw], GLA_HEADS, GLA_DV, GLA_DVP)
    ggk = jnp.pad(wi[:, g0 + 2 * qk + vw:g0 + 2 * qk + vw + GLA_GATE_LORA], ((0, 0), (0, LANES - GLA_GATE_LORA)))
    gg = _pad_heads(wi[:, g0 + 2 * qk + vw + GLA_GATE_LORA:], GLA_HEADS, GLA_DV, GLA_DVP)
    w_in_p = _bf(jnp.concatenate([wi[:, :g0], gq, gkk, gv, ggk, gg], axis=1))
    wo = w_out[l]
    m0 = RET_W + RWKV_W
    wo_gla = jnp.pad(wo[m0:].reshape(GLA_HEADS, GLA_DV, D_MODEL), ((0, 0), (0, GLA_DVP - GLA_DV), (0, 0)))
    w_out_p = _bf(jnp.concatenate([wo[:m0], wo_gla.reshape(GLA_VW, D_MODEL)], axis=0))
    zeros = jnp.zeros((DECAY_LORA, RWKV_W), F32)
    rw = dict(
        mu=rwkv_mu[l][None, :],
        wwa=jnp.concatenate([jnp.concatenate([rwkv_w_up[l], zeros], axis=1),
                             jnp.concatenate([zeros, rwkv_a_up[l]], axis=1)], axis=0),
        w_bias=rwkv_w_bias[l][None, :], a_bias=rwkv_a_bias[l][None, :], g_up=_bf(rwkv_g_up[l]),
        k_k=rwkv_k_k[l][None, :], k_a=rwkv_k_a[l][None, :], r_k=rwkv_r_k[l].reshape(1, RWKV_W),
        ln_w=rwkv_ln_w[l][None, :], ln_b=rwkv_ln_b[l][None, :])
    gl = dict(
        gk_up=jnp.pad(_pad_heads(gla_gk_up[l], GLA_HEADS, GLA_DK, GLA_DKP), ((0, LANES - GLA_GATE_LORA), (0, 0))),
        gk_bias=_pad_heads(gla_gk_bias[l][None, :], GLA_HEADS, GLA_DK, GLA_DKP),
        norm_w=_pad_heads(gla_norm_w[l][None, :], GLA_HEADS, GLA_DV, GLA_DVP))
    wq = peer_w_q[l].reshape(D_MODEL, PEER_HEADS, 2, PEER_QHALF)
    wq_t = _bf(jnp.transpose(wq, (2, 1, 3, 0)).reshape(2 * PEER_HEADS * PEER_QHALF, D_MODEL))
    keys = _bf(jnp.transpose(peer_sub_keys[l], (1, 0, 2, 3)))
    u_p, vt_p = _pack_experts(peer_u, peer_v, l)
    return dict(w_in=w_in_p, w_out=w_out_p, rwkv=rw, gla=gl, wq_t=wq_t, keys=keys, u=u_p, vt=vt_p)


def _pack_experts_kernel(u_ref, v_ref, up_ref, vtp_ref):
    up_ref[...] = pltpu.bitcast(_bf(u_ref[0]), jnp.uint32)
    vtp_ref[...] = pltpu.bitcast(_bf(v_ref[0].T), jnp.uint32)


def _pack_experts(u, v, l, eb=1024):
    n = u.shape[1]
    return pl.pallas_call(
        _pack_experts_kernel,
        grid=(n // eb,),
        in_specs=[pl.BlockSpec((1, eb, D_MODEL), lambda i: (l, i, 0)), pl.BlockSpec((1, eb, D_MODEL), lambda i: (l, i, 0))],
        out_specs=[pl.BlockSpec((eb // 2, D_MODEL), lambda i: (i, 0)), pl.BlockSpec((D_MODEL // 2, eb), lambda i: (0, i))],
        out_shape=[jax.ShapeDtypeStruct((n // 2, D_MODEL), jnp.uint32),
                   jax.ShapeDtypeStruct((D_MODEL // 2, n), jnp.uint32)],
        compiler_params=_params(("arbitrary",)),
        name="pack_experts",
    )(u, v)


def _layer(x2, b, s, pr, gain_mix, gain_ffn, gain_final, final_norm):
    z_ret, z_rwkv, z_gla = _norm_proj(x2, gain_mix, pr["w_in"])
    o_ret = _retention(z_ret.reshape(b, s, RET_IN)).reshape(b * s, RET_W)
    o_rwkv = _rwkv(z_rwkv.reshape(b, s, RWKV_IN), pr["rwkv"]).reshape(b * s, RWKV_W)
    o_gla = _gla(z_gla.reshape(b, s, GLA_IN), pr["gla"]).reshape(b * s, GLA_VW)
    x1, hnt = _out_proj(x2, o_ret, o_rwkv, o_gla, pr["w_out"], gain_ffn)
    n1, e1, r2, e2 = _route(hnt, pr["wq_t"], pr["keys"])
    return _experts(hnt, pr["u"], pr["vt"], n1, e1, r2, e2, x1, gain_final, final_norm)


def kernel(x, norm_mix, norm_ffn, norm_final, w_in, w_out, rwkv_mu, rwkv_w_up, rwkv_w_bias, rwkv_a_up, rwkv_a_bias, rwkv_g_up, rwkv_k_k, rwkv_k_a, rwkv_r_k, rwkv_ln_w, rwkv_ln_b, gla_gk_up, gla_gk_bias, gla_norm_w, peer_w_q, peer_sub_keys, peer_u, peer_v):
    b, s, d = x.shape
    x2 = x.reshape(b * s, d)
    gain_final = norm_final[None, :]
    for l in range(DEPTH):
        pr = _layer_params(l, w_in, w_out, rwkv_mu, rwkv_w_up, rwkv_w_bias, rwkv_a_up, rwkv_a_bias, rwkv_g_up,
                           rwkv_k_k, rwkv_k_a, rwkv_r_k, rwkv_ln_w, rwkv_ln_b, gla_gk_up, gla_gk_bias, gla_norm_w,
                           peer_w_q, peer_sub_keys, peer_u, peer_v)
        x2 = _layer(x2, b, s, pr, norm_mix[l][None, :], norm_ffn[l][None, :], gain_final, l == DEPTH - 1)
    return x2.reshape(b, s, d)
```

```python
import functools

import numpy as np
import jax
import jax.numpy as jnp
from jax import lax
from jax.experimental import pallas as pl
from jax.experimental.pallas import tpu as pltpu

F32 = jnp.float32
BF16 = jnp.bfloat16
HI = lax.Precision.HIGHEST

D_MODEL = 1024
DEPTH = 2
NORM_EPS = 1e-6
HEAD_NORM_EPS = 1e-5
RET_HEADS, RET_DK, RET_DV, ROPE_BASE = 4, 64, 64, 10000.0
RET_W = RET_HEADS * RET_DK
RET_CHUNK = 128
RWKV_HEADS, RWKV_HEAD = 6, 64
RWKV_W = RWKV_HEADS * RWKV_HEAD
DECAY_LORA, AAA_LORA, GATE_LORA = 64, 64, 128
RWKV_GN_EPS = 64e-5
RWKV_CHUNK = 64
RWKV_SUB = 16
RWKV_IN = 3 * RWKV_W + DECAY_LORA + AAA_LORA + GATE_LORA
GLA_HEADS, GLA_DK, GLA_DV, GLA_GATE_LORA = 4, 48, 96, 16
GLA_GATE_NORMALIZER = 16.0
GLA_DKP, GLA_DVP = 64, 128
GLA_QW = GLA_HEADS * GLA_DKP
GLA_VW = GLA_HEADS * GLA_DVP
GLA_IN = 2 * GLA_QW + GLA_VW + 128 + GLA_VW
GLA_SUB = 16
GLA_BLOCK = 256
RET_IN = 4 * RET_W
D_INP = RET_IN + RWKV_IN + GLA_IN
D_MIXP = RET_W + RWKV_W + GLA_VW
PEER_HEADS, PEER_NKEYS, PEER_QHALF, PEER_TOPK = 8, 128, 128, 16
PEER_NEXPERTS = PEER_NKEYS * PEER_NKEYS
LANES = 128
SUBLANES = 8
VMEM_LIMIT = 56 * 1024 * 1024


def _params(sem):
    return pltpu.CompilerParams(dimension_semantics=sem, vmem_limit_bytes=VMEM_LIMIT)


def _dot(a, b, prec=None):
    return jnp.dot(a, b, precision=prec, preferred_element_type=F32)


def _dot_nt(a, b, prec=None):
    return lax.dot_general(a, b, (((1,), (1,)), ((), ())), precision=prec, preferred_element_type=F32)


def _dot_tn(a, b, prec=None):
    return lax.dot_general(a, b, (((0,), (0,)), ((), ())), precision=prec, preferred_element_type=F32)


def _bf(x):
    return x.astype(BF16)


def _iota(shape, dim):
    return lax.broadcasted_iota(jnp.int32, shape, dim)


def _same_block(shape, rblk, cblk):
    r = _iota(shape, 0) >> (rblk.bit_length() - 1)
    c = _iota(shape, 1) >> (cblk.bit_length() - 1)
    return r == c


def _full(shape):
    n = len(shape)
    return pl.BlockSpec(shape, lambda *_: (0,) * n)


def _block_ones(n, blk):
    i = np.arange(n) // blk
    return jnp.asarray((i[:, None] == i[None, :]).astype(np.float32))


def _norm_proj_kernel(x_ref, g_ref, w_ref, zr_ref, zk_ref, zg_ref):
    x = x_ref[...]
    y = x * lax.rsqrt(jnp.mean(x * x, axis=-1, keepdims=True) + NORM_EPS) * g_ref[...]
    yb = _bf(y)
    zr_ref[...] = _dot(yb, w_ref[:, 0:RET_IN])
    zk_ref[...] = _dot(yb, w_ref[:, RET_IN:RET_IN + RWKV_IN])
    zg_ref[...] = _dot(yb, w_ref[:, RET_IN + RWKV_IN:D_INP])


def _norm_proj(x2, gain, w_in_p, tm=512):
    t = x2.shape[0]
    return pl.pallas_call(
        _norm_proj_kernel,
        grid=(t // tm,),
        in_specs=[pl.BlockSpec((tm, D_MODEL), lambda i: (i, 0)), _full((1, D_MODEL)), _full((D_MODEL, D_INP))],
        out_specs=[pl.BlockSpec((tm, RET_IN), lambda i: (i, 0)), pl.BlockSpec((tm, RWKV_IN), lambda i: (i, 0)),
                   pl.BlockSpec((tm, GLA_IN), lambda i: (i, 0))],
        out_shape=[jax.ShapeDtypeStruct((t, RET_IN), F32), jax.ShapeDtypeStruct((t, RWKV_IN), F32),
                   jax.ShapeDtypeStruct((t, GLA_IN), F32)],
        compiler_params=_params(("arbitrary",)),
        name="norm_proj",
    )(x2, gain, w_in_p)


RET_BLOCK = 512


def _ret_kernel(z_ref, cos_ref, sin_ref, dmat_ref, qdec_ref, kdec_ref, cdec_ref, ones_ref, o_ref, st_ref):
    c = RET_CHUNK
    tb = RET_BLOCK

    @pl.when(pl.program_id(1) == 0)
    def _():
        st_ref[...] = jnp.zeros_like(st_ref)

    z = z_ref[0]
    q, k, v, g = (z[:, i * RET_W:(i + 1) * RET_W] for i in range(4))
    cos, sin = cos_ref[...], sin_ref[...]
    first_half = (_iota((tb, RET_W), 1) & (RET_DK - 1)) < RET_DK // 2

    def rot(t):
        return jnp.where(first_half, -pltpu.roll(t, RET_W - RET_DK // 2, 1), pltpu.roll(t, RET_DK // 2, 1))

    q = q * cos + rot(q) * sin
    k = (k * cos + rot(k) * sin) * (RET_DK ** -0.5)
    qd = q * qdec_ref[...]
    kd = k * kdec_ref[...]
    head0 = _iota((c, LANES), 1) < RET_DK
    bd = _same_block((LANES, LANES), RET_DV, RET_DK)
    units = [(ch, p) for ch in range(tb // c) for p in range(RET_HEADS // 2)]
    tile = lambda t, u: t[u[0] * c:(u[0] + 1) * c, u[1] * LANES:(u[1] + 1) * LANES]
    kb = [_bf(tile(k, u)) for u in units]
    intra = []
    for hh in range(2):
        mh = head0 if hh == 0 else jnp.logical_not(head0)
        sc = [_dot_nt(_bf(jnp.where(mh, tile(q, u), 0.0)), kb[i]) * dmat_ref[2 * u[1] + hh] for i, u in enumerate(units)]
        intra.append([_dot(_bf(s_), _bf(jnp.where(mh, tile(v, u), 0.0))) for s_, u in zip(sc, units)])
    upd = [jnp.where(bd, _dot_tn(_bf(tile(v, u)), _bf(tile(kd, u))), 0.0) for u in units]
    qdb = [_bf(tile(qd, u)) for u in units]
    st = [st_ref[p] for p in range(RET_HEADS // 2)]
    rows = []
    for ch in range(tb // c):
        row = []
        for p in range(RET_HEADS // 2):
            i = units.index((ch, p))
            row.append(intra[0][i] + intra[1][i] + _dot_nt(qdb[i], _bf(st[p])))
            st[p] = st[p] * cdec_ref[:, p * LANES:(p + 1) * LANES] + upd[i]
        rows.append(jnp.concatenate(row, axis=1))
    for p in range(RET_HEADS // 2):
        st_ref[p] = st[p]
    o = jnp.concatenate(rows, axis=0)
    ms = _dot_exact_rhs(o * o, ones_ref[...], 2) * (1.0 / RET_DV)
    o = o * lax.rsqrt(ms + HEAD_NORM_EPS)
    o_ref[0] = o * (g * jax.nn.sigmoid(g))


def _retention_consts(s):
    c = RET_CHUNK
    reps = RET_BLOCK // c
    log_gamma = np.log1p(-np.exp(np.linspace(np.log(1.0 / 32.0), np.log(1.0 / 512.0), RET_HEADS)))
    idx = np.arange(c, dtype=np.float64)
    diff = idx[:, None] - idx[None, :]
    dmat = np.where(diff >= 0, np.exp(log_gamma[:, None, None] * np.where(diff >= 0, diff, 0.0)), 0.0)
    lg_lane = np.repeat(log_gamma, RET_DK)[None, :]
    qdec = np.tile(np.exp(lg_lane * (idx[:, None] + 1.0)), (reps, 1))
    kdec = np.tile(np.exp(lg_lane * (c - 1.0 - idx[:, None])), (reps, 1))
    cdec = np.exp(lg_lane * c)
    half = RET_DK // 2
    inv = ROPE_BASE ** (-np.arange(half, dtype=np.float64) / half)
    ang = np.arange(s, dtype=np.float64)[:, None] * inv[None, :]
    cos = np.tile(np.cos(ang), (1, 2 * RET_HEADS))
    sin = np.tile(np.sin(ang), (1, 2 * RET_HEADS))
    f = lambda a: jnp.asarray(a.astype(np.float32))
    return f(cos), f(sin), f(dmat), f(qdec), f(kdec), f(cdec)


def _retention(z_ret):
    b, s, _ = z_ret.shape
    c = RET_CHUNK
    tb = RET_BLOCK
    cos, sin, dmat, qdec, kdec, cdec = _retention_consts(s)
    return pl.pallas_call(
        _ret_kernel,
        grid=(b, s // tb),
        in_specs=[pl.BlockSpec((1, tb, RET_IN), lambda i, j: (i, j, 0)),
                  pl.BlockSpec((tb, RET_W), lambda i, j: (j, 0)), pl.BlockSpec((tb, RET_W), lambda i, j: (j, 0)),
                  _full((RET_HEADS, c, c)), _full((tb, RET_W)), _full((tb, RET_W)), _full((1, RET_W)),
                  _full((RET_W, RET_W))],
        out_specs=pl.BlockSpec((1, tb, RET_W), lambda i, j: (i, j, 0)),
        out_shape=jax.ShapeDtypeStruct((b, s, RET_W), F32),
        scratch_shapes=[pltpu.VMEM((RET_HEADS // 2, LANES, LANES), F32)],
        compiler_params=_params(("arbitrary", "arbitrary")),
        name="retention",
    )(z_ret, cos, sin, dmat, qdec, kdec, cdec, _bf(_block_ones(RET_W, RET_DV)))


RWKV_BLOCK = 512
INV_PASSES = 3


def _split(x, parts):
    out = []
    for _ in range(parts - 1):
        h = _bf(x)
        out.append(h)
        x = x - h.astype(F32)
    out.append(_bf(x))
    return out


def _dot_exact_lhs(a_bf, x, parts):
    acc = None
    for p in _split(x, parts):
        t = _dot(a_bf, p)
        acc = t if acc is None else acc + t
    return acc


def _dot_exact_rhs(x, b_bf, parts):
    acc = None
    for p in _split(x, parts):
        t = _dot(p, b_bf)
        acc = t if acc is None else acc + t
    return acc


def _dot_f32(a, b, passes):
    if passes == 1:
        return _dot(_bf(a), _bf(b))
    if passes == 3:
        ah, al = _split(a, 2)
        bh, bl = _split(b, 2)
        return _dot(ah, bh) + (_dot(ah, bl) + _dot(al, bh))
    return _dot(a, b, HI)


def _rwkv_kernel(z_ref, mu_ref, wwa_ref, wb_ref, ab_ref, gup_ref, kk_ref, ka_ref, rk_ref, lnw_ref, lnb_ref,
                 ones_ref, ltri_ref, o_ref, st_ref, carry_ref):
    c = RWKV_CHUNK
    tb = RWKV_BLOCK
    w_ = RWKV_W

    @pl.when(pl.program_id(1) == 0)
    def _():
        st_ref[...] = jnp.zeros_like(st_ref)
        carry_ref[...] = jnp.zeros_like(carry_ref)

    f = z_ref[0]
    f_prev = jnp.where(_iota((tb, RWKV_IN), 0) == 0, carry_ref[...], pltpu.roll(f, 1, 0))
    carry_ref[...] = f[tb - 1:tb, :]
    f = f + (f_prev - f) * mu_ref[...]
    r, k, v = f[:, 0:w_], f[:, w_:2 * w_], f[:, 2 * w_:3 * w_]
    wa, gd = f[:, 3 * w_:3 * w_ + LANES], f[:, 3 * w_ + LANES:RWKV_IN]
    lane = _iota((tb, LANES), 1)
    proj = _dot_f32(jnp.where(lane < DECAY_LORA, jnp.tanh(wa), wa), wwa_ref[...], 3)
    wlog = -jax.nn.softplus(-(wb_ref[...] + proj[:, 0:w_])) - 0.5
    ld = -jnp.exp(wlog)
    a = jax.nn.sigmoid(ab_ref[...] + proj[:, w_:2 * w_])
    g = _dot(_bf(jax.nn.sigmoid(gd)), gup_ref[...])
    ones = ones_ref[...]
    kk = k * kk_ref[...]
    kk = kk * lax.rsqrt(jnp.maximum(_dot_exact_rhs(kk * kk, ones, 2), 1e-24))
    k = k * (1.0 + (a - 1.0) * ka_ref[...])
    cum = _dot_exact_lhs(ltri_ref[...], ld, 3)
    e_neg = jnp.exp(-cum)
    b = kk * a
    a_t_all = -kk * jnp.exp(cum - ld)
    b_t_all, k_t_all, r_t_all = b * e_neg, k * e_neg, r * jnp.exp(cum)

    n2 = 2 * c
    ri, ci = _iota((n2, n2), 0), _iota((n2, n2), 1)
    same_head = _same_block((n2, n2), c, c)
    strict = jnp.logical_and(same_head, ri > ci)
    incl = jnp.logical_and(same_head, ri >= ci)
    sub = _same_block((n2, n2), RWKV_SUB, RWKV_SUB)
    eye = (ri == ci).astype(F32)
    bd = _same_block((LANES, LANES), RWKV_HEAD, RWKV_HEAD)
    head0 = _iota((c, LANES), 1) < RWKV_HEAD
    mm = functools.partial(_dot_f32, passes=INV_PASSES)
    mm1 = functools.partial(_dot_f32, passes=1)

    def stack(x):
        return jnp.concatenate([jnp.where(head0, x, 0.0), jnp.where(head0, 0.0, x)], axis=0)

    def fold(x):
        return x[:c] + x[c:]

    units = [(ch, p) for ch in range(tb // c) for p in range(RWKV_HEADS // 2)]
    each = lambda fn, *lists: [fn(*args) for args in zip(*lists)]
    rows_of = lambda ch: slice(ch * c, (ch + 1) * c)
    lanes_of = lambda p: slice(p * LANES, (p + 1) * LANES)
    tile = lambda t: [stack(t[rows_of(ch), lanes_of(p)]) for ch, p in units]
    a_s, r_s, b_s, k_s, v_s = tile(a_t_all), tile(r_t_all), tile(b_t_all), tile(k_t_all), tile(v)
    cl = [cum[(ch + 1) * c - 1:(ch + 1) * c, lanes_of(p)] for ch, p in units]
    e_rem = [jnp.exp(cl_u - cum[rows_of(ch), lanes_of(p)]) for cl_u, (ch, p) in zip(cl, units)]
    b_h = [b[rows_of(ch), lanes_of(p)] * e for e, (ch, p) in zip(e_rem, units)]
    k_h = [k[rows_of(ch), lanes_of(p)] * e for e, (ch, p) in zip(e_rem, units)]
    v_p = [v[rows_of(ch), lanes_of(p)] for ch, p in units]
    big = each(lambda x, y, z, w: _dot_nt(_bf(jnp.concatenate([x, y], axis=0)), _bf(jnp.concatenate([z, w], axis=0))),
               a_s, r_s, b_s, k_s)
    a_ab = [jnp.where(strict, t[:n2, :n2], 0.0) for t in big]
    a_ak = [jnp.where(strict, t[:n2, n2:], 0.0) for t in big]
    a_rb = [jnp.where(incl, t[n2:, :n2], 0.0) for t in big]
    a_rk = [jnp.where(incl, t[n2:, n2:], 0.0) for t in big]
    d1 = [jnp.where(sub, t, 0.0) for t in a_ab]
    lo = each(lambda x, y: x - y, a_ab, d1)
    d2 = each(mm, d1, d1)
    t_d = each(lambda x, y: mm1(eye + x, eye + y), d1, d2)
    d4 = each(mm, d2, d2)
    t_d = each(lambda x, y: mm1(x, eye + y), t_d, d4)
    d8 = each(mm, d4, d4)
    t_d = each(lambda x, y: mm1(x, eye + y), t_d, d8)
    e1 = each(mm1, t_d, lo)
    e2 = each(mm1, e1, e1)
    t_inv = each(lambda x, y: mm1(eye + x, eye + y), e1, e2)
    t_inv = each(mm1, t_inv, t_d)
    akv = each(lambda x, y: _dot(_bf(x), _bf(y)), a_ak, v_s)
    wy = each(lambda t, x, y: mm1(t, jnp.concatenate([x, y], axis=1)), t_inv, a_s, akv)
    qo = each(lambda x, y: _dot(_bf(x), _bf(y)), a_rb, wy)
    rkv = each(lambda x, y: _dot(_bf(x), _bf(y)), a_rk, v_s)
    w_p = [fold(t[:, :LANES]) for t in wy]
    y_p = [fold(t[:, LANES:]) for t in wy]
    q_p = each(lambda x, y: fold(x + y[:, :LANES]), r_s, qo)
    ol_p = each(lambda x, y: fold(x[:, LANES:] + y), qo, rkv)
    gmat = each(lambda x, y: jnp.where(bd, _dot_tn(_bf(x), _bf(y)), 0.0), w_p, b_h)
    nmat = each(lambda y, vv, bb, kk_: jnp.where(bd, _dot_tn(_bf(jnp.concatenate([y, vv], axis=0)),
                                                             _bf(jnp.concatenate([bb, kk_], axis=0))), 0.0),
                y_p, v_p, b_h, k_h)
    pre = {u: (q_p[i], ol_p[i], gmat[i], nmat[i], jnp.exp(cl[i])) for i, u in enumerate(units)}

    outs = [[None] * (RWKV_HEADS // 2) for _ in range(tb // c)]
    st = [st_ref[p] for p in range(RWKV_HEADS // 2)]
    for ch in range(tb // c):
        for p in range(RWKV_HEADS // 2):
            q_u, ol_u, g_u, n_u, pc_u = pre[ch, p]
            stb = _bf(st[p])
            outs[ch][p] = _dot_nt(_bf(q_u), stb) + ol_u
            st[p] = st[p] * pc_u + _dot(stb, _bf(g_u)) + n_u
    for p in range(RWKV_HEADS // 2):
        st_ref[p] = st[p]
    o = jnp.concatenate([jnp.concatenate(row, axis=1) for row in outs], axis=0)
    inv_n = 1.0 / RWKV_HEAD
    d = o - _dot_exact_rhs(o, ones, 2) * inv_n
    var = _dot_exact_rhs(d * d, ones, 2) * inv_n
    o = d * lax.rsqrt(var + RWKV_GN_EPS) * lnw_ref[...] + lnb_ref[...]
    o = o + _dot_exact_rhs(r * k * rk_ref[...], ones, 2) * v
    o_ref[0] = o * g


def _rwkv(z_rwkv, pr):
    b, s, _ = z_rwkv.shape
    tb = RWKV_BLOCK
    row = lambda n: _full((1, n))
    ltri = np.kron(np.eye(tb // RWKV_CHUNK, dtype=np.float32), np.tril(np.ones((RWKV_CHUNK, RWKV_CHUNK), np.float32)))
    return pl.pallas_call(
        _rwkv_kernel,
        grid=(b, s // tb),
        in_specs=[pl.BlockSpec((1, tb, RWKV_IN), lambda i, j: (i, j, 0)), row(RWKV_IN), _full((LANES, 2 * RWKV_W)),
                  row(RWKV_W), row(RWKV_W), _full((GATE_LORA, RWKV_W)), row(RWKV_W), row(RWKV_W), row(RWKV_W),
                  row(RWKV_W), row(RWKV_W), _full((RWKV_W, RWKV_W)), _full((tb, tb))],
        out_specs=pl.BlockSpec((1, tb, RWKV_W), lambda i, j: (i, j, 0)),
        out_shape=jax.ShapeDtypeStruct((b, s, RWKV_W), F32),
        scratch_shapes=[pltpu.VMEM((RWKV_HEADS // 2, LANES, LANES), F32), pltpu.VMEM((1, RWKV_IN), F32)],
        compiler_params=_params(("arbitrary", "arbitrary")),
        name="rwkv7",
    )(z_rwkv, pr["mu"], pr["wwa"], pr["w_bias"], pr["a_bias"], pr["g_up"], pr["k_k"], pr["k_a"], pr["r_k"],
      pr["ln_w"], pr["ln_b"], _bf(_block_ones(RWKV_W, RWKV_HEAD)), jnp.asarray(ltri, dtype=BF16))


def _gla_kernel(z_ref, gkup_ref, gkb_ref, nw_ref, ind_ref, ones_ref, ltri_ref, o_ref, st_ref, x_ref):
    n = GLA_SUB
    tb = GLA_BLOCK
    nsub = tb // n

    @pl.when(pl.program_id(1) == 0)
    def _():
        st_ref[...] = jnp.zeros_like(st_ref)

    z = z_ref[0]
    q_all = z[:, 0:GLA_QW] * (GLA_DK ** -0.5)
    k_all = z[:, GLA_QW:2 * GLA_QW]
    v_all = z[:, 2 * GLA_QW:2 * GLA_QW + GLA_VW]
    gkd = z[:, 2 * GLA_QW + GLA_VW:2 * GLA_QW + GLA_VW + LANES]
    g = z[:, 2 * GLA_QW + GLA_VW + LANES:GLA_IN]
    gk = jax.nn.log_sigmoid(_dot_f32(gkd, gkup_ref[...], 3) + gkb_ref[...]) * (1.0 / GLA_GATE_NORMALIZER)
    bc_all = _dot_exact_lhs(ltri_ref[...], gk, 3)
    bd = _same_block((GLA_VW, GLA_QW), GLA_DVP, GLA_DKP)
    rowi = _iota((n, GLA_QW), 0)
    subs = range(nsub)
    rows = [slice(sc * n, (sc + 1) * n) for sc in subs]
    q = [q_all[r] for r in rows]
    k = [k_all[r] for r in rows]
    v = [v_all[r] for r in rows]
    bc = [bc_all[r] for r in rows]
    bl = [t[n - 1:n, :] for t in bc]
    for sc in subs:
        for j in range(n):
            xj = q[sc] * jnp.exp(jnp.minimum(bc[sc] - bc[sc][j:j + 1, :], 0.0)) * k[sc][j:j + 1, :]
            x_ref[(sc * n + j) * n:(sc * n + j + 1) * n, :] = jnp.where(rowi >= j, xj, 0.0)
    e = _dot(_bf(x_ref[...]), ind_ref[...])
    o_intra = []
    for sc in subs:
        acc = None
        for j in range(n):
            t = e[(sc * n + j) * n:(sc * n + j + 1) * n, :] * v[sc][j:j + 1, :]
            acc = t if acc is None else acc + t
        o_intra.append(acc)
    upd = [jnp.where(bd, _dot_tn(_bf(v[sc]), _bf(k[sc] * jnp.exp(bl[sc] - bc[sc]))), 0.0) for sc in subs]
    qe = [_bf(q[sc] * jnp.exp(bc[sc])) for sc in subs]
    st = st_ref[...]
    outs = []
    for sc in subs:
        outs.append(_dot_nt(qe[sc], _bf(st)) + o_intra[sc])
        st = st * jnp.exp(bl[sc]) + upd[sc]
    st_ref[...] = st
    o = jnp.concatenate(outs, axis=0)
    ms = _dot_exact_rhs(o * o, ones_ref[...], 2) * (1.0 / GLA_DV)
    o_ref[0] = o * lax.rsqrt(ms + HEAD_NORM_EPS) * nw_ref[...] * (g * jax.nn.sigmoid(g))


def _gla(z_gla, pr):
    b, s, _ = z_gla.shape
    tb = GLA_BLOCK
    hq = np.arange(GLA_QW) // GLA_DKP
    hv = np.arange(GLA_VW) // GLA_DVP
    ind = jnp.asarray((hq[:, None] == hv[None, :]).astype(np.float32), dtype=BF16)
    ltri = np.kron(np.eye(tb // GLA_SUB, dtype=np.float32), np.tril(np.ones((GLA_SUB, GLA_SUB), np.float32)))
    return pl.pallas_call(
        _gla_kernel,
        grid=(b, s // tb),
        in_specs=[pl.BlockSpec((1, tb, GLA_IN), lambda i, j: (i, j, 0)), _full((LANES, GLA_QW)), _full((1, GLA_QW)),
                  _full((1, GLA_VW)), _full((GLA_QW, GLA_VW)), _full((GLA_VW, GLA_VW)), _full((tb, tb))],
        out_specs=pl.BlockSpec((1, tb, GLA_VW), lambda i, j: (i, j, 0)),
        out_shape=jax.ShapeDtypeStruct((b, s, GLA_VW), F32),
        scratch_shapes=[pltpu.VMEM((GLA_VW, GLA_QW), F32), pltpu.VMEM((tb * GLA_SUB, GLA_QW), F32)],
        compiler_params=_params(("arbitrary", "arbitrary")),
        name="gla",
    )(z_gla, pr["gk_up"], pr["gk_bias"], pr["norm_w"], ind, _bf(_block_ones(GLA_VW, GLA_DVP)),
      jnp.asarray(ltri, dtype=BF16))


def _out_proj_kernel(x_ref, a_ref, b_ref, c_ref, w_ref, g_ref, x1_ref, hnt_ref):
    acc = _dot(_bf(a_ref[...]), w_ref[0:RET_W, :])
    acc += _dot(_bf(b_ref[...]), w_ref[RET_W:RET_W + RWKV_W, :])
    acc += _dot(_bf(c_ref[...]), w_ref[RET_W + RWKV_W:D_MIXP, :])
    x1 = x_ref[...] + acc
    x1_ref[...] = x1
    hn = x1 * lax.rsqrt(jnp.mean(x1 * x1, axis=-1, keepdims=True) + NORM_EPS) * g_ref[...]
    hnt_ref[...] = pltpu.bitcast(_bf(hn.T), jnp.uint32)


def _out_proj(x2, o_ret, o_rwkv, o_gla, w_out_p, gain, tm=512):
    t = x2.shape[0]
    blk = lambda n: pl.BlockSpec((tm, n), lambda i: (i, 0))
    return pl.pallas_call(
        _out_proj_kernel,
        grid=(t // tm,),
        in_specs=[blk(D_MODEL), blk(RET_W), blk(RWKV_W), blk(GLA_VW), _full((D_MIXP, D_MODEL)), _full((1, D_MODEL))],
        out_specs=[blk(D_MODEL), pl.BlockSpec((D_MODEL // 2, tm), lambda i: (0, i))],
        out_shape=[jax.ShapeDtypeStruct((t, D_MODEL), F32), jax.ShapeDtypeStruct((D_MODEL // 2, t), jnp.uint32)],
        compiler_params=_params(("arbitrary",)),
        name="out_proj",
    )(x2, o_ret, o_rwkv, o_gla, w_out_p, gain)


def _staircase_pairs():
    k = PEER_TOPK
    return [(a, b) for a in range(k) for b in range(k) if (a + 1) * (b + 1) <= k]


def _tree_max(xs):
    xs = list(xs)
    while len(xs) > 1:
        nxt = [jnp.maximum(xs[2 * i], xs[2 * i + 1]) for i in range(len(xs) // 2)]
        if len(xs) % 2:
            nxt.append(xs[-1])
        xs = nxt
    return xs[0]


def _route_kernel(hnt_ref, wq_ref, keys_ref, n1_ref, e1_ref, r2_ref, e2_ref, work_ref, s_ref, vals_ref, rank_ref):
    nh, nk, k_top = PEER_HEADS, PEER_NKEYS, PEER_TOPK
    qt = _bf(_dot(wq_ref[...], pltpu.bitcast(hnt_ref[...], BF16)))
    for p in range(2):
        for h in range(nh):
            r0 = (p * nh + h) * PEER_QHALF
            s = _dot(keys_ref[p, h], qt[r0:r0 + PEER_QHALF, :])
            work_ref[p, h] = s
            s_ref[p, h] = s
    rank_ref[...] = jnp.full(rank_ref.shape, float(k_top), F32)
    groups = nk // SUBLANES

    def extract(r, carry):
        rf = jnp.asarray(r, F32)
        for p in range(2):
            for h in range(nh):
                tiles = [work_ref[p, h, i * SUBLANES:(i + 1) * SUBLANES, :] for i in range(groups)]
                m = _tree_max(tiles)
                for sh in (4, 2, 1):
                    m = jnp.maximum(m, pltpu.roll(m, sh, 0))
                vals_ref[p, r, pl.ds(h, 1), :] = m[0:1, :]
                for i in range(groups):
                    rows = slice(i * SUBLANES, (i + 1) * SUBLANES)
                    is_max = tiles[i] == m
                    work_ref[p, h, rows, :] = jnp.where(is_max, -jnp.inf, tiles[i])
                    rank_ref[p, h, rows, :] = jnp.where(is_max, rf, rank_ref[p, h, rows, :])
        return carry

    lax.fori_loop(0, k_top, extract, 0)

    v1 = [vals_ref[0, r] for r in range(k_top)]
    v2 = [vals_ref[1, r] for r in range(k_top)]
    cand = {(a, b): v1[a] + v2[b] for a, b in _staircase_pairs()}
    work = list(cand.values())
    tau = None
    for it in range(k_top):
        tau = _tree_max(work)
        if it + 1 < k_top:
            work = [jnp.where(w == tau, -jnp.inf, w) for w in work]
    top = cand[(0, 0)]
    z = None
    for c in cand.values():
        zi = jnp.where(c >= tau, jnp.exp(c - top), 0.0)
        z = zi if z is None else z + zi
    scale2 = 0.5 / z
    n_of_rank = []
    for a in range(k_top):
        cnt = None
        for b in range(k_top // (a + 1)):
            ge = (cand[(a, b)] >= tau).astype(F32)
            cnt = ge if cnt is None else cnt + ge
        n_of_rank.append(cnt)
    m1, m2 = v1[0], v2[0]
    for h in range(nh):
        r1 = _bf(rank_ref[0, h])
        n1 = jnp.zeros_like(r1)
        for a in range(k_top):
            n1 = jnp.where(r1 == float(a), _bf(n_of_rank[a][h:h + 1, :]), n1)
        n1_ref[h] = n1.astype(F32)
        e1_ref[h] = jnp.exp(s_ref[0, h] - m1[h:h + 1, :])
        r2_ref[h] = pltpu.bitcast(_bf(rank_ref[1, h]), jnp.uint32)
        e2_ref[h] = pltpu.bitcast(_bf(jnp.exp(s_ref[1, h] - m2[h:h + 1, :]) * scale2[h:h + 1, :]), jnp.uint32)


def _route(hnt, wq_t, keys, tb=256):
    t = hnt.shape[1]
    nh, nk = PEER_HEADS, PEER_NKEYS
    out = pl.BlockSpec((nh, nk, tb), lambda i: (0, 0, i))
    packed = pl.BlockSpec((nh, nk // 2, tb), lambda i: (0, 0, i))
    return pl.pallas_call(
        _route_kernel,
        grid=(t // tb,),
        in_specs=[pl.BlockSpec((D_MODEL // 2, tb), lambda i: (0, i)), _full((2 * nh * PEER_QHALF, D_MODEL)),
                  _full((2, nh, nk, PEER_QHALF))],
        out_specs=[out, out, packed, packed],
        out_shape=[jax.ShapeDtypeStruct((nh, nk, t), F32)] * 2 + [jax.ShapeDtypeStruct((nh, nk // 2, t), jnp.uint32)] * 2,
        scratch_shapes=[pltpu.VMEM((2, nh, nk, tb), F32), pltpu.VMEM((2, nh, nk, tb), F32),
                        pltpu.VMEM((2, PEER_TOPK, nh, tb), F32), pltpu.VMEM((2, nh, nk, tb), F32)],
        compiler_params=_params(("arbitrary",)),
        name="peer_route",
    )(hnt, wq_t, keys)


EXPERT_SLABS = 8
EXPERT_PAIR = 2
PIPE_LAG = 2
ROW_SPLIT = 2


def _expert_kernel(hnt_ref, u_ref, vt_ref, n1_ref, e1_ref, r2_ref, e2_ref, x1_ref, gf_ref, o_ref, acc_ref, ht0, ht1,
                   act0, act1, *, n_blocks, final_norm):
    s = pl.program_id(0)
    n_pairs = pl.num_programs(0) - PIPE_LAG
    nk = PEER_NKEYS
    tb = hnt_ref.shape[1]
    j3 = lax.rem(jnp.clip(s - PIPE_LAG, 0, n_pairs - 1), n_blocks)

    @pl.when(s == 0)
    def _():
        for r in (ht0, ht1, act0, act1):
            r[...] = jnp.zeros_like(r)

    @pl.when(j3 == 0)
    def _():
        acc_ref[...] = jnp.zeros_like(acc_ref)

    def stages(ht_w, ht_r, act_w, act_r):
        half_w = 2 * LANES
        kc = 2 * LANES
        n_kc = D_MODEL // kc
        n_ec = (EXPERT_SLABS * nk) // kc

        assert n_kc == n_ec
        rws = (EXPERT_SLABS * nk) // ROW_SPLIT
        mws = D_MODEL // ROW_SPLIT
        hid = {}
        prj = {}

        def hidden_piece(half, c, rs):
            cols = slice(half * half_w, (half + 1) * half_w)
            t = _dot(pltpu.bitcast(u_ref[rs * rws // 2:(rs + 1) * rws // 2, c * kc:(c + 1) * kc], BF16),
                     pltpu.bitcast(hnt_ref[c * kc // 2:(c + 1) * kc // 2, cols], BF16))
            hid[half, rs] = t if c == 0 else hid[half, rs] + t

        def project_piece(half, c, rs):
            cols = slice(half * half_w, (half + 1) * half_w)
            t = _dot(pltpu.bitcast(vt_ref[rs * mws // 2:(rs + 1) * mws // 2, c * kc:(c + 1) * kc], BF16),
                     act_r[c * kc:(c + 1) * kc, cols])
            prj[half, rs] = t if c == 0 else prj[half, rs] + t

        gates = {}

        def gate_part(st, g0, heads, last):
            cols = slice(st * LANES, (st + 1) * LANES)
            for h in heads:
                r2 = pltpu.bitcast(r2_ref[h, :, cols], BF16)
                e2 = pltpu.bitcast(e2_ref[h, :, cols], BF16)
                for kk in range(EXPERT_PAIR):
                    n1 = _bf(n1_ref[h, g0 + kk:g0 + kk + 1, cols])
                    e1 = _bf(e1_ref[h, g0 + kk:g0 + kk + 1, cols])
                    gh = jnp.where(r2 < n1, e2 * e1, jnp.zeros_like(e2))
                    gates[st, g0 + kk] = gh if h == 0 else gates[st, g0 + kk] + gh
            if last:
                for kk in range(EXPERT_PAIR):
                    rows = slice((g0 + kk) * nk, (g0 + kk + 1) * nk)
                    hk = ht_r[rows, cols]
                    act_w[rows, cols] = _bf(hk * (1.0 + lax.erf(hk * (2.0 ** -0.5)))) * gates[st, g0 + kk]

        hh = PEER_HEADS // 2
        vpu_work = [functools.partial(gate_part, st, g0, heads, last)
                    for st in range(tb // LANES) for g0 in range(0, EXPERT_SLABS, EXPERT_PAIR)
                    for heads, last in ((range(0, hh), False), (range(hh, PEER_HEADS), True))]
        mxu_work = [functools.partial(piece, half, c, rs)
                    for half in range(2) for c in range(n_kc) for rs in range(ROW_SPLIT)
                    for piece in (hidden_piece, project_piece)]
        assert len(mxu_work) == len(vpu_work)
        for m, v_ in zip(mxu_work, vpu_work):
            m()
            v_()
        for half in range(2):
            cols = slice(half * half_w, (half + 1) * half_w)
            for rs in range(ROW_SPLIT):
                ht_w[rs * rws:(rs + 1) * rws, cols] = hid[half, rs]
                acc_ref[rs * mws:(rs + 1) * mws, cols] += prj[half, rs]

    parity = lax.rem(s, 2)

    @pl.when(parity == 0)
    def _():
        stages(ht0, ht1, act1, act0)

    @pl.when(parity == 1)
    def _():
        stages(ht1, ht0, act0, act1)

    @pl.when(jnp.logical_and(j3 == n_blocks - 1, s >= PIPE_LAG))
    def _():
        y = acc_ref[...].T + x1_ref[...]
        if final_norm:
            y = y * lax.rsqrt(jnp.mean(y * y, axis=-1, keepdims=True) + NORM_EPS) * gf_ref[...]
        o_ref[...] = y


def _experts(hnt, u_b, vt_b, n1, e1, r2, e2, x1, gain_f, final_norm, tb=512):
    t = hnt.shape[1]
    nh, nk = PEER_HEADS, PEER_NKEYS
    eb = EXPERT_SLABS * nk
    n_blocks = PEER_NEXPERTS // eb
    n_pairs = (t // tb) * n_blocks

    def pair(lag):
        def f(s):
            p = jnp.clip(s - lag, 0, n_pairs - 1)
            return p // n_blocks, lax.rem(p, n_blocks)
        return f

    tok = lambda lag: (lambda s: pair(lag)(s)[0])
    blk = lambda lag: (lambda s: pair(lag)(s)[1])
    routed = lambda rows: pl.BlockSpec((nh, rows, tb), lambda s: (0, 0, tok(1)(s)))
    slabs = pl.BlockSpec((nh, EXPERT_SLABS, tb), lambda s: (0, blk(1)(s), tok(1)(s)))
    return pl.pallas_call(
        functools.partial(_expert_kernel, n_blocks=n_blocks, final_norm=final_norm),
        grid=(n_pairs + PIPE_LAG,),
        in_specs=[pl.BlockSpec((D_MODEL // 2, tb), lambda s: (0, tok(0)(s))),
                  pl.BlockSpec((eb // 2, D_MODEL), lambda s: (blk(0)(s), 0)),
                  pl.BlockSpec((D_MODEL // 2, eb), lambda s: (0, blk(PIPE_LAG)(s))),
                  slabs, slabs, routed(nk // 2), routed(nk // 2),
                  pl.BlockSpec((tb, D_MODEL), lambda s: (tok(PIPE_LAG)(s), 0)), _full((1, D_MODEL))],
        out_specs=pl.BlockSpec((tb, D_MODEL), lambda s: (tok(PIPE_LAG)(s), 0)),
        out_shape=jax.ShapeDtypeStruct((t, D_MODEL), F32),
        scratch_shapes=[pltpu.VMEM((D_MODEL, tb), F32), pltpu.VMEM((eb, tb), F32), pltpu.VMEM((eb, tb), F32),
                        pltpu.VMEM((eb, tb), BF16), pltpu.VMEM((eb, tb), BF16)],
        compiler_params=_params(("arbitrary",)),
        name="peer_experts",
    )(hnt, u_b, vt_b, n1, e1, r2, e2, x1, gain_f)


def _pad_heads(w, heads, d, dp):
    lead = w.shape[:-1]
    w = w.reshape(*lead, heads, d)
    w = jnp.pad(w, [(0, 0)] * len(lead) + [(0, 0), (0, dp - d)])
    return w.reshape(*lead, heads * dp)


def _layer_params(l, w_in, w_out, rwkv_mu, rwkv_w_up, rwkv_w_bias, rwkv_a_up, rwkv_a_bias, rwkv_g_up, rwkv_k_k,
                  rwkv_k_a, rwkv_r_k, rwkv_ln_w, rwkv_ln_b, gla_gk_up, gla_gk_bias, gla_norm_w, peer_w_q,
                  peer_sub_keys, peer_u, peer_v):
    wi = w_in[l]
    g0 = RET_IN + RWKV_IN
    qk, vw = GLA_HEADS * GLA_DK, GLA_HEADS * GLA_DV
    gq = _pad_heads(wi[:, g0:g0 + qk], GLA_HEADS, GLA_DK, GLA_DKP)
    gkk = _pad_heads(wi[:, g0 + qk:g0 + 2 * qk], GLA_HEADS, GLA_DK, GLA_DKP)
    gv = _pad_heads(wi[:, g0 + 2 * qk:g0 + 2 * qk + vw], GLA_HEADS, GLA_DV, GLA_DVP)
    ggk = jnp.pad(wi[:, g0 + 2 * qk + vw:g0 + 2 * qk + vw + GLA_GATE_LORA], ((0, 0), (0, LANES - GLA_GATE_LORA)))
    gg = _pad_heads(wi[:, g0 + 2 * qk + vw + GLA_GATE_LORA:], GLA_HEADS, GLA_DV, GLA_DVP)
    w_in_p = _bf(jnp.concatenate([wi[:, :g0], gq, gkk, gv, ggk, gg], axis=1))
    wo = w_out[l]
    m0 = RET_W + RWKV_W
    wo_gla = jnp.pad(wo[m0:].reshape(GLA_HEADS, GLA_DV, D_MODEL), ((0, 0), (0, GLA_DVP - GLA_DV), (0, 0)))
    w_out_p = _bf(jnp.concatenate([wo[:m0], wo_gla.reshape(GLA_VW, D_MODEL)], axis=0))
    zeros = jnp.zeros((DECAY_LORA, RWKV_W), F32)
    rw = dict(
        mu=rwkv_mu[l][None, :],
        wwa=jnp.concatenate([jnp.concatenate([rwkv_w_up[l], zeros], axis=1),
                             jnp.concatenate([zeros, rwkv_a_up[l]], axis=1)], axis=0),
        w_bias=rwkv_w_bias[l][None, :], a_bias=rwkv_a_bias[l][None, :], g_up=_bf(rwkv_g_up[l]),
        k_k=rwkv_k_k[l][None, :], k_a=rwkv_k_a[l][None, :], r_k=rwkv_r_k[l].reshape(1, RWKV_W),
        ln_w=rwkv_ln_w[l][None, :], ln_b=rwkv_ln_b[l][None, :])
    gl = dict(
        gk_up=jnp.pad(_pad_heads(gla_gk_up[l], GLA_HEADS, GLA_DK, GLA_DKP), ((0, LANES - GLA_GATE_LORA), (0, 0))),
        gk_bias=_pad_heads(gla_gk_bias[l][None, :], GLA_HEADS, GLA_DK, GLA_DKP),
        norm_w=_pad_heads(gla_norm_w[l][None, :], GLA_HEADS, GLA_DV, GLA_DVP))
    wq = peer_w_q[l].reshape(D_MODEL, PEER_HEADS, 2, PEER_QHALF)
    wq_t = _bf(jnp.transpose(wq, (2, 1, 3, 0)).reshape(2 * PEER_HEADS * PEER_QHALF, D_MODEL))
    keys = _bf(jnp.transpose(peer_sub_keys[l], (1, 0, 2, 3)))
    u_p, vt_p = _pack_experts(peer_u, peer_v, l)
    return dict(w_in=w_in_p, w_out=w_out_p, rwkv=rw, gla=gl, wq_t=wq_t, keys=keys, u=u_p, vt=vt_p)


def _pack_experts_kernel(u_ref, v_ref, up_ref, vtp_ref):
    up_ref[...] = pltpu.bitcast(_bf(u_ref[0]), jnp.uint32)
    vtp_ref[...] = pltpu.bitcast(_bf(v_ref[0].T), jnp.uint32)


def _pack_experts(u, v, l, eb=1024):
    n = u.shape[1]
    return pl.pallas_call(
        _pack_experts_kernel,
        grid=(n // eb,),
        in_specs=[pl.BlockSpec((1, eb, D_MODEL), lambda i: (l, i, 0)), pl.BlockSpec((1, eb, D_MODEL), lambda i: (l, i, 0))],
        out_specs=[pl.BlockSpec((eb // 2, D_MODEL), lambda i: (i, 0)), pl.BlockSpec((D_MODEL // 2, eb), lambda i: (0, i))],
        out_shape=[jax.ShapeDtypeStruct((n // 2, D_MODEL), jnp.uint32),
                   jax.ShapeDtypeStruct((D_MODEL // 2, n), jnp.uint32)],
        compiler_params=_params(("arbitrary",)),
        name="pack_experts",
    )(u, v)


def _layer(x2, b, s, pr, gain_mix, gain_ffn, gain_final, final_norm):
    z_ret, z_rwkv, z_gla = _norm_proj(x2, gain_mix, pr["w_in"])
    o_ret = _retention(z_ret.reshape(b, s, RET_IN)).reshape(b * s, RET_W)
    o_rwkv = _rwkv(z_rwkv.reshape(b, s, RWKV_IN), pr["rwkv"]).reshape(b * s, RWKV_W)
    o_gla = _gla(z_gla.reshape(b, s, GLA_IN), pr["gla"]).reshape(b * s, GLA_VW)
    x1, hnt = _out_proj(x2, o_ret, o_rwkv, o_gla, pr["w_out"], gain_ffn)
    n1, e1, r2, e2 = _route(hnt, pr["wq_t"], pr["keys"])
    return _experts(hnt, pr["u"], pr["vt"], n1, e1, r2, e2, x1, gain_final, final_norm)


def kernel(x, norm_mix, norm_ffn, norm_final, w_in, w_out, rwkv_mu, rwkv_w_up, rwkv_w_bias, rwkv_a_up, rwkv_a_bias, rwkv_g_up, rwkv_k_k, rwkv_k_a, rwkv_r_k, rwkv_ln_w, rwkv_ln_b, gla_gk_up, gla_gk_bias, gla_norm_w, peer_w_q, peer_sub_keys, peer_u, peer_v):
    b, s, d = x.shape
    x2 = x.reshape(b * s, d)
    gain_final = norm_final[None, :]
    for l in range(DEPTH):
        pr = _layer_params(l, w_in, w_out, rwkv_mu, rwkv_w_up, rwkv_w_bias, rwkv_a_up, rwkv_a_bias, rwkv_g_up,
                           rwkv_k_k, rwkv_k_a, rwkv_r_k, rwkv_ln_w, rwkv_ln_b, gla_gk_up, gla_gk_bias, gla_norm_w,
                           peer_w_q, peer_sub_keys, peer_u, peer_v)
        x2 = _layer(x2, b, s, pr, norm_mix[l][None, :], norm_ffn[l][None, :], gain_final, l == DEPTH - 1)
    return x2.reshape(b, s, d)
```

```python
import functools

import numpy as np
import jax
import jax.numpy as jnp
from jax import lax
from jax.experimental import pallas as pl
from jax.experimental.pallas import tpu as pltpu

F32 = jnp.float32
BF16 = jnp.bfloat16
HI = lax.Precision.HIGHEST

D_MODEL = 1024
DEPTH = 2
NORM_EPS = 1e-6
HEAD_NORM_EPS = 1e-5
RET_HEADS, RET_DK, RET_DV, ROPE_BASE = 4, 64, 64, 10000.0
RET_W = RET_HEADS * RET_DK
RET_CHUNK = 128
RWKV_HEADS, RWKV_HEAD = 6, 64
RWKV_W = RWKV_HEADS * RWKV_HEAD
DECAY_LORA, AAA_LORA, GATE_LORA = 64, 64, 128
RWKV_GN_EPS = 64e-5
RWKV_CHUNK = 64
RWKV_SUB = 16
RWKV_IN = 3 * RWKV_W + DECAY_LORA + AAA_LORA + GATE_LORA
GLA_HEADS, GLA_DK, GLA_DV, GLA_GATE_LORA = 4, 48, 96, 16
GLA_GATE_NORMALIZER = 16.0
GLA_DKP, GLA_DVP = 64, 128
GLA_QW = GLA_HEADS * GLA_DKP
GLA_VW = GLA_HEADS * GLA_DVP
GLA_IN = 2 * GLA_QW + GLA_VW + 128 + GLA_VW
GLA_SUB = 16
GLA_BLOCK = 256
RET_IN = 4 * RET_W
D_INP = RET_IN + RWKV_IN + GLA_IN
D_MIXP = RET_W + RWKV_W + GLA_VW
PEER_HEADS, PEER_NKEYS, PEER_QHALF, PEER_TOPK = 8, 128, 128, 16
PEER_NEXPERTS = PEER_NKEYS * PEER_NKEYS
LANES = 128
SUBLANES = 8
VMEM_LIMIT = 56 * 1024 * 1024


def _params(sem):
    return pltpu.CompilerParams(dimension_semantics=sem, vmem_limit_bytes=VMEM_LIMIT)


def _dot(a, b, prec=None):
    return jnp.dot(a, b, precision=prec, preferred_element_type=F32)


def _dot_nt(a, b, prec=None):
    return lax.dot_general(a, b, (((1,), (1,)), ((), ())), precision=prec, preferred_element_type=F32)


def _dot_tn(a, b, prec=None):
    return lax.dot_general(a, b, (((0,), (0,)), ((), ())), precision=prec, preferred_element_type=F32)


def _bf(x):
    return x.astype(BF16)


def _iota(shape, dim):
    return lax.broadcasted_iota(jnp.int32, shape, dim)


def _same_block(shape, rblk, cblk):
    r = _iota(shape, 0) >> (rblk.bit_length() - 1)
    c = _iota(shape, 1) >> (cblk.bit_length() - 1)
    return r == c


def _full(shape):
    n = len(shape)
    return pl.BlockSpec(shape, lambda *_: (0,) * n)


def _block_ones(n, blk):
    i = np.arange(n) // blk
    return jnp.asarray((i[:, None] == i[None, :]).astype(np.float32))


def _norm_proj_kernel(x_ref, g_ref, w_ref, zr_ref, zk_ref, zg_ref):
    x = x_ref[...]
    y = x * lax.rsqrt(jnp.mean(x * x, axis=-1, keepdims=True) + NORM_EPS) * g_ref[...]
    yb = _bf(y)
    zr_ref[...] = _dot(yb, w_ref[:, 0:RET_IN])
    zk_ref[...] = _dot(yb, w_ref[:, RET_IN:RET_IN + RWKV_IN])
    zg_ref[...] = _dot(yb, w_ref[:, RET_IN + RWKV_IN:D_INP])


def _norm_proj(x2, gain, w_in_p, tm=512):
    t = x2.shape[0]
    return pl.pallas_call(
        _norm_proj_kernel,
        grid=(t // tm,),
        in_specs=[pl.BlockSpec((tm, D_MODEL), lambda i: (i, 0)), _full((1, D_MODEL)), _full((D_MODEL, D_INP))],
        out_specs=[pl.BlockSpec((tm, RET_IN), lambda i: (i, 0)), pl.BlockSpec((tm, RWKV_IN), lambda i: (i, 0)),
                   pl.BlockSpec((tm, GLA_IN), lambda i: (i, 0))],
        out_shape=[jax.ShapeDtypeStruct((t, RET_IN), F32), jax.ShapeDtypeStruct((t, RWKV_IN), F32),
                   jax.ShapeDtypeStruct((t, GLA_IN), F32)],
        compiler_params=_params(("arbitrary",)),
        name="norm_proj",
    )(x2, gain, w_in_p)


RET_BLOCK = 512


def _ret_kernel(z_ref, cos_ref, sin_ref, dmat_ref, qdec_ref, kdec_ref, cdec_ref, ones_ref, o_ref, st_ref):
    c = RET_CHUNK
    tb = RET_BLOCK

    @pl.when(pl.program_id(1) == 0)
    def _():
        st_ref[...] = jnp.zeros_like(st_ref)

    z = z_ref[0]
    q, k, v, g = (z[:, i * RET_W:(i + 1) * RET_W] for i in range(4))
    cos, sin = cos_ref[...], sin_ref[...]
    first_half = (_iota((tb, RET_W), 1) & (RET_DK - 1)) < RET_DK // 2

    def rot(t):
        return jnp.where(first_half, -pltpu.roll(t, RET_W - RET_DK // 2, 1), pltpu.roll(t, RET_DK // 2, 1))

    q = q * cos + rot(q) * sin
    k = (k * cos + rot(k) * sin) * (RET_DK ** -0.5)
    qd = q * qdec_ref[...]
    kd = k * kdec_ref[...]
    head0 = _iota((c, LANES), 1) < RET_DK
    bd = _same_block((LANES, LANES), RET_DV, RET_DK)
    units = [(ch, p) for ch in range(tb // c) for p in range(RET_HEADS // 2)]
    tile = lambda t, u: t[u[0] * c:(u[0] + 1) * c, u[1] * LANES:(u[1] + 1) * LANES]
    kb = [_bf(tile(k, u)) for u in units]
    intra = []
    for hh in range(2):
        mh = head0 if hh == 0 else jnp.logical_not(head0)
        sc = [_dot_nt(_bf(jnp.where(mh, tile(q, u), 0.0)), kb[i]) * dmat_ref[2 * u[1] + hh] for i, u in enumerate(units)]
        intra.append([_dot(_bf(s_), _bf(jnp.where(mh, tile(v, u), 0.0))) for s_, u in zip(sc, units)])
    upd = [jnp.where(bd, _dot_tn(_bf(tile(v, u)), _bf(tile(kd, u))), 0.0) for u in units]
    qdb = [_bf(tile(qd, u)) for u in units]
    st = [st_ref[p] for p in range(RET_HEADS // 2)]
    rows = []
    for ch in range(tb // c):
        row = []
        for p in range(RET_HEADS // 2):
            i = units.index((ch, p))
            row.append(intra[0][i] + intra[1][i] + _dot_nt(qdb[i], _bf(st[p])))
            st[p] = st[p] * cdec_ref[:, p * LANES:(p + 1) * LANES] + upd[i]
        rows.append(jnp.concatenate(row, axis=1))
    for p in range(RET_HEADS // 2):
        st_ref[p] = st[p]
    o = jnp.concatenate(rows, axis=0)
    ms = _dot_exact_rhs(o * o, ones_ref[...], 2) * (1.0 / RET_DV)
    o = o * lax.rsqrt(ms + HEAD_NORM_EPS)
    o_ref[0] = o * (g * jax.nn.sigmoid(g))


def _retention_consts(s):
    c = RET_CHUNK
    reps = RET_BLOCK // c
    log_gamma = np.log1p(-np.exp(np.linspace(np.log(1.0 / 32.0), np.log(1.0 / 512.0), RET_HEADS)))
    idx = np.arange(c, dtype=np.float64)
    diff = idx[:, None] - idx[None, :]
    dmat = np.where(diff >= 0, np.exp(log_gamma[:, None, None] * np.where(diff >= 0, diff, 0.0)), 0.0)
    lg_lane = np.repeat(log_gamma, RET_DK)[None, :]
    qdec = np.tile(np.exp(lg_lane * (idx[:, None] + 1.0)), (reps, 1))
    kdec = np.tile(np.exp(lg_lane * (c - 1.0 - idx[:, None])), (reps, 1))
    cdec = np.exp(lg_lane * c)
    half = RET_DK // 2
    inv = ROPE_BASE ** (-np.arange(half, dtype=np.float64) / half)
    ang = np.arange(s, dtype=np.float64)[:, None] * inv[None, :]
    cos = np.tile(np.cos(ang), (1, 2 * RET_HEADS))
    sin = np.tile(np.sin(ang), (1, 2 * RET_HEADS))
    f = lambda a: jnp.asarray(a.astype(np.float32))
    return f(cos), f(sin), f(dmat), f(qdec), f(kdec), f(cdec)


def _retention(z_ret):
    b, s, _ = z_ret.shape
    c = RET_CHUNK
    tb = RET_BLOCK
    cos, sin, dmat, qdec, kdec, cdec = _retention_consts(s)
    return pl.pallas_call(
        _ret_kernel,
        grid=(b, s // tb),
        in_specs=[pl.BlockSpec((1, tb, RET_IN), lambda i, j: (i, j, 0)),
                  pl.BlockSpec((tb, RET_W), lambda i, j: (j, 0)), pl.BlockSpec((tb, RET_W), lambda i, j: (j, 0)),
                  _full((RET_HEADS, c, c)), _full((tb, RET_W)), _full((tb, RET_W)), _full((1, RET_W)),
                  _full((RET_W, RET_W))],
        out_specs=pl.BlockSpec((1, tb, RET_W), lambda i, j: (i, j, 0)),
        out_shape=jax.ShapeDtypeStruct((b, s, RET_W), F32),
        scratch_shapes=[pltpu.VMEM((RET_HEADS // 2, LANES, LANES), F32)],
        compiler_params=_params(("arbitrary", "arbitrary")),
        name="retention",
    )(z_ret, cos, sin, dmat, qdec, kdec, cdec, _bf(_block_ones(RET_W, RET_DV)))


RWKV_BLOCK = 512
INV_PASSES = 3


def _split(x, parts):
    out = []
    for _ in range(parts - 1):
        h = _bf(x)
        out.append(h)
        x = x - h.astype(F32)
    out.append(_bf(x))
    return out


def _dot_exact_lhs(a_bf, x, parts):
    acc = None
    for p in _split(x, parts):
        t = _dot(a_bf, p)
        acc = t if acc is None else acc + t
    return acc


def _dot_exact_rhs(x, b_bf, parts):
    acc = None
    for p in _split(x, parts):
        t = _dot(p, b_bf)
        acc = t if acc is None else acc + t
    return acc


def _dot_f32(a, b, passes):
    if passes == 1:
        return _dot(_bf(a), _bf(b))
    if passes == 3:
        ah, al = _split(a, 2)
        bh, bl = _split(b, 2)
        return _dot(ah, bh) + (_dot(ah, bl) + _dot(al, bh))
    return _dot(a, b, HI)


def _rwkv_kernel(z_ref, mu_ref, wwa_ref, wb_ref, ab_ref, gup_ref, kk_ref, ka_ref, rk_ref, lnw_ref, lnb_ref,
                 ones_ref, ltri_ref, o_ref, st_ref, carry_ref):
    c = RWKV_CHUNK
    tb = RWKV_BLOCK
    w_ = RWKV_W

    @pl.when(pl.program_id(1) == 0)
    def _():
        st_ref[...] = jnp.zeros_like(st_ref)
        carry_ref[...] = jnp.zeros_like(carry_ref)

    f = z_ref[0]
    f_prev = jnp.where(_iota((tb, RWKV_IN), 0) == 0, carry_ref[...], pltpu.roll(f, 1, 0))
    carry_ref[...] = f[tb - 1:tb, :]
    f = f + (f_prev - f) * mu_ref[...]
    r, k, v = f[:, 0:w_], f[:, w_:2 * w_], f[:, 2 * w_:3 * w_]
    wa, gd = f[:, 3 * w_:3 * w_ + LANES], f[:, 3 * w_ + LANES:RWKV_IN]
    lane = _iota((tb, LANES), 1)
    proj = _dot_f32(jnp.where(lane < DECAY_LORA, jnp.tanh(wa), wa), wwa_ref[...], 3)
    wlog = -jax.nn.softplus(-(wb_ref[...] + proj[:, 0:w_])) - 0.5
    ld = -jnp.exp(wlog)
    a = jax.nn.sigmoid(ab_ref[...] + proj[:, w_:2 * w_])
    g = _dot(_bf(jax.nn.sigmoid(gd)), gup_ref[...])
    ones = ones_ref[...]
    kk = k * kk_ref[...]
    kk = kk * lax.rsqrt(jnp.maximum(_dot_exact_rhs(kk * kk, ones, 2), 1e-24))
    k = k * (1.0 + (a - 1.0) * ka_ref[...])
    cum = _dot_exact_lhs(ltri_ref[...], ld, 3)
    e_neg = jnp.exp(-cum)
    b = kk * a
    a_t_all = -kk * jnp.exp(cum - ld)
    b_t_all, k_t_all, r_t_all = b * e_neg, k * e_neg, r * jnp.exp(cum)

    n2 = 2 * c
    ri, ci = _iota((n2, n2), 0), _iota((n2, n2), 1)
    same_head = _same_block((n2, n2), c, c)
    strict = jnp.logical_and(same_head, ri > ci)
    incl = jnp.logical_and(same_head, ri >= ci)
    sub = _same_block((n2, n2), RWKV_SUB, RWKV_SUB)
    eye = (ri == ci).astype(F32)
    bd = _same_block((LANES, LANES), RWKV_HEAD, RWKV_HEAD)
    head0 = _iota((c, LANES), 1) < RWKV_HEAD
    mm = functools.partial(_dot_f32, passes=INV_PASSES)
    mm1 = functools.partial(_dot_f32, passes=1)

    def stack(x):
        return jnp.concatenate([jnp.where(head0, x, 0.0), jnp.where(head0, 0.0, x)], axis=0)

    def fold(x):
        return x[:c] + x[c:]

    units = [(ch, p) for ch in range(tb // c) for p in range(RWKV_HEADS // 2)]
    each = lambda fn, *lists: [fn(*args) for args in zip(*lists)]
    rows_of = lambda ch: slice(ch * c, (ch + 1) * c)
    lanes_of = lambda p: slice(p * LANES, (p + 1) * LANES)
    tile = lambda t: [stack(t[rows_of(ch), lanes_of(p)]) for ch, p in units]
    a_s, r_s, b_s, k_s, v_s = tile(a_t_all), tile(r_t_all), tile(b_t_all), tile(k_t_all), tile(v)
    cl = [cum[(ch + 1) * c - 1:(ch + 1) * c, lanes_of(p)] for ch, p in units]
    e_rem = [jnp.exp(cl_u - cum[rows_of(ch), lanes_of(p)]) for cl_u, (ch, p) in zip(cl, units)]
    b_h = [b[rows_of(ch), lanes_of(p)] * e for e, (ch, p) in zip(e_rem, units)]
    k_h = [k[rows_of(ch), lanes_of(p)] * e for e, (ch, p) in zip(e_rem, units)]
    v_p = [v[rows_of(ch), lanes_of(p)] for ch, p in units]
    big = each(lambda x, y, z, w: _dot_nt(_bf(jnp.concatenate([x, y], axis=0)), _bf(jnp.concatenate([z, w], axis=0))),
               a_s, r_s, b_s, k_s)
    a_ab = [jnp.where(strict, t[:n2, :n2], 0.0) for t in big]
    a_ak = [jnp.where(strict, t[:n2, n2:], 0.0) for t in big]
    a_rb = [jnp.where(incl, t[n2:, :n2], 0.0) for t in big]
    a_rk = [jnp.where(incl, t[n2:, n2:], 0.0) for t in big]
    d1 = [jnp.where(sub, t, 0.0) for t in a_ab]
    lo = each(lambda x, y: x - y, a_ab, d1)
    d2 = each(mm, d1, d1)
    t_d = each(lambda x, y: mm1(eye + x, eye + y), d1, d2)
    d4 = each(mm, d2, d2)
    t_d = each(lambda x, y: mm1(x, eye + y), t_d, d4)
    d8 = each(mm, d4, d4)
    t_d = each(lambda x, y: mm1(x, eye + y), t_d, d8)
    e1 = each(mm1, t_d, lo)
    e2 = each(mm1, e1, e1)
    t_inv = each(lambda x, y: mm1(eye + x, eye + y), e1, e2)
    t_inv = each(mm1, t_inv, t_d)
    akv = each(lambda x, y: _dot(_bf(x), _bf(y)), a_ak, v_s)
    wy = each(lambda t, x, y: mm1(t, jnp.concatenate([x, y], axis=1)), t_inv, a_s, akv)
    qo = each(lambda x, y: _dot(_bf(x), _bf(y)), a_rb, wy)
    rkv = each(lambda x, y: _dot(_bf(x), _bf(y)), a_rk, v_s)
    w_p = [fold(t[:, :LANES]) for t in wy]
    y_p = [fold(t[:, LANES:]) for t in wy]
    q_p = each(lambda x, y: fold(x + y[:, :LANES]), r_s, qo)
    ol_p = each(lambda x, y: fold(x[:, LANES:] + y), qo, rkv)
    gmat = each(lambda x, y: jnp.where(bd, _dot_tn(_bf(x), _bf(y)), 0.0), w_p, b_h)
    nmat = each(lambda y, vv, bb, kk_: jnp.where(bd, _dot_tn(_bf(jnp.concatenate([y, vv], axis=0)),
                                                             _bf(jnp.concatenate([bb, kk_], axis=0))), 0.0),
                y_p, v_p, b_h, k_h)
    pre = {u: (q_p[i], ol_p[i], gmat[i], nmat[i], jnp.exp(cl[i])) for i, u in enumerate(units)}

    outs = [[None] * (RWKV_HEADS // 2) for _ in range(tb // c)]
    st = [st_ref[p] for p in range(RWKV_HEADS // 2)]
    for ch in range(tb // c):
        for p in range(RWKV_HEADS // 2):
            q_u, ol_u, g_u, n_u, pc_u = pre[ch, p]
            stb = _bf(st[p])
            outs[ch][p] = _dot_nt(_bf(q_u), stb) + ol_u
            st[p] = st[p] * pc_u + _dot(stb, _bf(g_u)) + n_u
    for p in range(RWKV_HEADS // 2):
        st_ref[p] = st[p]
    o = jnp.concatenate([jnp.concatenate(row, axis=1) for row in outs], axis=0)
    inv_n = 1.0 / RWKV_HEAD
    d = o - _dot_exact_rhs(o, ones, 2) * inv_n
    var = _dot_exact_rhs(d * d, ones, 2) * inv_n
    o = d * lax.rsqrt(var + RWKV_GN_EPS) * lnw_ref[...] + lnb_ref[...]
    o = o + _dot_exact_rhs(r * k * rk_ref[...], ones, 2) * v
    o_ref[0] = o * g


def _rwkv(z_rwkv, pr):
    b, s, _ = z_rwkv.shape
    tb = RWKV_BLOCK
    row = lambda n: _full((1, n))
    ltri = np.kron(np.eye(tb // RWKV_CHUNK, dtype=np.float32), np.tril(np.ones((RWKV_CHUNK, RWKV_CHUNK), np.float32)))
    return pl.pallas_call(
        _rwkv_kernel,
        grid=(b, s // tb),
        in_specs=[pl.BlockSpec((1, tb, RWKV_IN), lambda i, j: (i, j, 0)), row(RWKV_IN), _full((LANES, 2 * RWKV_W)),
                  row(RWKV_W), row(RWKV_W), _full((GATE_LORA, RWKV_W)), row(RWKV_W), row(RWKV_W), row(RWKV_W),
                  row(RWKV_W), row(RWKV_W), _full((RWKV_W, RWKV_W)), _full((tb, tb))],
        out_specs=pl.BlockSpec((1, tb, RWKV_W), lambda i, j: (i, j, 0)),
        out_shape=jax.ShapeDtypeStruct((b, s, RWKV_W), F32),
        scratch_shapes=[pltpu.VMEM((RWKV_HEADS // 2, LANES, LANES), F32), pltpu.VMEM((1, RWKV_IN), F32)],
        compiler_params=_params(("arbitrary", "arbitrary")),
        name="rwkv7",
    )(z_rwkv, pr["mu"], pr["wwa"], pr["w_bias"], pr["a_bias"], pr["g_up"], pr["k_k"], pr["k_a"], pr["r_k"],
      pr["ln_w"], pr["ln_b"], _bf(_block_ones(RWKV_W, RWKV_HEAD)), jnp.asarray(ltri, dtype=BF16))


def _gla_kernel(z_ref, gkup_ref, gkb_ref, nw_ref, ind_ref, ones_ref, ltri_ref, o_ref, st_ref, x_ref):
    n = GLA_SUB
    tb = GLA_BLOCK
    nsub = tb // n

    @pl.when(pl.program_id(1) == 0)
    def _():
        st_ref[...] = jnp.zeros_like(st_ref)

    z = z_ref[0]
    q_all = z[:, 0:GLA_QW] * (GLA_DK ** -0.5)
    k_all = z[:, GLA_QW:2 * GLA_QW]
    v_all = z[:, 2 * GLA_QW:2 * GLA_QW + GLA_VW]
    gkd = z[:, 2 * GLA_QW + GLA_VW:2 * GLA_QW + GLA_VW + LANES]
    g = z[:, 2 * GLA_QW + GLA_VW + LANES:GLA_IN]
    gk = jax.nn.log_sigmoid(_dot_f32(gkd, gkup_ref[...], 3) + gkb_ref[...]) * (1.0 / GLA_GATE_NORMALIZER)
    bc_all = _dot_exact_lhs(ltri_ref[...], gk, 3)
    bd = _same_block((GLA_VW, GLA_QW), GLA_DVP, GLA_DKP)
    rowi = _iota((n, GLA_QW), 0)
    subs = range(nsub)
    rows = [slice(sc * n, (sc + 1) * n) for sc in subs]
    q = [q_all[r] for r in rows]
    k = [k_all[r] for r in rows]
    v = [v_all[r] for r in rows]
    bc = [bc_all[r] for r in rows]
    bl = [t[n - 1:n, :] for t in bc]
    for sc in subs:
        for j in range(n):
            xj = q[sc] * jnp.exp(jnp.minimum(bc[sc] - bc[sc][j:j + 1, :], 0.0)) * k[sc][j:j + 1, :]
            x_ref[(sc * n + j) * n:(sc * n + j + 1) * n, :] = jnp.where(rowi >= j, xj, 0.0)
    e = _dot(_bf(x_ref[...]), ind_ref[...])
    o_intra = []
    for sc in subs:
        acc = None
        for j in range(n):
            t = e[(sc * n + j) * n:(sc * n + j + 1) * n, :] * v[sc][j:j + 1, :]
            acc = t if acc is None else acc + t
        o_intra.append(acc)
    upd = [jnp.where(bd, _dot_tn(_bf(v[sc]), _bf(k[sc] * jnp.exp(bl[sc] - bc[sc]))), 0.0) for sc in subs]
    qe = [_bf(q[sc] * jnp.exp(bc[sc])) for sc in subs]
    st = st_ref[...]
    outs = []
    for sc in subs:
        outs.append(_dot_nt(qe[sc], _bf(st)) + o_intra[sc])
        st = st * jnp.exp(bl[sc]) + upd[sc]
    st_ref[...] = st
    o = jnp.concatenate(outs, axis=0)
    ms = _dot_exact_rhs(o * o, ones_ref[...], 2) * (1.0 / GLA_DV)
    o_ref[0] = o * lax.rsqrt(ms + HEAD_NORM_EPS) * nw_ref[...] * (g * jax.nn.sigmoid(g))


def _gla(z_gla, pr):
    b, s, _ = z_gla.shape
    tb = GLA_BLOCK
    hq = np.arange(GLA_QW) // GLA_DKP
    hv = np.arange(GLA_VW) // GLA_DVP
    ind = jnp.asarray((hq[:, None] == hv[None, :]).astype(np.float32), dtype=BF16)
    ltri = np.kron(np.eye(tb // GLA_SUB, dtype=np.float32), np.tril(np.ones((GLA_SUB, GLA_SUB), np.float32)))
    return pl.pallas_call(
        _gla_kernel,
        grid=(b, s // tb),
        in_specs=[pl.BlockSpec((1, tb, GLA_IN), lambda i, j: (i, j, 0)), _full((LANES, GLA_QW)), _full((1, GLA_QW)),
                  _full((1, GLA_VW)), _full((GLA_QW, GLA_VW)), _full((GLA_VW, GLA_VW)), _full((tb, tb))],
        out_specs=pl.BlockSpec((1, tb, GLA_VW), lambda i, j: (i, j, 0)),
        out_shape=jax.ShapeDtypeStruct((b, s, GLA_VW), F32),
        scratch_shapes=[pltpu.VMEM((GLA_VW, GLA_QW), F32), pltpu.VMEM((tb * GLA_SUB, GLA_QW), F32)],
        compiler_params=_params(("arbitrary", "arbitrary")),
        name="gla",
    )(z_gla, pr["gk_up"], pr["gk_bias"], pr["norm_w"], ind, _bf(_block_ones(GLA_VW, GLA_DVP)),
      jnp.asarray(ltri, dtype=BF16))


def _out_proj_kernel(x_ref, a_ref, b_ref, c_ref, w_ref, g_ref, x1_ref, hnt_ref):
    acc = _dot(_bf(a_ref[...]), w_ref[0:RET_W, :])
    acc += _dot(_bf(b_ref[...]), w_ref[RET_W:RET_W + RWKV_W, :])
    acc += _dot(_bf(c_ref[...]), w_ref[RET_W + RWKV_W:D_MIXP, :])
    x1 = x_ref[...] + acc
    x1_ref[...] = x1
    hn = x1 * lax.rsqrt(jnp.mean(x1 * x1, axis=-1, keepdims=True) + NORM_EPS) * g_ref[...]
    hnt_ref[...] = pltpu.bitcast(_bf(hn.T), jnp.uint32)


def _out_proj(x2, o_ret, o_rwkv, o_gla, w_out_p, gain, tm=512):
    t = x2.shape[0]
    blk = lambda n: pl.BlockSpec((tm, n), lambda i: (i, 0))
    return pl.pallas_call(
        _out_proj_kernel,
        grid=(t // tm,),
        in_specs=[blk(D_MODEL), blk(RET_W), blk(RWKV_W), blk(GLA_VW), _full((D_MIXP, D_MODEL)), _full((1, D_MODEL))],
        out_specs=[blk(D_MODEL), pl.BlockSpec((D_MODEL // 2, tm), lambda i: (0, i))],
        out_shape=[jax.ShapeDtypeStruct((t, D_MODEL), F32), jax.ShapeDtypeStruct((D_MODEL // 2, t), jnp.uint32)],
        compiler_params=_params(("arbitrary",)),
        name="out_proj",
    )(x2, o_ret, o_rwkv, o_gla, w_out_p, gain)


def _staircase_pairs():
    k = PEER_TOPK
    return [(a, b) for a in range(k) for b in range(k) if (a + 1) * (b + 1) <= k]


def _tree_max(xs):
    xs = list(xs)
    while len(xs) > 1:
        nxt = [jnp.maximum(xs[2 * i], xs[2 * i + 1]) for i in range(len(xs) // 2)]
        if len(xs) % 2:
            nxt.append(xs[-1])
        xs = nxt
    return xs[0]


def _route_kernel(hnt_ref, wq_ref, keys_ref, n1_ref, e1_ref, r2_ref, e2_ref, work_ref, s_ref, vals_ref, rank_ref):
    nh, nk, k_top = PEER_HEADS, PEER_NKEYS, PEER_TOPK
    qt = _bf(_dot(wq_ref[...], pltpu.bitcast(hnt_ref[...], BF16)))
    for p in range(2):
        for h in range(nh):
            r0 = (p * nh + h) * PEER_QHALF
            s = _dot(keys_ref[p, h], qt[r0:r0 + PEER_QHALF, :])
            work_ref[p, h] = s
            s_ref[p, h] = s
    rank_ref[...] = jnp.full(rank_ref.shape, float(k_top), F32)
    groups = nk // SUBLANES

    def extract(r, carry):
        rf = jnp.asarray(r, F32)
        for p in range(2):
            for h in range(nh):
                tiles = [work_ref[p, h, i * SUBLANES:(i + 1) * SUBLANES, :] for i in range(groups)]
                m = _tree_max(tiles)
                for sh in (4, 2, 1):
                    m = jnp.maximum(m, pltpu.roll(m, sh, 0))
                vals_ref[p, r, pl.ds(h, 1), :] = m[0:1, :]
                for i in range(groups):
                    rows = slice(i * SUBLANES, (i + 1) * SUBLANES)
                    is_max = tiles[i] == m
                    work_ref[p, h, rows, :] = jnp.where(is_max, -jnp.inf, tiles[i])
                    if p == 1:
                        rank_ref[h, rows, :] = jnp.where(is_max, rf, rank_ref[h, rows, :])
        return carry

    lax.fori_loop(0, k_top, extract, 0)

    v1 = [vals_ref[0, r] for r in range(k_top)]
    v2 = [vals_ref[1, r] for r in range(k_top)]
    cand = {(a, b): v1[a] + v2[b] for a, b in _staircase_pairs()}
    work = list(cand.values())
    tau = None
    for it in range(k_top):
        tau = _tree_max(work)
        if it + 1 < k_top:
            work = [jnp.where(w == tau, -jnp.inf, w) for w in work]
    top = cand[(0, 0)]
    z = None
    for c in cand.values():
        zi = jnp.where(c >= tau, jnp.exp(c - top), 0.0)
        z = zi if z is None else z + zi
    scale2 = 0.5 / z
    n_of_rank = []
    for a in range(k_top):
        cnt = None
        for b in range(k_top // (a + 1)):
            ge = (cand[(a, b)] >= tau).astype(F32)
            cnt = ge if cnt is None else cnt + ge
        n_of_rank.append(cnt)
    m1, m2 = v1[0], v2[0]
    for h in range(nh):
        s1 = s_ref[0, h]
        n1 = jnp.zeros_like(s1)
        for a in range(k_top):
            n1 = jnp.where(s1 == v1[a][h:h + 1, :], n_of_rank[a][h:h + 1, :], n1)
        n1_ref[h] = n1
        e1_ref[h] = jnp.exp(s1 - m1[h:h + 1, :])
        r2_ref[h] = pltpu.bitcast(_bf(rank_ref[h]), jnp.uint32)
        e2_ref[h] = pltpu.bitcast(_bf(jnp.exp(s_ref[1, h] - m2[h:h + 1, :]) * scale2[h:h + 1, :]), jnp.uint32)


def _route(hnt, wq_t, keys, tb=256):
    t = hnt.shape[1]
    nh, nk = PEER_HEADS, PEER_NKEYS
    out = pl.BlockSpec((nh, nk, tb), lambda i: (0, 0, i))
    packed = pl.BlockSpec((nh, nk // 2, tb), lambda i: (0, 0, i))
    return pl.pallas_call(
        _route_kernel,
        grid=(t // tb,),
        in_specs=[pl.BlockSpec((D_MODEL // 2, tb), lambda i: (0, i)), _full((2 * nh * PEER_QHALF, D_MODEL)),
                  _full((2, nh, nk, PEER_QHALF))],
        out_specs=[out, out, packed, packed],
        out_shape=[jax.ShapeDtypeStruct((nh, nk, t), F32)] * 2 + [jax.ShapeDtypeStruct((nh, nk // 2, t), jnp.uint32)] * 2,
        scratch_shapes=[pltpu.VMEM((2, nh, nk, tb), F32), pltpu.VMEM((2, nh, nk, tb), F32),
                        pltpu.VMEM((2, PEER_TOPK, nh, tb), F32), pltpu.VMEM((nh, nk, tb), F32)],
        compiler_params=_params(("arbitrary",)),
        name="peer_route",
    )(hnt, wq_t, keys)


EXPERT_SLABS = 8
EXPERT_PAIR = 2
PIPE_LAG = 2
ROW_SPLIT = 2


def _expert_kernel(hnt_ref, u_ref, vt_ref, n1_ref, e1_ref, r2_ref, e2_ref, x1_ref, gf_ref, o_ref, acc_ref, ht0, ht1,
                   act0, act1, *, n_blocks, final_norm):
    s = pl.program_id(0)
    n_pairs = pl.num_programs(0) - PIPE_LAG
    nk = PEER_NKEYS
    tb = hnt_ref.shape[1]
    j3 = lax.rem(jnp.clip(s - PIPE_LAG, 0, n_pairs - 1), n_blocks)

    @pl.when(s == 0)
    def _():
        for r in (ht0, ht1, act0, act1):
            r[...] = jnp.zeros_like(r)

    @pl.when(j3 == 0)
    def _():
        acc_ref[...] = jnp.zeros_like(acc_ref)

    def stages(ht_w, ht_r, act_w, act_r):
        half_w = 2 * LANES
        kc = 2 * LANES
        n_kc = D_MODEL // kc
        n_ec = (EXPERT_SLABS * nk) // kc

        assert n_kc == n_ec
        rws = (EXPERT_SLABS * nk) // ROW_SPLIT
        mws = D_MODEL // ROW_SPLIT
        hid = {}
        prj = {}

        def hidden_piece(half, c, rs):
            cols = slice(half * half_w, (half + 1) * half_w)
            t = _dot(pltpu.bitcast(u_ref[rs * rws // 2:(rs + 1) * rws // 2, c * kc:(c + 1) * kc], BF16),
                     pltpu.bitcast(hnt_ref[c * kc // 2:(c + 1) * kc // 2, cols], BF16))
            hid[half, rs] = t if c == 0 else hid[half, rs] + t

        def project_piece(half, c, rs):
            cols = slice(half * half_w, (half + 1) * half_w)
            t = _dot(pltpu.bitcast(vt_ref[rs * mws // 2:(rs + 1) * mws // 2, c * kc:(c + 1) * kc], BF16),
                     act_r[c * kc:(c + 1) * kc, cols])
            prj[half, rs] = t if c == 0 else prj[half, rs] + t

        gates = {}

        def gate_part(st, g0, heads, last):
            cols = slice(st * LANES, (st + 1) * LANES)
            for h in heads:
                r2 = pltpu.bitcast(r2_ref[h, :, cols], BF16)
                e2 = pltpu.bitcast(e2_ref[h, :, cols], BF16)
                for kk in range(EXPERT_PAIR):
                    n1 = _bf(n1_ref[h, g0 + kk:g0 + kk + 1, cols])
                    e1 = _bf(e1_ref[h, g0 + kk:g0 + kk + 1, cols])
                    gh = jnp.where(r2 < n1, e2 * e1, jnp.zeros_like(e2))
                    gates[st, g0 + kk] = gh if h == 0 else gates[st, g0 + kk] + gh
            if last:
                for kk in range(EXPERT_PAIR):
                    rows = slice((g0 + kk) * nk, (g0 + kk + 1) * nk)
                    hk = ht_r[rows, cols]
                    act_w[rows, cols] = _bf(hk * (1.0 + lax.erf(hk * (2.0 ** -0.5)))) * gates[st, g0 + kk]

        hh = PEER_HEADS // 2
        vpu_work = [functools.partial(gate_part, st, g0, heads, last)
                    for st in range(tb // LANES) for g0 in range(0, EXPERT_SLABS, EXPERT_PAIR)
                    for heads, last in ((range(0, hh), False), (range(hh, PEER_HEADS), True))]
        mxu_work = [functools.partial(piece, half, c, rs)
                    for half in range(2) for c in range(n_kc) for rs in range(ROW_SPLIT)
                    for piece in (hidden_piece, project_piece)]
        assert len(mxu_work) == len(vpu_work)
        for m, v_ in zip(mxu_work, vpu_work):
            m()
            v_()
        for half in range(2):
            cols = slice(half * half_w, (half + 1) * half_w)
            for rs in range(ROW_SPLIT):
                ht_w[rs * rws:(rs + 1) * rws, cols] = hid[half, rs]
                acc_ref[rs * mws:(rs + 1) * mws, cols] += prj[half, rs]

    parity = lax.rem(s, 2)

    @pl.when(parity == 0)
    def _():
        stages(ht0, ht1, act1, act0)

    @pl.when(parity == 1)
    def _():
        stages(ht1, ht0, act0, act1)

    @pl.when(jnp.logical_and(j3 == n_blocks - 1, s >= PIPE_LAG))
    def _():
        y = acc_ref[...].T + x1_ref[...]
        if final_norm:
            y = y * lax.rsqrt(jnp.mean(y * y, axis=-1, keepdims=True) + NORM_EPS) * gf_ref[...]
        o_ref[...] = y


def _experts(hnt, u_b, vt_b, n1, e1, r2, e2, x1, gain_f, final_norm, tb=512):
    t = hnt.shape[1]
    nh, nk = PEER_HEADS, PEER_NKEYS
    eb = EXPERT_SLABS * nk
    n_blocks = PEER_NEXPERTS // eb
    n_pairs = (t // tb) * n_blocks

    def pair(lag):
        def f(s):
            p = jnp.clip(s - lag, 0, n_pairs - 1)
            return p // n_blocks, lax.rem(p, n_blocks)
        return f

    tok = lambda lag: (lambda s: pair(lag)(s)[0])
    blk = lambda lag: (lambda s: pair(lag)(s)[1])
    routed = lambda rows: pl.BlockSpec((nh, rows, tb), lambda s: (0, 0, tok(1)(s)))
    slabs = pl.BlockSpec((nh, EXPERT_SLABS, tb), lambda s: (0, blk(1)(s), tok(1)(s)))
    return pl.pallas_call(
        functools.partial(_expert_kernel, n_blocks=n_blocks, final_norm=final_norm),
        grid=(n_pairs + PIPE_LAG,),
        in_specs=[pl.BlockSpec((D_MODEL // 2, tb), lambda s: (0, tok(0)(s))),
                  pl.BlockSpec((eb // 2, D_MODEL), lambda s: (blk(0)(s), 0)),
                  pl.BlockSpec((D_MODEL // 2, eb), lambda s: (0, blk(PIPE_LAG)(s))),
                  slabs, slabs, routed(nk // 2), routed(nk // 2),
                  pl.BlockSpec((tb, D_MODEL), lambda s: (tok(PIPE_LAG)(s), 0)), _full((1, D_MODEL))],
        out_specs=pl.BlockSpec((tb, D_MODEL), lambda s: (tok(PIPE_LAG)(s), 0)),
        out_shape=jax.ShapeDtypeStruct((t, D_MODEL), F32),
        scratch_shapes=[pltpu.VMEM((D_MODEL, tb), F32), pltpu.VMEM((eb, tb), F32), pltpu.VMEM((eb, tb), F32),
                        pltpu.VMEM((eb, tb), BF16), pltpu.VMEM((eb, tb), BF16)],
        compiler_params=_params(("arbitrary",)),
        name="peer_experts",
    )(hnt, u_b, vt_b, n1, e1, r2, e2, x1, gain_f)


def _pad_heads(w, heads, d, dp):
    lead = w.shape[:-1]
    w = w.reshape(*lead, heads, d)
    w = jnp.pad(w, [(0, 0)] * len(lead) + [(0, 0), (0, dp - d)])
    return w.reshape(*lead, heads * dp)


def _layer_params(l, w_in, w_out, rwkv_mu, rwkv_w_up, rwkv_w_bias, rwkv_a_up, rwkv_a_bias, rwkv_g_up, rwkv_k_k,
                  rwkv_k_a, rwkv_r_k, rwkv_ln_w, rwkv_ln_b, gla_gk_up, gla_gk_bias, gla_norm_w, peer_w_q,
                  peer_sub_keys, peer_u, peer_v):
    wi = w_in[l]
    g0 = RET_IN + RWKV_IN
    qk, vw = GLA_HEADS * GLA_DK, GLA_HEADS * GLA_DV
    gq = _pad_heads(wi[:, g0:g0 + qk], GLA_HEADS, GLA_DK, GLA_DKP)
    gkk = _pad_heads(wi[:, g0 + qk:g0 + 2 * qk], GLA_HEADS, GLA_DK, GLA_DKP)
    gv = _pad_heads(wi[:, g0 + 2 * qk:g0 + 2 * qk + vw], GLA_HEADS, GLA_DV, GLA_DVP)
    ggk = jnp.pad(wi[:, g0 + 2 * qk + vw:g0 + 2 * qk + vw + GLA_GATE_LORA], ((0, 0), (0, LANES - GLA_GATE_LORA)))
    gg = _pad_heads(wi[:, g0 + 2 * qk + vw + GLA_GATE_LORA:], GLA_HEADS, GLA_DV, GLA_DVP)
    w_in_p = _bf(jnp.concatenate([wi[:, :g0], gq, gkk, gv, ggk, gg], axis=1))
    wo = w_out[l]
    m0 = RET_W + RWKV_W
    wo_gla = jnp.pad(wo[m0:].reshape(GLA_HEADS, GLA_DV, D_MODEL), ((0, 0), (0, GLA_DVP - GLA_DV), (0, 0)))
    w_out_p = _bf(jnp.concatenate([wo[:m0], wo_gla.reshape(GLA_VW, D_MODEL)], axis=0))
    zeros = jnp.zeros((DECAY_LORA, RWKV_W), F32)
    rw = dict(
        mu=rwkv_mu[l][None, :],
        wwa=jnp.concatenate([jnp.concatenate([rwkv_w_up[l], zeros], axis=1),
                             jnp.concatenate([zeros, rwkv_a_up[l]], axis=1)], axis=0),
        w_bias=rwkv_w_bias[l][None, :], a_bias=rwkv_a_bias[l][None, :], g_up=_bf(rwkv_g_up[l]),
        k_k=rwkv_k_k[l][None, :], k_a=rwkv_k_a[l][None, :], r_k=rwkv_r_k[l].reshape(1, RWKV_W),
        ln_w=rwkv_ln_w[l][None, :], ln_b=rwkv_ln_b[l][None, :])
    gl = dict(
        gk_up=jnp.pad(_pad_heads(gla_gk_up[l], GLA_HEADS, GLA_DK, GLA_DKP), ((0, LANES - GLA_GATE_LORA), (0, 0))),
        gk_bias=_pad_heads(gla_gk_bias[l][None, :], GLA_HEADS, GLA_DK, GLA_DKP),
        norm_w=_pad_heads(gla_norm_w[l][None, :], GLA_HEADS, GLA_DV, GLA_DVP))
    wq = peer_w_q[l].reshape(D_MODEL, PEER_HEADS, 2, PEER_QHALF)
    wq_t = _bf(jnp.transpose(wq, (2, 1, 3, 0)).reshape(2 * PEER_HEADS * PEER_QHALF, D_MODEL))
    keys = _bf(jnp.transpose(peer_sub_keys[l], (1, 0, 2, 3)))
    u_p, vt_p = _pack_experts(peer_u, peer_v, l)
    return dict(w_in=w_in_p, w_out=w_out_p, rwkv=rw, gla=gl, wq_t=wq_t, keys=keys, u=u_p, vt=vt_p)


def _pack_experts_kernel(u_ref, v_ref, up_ref, vtp_ref):
    up_ref[...] = pltpu.bitcast(_bf(u_ref[0]), jnp.uint32)
    vtp_ref[...] = pltpu.bitcast(_bf(v_ref[0].T), jnp.uint32)


def _pack_experts(u, v, l, eb=1024):
    n = u.shape[1]
    return pl.pallas_call(
        _pack_experts_kernel,
        grid=(n // eb,),
        in_specs=[pl.BlockSpec((1, eb, D_MODEL), lambda i: (l, i, 0)), pl.BlockSpec((1, eb, D_MODEL), lambda i: (l, i, 0))],
        out_specs=[pl.BlockSpec((eb // 2, D_MODEL), lambda i: (i, 0)), pl.BlockSpec((D_MODEL // 2, eb), lambda i: (0, i))],
        out_shape=[jax.ShapeDtypeStruct((n // 2, D_MODEL), jnp.uint32),
                   jax.ShapeDtypeStruct((D_MODEL // 2, n), jnp.uint32)],
        compiler_params=_params(("arbitrary",)),
        name="pack_experts",
    )(u, v)


def _layer(x2, b, s, pr, gain_mix, gain_ffn, gain_final, final_norm):
    z_ret, z_rwkv, z_gla = _norm_proj(x2, gain_mix, pr["w_in"])
    o_ret = _retention(z_ret.reshape(b, s, RET_IN)).reshape(b * s, RET_W)
    o_rwkv = _rwkv(z_rwkv.reshape(b, s, RWKV_IN), pr["rwkv"]).reshape(b * s, RWKV_W)
    o_gla = _gla(z_gla.reshape(b, s, GLA_IN), pr["gla"]).reshape(b * s, GLA_VW)
    x1, hnt = _out_proj(x2, o_ret, o_rwkv, o_gla, pr["w_out"], gain_ffn)
    n1, e1, r2, e2 = _route(hnt, pr["wq_t"], pr["keys"])
    return _experts(hnt, pr["u"], pr["vt"], n1, e1, r2, e2, x1, gain_final, final_norm)


def kernel(x, norm_mix, norm_ffn, norm_final, w_in, w_out, rwkv_mu, rwkv_w_up, rwkv_w_bias, rwkv_a_up, rwkv_a_bias, rwkv_g_up, rwkv_k_k, rwkv_k_a, rwkv_r_k, rwkv_ln_w, rwkv_ln_b, gla_gk_up, gla_gk_bias, gla_norm_w, peer_w_q, peer_sub_keys, peer_u, peer_v):
    b, s, d = x.shape
    x2 = x.reshape(b * s, d)
    gain_final = norm_final[None, :]
    for l in range(DEPTH):
        pr = _layer_params(l, w_in, w_out, rwkv_mu, rwkv_w_up, rwkv_w_bias, rwkv_a_up, rwkv_a_bias, rwkv_g_up,
                           rwkv_k_k, rwkv_k_a, rwkv_r_k, rwkv_ln_w, rwkv_ln_b, gla_gk_up, gla_gk_bias, gla_norm_w,
                           peer_w_q, peer_sub_keys, peer_u, peer_v)
        x2 = _layer(x2, b, s, pr, norm_mix[l][None, :], norm_ffn[l][None, :], gain_final, l == DEPTH - 1)
    return x2.reshape(b, s, d)
```

```python
import functools

import numpy as np
import jax
import jax.numpy as jnp
from jax import lax
from jax.experimental import pallas as pl
from jax.experimental.pallas import tpu as pltpu

F32 = jnp.float32
BF16 = jnp.bfloat16
HI = lax.Precision.HIGHEST

D_MODEL = 1024
DEPTH = 2
NORM_EPS = 1e-6
HEAD_NORM_EPS = 1e-5
RET_HEADS, RET_DK, RET_DV, ROPE_BASE = 4, 64, 64, 10000.0
RET_W = RET_HEADS * RET_DK
RET_CHUNK = 128
RWKV_HEADS, RWKV_HEAD = 6, 64
RWKV_W = RWKV_HEADS * RWKV_HEAD
DECAY_LORA, AAA_LORA, GATE_LORA = 64, 64, 128
RWKV_GN_EPS = 64e-5
RWKV_CHUNK = 64
RWKV_SUB = 16
RWKV_IN = 3 * RWKV_W + DECAY_LORA + AAA_LORA + GATE_LORA
GLA_HEADS, GLA_DK, GLA_DV, GLA_GATE_LORA = 4, 48, 96, 16
GLA_GATE_NORMALIZER = 16.0
GLA_DKP, GLA_DVP = 64, 128
GLA_QW = GLA_HEADS * GLA_DKP
GLA_VW = GLA_HEADS * GLA_DVP
GLA_IN = 2 * GLA_QW + GLA_VW + 128 + GLA_VW
GLA_SUB = 16
GLA_BLOCK = 256
RET_IN = 4 * RET_W
D_INP = RET_IN + RWKV_IN + GLA_IN
D_MIXP = RET_W + RWKV_W + GLA_VW
PEER_HEADS, PEER_NKEYS, PEER_QHALF, PEER_TOPK = 8, 128, 128, 16
PEER_NEXPERTS = PEER_NKEYS * PEER_NKEYS
LANES = 128
SUBLANES = 8
VMEM_LIMIT = 56 * 1024 * 1024


def _params(sem):
    return pltpu.CompilerParams(dimension_semantics=sem, vmem_limit_bytes=VMEM_LIMIT)


def _dot(a, b, prec=None):
    return jnp.dot(a, b, precision=prec, preferred_element_type=F32)


def _dot_nt(a, b, prec=None):
    return lax.dot_general(a, b, (((1,), (1,)), ((), ())), precision=prec, preferred_element_type=F32)


def _dot_tn(a, b, prec=None):
    return lax.dot_general(a, b, (((0,), (0,)), ((), ())), precision=prec, preferred_element_type=F32)


def _bf(x):
    return x.astype(BF16)


def _iota(shape, dim):
    return lax.broadcasted_iota(jnp.int32, shape, dim)


def _same_block(shape, rblk, cblk):
    r = _iota(shape, 0) >> (rblk.bit_length() - 1)
    c = _iota(shape, 1) >> (cblk.bit_length() - 1)
    return r == c


def _full(shape):
    n = len(shape)
    return pl.BlockSpec(shape, lambda *_: (0,) * n)


def _block_ones(n, blk):
    i = np.arange(n) // blk
    return jnp.asarray((i[:, None] == i[None, :]).astype(np.float32))


def _norm_proj_kernel(x_ref, g_ref, w_ref, zr_ref, zk_ref, zg_ref):
    x = x_ref[...]
    y = x * lax.rsqrt(jnp.mean(x * x, axis=-1, keepdims=True) + NORM_EPS) * g_ref[...]
    yb = _bf(y)
    zr_ref[...] = _dot(yb, w_ref[:, 0:RET_IN])
    zk_ref[...] = _dot(yb, w_ref[:, RET_IN:RET_IN + RWKV_IN])
    zg_ref[...] = _dot(yb, w_ref[:, RET_IN + RWKV_IN:D_INP])


def _norm_proj(x2, gain, w_in_p, tm=512):
    t = x2.shape[0]
    return pl.pallas_call(
        _norm_proj_kernel,
        grid=(t // tm,),
        in_specs=[pl.BlockSpec((tm, D_MODEL), lambda i: (i, 0)), _full((1, D_MODEL)), _full((D_MODEL, D_INP))],
        out_specs=[pl.BlockSpec((tm, RET_IN), lambda i: (i, 0)), pl.BlockSpec((tm, RWKV_IN), lambda i: (i, 0)),
                   pl.BlockSpec((tm, GLA_IN), lambda i: (i, 0))],
        out_shape=[jax.ShapeDtypeStruct((t, RET_IN), F32), jax.ShapeDtypeStruct((t, RWKV_IN), F32),
                   jax.ShapeDtypeStruct((t, GLA_IN), F32)],
        compiler_params=_params(("arbitrary",)),
        name="norm_proj",
    )(x2, gain, w_in_p)


RET_BLOCK = 512


def _ret_kernel(z_ref, cos_ref, sin_ref, dmat_ref, qdec_ref, kdec_ref, cdec_ref, ones_ref, o_ref, st_ref):
    c = RET_CHUNK
    tb = RET_BLOCK

    @pl.when(pl.program_id(1) == 0)
    def _():
        st_ref[...] = jnp.zeros_like(st_ref)

    z = z_ref[0]
    q, k, v, g = (z[:, i * RET_W:(i + 1) * RET_W] for i in range(4))
    cos, sin = cos_ref[...], sin_ref[...]
    first_half = (_iota((tb, RET_W), 1) & (RET_DK - 1)) < RET_DK // 2

    def rot(t):
        return jnp.where(first_half, -pltpu.roll(t, RET_W - RET_DK // 2, 1), pltpu.roll(t, RET_DK // 2, 1))

    q = q * cos + rot(q) * sin
    k = (k * cos + rot(k) * sin) * (RET_DK ** -0.5)
    qd = q * qdec_ref[...]
    kd = k * kdec_ref[...]
    head0 = _iota((c, LANES), 1) < RET_DK
    bd = _same_block((LANES, LANES), RET_DV, RET_DK)
    units = [(ch, p) for ch in range(tb // c) for p in range(RET_HEADS // 2)]
    tile = lambda t, u: t[u[0] * c:(u[0] + 1) * c, u[1] * LANES:(u[1] + 1) * LANES]
    kb = [_bf(tile(k, u)) for u in units]
    intra = []
    for hh in range(2):
        mh = head0 if hh == 0 else jnp.logical_not(head0)
        sc = [_dot_nt(_bf(jnp.where(mh, tile(q, u), 0.0)), kb[i]) * dmat_ref[2 * u[1] + hh] for i, u in enumerate(units)]
        intra.append([_dot(_bf(s_), _bf(jnp.where(mh, tile(v, u), 0.0))) for s_, u in zip(sc, units)])
    upd = [jnp.where(bd, _dot_tn(_bf(tile(v, u)), _bf(tile(kd, u))), 0.0) for u in units]
    qdb = [_bf(tile(qd, u)) for u in units]
    st = [st_ref[p] for p in range(RET_HEADS // 2)]
    rows = []
    for ch in range(tb // c):
        row = []
        for p in range(RET_HEADS // 2):
            i = units.index((ch, p))
            row.append(intra[0][i] + intra[1][i] + _dot_nt(qdb[i], _bf(st[p])))
            st[p] = st[p] * cdec_ref[:, p * LANES:(p + 1) * LANES] + upd[i]
        rows.append(jnp.concatenate(row, axis=1))
    for p in range(RET_HEADS // 2):
        st_ref[p] = st[p]
    o = jnp.concatenate(rows, axis=0)
    ms = _dot_exact_rhs(o * o, ones_ref[...], 2) * (1.0 / RET_DV)
    o = o * lax.rsqrt(ms + HEAD_NORM_EPS)
    o_ref[0] = o * (g * jax.nn.sigmoid(g))


def _retention_consts(s):
    c = RET_CHUNK
    reps = RET_BLOCK // c
    log_gamma = np.log1p(-np.exp(np.linspace(np.log(1.0 / 32.0), np.log(1.0 / 512.0), RET_HEADS)))
    idx = np.arange(c, dtype=np.float64)
    diff = idx[:, None] - idx[None, :]
    dmat = np.where(diff >= 0, np.exp(log_gamma[:, None, None] * np.where(diff >= 0, diff, 0.0)), 0.0)
    lg_lane = np.repeat(log_gamma, RET_DK)[None, :]
    qdec = np.tile(np.exp(lg_lane * (idx[:, None] + 1.0)), (reps, 1))
    kdec = np.tile(np.exp(lg_lane * (c - 1.0 - idx[:, None])), (reps, 1))
    cdec = np.exp(lg_lane * c)
    half = RET_DK // 2
    inv = ROPE_BASE ** (-np.arange(half, dtype=np.float64) / half)
    ang = np.arange(s, dtype=np.float64)[:, None] * inv[None, :]
    cos = np.tile(np.cos(ang), (1, 2 * RET_HEADS))
    sin = np.tile(np.sin(ang), (1, 2 * RET_HEADS))
    f = lambda a: jnp.asarray(a.astype(np.float32))
    return f(cos), f(sin), f(dmat), f(qdec), f(kdec), f(cdec)


def _retention(z_ret):
    b, s, _ = z_ret.shape
    c = RET_CHUNK
    tb = RET_BLOCK
    cos, sin, dmat, qdec, kdec, cdec = _retention_consts(s)
    return pl.pallas_call(
        _ret_kernel,
        grid=(b, s // tb),
        in_specs=[pl.BlockSpec((1, tb, RET_IN), lambda i, j: (i, j, 0)),
                  pl.BlockSpec((tb, RET_W), lambda i, j: (j, 0)), pl.BlockSpec((tb, RET_W), lambda i, j: (j, 0)),
                  _full((RET_HEADS, c, c)), _full((tb, RET_W)), _full((tb, RET_W)), _full((1, RET_W)),
                  _full((RET_W, RET_W))],
        out_specs=pl.BlockSpec((1, tb, RET_W), lambda i, j: (i, j, 0)),
        out_shape=jax.ShapeDtypeStruct((b, s, RET_W), F32),
        scratch_shapes=[pltpu.VMEM((RET_HEADS // 2, LANES, LANES), F32)],
        compiler_params=_params(("arbitrary", "arbitrary")),
        name="retention",
    )(z_ret, cos, sin, dmat, qdec, kdec, cdec, _bf(_block_ones(RET_W, RET_DV)))


RWKV_BLOCK = 512
INV_PASSES = 3


def _split(x, parts):
    out = []
    for _ in range(parts - 1):
        h = _bf(x)
        out.append(h)
        x = x - h.astype(F32)
    out.append(_bf(x))
    return out


def _dot_exact_lhs(a_bf, x, parts):
    acc = None
    for p in _split(x, parts):
        t = _dot(a_bf, p)
        acc = t if acc is None else acc + t
    return acc


def _dot_exact_rhs(x, b_bf, parts):
    acc = None
    for p in _split(x, parts):
        t = _dot(p, b_bf)
        acc = t if acc is None else acc + t
    return acc


def _dot_f32(a, b, passes):
    if passes == 1:
        return _dot(_bf(a), _bf(b))
    if passes == 3:
        ah, al = _split(a, 2)
        bh, bl = _split(b, 2)
        return _dot(ah, bh) + (_dot(ah, bl) + _dot(al, bh))
    return _dot(a, b, HI)


def _rwkv_stages(z_ref, mu_ref, wwa_ref, wb_ref, ab_ref, gup_ref, kk_ref, ka_ref, rk_ref, lnw_ref, lnb_ref,
                 ones_ref, ltri_ref, o_ref, st_ref, carry_ref):
    c = RWKV_CHUNK
    tb = RWKV_BLOCK
    w_ = RWKV_W

    @pl.when(pl.program_id(1) == 0)
    def _():
        st_ref[...] = jnp.zeros_like(st_ref)
        carry_ref[...] = jnp.zeros_like(carry_ref)

    f = z_ref[0]
    f_prev = jnp.where(_iota((tb, RWKV_IN), 0) == 0, carry_ref[...], pltpu.roll(f, 1, 0))
    carry_ref[...] = f[tb - 1:tb, :]
    f = f + (f_prev - f) * mu_ref[...]
    r, k, v = f[:, 0:w_], f[:, w_:2 * w_], f[:, 2 * w_:3 * w_]
    wa, gd = f[:, 3 * w_:3 * w_ + LANES], f[:, 3 * w_ + LANES:RWKV_IN]
    lane = _iota((tb, LANES), 1)
    proj = _dot_f32(jnp.where(lane < DECAY_LORA, jnp.tanh(wa), wa), wwa_ref[...], 3)
    wlog = -jax.nn.softplus(-(wb_ref[...] + proj[:, 0:w_])) - 0.5
    ld = -jnp.exp(wlog)
    a = jax.nn.sigmoid(ab_ref[...] + proj[:, w_:2 * w_])
    g = _dot(_bf(jax.nn.sigmoid(gd)), gup_ref[...])
    yield
    ones = ones_ref[...]
    kk = k * kk_ref[...]
    kk = kk * lax.rsqrt(jnp.maximum(_dot_exact_rhs(kk * kk, ones, 2), 1e-24))
    k = k * (1.0 + (a - 1.0) * ka_ref[...])
    yield
    cum = _dot_exact_lhs(ltri_ref[...], ld, 3)
    yield
    e_neg = jnp.exp(-cum)
    b = kk * a
    a_t_all = -kk * jnp.exp(cum - ld)
    b_t_all, k_t_all, r_t_all = b * e_neg, k * e_neg, r * jnp.exp(cum)

    n2 = 2 * c
    ri, ci = _iota((n2, n2), 0), _iota((n2, n2), 1)
    same_head = _same_block((n2, n2), c, c)
    strict = jnp.logical_and(same_head, ri > ci)
    incl = jnp.logical_and(same_head, ri >= ci)
    sub = _same_block((n2, n2), RWKV_SUB, RWKV_SUB)
    eye = (ri == ci).astype(F32)
    bd = _same_block((LANES, LANES), RWKV_HEAD, RWKV_HEAD)
    head0 = _iota((c, LANES), 1) < RWKV_HEAD
    mm = functools.partial(_dot_f32, passes=INV_PASSES)
    mm1 = functools.partial(_dot_f32, passes=1)

    def stack(x):
        return jnp.concatenate([jnp.where(head0, x, 0.0), jnp.where(head0, 0.0, x)], axis=0)

    def fold(x):
        return x[:c] + x[c:]

    units = [(ch, p) for ch in range(tb // c) for p in range(RWKV_HEADS // 2)]
    each = lambda fn, *lists: [fn(*args) for args in zip(*lists)]
    rows_of = lambda ch: slice(ch * c, (ch + 1) * c)
    lanes_of = lambda p: slice(p * LANES, (p + 1) * LANES)
    tile = lambda t: [stack(t[rows_of(ch), lanes_of(p)]) for ch, p in units]
    a_s, r_s, b_s, k_s, v_s = tile(a_t_all), tile(r_t_all), tile(b_t_all), tile(k_t_all), tile(v)
    cl = [cum[(ch + 1) * c - 1:(ch + 1) * c, lanes_of(p)] for ch, p in units]
    e_rem = [jnp.exp(cl_u - cum[rows_of(ch), lanes_of(p)]) for cl_u, (ch, p) in zip(cl, units)]
    b_h = [b[rows_of(ch), lanes_of(p)] * e for e, (ch, p) in zip(e_rem, units)]
    k_h = [k[rows_of(ch), lanes_of(p)] * e for e, (ch, p) in zip(e_rem, units)]
    v_p = [v[rows_of(ch), lanes_of(p)] for ch, p in units]
    big = each(lambda x, y, z, w: _dot_nt(_bf(jnp.concatenate([x, y], axis=0)), _bf(jnp.concatenate([z, w], axis=0))),
               a_s, r_s, b_s, k_s)
    yield
    a_ab = [jnp.where(strict, t[:n2, :n2], 0.0) for t in big]
    a_ak = [jnp.where(strict, t[:n2, n2:], 0.0) for t in big]
    a_rb = [jnp.where(incl, t[n2:, :n2], 0.0) for t in big]
    a_rk = [jnp.where(incl, t[n2:, n2:], 0.0) for t in big]
    d1 = [jnp.where(sub, t, 0.0) for t in a_ab]
    lo = each(lambda x, y: x - y, a_ab, d1)
    yield
    d2 = each(mm, d1, d1)
    yield
    t_d = each(lambda x, y: mm1(eye + x, eye + y), d1, d2)
    yield
    d4 = each(mm, d2, d2)
    yield
    t_d = each(lambda x, y: mm1(x, eye + y), t_d, d4)
    yield
    d8 = each(mm, d4, d4)
    yield
    t_d = each(lambda x, y: mm1(x, eye + y), t_d, d8)
    yield
    e1 = each(mm1, t_d, lo)
    yield
    e2 = each(mm1, e1, e1)
    yield
    t_inv = each(lambda x, y: mm1(eye + x, eye + y), e1, e2)
    yield
    t_inv = each(mm1, t_inv, t_d)
    yield
    akv = each(lambda x, y: _dot(_bf(x), _bf(y)), a_ak, v_s)
    yield
    wy = each(lambda t, x, y: mm1(t, jnp.concatenate([x, y], axis=1)), t_inv, a_s, akv)
    yield
    qo = each(lambda x, y: _dot(_bf(x), _bf(y)), a_rb, wy)
    yield
    rkv = each(lambda x, y: _dot(_bf(x), _bf(y)), a_rk, v_s)
    yield
    w_p = [fold(t[:, :LANES]) for t in wy]
    y_p = [fold(t[:, LANES:]) for t in wy]
    q_p = each(lambda x, y: fold(x + y[:, :LANES]), r_s, qo)
    yield
    ol_p = each(lambda x, y: fold(x[:, LANES:] + y), qo, rkv)
    yield
    gmat = each(lambda x, y: jnp.where(bd, _dot_tn(_bf(x), _bf(y)), 0.0), w_p, b_h)
    yield
    nmat = each(lambda y, vv, bb, kk_: jnp.where(bd, _dot_tn(_bf(jnp.concatenate([y, vv], axis=0)),
                                                             _bf(jnp.concatenate([bb, kk_], axis=0))), 0.0),
                y_p, v_p, b_h, k_h)
    pre = {u: (q_p[i], ol_p[i], gmat[i], nmat[i], jnp.exp(cl[i])) for i, u in enumerate(units)}

    outs = [[None] * (RWKV_HEADS // 2) for _ in range(tb // c)]
    st = [st_ref[p] for p in range(RWKV_HEADS // 2)]
    for ch in range(tb // c):
        for p in range(RWKV_HEADS // 2):
            q_u, ol_u, g_u, n_u, pc_u = pre[ch, p]
            stb = _bf(st[p])
            outs[ch][p] = _dot_nt(_bf(q_u), stb) + ol_u
            st[p] = st[p] * pc_u + _dot(stb, _bf(g_u)) + n_u
        yield
    for p in range(RWKV_HEADS // 2):
        st_ref[p] = st[p]
    o = jnp.concatenate([jnp.concatenate(row, axis=1) for row in outs], axis=0)
    inv_n = 1.0 / RWKV_HEAD
    d = o - _dot_exact_rhs(o, ones, 2) * inv_n
    yield
    var = _dot_exact_rhs(d * d, ones, 2) * inv_n
    yield
    o = d * lax.rsqrt(var + RWKV_GN_EPS) * lnw_ref[...] + lnb_ref[...]
    o = o + _dot_exact_rhs(r * k * rk_ref[...], ones, 2) * v
    o_ref[0] = o * g


def _drain(*gens):
    live = list(gens)
    while live:
        for g_ in list(live):
            if next(g_, StopIteration) is StopIteration:
                live.remove(g_)


def _gla_stages(z_ref, gkup_ref, gkb_ref, nw_ref, ind_ref, ones_ref, ltri_ref, o_ref, st_ref, x_ref, part):
    n = GLA_SUB
    tb = GLA_BLOCK
    nsub = tb // n
    x_ref = x_ref.at[part]
    tok = slice(part * tb, (part + 1) * tb)

    if part == 0:
        @pl.when(pl.program_id(1) == 0)
        def _():
            st_ref[...] = jnp.zeros_like(st_ref)

    z = z_ref[0, tok, :]
    q_all = z[:, 0:GLA_QW] * (GLA_DK ** -0.5)
    k_all = z[:, GLA_QW:2 * GLA_QW]
    v_all = z[:, 2 * GLA_QW:2 * GLA_QW + GLA_VW]
    gkd = z[:, 2 * GLA_QW + GLA_VW:2 * GLA_QW + GLA_VW + LANES]
    g = z[:, 2 * GLA_QW + GLA_VW + LANES:GLA_IN]
    gk = jax.nn.log_sigmoid(_dot_f32(gkd, gkup_ref[...], 3) + gkb_ref[...]) * (1.0 / GLA_GATE_NORMALIZER)
    yield
    bc_all = _dot_exact_lhs(ltri_ref[...], gk, 3)
    yield
    bd = _same_block((GLA_VW, GLA_QW), GLA_DVP, GLA_DKP)
    rowi = _iota((n, GLA_QW), 0)
    subs = range(nsub)
    rows = [slice(sc * n, (sc + 1) * n) for sc in subs]
    q = [q_all[r] for r in rows]
    k = [k_all[r] for r in rows]
    v = [v_all[r] for r in rows]
    bc = [bc_all[r] for r in rows]
    bl = [t[n - 1:n, :] for t in bc]
    for sc in subs:
        for j in range(n):
            xj = q[sc] * jnp.exp(jnp.minimum(bc[sc] - bc[sc][j:j + 1, :], 0.0)) * k[sc][j:j + 1, :]
            x_ref[(sc * n + j) * n:(sc * n + j + 1) * n, :] = jnp.where(rowi >= j, xj, 0.0)
        if sc % 2:
            yield
    e = _dot(_bf(x_ref[...]), ind_ref[...])
    yield
    o_intra = []
    for sc in subs:
        acc = None
        for j in range(n):
            t = e[(sc * n + j) * n:(sc * n + j + 1) * n, :] * v[sc][j:j + 1, :]
            acc = t if acc is None else acc + t
        o_intra.append(acc)
        if sc % 4 == 3:
            yield
    upd = [jnp.where(bd, _dot_tn(_bf(v[sc]), _bf(k[sc] * jnp.exp(bl[sc] - bc[sc]))), 0.0) for sc in subs]
    yield
    qe = [_bf(q[sc] * jnp.exp(bc[sc])) for sc in subs]
    st = st_ref[...]
    outs = []
    for sc in subs:
        outs.append(_dot_nt(qe[sc], _bf(st)) + o_intra[sc])
        st = st * jnp.exp(bl[sc]) + upd[sc]
        if sc % 4 == 3:
            yield
    st_ref[...] = st
    o = jnp.concatenate(outs, axis=0)
    ms = _dot_exact_rhs(o * o, ones_ref[...], 2) * (1.0 / GLA_DV)
    o_ref[0, tok, :] = o * lax.rsqrt(ms + HEAD_NORM_EPS) * nw_ref[...] * (g * jax.nn.sigmoid(g))


N_RWKV_IN, N_GLA_IN = 13, 7


def _mix_kernel(*refs):
    rw_in, gl_in = refs[:N_RWKV_IN], refs[N_RWKV_IN:N_RWKV_IN + N_GLA_IN]
    o_rw, o_gl, st_rw, carry_rw, st_gl, x_gl = refs[N_RWKV_IN + N_GLA_IN:]
    assert RWKV_BLOCK == 2 * GLA_BLOCK

    def gla_both():
        yield from _gla_stages(*gl_in, o_gl, st_gl, x_gl, 0)
        yield from _gla_stages(*gl_in, o_gl, st_gl, x_gl, 1)

    _drain(_rwkv_stages(*rw_in, o_rw, st_rw, carry_rw), gla_both())


def _mixers(z_rwkv, z_gla, pr_rw, pr_gl):
    b, s, _ = z_gla.shape
    tb = RWKV_BLOCK
    row = lambda n: _full((1, n))
    blk = lambda n: pl.BlockSpec((1, tb, n), lambda i, j: (i, j, 0))
    chunk_tri = lambda t, n: jnp.asarray(np.kron(np.eye(t // n, dtype=np.float32), np.tril(np.ones((n, n), np.float32))),
                                         dtype=BF16)
    hq = np.arange(GLA_QW) // GLA_DKP
    hv = np.arange(GLA_VW) // GLA_DVP
    ind = jnp.asarray((hq[:, None] == hv[None, :]).astype(np.float32), dtype=BF16)
    rw_specs = [blk(RWKV_IN), row(RWKV_IN), _full((LANES, 2 * RWKV_W)), row(RWKV_W), row(RWKV_W),
                _full((GATE_LORA, RWKV_W)), row(RWKV_W), row(RWKV_W), row(RWKV_W), row(RWKV_W), row(RWKV_W),
                _full((RWKV_W, RWKV_W)), _full((tb, tb))]
    gl_specs = [blk(GLA_IN), _full((LANES, GLA_QW)), _full((1, GLA_QW)), _full((1, GLA_VW)), _full((GLA_QW, GLA_VW)),
                _full((GLA_VW, GLA_VW)), _full((GLA_BLOCK, GLA_BLOCK))]
    assert len(rw_specs) == N_RWKV_IN and len(gl_specs) == N_GLA_IN
    return pl.pallas_call(
        _mix_kernel,
        grid=(b, s // tb),
        in_specs=rw_specs + gl_specs,
        out_specs=[blk(RWKV_W), blk(GLA_VW)],
        out_shape=[jax.ShapeDtypeStruct((b, s, RWKV_W), F32), jax.ShapeDtypeStruct((b, s, GLA_VW), F32)],
        scratch_shapes=[pltpu.VMEM((RWKV_HEADS // 2, LANES, LANES), F32), pltpu.VMEM((1, RWKV_IN), F32),
                        pltpu.VMEM((GLA_VW, GLA_QW), F32), pltpu.VMEM((2, GLA_BLOCK * GLA_SUB, GLA_QW), F32)],
        compiler_params=_params(("arbitrary", "arbitrary")),
        name="rwkv7_gla",
    )(z_rwkv, pr_rw["mu"], pr_rw["wwa"], pr_rw["w_bias"], pr_rw["a_bias"], pr_rw["g_up"], pr_rw["k_k"], pr_rw["k_a"],
      pr_rw["r_k"], pr_rw["ln_w"], pr_rw["ln_b"], _bf(_block_ones(RWKV_W, RWKV_HEAD)), chunk_tri(tb, RWKV_CHUNK),
      z_gla, pr_gl["gk_up"], pr_gl["gk_bias"], pr_gl["norm_w"], ind, _bf(_block_ones(GLA_VW, GLA_DVP)),
      chunk_tri(GLA_BLOCK, GLA_SUB))


def _out_proj_kernel(x_ref, a_ref, b_ref, c_ref, w_ref, g_ref, x1_ref, hnt_ref):
    acc = _dot(_bf(a_ref[...]), w_ref[0:RET_W, :])
    acc += _dot(_bf(b_ref[...]), w_ref[RET_W:RET_W + RWKV_W, :])
    acc += _dot(_bf(c_ref[...]), w_ref[RET_W + RWKV_W:D_MIXP, :])
    x1 = x_ref[...] + acc
    x1_ref[...] = x1
    hn = x1 * lax.rsqrt(jnp.mean(x1 * x1, axis=-1, keepdims=True) + NORM_EPS) * g_ref[...]
    hnt_ref[...] = pltpu.bitcast(_bf(hn.T), jnp.uint32)


def _out_proj(x2, o_ret, o_rwkv, o_gla, w_out_p, gain, tm=512):
    t = x2.shape[0]
    blk = lambda n: pl.BlockSpec((tm, n), lambda i: (i, 0))
    return pl.pallas_call(
        _out_proj_kernel,
        grid=(t // tm,),
        in_specs=[blk(D_MODEL), blk(RET_W), blk(RWKV_W), blk(GLA_VW), _full((D_MIXP, D_MODEL)), _full((1, D_MODEL))],
        out_specs=[blk(D_MODEL), pl.BlockSpec((D_MODEL // 2, tm), lambda i: (0, i))],
        out_shape=[jax.ShapeDtypeStruct((t, D_MODEL), F32), jax.ShapeDtypeStruct((D_MODEL // 2, t), jnp.uint32)],
        compiler_params=_params(("arbitrary",)),
        name="out_proj",
    )(x2, o_ret, o_rwkv, o_gla, w_out_p, gain)


def _staircase_pairs():
    k = PEER_TOPK
    return [(a, b) for a in range(k) for b in range(k) if (a + 1) * (b + 1) <= k]


def _tree_max(xs):
    xs = list(xs)
    while len(xs) > 1:
        nxt = [jnp.maximum(xs[2 * i], xs[2 * i + 1]) for i in range(len(xs) // 2)]
        if len(xs) % 2:
            nxt.append(xs[-1])
        xs = nxt
    return xs[0]


def _route_kernel(hnt_ref, wq_ref, keys_ref, n1_ref, e1_ref, r2_ref, e2_ref, work_ref, s_ref, vals_ref, rank_ref):
    nh, nk, k_top = PEER_HEADS, PEER_NKEYS, PEER_TOPK
    qt = _bf(_dot(wq_ref[...], pltpu.bitcast(hnt_ref[...], BF16)))
    for p in range(2):
        for h in range(nh):
            r0 = (p * nh + h) * PEER_QHALF
            s = _dot(keys_ref[p, h], qt[r0:r0 + PEER_QHALF, :])
            work_ref[p, h] = s
            s_ref[p, h] = s
    rank_ref[...] = jnp.full(rank_ref.shape, float(k_top), F32)
    groups = nk // SUBLANES

    def extract(r, carry):
        rf = jnp.asarray(r, F32)
        for p in range(2):
            for h in range(nh):
                tiles = [work_ref[p, h, i * SUBLANES:(i + 1) * SUBLANES, :] for i in range(groups)]
                m = _tree_max(tiles)
                for sh in (4, 2, 1):
                    m = jnp.maximum(m, pltpu.roll(m, sh, 0))
                vals_ref[p, r, pl.ds(h, 1), :] = m[0:1, :]
                for i in range(groups):
                    rows = slice(i * SUBLANES, (i + 1) * SUBLANES)
                    is_max = tiles[i] == m
                    work_ref[p, h, rows, :] = jnp.where(is_max, -jnp.inf, tiles[i])
                    if p == 1:
                        rank_ref[h, rows, :] = jnp.where(is_max, rf, rank_ref[h, rows, :])
        return carry

    lax.fori_loop(0, k_top, extract, 0)

    v1 = [vals_ref[0, r] for r in range(k_top)]
    v2 = [vals_ref[1, r] for r in range(k_top)]
    cand = {(a, b): v1[a] + v2[b] for a, b in _staircase_pairs()}
    work = list(cand.values())
    tau = None
    for it in range(k_top):
        tau = _tree_max(work)
        if it + 1 < k_top:
            work = [jnp.where(w == tau, -jnp.inf, w) for w in work]
    top = cand[(0, 0)]
    z = None
    for c in cand.values():
        zi = jnp.where(c >= tau, jnp.exp(c - top), 0.0)
        z = zi if z is None else z + zi
    scale2 = 0.5 / z
    n_of_rank = []
    for a in range(k_top):
        cnt = None
        for b in range(k_top // (a + 1)):
            ge = (cand[(a, b)] >= tau).astype(F32)
            cnt = ge if cnt is None else cnt + ge
        n_of_rank.append(cnt)
    m1, m2 = v1[0], v2[0]
    for h in range(nh):
        s1 = s_ref[0, h]
        n1 = jnp.zeros_like(s1)
        for a in range(k_top):
            n1 = jnp.where(s1 == v1[a][h:h + 1, :], n_of_rank[a][h:h + 1, :], n1)
        n1_ref[h] = n1
        e1_ref[h] = jnp.exp(s1 - m1[h:h + 1, :])
        r2_ref[h] = pltpu.bitcast(_bf(rank_ref[h]), jnp.uint32)
        e2_ref[h] = pltpu.bitcast(_bf(jnp.exp(s_ref[1, h] - m2[h:h + 1, :]) * scale2[h:h + 1, :]), jnp.uint32)


def _route(hnt, wq_t, keys, tb=256):
    t = hnt.shape[1]
    nh, nk = PEER_HEADS, PEER_NKEYS
    out = pl.BlockSpec((nh, nk, tb), lambda i: (0, 0, i))
    packed = pl.BlockSpec((nh, nk // 2, tb), lambda i: (0, 0, i))
    return pl.pallas_call(
        _route_kernel,
        grid=(t // tb,),
        in_specs=[pl.BlockSpec((D_MODEL // 2, tb), lambda i: (0, i)), _full((2 * nh * PEER_QHALF, D_MODEL)),
                  _full((2, nh, nk, PEER_QHALF))],
        out_specs=[out, out, packed, packed],
        out_shape=[jax.ShapeDtypeStruct((nh, nk, t), F32)] * 2 + [jax.ShapeDtypeStruct((nh, nk // 2, t), jnp.uint32)] * 2,
        scratch_shapes=[pltpu.VMEM((2, nh, nk, tb), F32), pltpu.VMEM((2, nh, nk, tb), F32),
                        pltpu.VMEM((2, PEER_TOPK, nh, tb), F32), pltpu.VMEM((nh, nk, tb), F32)],
        compiler_params=_params(("arbitrary",)),
        name="peer_route",
    )(hnt, wq_t, keys)


EXPERT_SLABS = 8
EXPERT_PAIR = 2
PIPE_LAG = 2
ROW_SPLIT = 2


def _expert_kernel(hnt_ref, u_ref, vt_ref, n1_ref, e1_ref, r2_ref, e2_ref, x1_ref, gf_ref, o_ref, acc_ref, ht0, ht1,
                   act0, act1, *, n_blocks, final_norm):
    s = pl.program_id(0)
    n_pairs = pl.num_programs(0) - PIPE_LAG
    nk = PEER_NKEYS
    tb = hnt_ref.shape[1]
    j3 = lax.rem(jnp.clip(s - PIPE_LAG, 0, n_pairs - 1), n_blocks)

    @pl.when(s == 0)
    def _():
        for r in (ht0, ht1, act0, act1):
            r[...] = jnp.zeros_like(r)

    @pl.when(j3 == 0)
    def _():
        acc_ref[...] = jnp.zeros_like(acc_ref)

    def stages(ht_w, ht_r, act_w, act_r):
        half_w = 2 * LANES
        kc = 2 * LANES
        n_kc = D_MODEL // kc
        n_ec = (EXPERT_SLABS * nk) // kc

        assert n_kc == n_ec
        rws = (EXPERT_SLABS * nk) // ROW_SPLIT
        mws = D_MODEL // ROW_SPLIT
        hid = {}
        prj = {}

        def hidden_piece(half, c, rs):
            cols = slice(half * half_w, (half + 1) * half_w)
            t = _dot(pltpu.bitcast(u_ref[rs * rws // 2:(rs + 1) * rws // 2, c * kc:(c + 1) * kc], BF16),
                     pltpu.bitcast(hnt_ref[c * kc // 2:(c + 1) * kc // 2, cols], BF16))
            hid[half, rs] = t if c == 0 else hid[half, rs] + t

        def project_piece(half, c, rs):
            cols = slice(half * half_w, (half + 1) * half_w)
            t = _dot(pltpu.bitcast(vt_ref[rs * mws // 2:(rs + 1) * mws // 2, c * kc:(c + 1) * kc], BF16),
                     act_r[c * kc:(c + 1) * kc, cols])
            prj[half, rs] = t if c == 0 else prj[half, rs] + t

        gates = {}

        def gate_part(st, g0, heads, last):
            cols = slice(st * LANES, (st + 1) * LANES)
            for h in heads:
                r2 = pltpu.bitcast(r2_ref[h, :, cols], BF16)
                e2 = pltpu.bitcast(e2_ref[h, :, cols], BF16)
                for kk in range(EXPERT_PAIR):
                    n1 = _bf(n1_ref[h, g0 + kk:g0 + kk + 1, cols])
                    e1 = _bf(e1_ref[h, g0 + kk:g0 + kk + 1, cols])
                    gh = jnp.where(r2 < n1, e2 * e1, jnp.zeros_like(e2))
                    gates[st, g0 + kk] = gh if h == 0 else gates[st, g0 + kk] + gh
            if last:
                for kk in range(EXPERT_PAIR):
                    rows = slice((g0 + kk) * nk, (g0 + kk + 1) * nk)
                    hk = ht_r[rows, cols]
                    act_w[rows, cols] = _bf(hk * (1.0 + lax.erf(hk * (2.0 ** -0.5)))) * gates[st, g0 + kk]

        hh = PEER_HEADS // 2
        vpu_work = [functools.partial(gate_part, st, g0, heads, last)
                    for st in range(tb // LANES) for g0 in range(0, EXPERT_SLABS, EXPERT_PAIR)
                    for heads, last in ((range(0, hh), False), (range(hh, PEER_HEADS), True))]
        mxu_work = [functools.partial(piece, half, c, rs)
                    for half in range(2) for c in range(n_kc) for rs in range(ROW_SPLIT)
                    for piece in (hidden_piece, project_piece)]
        assert len(mxu_work) == len(vpu_work)
        for m, v_ in zip(mxu_work, vpu_work):
            m()
            v_()
        for half in range(2):
            cols = slice(half * half_w, (half + 1) * half_w)
            for rs in range(ROW_SPLIT):
                ht_w[rs * rws:(rs + 1) * rws, cols] = hid[half, rs]
                acc_ref[rs * mws:(rs + 1) * mws, cols] += prj[half, rs]

    parity = lax.rem(s, 2)

    @pl.when(parity == 0)
    def _():
        stages(ht0, ht1, act1, act0)

    @pl.when(parity == 1)
    def _():
        stages(ht1, ht0, act0, act1)

    @pl.when(jnp.logical_and(j3 == n_blocks - 1, s >= PIPE_LAG))
    def _():
        y = acc_ref[...].T + x1_ref[...]
        if final_norm:
            y = y * lax.rsqrt(jnp.mean(y * y, axis=-1, keepdims=True) + NORM_EPS) * gf_ref[...]
        o_ref[...] = y


def _experts(hnt, u_b, vt_b, n1, e1, r2, e2, x1, gain_f, final_norm, tb=512):
    t = hnt.shape[1]
    nh, nk = PEER_HEADS, PEER_NKEYS
    eb = EXPERT_SLABS * nk
    n_blocks = PEER_NEXPERTS // eb
    n_pairs = (t // tb) * n_blocks

    def pair(lag):
        def f(s):
            p = jnp.clip(s - lag, 0, n_pairs - 1)
            return p // n_blocks, lax.rem(p, n_blocks)
        return f

    tok = lambda lag: (lambda s: pair(lag)(s)[0])
    blk = lambda lag: (lambda s: pair(lag)(s)[1])
    routed = lambda rows: pl.BlockSpec((nh, rows, tb), lambda s: (0, 0, tok(1)(s)))
    slabs = pl.BlockSpec((nh, EXPERT_SLABS, tb), lambda s: (0, blk(1)(s), tok(1)(s)))
    return pl.pallas_call(
        functools.partial(_expert_kernel, n_blocks=n_blocks, final_norm=final_norm),
        grid=(n_pairs + PIPE_LAG,),
        in_specs=[pl.BlockSpec((D_MODEL // 2, tb), lambda s: (0, tok(0)(s))),
                  pl.BlockSpec((eb // 2, D_MODEL), lambda s: (blk(0)(s), 0)),
                  pl.BlockSpec((D_MODEL // 2, eb), lambda s: (0, blk(PIPE_LAG)(s))),
                  slabs, slabs, routed(nk // 2), routed(nk // 2),
                  pl.BlockSpec((tb, D_MODEL), lambda s: (tok(PIPE_LAG)(s), 0)), _full((1, D_MODEL))],
        out_specs=pl.BlockSpec((tb, D_MODEL), lambda s: (tok(PIPE_LAG)(s), 0)),
        out_shape=jax.ShapeDtypeStruct((t, D_MODEL), F32),
        scratch_shapes=[pltpu.VMEM((D_MODEL, tb), F32), pltpu.VMEM((eb, tb), F32), pltpu.VMEM((eb, tb), F32),
                        pltpu.VMEM((eb, tb), BF16), pltpu.VMEM((eb, tb), BF16)],
        compiler_params=_params(("arbitrary",)),
        name="peer_experts",
    )(hnt, u_b, vt_b, n1, e1, r2, e2, x1, gain_f)


def _pad_heads(w, heads, d, dp):
    lead = w.shape[:-1]
    w = w.reshape(*lead, heads, d)
    w = jnp.pad(w, [(0, 0)] * len(lead) + [(0, 0), (0, dp - d)])
    return w.reshape(*lead, heads * dp)


def _layer_params(l, w_in, w_out, rwkv_mu, rwkv_w_up, rwkv_w_bias, rwkv_a_up, rwkv_a_bias, rwkv_g_up, rwkv_k_k,
                  rwkv_k_a, rwkv_r_k, rwkv_ln_w, rwkv_ln_b, gla_gk_up, gla_gk_bias, gla_norm_w, peer_w_q,
                  peer_sub_keys, peer_u, peer_v):
    wi = w_in[l]
    g0 = RET_IN + RWKV_IN
    qk, vw = GLA_HEADS * GLA_DK, GLA_HEADS * GLA_DV
    gq = _pad_heads(wi[:, g0:g0 + qk], GLA_HEADS, GLA_DK, GLA_DKP)
    gkk = _pad_heads(wi[:, g0 + qk:g0 + 2 * qk], GLA_HEADS, GLA_DK, GLA_DKP)
    gv = _pad_heads(wi[:, g0 + 2 * qk:g0 + 2 * qk + vw], GLA_HEADS, GLA_DV, GLA_DVP)
    ggk = jnp.pad(wi[:, g0 + 2 * qk + vw:g0 + 2 * qk + vw + GLA_GATE_LORA], ((0, 0), (0, LANES - GLA_GATE_LORA)))
    gg = _pad_heads(wi[:, g0 + 2 * qk + vw + GLA_GATE_LORA:], GLA_HEADS, GLA_DV, GLA_DVP)
    w_in_p = _bf(jnp.concatenate([wi[:, :g0], gq, gkk, gv, ggk, gg], axis=1))
    wo = w_out[l]
    m0 = RET_W + RWKV_W
    wo_gla = jnp.pad(wo[m0:].reshape(GLA_HEADS, GLA_DV, D_MODEL), ((0, 0), (0, GLA_DVP - GLA_DV), (0, 0)))
    w_out_p = _bf(jnp.concatenate([wo[:m0], wo_gla.reshape(GLA_VW, D_MODEL)], axis=0))
    zeros = jnp.zeros((DECAY_LORA, RWKV_W), F32)
    rw = dict(
        mu=rwkv_mu[l][None, :],
        wwa=jnp.concatenate([jnp.concatenate([rwkv_w_up[l], zeros], axis=1),
                             jnp.concatenate([zeros, rwkv_a_up[l]], axis=1)], axis=0),
        w_bias=rwkv_w_bias[l][None, :], a_bias=rwkv_a_bias[l][None, :], g_up=_bf(rwkv_g_up[l]),
        k_k=rwkv_k_k[l][None, :], k_a=rwkv_k_a[l][None, :], r_k=rwkv_r_k[l].reshape(1, RWKV_W),
        ln_w=rwkv_ln_w[l][None, :], ln_b=rwkv_ln_b[l][None, :])
    gl = dict(
        gk_up=jnp.pad(_pad_heads(gla_gk_up[l], GLA_HEADS, GLA_DK, GLA_DKP), ((0, LANES - GLA_GATE_LORA), (0, 0))),
        gk_bias=_pad_heads(gla_gk_bias[l][None, :], GLA_HEADS, GLA_DK, GLA_DKP),
        norm_w=_pad_heads(gla_norm_w[l][None, :], GLA_HEADS, GLA_DV, GLA_DVP))
    wq = peer_w_q[l].reshape(D_MODEL, PEER_HEADS, 2, PEER_QHALF)
    wq_t = _bf(jnp.transpose(wq, (2, 1, 3, 0)).reshape(2 * PEER_HEADS * PEER_QHALF, D_MODEL))
    keys = _bf(jnp.transpose(peer_sub_keys[l], (1, 0, 2, 3)))
    u_p, vt_p = _pack_experts(peer_u, peer_v, l)
    return dict(w_in=w_in_p, w_out=w_out_p, rwkv=rw, gla=gl, wq_t=wq_t, keys=keys, u=u_p, vt=vt_p)


def _pack_experts_kernel(u_ref, v_ref, up_ref, vtp_ref):
    up_ref[...] = pltpu.bitcast(_bf(u_ref[0]), jnp.uint32)
    vtp_ref[...] = pltpu.bitcast(_bf(v_ref[0].T), jnp.uint32)


def _pack_experts(u, v, l, eb=1024):
    n = u.shape[1]
    return pl.pallas_call(
        _pack_experts_kernel,
        grid=(n // eb,),
        in_specs=[pl.BlockSpec((1, eb, D_MODEL), lambda i: (l, i, 0)), pl.BlockSpec((1, eb, D_MODEL), lambda i: (l, i, 0))],
        out_specs=[pl.BlockSpec((eb // 2, D_MODEL), lambda i: (i, 0)), pl.BlockSpec((D_MODEL // 2, eb), lambda i: (0, i))],
        out_shape=[jax.ShapeDtypeStruct((n // 2, D_MODEL), jnp.uint32),
                   jax.ShapeDtypeStruct((D_MODEL // 2, n), jnp.uint32)],
        compiler_params=_params(("arbitrary",)),
        name="pack_experts",
    )(u, v)


def _layer(x2, b, s, pr, gain_mix, gain_ffn, gain_final, final_norm):
    z_ret, z_rwkv, z_gla = _norm_proj(x2, gain_mix, pr["w_in"])
    o_ret = _retention(z_ret.reshape(b, s, RET_IN)).reshape(b * s, RET_W)
    o_rwkv, o_gla = _mixers(z_rwkv.reshape(b, s, RWKV_IN), z_gla.reshape(b, s, GLA_IN), pr["rwkv"], pr["gla"])
    o_rwkv, o_gla = o_rwkv.reshape(b * s, RWKV_W), o_gla.reshape(b * s, GLA_VW)
    x1, hnt = _out_proj(x2, o_ret, o_rwkv, o_gla, pr["w_out"], gain_ffn)
    n1, e1, r2, e2 = _route(hnt, pr["wq_t"], pr["keys"])
    return _experts(hnt, pr["u"], pr["vt"], n1, e1, r2, e2, x1, gain_final, final_norm)


def kernel(x, norm_mix, norm_ffn, norm_final, w_in, w_out, rwkv_mu, rwkv_w_up, rwkv_w_bias, rwkv_a_up, rwkv_a_bias, rwkv_g_up, rwkv_k_k, rwkv_k_a, rwkv_r_k, rwkv_ln_w, rwkv_ln_b, gla_gk_up, gla_gk_bias, gla_norm_w, peer_w_q, peer_sub_keys, peer_u, peer_v):
    b, s, d = x.shape
    x2 = x.reshape(b * s, d)
    gain_final = norm_final[None, :]
    for l in range(DEPTH):
        pr = _layer_params(l, w_in, w_out, rwkv_mu, rwkv_w_up, rwkv_w_bias, rwkv_a_up, rwkv_a_bias, rwkv_g_up,
                           rwkv_k_k, rwkv_k_a, rwkv_r_k, rwkv_ln_w, rwkv_ln_b, gla_gk_up, gla_gk_bias, gla_norm_w,
                           peer_w_q, peer_sub_keys, peer_u, peer_v)
        x2 = _layer(x2, b, s, pr, norm_mix[l][None, :], norm_ffn[l][None, :], gain_final, l == DEPTH - 1)
    return x2.reshape(b, s, d)
```
